```python
import math, functools
import jax, jax.numpy as jnp
from jax import lax
import numpy as np

D_MODEL = 1024
BATCH = 8
SEQ = 2048
DEPTH = 2
DEC_BATCH = 128
DEC_SEQ = 4
PAST_LEN = 16384
PAGE_SIZE = 128

HEAD_DIM = 64
N_HEADS = D_MODEL // 128
N_KV_HEADS = 2
Q_GROUP = N_HEADS // N_KV_HEADS
WINDOW = 128
ATTN_WIDTH = N_HEADS * HEAD_DIM
KV_WIDTH = N_KV_HEADS * HEAD_DIM
SSM_WIDTH = D_MODEL // 2
GROUP_CH = 16
SSM_GROUPS = SSM_WIDTH // GROUP_CH
SSM_STATE = 64
D_FF = 4 * D_MODEL
IN_COLS = ATTN_WIDTH + 2 * KV_WIDTH + SSM_WIDTH + 2 * D_MODEL
EPS = 1e-6
DT_MIN = 0.001
DT_MAX = 0.1
NEG_INF = -1e30

kernel_name = "gated_s5_swa_sink_hybrid_step"


def _rmsnorm(x, g):
    xf = x.astype(jnp.float32)
    y = xf * lax.rsqrt(jnp.mean(xf * xf, axis=-1, keepdims=True) + EPS) * g.astype(jnp.float32)
    return y.astype(x.dtype)


def _attend(q, k, v, q_pos, k_pos, sinks):
    b, n, nq = q.shape[:3]
    qg = q.reshape(b, n, nq, N_KV_HEADS, Q_GROUP, HEAD_DIM)
    s = jnp.einsum('bnqkgd,bnskd->bnkgqs', qg, k, preferred_element_type=jnp.float32) * (HEAD_DIM ** -0.5)
    dist = q_pos[:, :, None] - k_pos[:, None, :]
    valid = (dist >= 0) & (dist <= WINDOW) & (k_pos >= 0)[:, None, :]
    slopes = jnp.exp2(-8.0 * jnp.arange(1, N_HEADS + 1, dtype=jnp.float32) / N_HEADS).reshape(N_KV_HEADS, Q_GROUP)
    s = s - slopes[None, :, :, None, None] * dist.astype(jnp.float32)[:, None, None]
    s = jnp.where(valid[:, None, None], s, NEG_INF)
    sink = sinks.astype(jnp.float32).reshape(N_KV_HEADS, Q_GROUP)[None, None, :, :, None, None]
    m = jnp.maximum(s.max(axis=-1, keepdims=True), sink)
    e = jnp.exp(s - m)
    p = e / (e.sum(axis=-1, keepdims=True) + jnp.exp(sink - m))
    o = jnp.einsum('bnkgqs,bnskd->bnqkgd', p.astype(v.dtype), v)
    return o.reshape(b, n, nq, ATTN_WIDTH)


def _prompt_attention(q, k, v, sinks):
    b, t = q.shape[:2]
    nb = t // WINDOW
    qb = q.reshape(b, nb, WINDOW, N_HEADS, HEAD_DIM)
    pad = jnp.zeros((b, WINDOW, N_KV_HEADS, HEAD_DIM), k.dtype)

    def bands(a):
        ap = jnp.concatenate([pad, a], axis=1)
        prev = ap[:, :-WINDOW].reshape(b, nb, WINDOW, N_KV_HEADS, HEAD_DIM)
        cur = ap[:, WINDOW:].reshape(b, nb, WINDOW, N_KV_HEADS, HEAD_DIM)
        return jnp.concatenate([prev, cur], axis=2)

    q_pos = jnp.arange(t).reshape(nb, WINDOW)
    k_pos = jnp.arange(nb)[:, None] * WINDOW - WINDOW + jnp.arange(2 * WINDOW)[None, :]
    out = _attend(qb, bands(k), bands(v), q_pos, k_pos, sinks).reshape(b, t, ATTN_WIDTH)
    rows = min(WINDOW, t)
    return out, k[:, t - rows:], v[:, t - rows:]


def _sample_attention(cache_k, cache_v, q, k, v, sinks):
    past = cache_k.shape[1]
    t = q.shape[1]
    kc = jnp.concatenate([cache_k.astype(k.dtype), k], axis=1)
    vc = jnp.concatenate([cache_v.astype(v.dtype), v], axis=1)
    k_pos = PAST_LEN - past + jnp.arange(past + t)
    q_pos = PAST_LEN + jnp.arange(t)
    out = _attend(q[:, None], kc[:, None], vc[:, None], q_pos[None], k_pos[None], sinks)[:, 0]
    return out, kc[:, -past:], vc[:, -past:]


def _combine(e1, e2):
    ar1, ai1, br1, bi1 = e1
    ar2, ai2, br2, bi2 = e2
    return (ar1 * ar2 - ai1 * ai2,
            ar1 * ai2 + ai1 * ar2,
            ar2 * br1 - ai2 * bi1 + br2,
            ar2 * bi1 + ai2 * br1 + bi2)


def _s5_branch(u, h0_re, h0_im, p):
    f32 = jnp.float32
    b, t, _ = u.shape
    uf = u.astype(f32).reshape(b, t, SSM_GROUPS, GROUP_CH)
    lam_re = p['lam_re'].astype(f32)
    lam_im = p['lam_im'].astype(f32)
    step = jnp.exp(p['log_step'].astype(f32))[:, None]
    mag = jnp.exp(lam_re * step)
    ar = mag * jnp.cos(lam_im * step)
    ai = mag * jnp.sin(lam_im * step)
    den = lam_re * lam_re + lam_im * lam_im
    cr = ((ar - 1.0) * lam_re + ai * lam_im) / den
    ci = (ai * lam_re - (ar - 1.0) * lam_im) / den
    b_re = p['b_re'].astype(f32)
    b_im = p['b_im'].astype(f32)
    bb_re = cr[..., None] * b_re - ci[..., None] * b_im
    bb_im = cr[..., None] * b_im + ci[..., None] * b_re
    x_re = jnp.einsum('bsgc,gpc->sbgp', uf, bb_re)
    x_im = jnp.einsum('bsgc,gpc->sbgp', uf, bb_im)
    h0r = h0_re.astype(f32)
    h0i = h0_im.astype(f32)
    x_re = x_re.at[0].add(ar * h0r - ai * h0i)
    x_im = x_im.at[0].add(ar * h0i + ai * h0r)
    a_re = jnp.broadcast_to(ar, x_re.shape)
    a_im = jnp.broadcast_to(ai, x_im.shape)
    _, _, h_re, h_im = lax.associative_scan(_combine, (a_re, a_im, x_re, x_im), axis=0)
    y = (jnp.einsum('sbgp,gcp->bsgc', h_re, p['c_re'].astype(f32))
         - jnp.einsum('sbgp,gcp->bsgc', h_im, p['c_im'].astype(f32)))
    y = y.reshape(b, t, SSM_WIDTH) + p['d_skip'].astype(f32) * uf.reshape(b, t, SSM_WIDTH)
    y = jax.nn.gelu(y).astype(u.dtype)
    y = y * jax.nn.sigmoid(y @ p['w_glu'] + p['b_glu'])
    return y, h_re[-1], h_im[-1]


def _layer(x, attn_fn, h0_re, h0_im, p):
    b, t, _ = x.shape
    z = _rmsnorm(x, p['norm1_g']) @ p['w_in']
    o1 = ATTN_WIDTH
    o2 = o1 + KV_WIDTH
    o3 = o2 + KV_WIDTH
    o4 = o3 + SSM_WIDTH
    q, k, v, u, g = jnp.split(z, [o1, o2, o3, o4], axis=-1)
    q = _rmsnorm(q.reshape(b, t, N_HEADS, HEAD_DIM), p['q_norm_g'])
    k = _rmsnorm(k.reshape(b, t, N_KV_HEADS, HEAD_DIM), p['k_norm_g'])
    v = v.reshape(b, t, N_KV_HEADS, HEAD_DIM)
    attn, k_state, v_state = attn_fn(q, k, v, p['attn_sinks'])
    ssm, h_re, h_im = _s5_branch(u, h0_re, h0_im, p)
    gates = jax.nn.sigmoid((g + p['b_gate']).astype(jnp.float32)).astype(x.dtype)
    g_attn, g_ssm = jnp.split(gates, 2, axis=-1)
    mixed = g_attn * (attn @ p['w_attn_o']) + g_ssm * (ssm @ p['w_ssm_o'])
    x = x + mixed @ p['w_out']
    hdn = jax.nn.relu(_rmsnorm(x, p['norm2_g']) @ p['w_up'])
    x = x + (hdn * hdn) @ p['w_down']
    return x, k_state, v_state, h_re, h_im


def setup_inputs(seed: int = 0) -> dict:
    key = jax.random.key(seed)
    ks = jax.random.split(key, 32)
    f32 = jnp.float32
    nrm = lambda k, shape, s: jax.random.normal(k, shape, f32) * s
    rows = min(WINDOW, PAST_LEN)
    L, G, P, GC = DEPTH, SSM_GROUPS, SSM_STATE, GROUP_CH
    n_idx = jnp.arange(P, dtype=f32)
    return {
        'x_prompt': nrm(ks[0], (BATCH, SEQ, D_MODEL), 1.0),
        'x_sample': nrm(ks[1], (DEC_BATCH, DEC_SEQ, D_MODEL), 1.0),
        'cache_k': nrm(ks[2], (L, DEC_BATCH, rows, N_KV_HEADS, HEAD_DIM), 1.0),
        'cache_v': nrm(ks[3], (L, DEC_BATCH, rows, N_KV_HEADS, HEAD_DIM), 1.0),
        'state_ssm_re': nrm(ks[4], (L, DEC_BATCH, G, P), 0.5),
        'state_ssm_im': nrm(ks[5], (L, DEC_BATCH, G, P), 0.5),
        'norm1_g': 1.0 + nrm(ks[6], (L, D_MODEL), 0.02),
        'w_in': nrm(ks[7], (L, D_MODEL, IN_COLS), D_MODEL ** -0.5),
        'b_gate': nrm(ks[8], (L, 2 * D_MODEL), 0.01),
        'q_norm_g': 1.0 + nrm(ks[9], (L, HEAD_DIM), 0.02),
        'k_norm_g': 1.0 + nrm(ks[10], (L, HEAD_DIM), 0.02),
        'attn_sinks': nrm(ks[11], (L, N_HEADS), 0.5),
        'lam_re': -0.5 + nrm(ks[12], (L, G, P), 0.01),
        'lam_im': math.pi * n_idx + nrm(ks[13], (L, G, P), 0.01),
        'log_step': jax.random.uniform(ks[14], (L, G), f32, math.log(DT_MIN), math.log(DT_MAX)),
        'b_re': nrm(ks[15], (L, G, P, GC), (2 * GC) ** -0.5),
        'b_im': nrm(ks[16], (L, G, P, GC), (2 * GC) ** -0.5),
        'c_re': nrm(ks[17], (L, G, GC, P), P ** -0.5),
        'c_im': nrm(ks[18], (L, G, GC, P), P ** -0.5),
        'd_skip': nrm(ks[19], (L, SSM_WIDTH), 1.0),
        'w_glu': nrm(ks[20], (L, SSM_WIDTH, SSM_WIDTH), SSM_WIDTH ** -0.5),
        'b_glu': nrm(ks[21], (L, SSM_WIDTH), 0.01),
        'w_attn_o': nrm(ks[22], (L, ATTN_WIDTH, D_MODEL), ATTN_WIDTH ** -0.5),
        'w_ssm_o': nrm(ks[23], (L, SSM_WIDTH, D_MODEL), SSM_WIDTH ** -0.5),
        'w_out': nrm(ks[24], (L, D_MODEL, D_MODEL), D_MODEL ** -0.5),
        'norm2_g': 1.0 + nrm(ks[25], (L, D_MODEL), 0.02),
        'w_up': nrm(ks[26], (L, D_MODEL, D_FF), D_MODEL ** -0.5),
        'w_down': nrm(ks[27], (L, D_FF, D_MODEL), D_FF ** -0.5),
    }


def reference(x_prompt, x_sample, cache_k, cache_v, state_ssm_re, state_ssm_im,
              norm1_g, w_in, b_gate, q_norm_g, k_norm_g, attn_sinks,
              lam_re, lam_im, log_step, b_re, b_im, c_re, c_im, d_skip, w_glu, b_glu,
              w_attn_o, w_ssm_o, w_out, norm2_g, w_up, w_down):
    yp, ys = x_prompt, x_sample
    kp_l, vp_l, hrp_l, hip_l = [], [], [], []
    ks_l, vs_l, hrs_l, his_l = [], [], [], []
    sdt = state_ssm_re.dtype
    for l in range(DEPTH):
        p = {'norm1_g': norm1_g[l], 'w_in': w_in[l], 'b_gate': b_gate[l],
             'q_norm_g': q_norm_g[l], 'k_norm_g': k_norm_g[l], 'attn_sinks': attn_sinks[l],
             'lam_re': lam_re[l], 'lam_im': lam_im[l], 'log_step': log_step[l],
             'b_re': b_re[l], 'b_im': b_im[l], 'c_re': c_re[l], 'c_im': c_im[l],
             'd_skip': d_skip[l], 'w_glu': w_glu[l], 'b_glu': b_glu[l],
             'w_attn_o': w_attn_o[l], 'w_ssm_o': w_ssm_o[l], 'w_out': w_out[l],
             'norm2_g': norm2_g[l], 'w_up': w_up[l], 'w_down': w_down[l]}
        h0 = jnp.zeros((yp.shape[0], SSM_GROUPS, SSM_STATE), jnp.float32)
        yp, kp, vp, hrp, hip = _layer(yp, _prompt_attention, h0, h0, p)
        ys, kss, vss, hrs, his = _layer(
            ys, functools.partial(_sample_attention, cache_k[l], cache_v[l]),
            state_ssm_re[l], state_ssm_im[l], p)
        kp_l.append(kp); vp_l.append(vp); hrp_l.append(hrp.astype(sdt)); hip_l.append(hip.astype(sdt))
        ks_l.append(kss); vs_l.append(vss); hrs_l.append(hrs.astype(sdt)); his_l.append(his.astype(sdt))
    k_prompt = jnp.stack(kp_l)
    v_prompt = jnp.stack(vp_l)
    ssm_re_prompt = jnp.stack(hrp_l)
    ssm_im_prompt = jnp.stack(hip_l)
    k_sample = jnp.stack(ks_l)
    v_sample = jnp.stack(vs_l)
    ssm_re_sample = jnp.stack(hrs_l)
    ssm_im_sample = jnp.stack(his_l)
    return (yp, ys, k_prompt, v_prompt, ssm_re_prompt, ssm_im_prompt,
            k_sample, v_sample, ssm_re_sample, ssm_im_sample)
```

```python
import functools
import math

import numpy as np
import jax
import jax.numpy as jnp
from jax import lax
from jax.experimental import pallas as pl
from jax.experimental.pallas import tpu as pltpu

F32 = jnp.float32
BF16 = jnp.bfloat16

HEAD_DIM = 64
N_HEADS = 8
N_KV_HEADS = 2
Q_GROUP = N_HEADS // N_KV_HEADS
WINDOW = 128
GROUP_CH = 16
SSM_STATE = 64
EPS = 1e-6
NEG_INF = -1e30
LANES = 128
SUBLANES = 8
ROW_TILE = 256
VMEM_LIMIT = 56 * 1024 * 1024

HEAD_ORDER = (0, 4, 1, 5, 2, 6, 3, 7)
SLOPES = tuple(2.0 ** (-8.0 * (h + 1) / N_HEADS) for h in range(N_HEADS))


def _rms(x, g):
    return x * lax.rsqrt(jnp.mean(x * x, axis=-1, keepdims=True) + EPS) * g


def _seg_rms(x, seg, g):
    ms = jnp.dot((x * x).astype(BF16), seg, preferred_element_type=F32)
    return x * lax.rsqrt(ms + EPS) * g


def _gelu_tanh(y):
    return 0.5 * y * (1.0 + jnp.tanh(math.sqrt(2.0 / math.pi) * (y + 0.044715 * (y * y * y))))


def _const_spec(shape):
    n = len(shape)
    return pl.BlockSpec(shape, lambda *_: (0,) * n, pipeline_mode=pl.Buffered(1))


def _ssm_prep_kernel(lr_ref, li_ref, ls_ref, btr_ref, bti_ref, ctr_ref, cti_ref,
                     ar_ref, ai_ref, bx_ref, cm_ref):
    lr = lr_ref[0]
    li = li_ref[0]
    step = jnp.exp(ls_ref[0])
    mag = jnp.exp(lr * step)
    ar = mag * jnp.cos(li * step)
    ai = mag * jnp.sin(li * step)
    den = lr * lr + li * li
    cr = ((ar - 1.0) * lr + ai * li) / den
    ci = (ai * lr - (ar - 1.0) * li) / den
    ar_ref[0] = ar
    ai_ref[0] = ai
    nj = btr_ref.shape[1]
    gp = SUBLANES * SSM_STATE
    rows_b = lax.broadcasted_iota(jnp.int32, (LANES, gp), 0) // GROUP_CH
    cols_b = lax.broadcasted_iota(jnp.int32, (LANES, gp), 1) // SSM_STATE
    mask_b = rows_b == cols_b
    rows_c = lax.broadcasted_iota(jnp.int32, (gp, LANES), 0) // SSM_STATE
    cols_c = lax.broadcasted_iota(jnp.int32, (gp, LANES), 1) // GROUP_CH
    mask_c = rows_c == cols_c
    for j in range(nj):
        crj = cr[:, j * gp:(j + 1) * gp]
        cij = ci[:, j * gp:(j + 1) * gp]
        b_re = btr_ref[0, j]
        b_im = bti_ref[0, j]
        bb_re = jnp.where(mask_b, crj * b_re - cij * b_im, 0.0)
        bb_im = jnp.where(mask_b, crj * b_im + cij * b_re, 0.0)
        bx_ref[0, j, :, :gp] = bb_re.astype(BF16)
        bx_ref[0, j, :, gp:] = bb_im.astype(BF16)
        cm_ref[0, j, :gp, :] = jnp.where(mask_c, ctr_ref[0, j], 0.0).astype(BF16)
        cm_ref[0, j, gp:, :] = jnp.where(mask_c, -cti_ref[0, j], 0.0).astype(BF16)


def _ssm_prep(lam_re, lam_im, log_step, b_re, b_im, c_re, c_im):
    L, G, P = lam_re.shape
    GC = b_re.shape[-1]
    gpb = LANES // GC
    nj = G // gpb
    gp = gpb * P
    flat = lambda a: a.reshape(L, 1, G * P)
    ls = jnp.repeat(log_step, P, axis=-1).reshape(L, 1, G * P)

    def b_tiles(b):
        t = b.reshape(L, nj, gpb, P, GC).transpose(0, 1, 4, 2, 3).reshape(L, nj, 1, GC, gp)
        return jnp.broadcast_to(t, (L, nj, gpb, GC, gp)).reshape(L, nj, LANES, gp)

    def c_tiles(c):
        t = c.reshape(L, nj, gpb, GC, P).transpose(0, 1, 2, 4, 3).reshape(L, nj, gp, 1, GC)
        return jnp.broadcast_to(t, (L, nj, gp, gpb, GC)).reshape(L, nj, gp, LANES)

    vec = pl.BlockSpec((1, 1, G * P), lambda l: (l, 0, 0))
    bt = pl.BlockSpec((1, nj, LANES, gp), lambda l: (l, 0, 0, 0))
    ct = pl.BlockSpec((1, nj, gp, LANES), lambda l: (l, 0, 0, 0))
    return pl.pallas_call(
        _ssm_prep_kernel,
        grid=(L,),
        in_specs=[vec, vec, vec, bt, bt, ct, ct],
        out_specs=[vec, vec,
                   pl.BlockSpec((1, nj, LANES, 2 * gp), lambda l: (l, 0, 0, 0)),
                   pl.BlockSpec((1, nj, 2 * gp, LANES), lambda l: (l, 0, 0, 0))],
        out_shape=[jax.ShapeDtypeStruct((L, 1, G * P), F32), jax.ShapeDtypeStruct((L, 1, G * P), F32),
                   jax.ShapeDtypeStruct((L, nj, LANES, 2 * gp), BF16),
                   jax.ShapeDtypeStruct((L, nj, 2 * gp, LANES), BF16)],
        name="ssm_prep",
    )(flat(lam_re), flat(lam_im), ls, b_tiles(b_re), b_tiles(b_im), c_tiles(c_re), c_tiles(c_im))


def _project_rows(x, g1_ref, w_ref, seg_ref, qg_ref, kg_ref):
    aw = N_HEADS * HEAD_DIM
    kw = N_KV_HEADS * HEAD_DIM
    xn = _rms(x, g1_ref[...]).astype(BF16)
    z = jnp.dot(xn, w_ref[...], preferred_element_type=F32)
    q = _seg_rms(z[:, :aw], seg_ref[...], qg_ref[...]) * (HEAD_DIM ** -0.5)
    k = _seg_rms(z[:, aw:aw + kw], seg_ref[:kw, :kw], kg_ref[...])
    v = z[:, aw + kw:aw + 2 * kw]
    u = z[:, aw + 2 * kw:]
    lower = (lax.broadcasted_iota(jnp.int32, q.shape, 1) % LANES) < HEAD_DIM
    q_lo = jnp.where(lower, q, 0.0)
    q_hi = jnp.where(lower, 0.0, q)
    return q_lo, q_hi, k, v, u


def _ssm_readout(y, u, d_ref, wglu_ref, bglu_ref):
    y = _gelu_tanh(y + d_ref[...] * u)
    gate = jnp.dot(y.astype(BF16), wglu_ref[...], preferred_element_type=F32) + bglu_ref[...]
    return y * jax.nn.sigmoid(gate)


def _softmax_pv(s, sink, v):
    m = jnp.maximum(jnp.max(s, axis=-1, keepdims=True), sink)
    e = jnp.exp(s - m)
    den = jnp.sum(e, axis=-1, keepdims=True) + jnp.exp(sink - m)
    return jnp.dot(e.astype(BF16), v, preferred_element_type=F32) * (1.0 / den)


def _mixer_prompt_kernel(sink_ref, x_ref, g1_ref, w_ref, seg_ref, qg_ref, kg_ref, perm_ref,
                         ar_ref, ai_ref, bx_ref, cm_ref, d_ref, wglu_ref, bglu_ref,
                         attn_ref, ssm_ref, kout_ref, vout_ref, hr_ref, hi_ref,
                         q_s, k_s, v_s, u_s, ub_s, xr_s, xi_s, y_s):
    c = pl.program_id(0)
    nb, tt, _ = x_ref.shape
    rows = nb * tt
    n_tiles = rows // ROW_TILE
    bpt = ROW_TILE // tt
    n_state = ar_ref.shape[-1]
    nj = bx_ref.shape[0]
    gp = n_state // nj

    @pl.when(c == 0)
    def _():
        k_s[:, :tt, :] = jnp.zeros((nb, tt, k_s.shape[-1]), BF16)
        v_s[:, :tt, :] = jnp.zeros((nb, tt, v_s.shape[-1]), BF16)
        hr_ref[...] = jnp.zeros(hr_ref.shape, F32)
        hi_ref[...] = jnp.zeros(hi_ref.shape, F32)

    def project(r, carry):
        x = x_ref[pl.ds(r * bpt, bpt)].reshape(ROW_TILE, x_ref.shape[-1])
        q_lo, q_hi, k, v, u = _project_rows(x, g1_ref, w_ref, seg_ref, qg_ref, kg_ref)
        q_s[0, pl.ds(r * bpt, bpt)] = q_lo.astype(BF16).reshape(bpt, tt, -1)
        q_s[1, pl.ds(r * bpt, bpt)] = q_hi.astype(BF16).reshape(bpt, tt, -1)
        k_s[pl.ds(r * bpt, bpt), tt:, :] = k.astype(BF16).reshape(bpt, tt, -1)
        v_s[pl.ds(r * bpt, bpt), tt:, :] = v.astype(BF16).reshape(bpt, tt, -1)
        kout_ref[pl.ds(r * bpt, bpt)] = k.reshape(bpt, tt, -1)
        vout_ref[pl.ds(r * bpt, bpt)] = v.reshape(bpt, tt, -1)
        u_s[pl.ds(r * ROW_TILE, ROW_TILE), :] = u
        ub_s[pl.ds(r * ROW_TILE, ROW_TILE), :] = u.astype(BF16)
        return carry

    lax.fori_loop(0, n_tiles, project, 0)

    t_idx = lax.broadcasted_iota(jnp.int32, (tt, 2 * tt), 0)
    j_idx = lax.broadcasted_iota(jnp.int32, (tt, 2 * tt), 1)
    dist = t_idx - j_idx + tt
    valid = (dist >= 0) & (dist <= WINDOW) & (j_idx >= jnp.where(c > 0, 0, tt))
    dist_f = dist.astype(F32)
    lower = lax.broadcasted_iota(jnp.int32, (tt, LANES), 1) < HEAD_DIM

    def attend(b, carry):
        kb = k_s[b]
        vb = v_s[b]
        for p in range(N_HEADS // 2):
            outs = []
            for e in range(2):
                h = HEAD_ORDER[2 * p + e]
                qm = q_s[e, b, :, p * LANES:(p + 1) * LANES]
                s = lax.dot_general(qm, kb, (((1,), (1,)), ((), ())), preferred_element_type=F32)
                s = jnp.where(valid, s - SLOPES[h] * dist_f, NEG_INF)
                outs.append(_softmax_pv(s, sink_ref[h], vb))
            attn_ref[b, :, p * LANES:(p + 1) * LANES] = jnp.where(lower, outs[0], outs[1]).astype(BF16)
        return carry

    lax.fori_loop(0, nb, attend, 0)

    k_s[:, :tt, :] = k_s[:, tt:, :]
    v_s[:, :tt, :] = v_s[:, tt:, :]

    def ssm_in(r, carry):
        up = jnp.dot(perm_ref[pl.ds(r * ROW_TILE, ROW_TILE), :], ub_s[...],
                     preferred_element_type=F32).astype(BF16)
        for j in range(nj):
            xj = jnp.dot(up[:, j * LANES:(j + 1) * LANES], bx_ref[j], preferred_element_type=F32)
            xr_s[pl.ds(r * ROW_TILE, ROW_TILE), j * gp:(j + 1) * gp] = xj[:, :gp]
            xi_s[pl.ds(r * ROW_TILE, ROW_TILE), j * gp:(j + 1) * gp] = xj[:, gp:]
        return carry

    lax.fori_loop(0, n_tiles, ssm_in, 0)

    lane_blk = 4 * LANES
    for lb in range(n_state // lane_blk):
        sl = slice(lb * lane_blk, (lb + 1) * lane_blk)
        a_re = jnp.broadcast_to(ar_ref[:, sl], (nb, lane_blk))
        a_im = jnp.broadcast_to(ai_ref[:, sl], (nb, lane_blk))

        def scan(t, h):
            h_re, h_im = h
            row = pl.multiple_of(t * nb, nb)
            n_re = a_re * h_re - a_im * h_im + xr_s[pl.ds(row, nb), sl]
            n_im = a_re * h_im + a_im * h_re + xi_s[pl.ds(row, nb), sl]
            xr_s[pl.ds(row, nb), sl] = n_re
            xi_s[pl.ds(row, nb), sl] = n_im
            return n_re, n_im

        h_re, h_im = lax.fori_loop(0, tt, scan, (hr_ref[:, sl], hi_ref[:, sl]), unroll=8)
        hr_ref[:, sl] = h_re
        hi_ref[:, sl] = h_im

    def ssm_out(r, carry):
        for j in range(nj):
            h_cat = jnp.concatenate(
                [xr_s[pl.ds(r * ROW_TILE, ROW_TILE), j * gp:(j + 1) * gp],
                 xi_s[pl.ds(r * ROW_TILE, ROW_TILE), j * gp:(j + 1) * gp]], axis=1).astype(BF16)
            y_s[j, pl.ds(r * ROW_TILE, ROW_TILE), :] = jnp.dot(h_cat, cm_ref[j], preferred_element_type=F32)
        return carry

    lax.fori_loop(0, n_tiles, ssm_out, 0)

    def readout(b, carry):
        y = jnp.concatenate([y_s[j, pl.ds(b, tt, stride=nb), :] for j in range(nj)], axis=1)
        u = u_s[pl.ds(pl.multiple_of(b * tt, tt), tt), :]
        ssm_ref[b] = _ssm_readout(y, u, d_ref, wglu_ref, bglu_ref).astype(BF16)
        return carry

    lax.fori_loop(0, nb, readout, 0)


def _mixer_prompt(x, sinks, g1, w, seg, qg, kg, perm, ar, ai, bx, cm, d, wglu, bglu):
    nb, T, D = x.shape
    tt = WINDOW
    rows = nb * tt
    aw = N_HEADS * HEAD_DIM
    kw = N_KV_HEADS * HEAD_DIM
    sw = d.shape[-1]
    n_state = ar.shape[-1]
    nj = bx.shape[0]
    chunk = lambda last: pl.BlockSpec((nb, tt, last), lambda c: (0, c, 0))
    whole = lambda last: pl.BlockSpec((nb, tt, last), lambda c: (0, 0, 0))
    state = pl.BlockSpec((nb, n_state), lambda c: (0, 0))
    return pl.pallas_call(
        _mixer_prompt_kernel,
        grid=(T // tt,),
        in_specs=[pl.BlockSpec(memory_space=pltpu.SMEM), chunk(D)]
                 + [_const_spec(a.shape) for a in (g1, w, seg, qg, kg, perm, ar, ai, bx, cm, d, wglu, bglu)],
        out_specs=[chunk(aw), chunk(sw), whole(kw), whole(kw), state, state],
        out_shape=[jax.ShapeDtypeStruct((nb, T, aw), BF16), jax.ShapeDtypeStruct((nb, T, sw), BF16),
                   jax.ShapeDtypeStruct((nb, tt, kw), F32), jax.ShapeDtypeStruct((nb, tt, kw), F32),
                   jax.ShapeDtypeStruct((nb, n_state), F32), jax.ShapeDtypeStruct((nb, n_state), F32)],
        scratch_shapes=[pltpu.VMEM((2, nb, tt, aw), BF16),
                        pltpu.VMEM((nb, 2 * tt, kw), BF16), pltpu.VMEM((nb, 2 * tt, kw), BF16),
                        pltpu.VMEM((rows, sw), F32), pltpu.VMEM((rows, sw), BF16),
                        pltpu.VMEM((rows, n_state), F32), pltpu.VMEM((rows, n_state), F32),
                        pltpu.VMEM((nj, rows, LANES), F32)],
        compiler_params=pltpu.CompilerParams(dimension_semantics=("arbitrary",), vmem_limit_bytes=VMEM_LIMIT),
        name="mixer_prompt",
    )(sinks, x, g1, w, seg, qg, kg, perm, ar, ai, bx, cm, d, wglu, bglu)


def _mixer_sample_kernel(sink_ref, x_ref, ck_ref, cv_ref, h0r_ref, h0i_ref, g1_ref, w_ref, seg_ref, qg_ref, kg_ref,
                         ar_ref, ai_ref, bx_ref, cm_ref, d_ref, wglu_ref, bglu_ref,
                         attn_ref, ssm_ref, knew_ref, vnew_ref, hr_ref, hi_ref,
                         q_s, o_s, u_s, xr_s, xi_s):
    i = pl.program_id(0)
    rows = x_ref.shape[0]
    nbatch = h0r_ref.shape[0]
    ts = rows // nbatch
    past = ck_ref.shape[1]
    gb = ck_ref.shape[0]
    n_tiles = rows // ROW_TILE
    n_state = ar_ref.shape[-1]
    nj = bx_ref.shape[0]
    gp = n_state // nj

    @pl.when(i == 0)
    def _():
        def project(r, carry):
            rs = pl.ds(r * ROW_TILE, ROW_TILE)
            q_lo, q_hi, k, v, u = _project_rows(x_ref[rs, :], g1_ref, w_ref, seg_ref, qg_ref, kg_ref)
            q_s[0, rs, :] = q_lo
            q_s[1, rs, :] = q_hi
            knew_ref[rs, :] = k
            vnew_ref[rs, :] = v
            u_s[rs, :] = u
            return carry

        lax.fori_loop(0, n_tiles, project, 0)

    n_q = ts * Q_GROUP * gb
    n_keys = gb * past + ts * gb
    r_idx = lax.broadcasted_iota(jnp.int32, (n_q, n_keys), 0)
    c_idx = lax.broadcasted_iota(jnp.int32, (n_q, n_keys), 1)
    q_t = r_idx // (Q_GROUP * gb)
    q_b = r_idx % gb
    is_cache = c_idx < gb * past
    key_b = jnp.where(is_cache, c_idx // past, (c_idx - gb * past) % gb)
    dist = jnp.where(is_cache, past + q_t - c_idx % past, q_t - (c_idx - gb * past) // gb)
    valid = (key_b == q_b) & (dist >= 0) & (dist <= WINDOW)
    dist_f = dist.astype(F32)
    hh = (lax.broadcasted_iota(jnp.int32, (n_q, 1), 0) // gb) % Q_GROUP
    lower = lax.broadcasted_iota(jnp.int32, (gb, LANES), 1) < HEAD_DIM
    base = pl.multiple_of(i * gb, gb)

    k_all = jnp.concatenate(
        [ck_ref[...].reshape(gb * past, -1)] + [knew_ref[pl.ds(base + t * nbatch, gb), :] for t in range(ts)],
        axis=0).astype(BF16)
    v_all = jnp.concatenate(
        [cv_ref[...].reshape(gb * past, -1)] + [vnew_ref[pl.ds(base + t * nbatch, gb), :] for t in range(ts)],
        axis=0).astype(BF16)
    outs = []
    for kvh in range(N_KV_HEADS):
        slope = jnp.zeros((n_q, 1), F32)
        sink = jnp.zeros((n_q, 1), F32)
        for g in range(Q_GROUP):
            h = kvh * Q_GROUP + g
            slope = jnp.where(hh == g, SLOPES[h], slope)
            sink = jnp.where(hh == g, sink_ref[h], sink)
        qm = jnp.concatenate(
            [q_s[kvh, pl.ds(base + t * nbatch, gb), g * LANES:(g + 1) * LANES]
             for t in range(ts) for g in range(Q_GROUP)], axis=0).astype(BF16)
        s = lax.dot_general(qm, k_all, (((1,), (1,)), ((), ())), preferred_element_type=F32)
        s = jnp.where(valid, s - slope * dist_f, NEG_INF)
        outs.append(_softmax_pv(s, sink, v_all))
    for t in range(ts):
        for g in range(Q_GROUP):
            r0 = (t * Q_GROUP + g) * gb
            o_s[pl.ds(base + t * nbatch, gb), g * LANES:(g + 1) * LANES] = jnp.where(
                lower, outs[0][r0:r0 + gb], outs[1][r0:r0 + gb])

    @pl.when(i == pl.num_programs(0) - 1)
    def _():
        attn_ref[...] = o_s[...].astype(BF16)

        def ssm_in(r, carry):
            rs = pl.ds(r * ROW_TILE, ROW_TILE)
            ub = u_s[rs, :].astype(BF16)
            for j in range(nj):
                xj = jnp.dot(ub[:, j * LANES:(j + 1) * LANES], bx_ref[j], preferred_element_type=F32)
                xr_s[rs, j * gp:(j + 1) * gp] = xj[:, :gp]
                xi_s[rs, j * gp:(j + 1) * gp] = xj[:, gp:]
            return carry

        lax.fori_loop(0, n_tiles, ssm_in, 0)

        lane_blk = 4 * LANES
        for lb in range(n_state // lane_blk):
            sl = slice(lb * lane_blk, (lb + 1) * lane_blk)
            a_re = jnp.broadcast_to(ar_ref[:, sl], (SUBLANES, lane_blk))
            a_im = jnp.broadcast_to(ai_ref[:, sl], (SUBLANES, lane_blk))

            def scan(bt, carry):
                b0 = pl.multiple_of(bt * SUBLANES, SUBLANES)
                h_re = h0r_ref[pl.ds(b0, SUBLANES), sl]
                h_im = h0i_ref[pl.ds(b0, SUBLANES), sl]
                for t in range(ts):
                    rs = pl.ds(b0 + t * nbatch, SUBLANES)
                    n_re = a_re * h_re - a_im * h_im + xr_s[rs, sl]
                    n_im = a_re * h_im + a_im * h_re + xi_s[rs, sl]
                    xr_s[rs, sl] = n_re
                    xi_s[rs, sl] = n_im
                    h_re, h_im = n_re, n_im
                hr_ref[pl.ds(b0, SUBLANES), sl] = h_re
                hi_ref[pl.ds(b0, SUBLANES), sl] = h_im
                return carry

            lax.fori_loop(0, nbatch // SUBLANES, scan, 0)

        def ssm_out(r, carry):
            rs = pl.ds(r * ROW_TILE, ROW_TILE)
            ys = []
            for j in range(nj):
                h_cat = jnp.concatenate([xr_s[rs, j * gp:(j + 1) * gp], xi_s[rs, j * gp:(j + 1) * gp]],
                                        axis=1).astype(BF16)
                ys.append(jnp.dot(h_cat, cm_ref[j], preferred_element_type=F32))
            y = jnp.concatenate(ys, axis=1)
            ssm_ref[rs, :] = _ssm_readout(y, u_s[rs, :], d_ref, wglu_ref, bglu_ref).astype(BF16)
            return carry

        lax.fori_loop(0, n_tiles, ssm_out, 0)


def _mixer_sample(x, ck, cv, h0r, h0i, sinks, g1, w, seg, qg, kg, ar, ai, bx, cm, d, wglu, bglu):
    rows, D = x.shape
    nbatch, past, kw = ck.shape
    aw = N_HEADS * HEAD_DIM
    sw = d.shape[-1]
    n_state = ar.shape[-1]
    gb = SUBLANES
    cache = pl.BlockSpec((gb, past, kw), lambda i: (i, 0, 0))
    full = lambda a: pl.BlockSpec(a.shape, lambda i: (0,) * a.ndim)
    out2 = lambda last: pl.BlockSpec((rows, last), lambda i: (0, 0))
    return pl.pallas_call(
        _mixer_sample_kernel,
        grid=(nbatch // gb,),
        in_specs=[pl.BlockSpec(memory_space=pltpu.SMEM), full(x), cache, cache, full(h0r), full(h0i)]
                 + [_const_spec(a.shape) for a in (g1, w, seg, qg, kg, ar, ai, bx, cm, d, wglu, bglu)],
        out_specs=[out2(aw), out2(sw), out2(kw), out2(kw),
                   pl.BlockSpec((nbatch, n_state), lambda i: (0, 0)), pl.BlockSpec((nbatch, n_state), lambda i: (0, 0))],
        out_shape=[jax.ShapeDtypeStruct((rows, aw), BF16), jax.ShapeDtypeStruct((rows, sw), BF16),
                   jax.ShapeDtypeStruct((rows, kw), F32), jax.ShapeDtypeStruct((rows, kw), F32),
                   jax.ShapeDtypeStruct((nbatch, n_state), F32), jax.ShapeDtypeStruct((nbatch, n_state), F32)],
        scratch_shapes=[pltpu.VMEM((2, rows, aw), F32), pltpu.VMEM((rows, aw), F32), pltpu.VMEM((rows, sw), F32),
                        pltpu.VMEM((rows, n_state), F32), pltpu.VMEM((rows, n_state), F32)],
        compiler_params=pltpu.CompilerParams(dimension_semantics=("arbitrary",), vmem_limit_bytes=VMEM_LIMIT),
        name="mixer_sample",
    )(sinks, x, ck, cv, h0r, h0i, g1, w, seg, qg, kg, ar, ai, bx, cm, d, wglu, bglu)


def _mix_ffn_kernel(x_ref, attn_ref, ssm_ref, g1_ref, wg_ref, bg_ref, wao_ref, wso_ref, wout_ref,
                    g2_ref, wup_ref, wdn_ref, o_ref):
    x = x_ref[...]
    D = x.shape[-1]
    xn = _rms(x, g1_ref[...]).astype(BF16)
    attn = attn_ref[...]
    ssm = ssm_ref[...]
    o_ref[...] = x
    cw = 2 * LANES
    for cc in range(D // cw):
        cs = slice(cc * cw, (cc + 1) * cw)
        gs = slice(D + cc * cw, D + (cc + 1) * cw)
        g_attn = jax.nn.sigmoid(jnp.dot(xn, wg_ref[:, cs], preferred_element_type=F32) + bg_ref[:, cs])
        g_ssm = jax.nn.sigmoid(jnp.dot(xn, wg_ref[:, gs], preferred_element_type=F32) + bg_ref[:, gs])
        mixed = (g_attn * jnp.dot(attn, wao_ref[:, cs], preferred_element_type=F32)
                 + g_ssm * jnp.dot(ssm, wso_ref[:, cs], preferred_element_type=F32))
        o_ref[...] += jnp.dot(mixed.astype(BF16), wout_ref[cs, :], preferred_element_type=F32)
    xn2 = _rms(o_ref[...], g2_ref[...]).astype(BF16)
    fw = 4 * LANES
    for fc in range(wup_ref.shape[-1] // fw):
        fs = slice(fc * fw, (fc + 1) * fw)
        hdn = jnp.maximum(jnp.dot(xn2, wup_ref[:, fs], preferred_element_type=F32), 0.0)
        o_ref[...] += jnp.dot((hdn * hdn).astype(BF16), wdn_ref[fs, :], preferred_element_type=F32)


def _mix_ffn(x, attn, ssm, g1, wg, bg, wao, wso, wout, g2, wup, wdn):
    rows, D = x.shape
    tm = min(512, rows)
    assert rows % tm == 0
    row = lambda last: pl.BlockSpec((tm, last), lambda r: (r, 0))
    return pl.pallas_call(
        _mix_ffn_kernel,
        grid=(rows // tm,),
        in_specs=[row(D), row(attn.shape[-1]), row(ssm.shape[-1])]
                 + [_const_spec(a.shape) for a in (g1, wg, bg, wao, wso, wout, g2, wup, wdn)],
        out_specs=row(D),
        out_shape=jax.ShapeDtypeStruct((rows, D), F32),
        compiler_params=pltpu.CompilerParams(dimension_semantics=("arbitrary",), vmem_limit_bytes=VMEM_LIMIT),
        name="mix_ffn",
    )(x, attn, ssm, g1, wg, bg, wao, wso, wout, g2, wup, wdn)


def _perm_matrix(nb, tt):
    p = np.zeros((nb * tt, nb * tt), np.float32)
    t, b = np.meshgrid(np.arange(tt), np.arange(nb), indexing="ij")
    p[(t * nb + b).ravel(), (b * tt + t).ravel()] = 1.0
    return jnp.asarray(p, BF16)


def _seg_matrix(width):
    i = np.arange(width) // HEAD_DIM
    return jnp.asarray((i[:, None] == i[None, :]).astype(np.float32) / HEAD_DIM, BF16)


def kernel(x_prompt, x_sample, cache_k, cache_v, state_ssm_re, state_ssm_im, norm1_g, w_in, b_gate, q_norm_g,
           k_norm_g, attn_sinks, lam_re, lam_im, log_step, b_re, b_im, c_re, c_im, d_skip, w_glu, b_glu,
           w_attn_o, w_ssm_o, w_out, norm2_g, w_up, w_down):
    B, T, D = x_prompt.shape
    DB, DS, _ = x_sample.shape
    L = w_in.shape[0]
    aw = N_HEADS * HEAD_DIM
    kw = N_KV_HEADS * HEAD_DIM
    sw = d_skip.shape[-1]
    G, P = lam_re.shape[1:]
    past = cache_k.shape[2]
    assert past == WINDOW and T % WINDOW == 0 and (B * WINDOW) % ROW_TILE == 0 and (DB * DS) % ROW_TILE == 0
    order = np.asarray(HEAD_ORDER)

    ar, ai, bx, cm = _ssm_prep(lam_re, lam_im, log_step, b_re, b_im, c_re, c_im)
    perm = _perm_matrix(B, WINDOW)
    seg = _seg_matrix(aw)
    row = lambda a: a.reshape(1, -1)

    yp = x_prompt
    ys = x_sample.transpose(1, 0, 2).reshape(DS * DB, D)
    outs = [[] for _ in range(8)]
    for l in range(L):
        wq = w_in[l][:, :aw].reshape(D, N_HEADS, HEAD_DIM)[:, order].reshape(D, aw)
        w_qkvu = jnp.concatenate([wq, w_in[l][:, aw:aw + 2 * kw + sw]], axis=1).astype(BF16)
        wg = w_in[l][:, aw + 2 * kw + sw:].astype(BF16)
        wao = w_attn_o[l].reshape(N_HEADS, HEAD_DIM, D)[order].reshape(aw, D).astype(BF16)
        qg = row(jnp.tile(q_norm_g[l], N_HEADS))
        kg = row(jnp.tile(k_norm_g[l], N_KV_HEADS))
        mixer_w = (row(norm1_g[l]), w_qkvu, seg, qg, kg)
        ssm_w = (ar[l], ai[l], bx[l], cm[l], row(d_skip[l]), w_glu[l].astype(BF16), row(b_glu[l]))
        ffn_w = (row(norm1_g[l]), wg, row(b_gate[l]), wao, w_ssm_o[l].astype(BF16), w_out[l].astype(BF16),
                 row(norm2_g[l]), w_up[l].astype(BF16), w_down[l].astype(BF16))

        attn_p, ssm_p, kp, vp, hrp, hip = _mixer_prompt(yp, attn_sinks[l], *mixer_w, perm, *ssm_w)
        yp = _mix_ffn(yp.reshape(B * T, D), attn_p.reshape(B * T, aw), ssm_p.reshape(B * T, sw),
                      *ffn_w).reshape(B, T, D)

        attn_s, ssm_s, knew, vnew, hrs, his = _mixer_sample(
            ys, cache_k[l].reshape(DB, past, kw), cache_v[l].reshape(DB, past, kw),
            state_ssm_re[l].reshape(DB, G * P), state_ssm_im[l].reshape(DB, G * P),
            attn_sinks[l], *mixer_w, *ssm_w)
        ys = _mix_ffn(ys, attn_s, ssm_s, *ffn_w)

        tail = lambda new: new.reshape(DS, DB, N_KV_HEADS, HEAD_DIM).transpose(1, 0, 2, 3)
        vals = (kp.reshape(B, WINDOW, N_KV_HEADS, HEAD_DIM), vp.reshape(B, WINDOW, N_KV_HEADS, HEAD_DIM),
                hrp.reshape(B, G, P), hip.reshape(B, G, P),
                jnp.concatenate([cache_k[l][:, DS:], tail(knew)], axis=1),
                jnp.concatenate([cache_v[l][:, DS:], tail(vnew)], axis=1),
                hrs.reshape(DB, G, P), his.reshape(DB, G, P))
        for o, v in zip(outs, vals):
            o.append(v)

    y_sample = ys.reshape(DS, DB, D).transpose(1, 0, 2)
    return (yp, y_sample) + tuple(jnp.stack(o) for o in outs)
```

```python
import functools
import math

import numpy as np
import jax
import jax.numpy as jnp
from jax import lax
from jax.experimental import pallas as pl
from jax.experimental.pallas import tpu as pltpu

F32 = jnp.float32
BF16 = jnp.bfloat16

HEAD_DIM = 64
N_HEADS = 8
N_KV_HEADS = 2
Q_GROUP = N_HEADS // N_KV_HEADS
WINDOW = 128
GROUP_CH = 16
SSM_STATE = 64
EPS = 1e-6
NEG_INF = -1e30
LANES = 128
SUBLANES = 8
ROW_TILE = 256
VMEM_LIMIT = 56 * 1024 * 1024

HEAD_ORDER = (0, 4, 1, 5, 2, 6, 3, 7)
SLOPES = tuple(2.0 ** (-8.0 * (h + 1) / N_HEADS) for h in range(N_HEADS))


def _rms(x, g):
    return x * lax.rsqrt(jnp.mean(x * x, axis=-1, keepdims=True) + EPS) * g


def _seg_rms(x, seg, g):
    ms = jnp.dot((x * x).astype(BF16), seg, preferred_element_type=F32)
    return x * lax.rsqrt(ms + EPS) * g


def _gelu_tanh(y):
    return 0.5 * y * (1.0 + jnp.tanh(math.sqrt(2.0 / math.pi) * (y + 0.044715 * (y * y * y))))


def _const_spec(a):
    n = a.ndim
    return pl.BlockSpec(a.shape, lambda *_: (0,) * n, pipeline_mode=pl.Buffered(1))


def _layer_spec(a, layer):
    n = a.ndim - 1
    return pl.BlockSpec((None,) + a.shape[1:], lambda *_: (layer,) + (0,) * n, pipeline_mode=pl.Buffered(1))


def _ssm_prep_kernel(lr_ref, li_ref, ls_ref, btr_ref, bti_ref, ctr_ref, cti_ref,
                     ar_ref, ai_ref, bx_ref, cm_ref):
    lr = lr_ref[0]
    li = li_ref[0]
    step = jnp.exp(ls_ref[0])
    mag = jnp.exp(lr * step)
    ar = mag * jnp.cos(li * step)
    ai = mag * jnp.sin(li * step)
    den = lr * lr + li * li
    cr = ((ar - 1.0) * lr + ai * li) / den
    ci = (ai * lr - (ar - 1.0) * li) / den
    ar_ref[0] = ar
    ai_ref[0] = ai
    nj = btr_ref.shape[1]
    gp = SUBLANES * SSM_STATE
    rows_b = lax.broadcasted_iota(jnp.int32, (LANES, gp), 0) // GROUP_CH
    cols_b = lax.broadcasted_iota(jnp.int32, (LANES, gp), 1) // SSM_STATE
    mask_b = rows_b == cols_b
    rows_c = lax.broadcasted_iota(jnp.int32, (gp, LANES), 0) // SSM_STATE
    cols_c = lax.broadcasted_iota(jnp.int32, (gp, LANES), 1) // GROUP_CH
    mask_c = rows_c == cols_c
    for j in range(nj):
        crj = cr[:, j * gp:(j + 1) * gp]
        cij = ci[:, j * gp:(j + 1) * gp]
        b_re = btr_ref[0, j]
        b_im = bti_ref[0, j]
        bb_re = jnp.where(mask_b, crj * b_re - cij * b_im, 0.0)
        bb_im = jnp.where(mask_b, crj * b_im + cij * b_re, 0.0)
        bx_ref[0, j, :, :gp] = bb_re.astype(BF16)
        bx_ref[0, j, :, gp:] = bb_im.astype(BF16)
        cm_ref[0, j, :gp, :] = jnp.where(mask_c, ctr_ref[0, j], 0.0).astype(BF16)
        cm_ref[0, j, gp:, :] = jnp.where(mask_c, -cti_ref[0, j], 0.0).astype(BF16)


def _ssm_prep(lam_re, lam_im, log_step, b_re, b_im, c_re, c_im):
    L, G, P = lam_re.shape
    GC = b_re.shape[-1]
    gpb = LANES // GC
    nj = G // gpb
    gp = gpb * P
    flat = lambda a: a.reshape(L, 1, G * P)
    ls = jnp.repeat(log_step, P, axis=-1).reshape(L, 1, G * P)

    def b_tiles(b):
        t = b.reshape(L, nj, gpb, P, GC).transpose(0, 1, 4, 2, 3).reshape(L, nj, 1, GC, gp)
        return jnp.broadcast_to(t, (L, nj, gpb, GC, gp)).reshape(L, nj, LANES, gp)

    def c_tiles(c):
        t = c.reshape(L, nj, gpb, GC, P).transpose(0, 1, 2, 4, 3).reshape(L, nj, gp, 1, GC)
        return jnp.broadcast_to(t, (L, nj, gp, gpb, GC)).reshape(L, nj, gp, LANES)

    vec = pl.BlockSpec((1, 1, G * P), lambda l: (l, 0, 0))
    bt = pl.BlockSpec((1, nj, LANES, gp), lambda l: (l, 0, 0, 0))
    ct = pl.BlockSpec((1, nj, gp, LANES), lambda l: (l, 0, 0, 0))
    return pl.pallas_call(
        _ssm_prep_kernel,
        grid=(L,),
        in_specs=[vec, vec, vec, bt, bt, ct, ct],
        out_specs=[vec, vec,
                   pl.BlockSpec((1, nj, LANES, 2 * gp), lambda l: (l, 0, 0, 0)),
                   pl.BlockSpec((1, nj, 2 * gp, LANES), lambda l: (l, 0, 0, 0))],
        out_shape=[jax.ShapeDtypeStruct((L, 1, G * P), F32), jax.ShapeDtypeStruct((L, 1, G * P), F32),
                   jax.ShapeDtypeStruct((L, nj, LANES, 2 * gp), BF16),
                   jax.ShapeDtypeStruct((L, nj, 2 * gp, LANES), BF16)],
        name="ssm_prep",
    )(flat(lam_re), flat(lam_im), ls, b_tiles(b_re), b_tiles(b_im), c_tiles(c_re), c_tiles(c_im))


def _project_rows(x, g1_ref, w_ref, seg_ref, qg_ref, kg_ref):
    aw = N_HEADS * HEAD_DIM
    kw = N_KV_HEADS * HEAD_DIM
    xn = _rms(x, g1_ref[...]).astype(BF16)
    z = jnp.dot(xn, w_ref[...], preferred_element_type=F32)
    q = _seg_rms(z[:, :aw], seg_ref[...], qg_ref[...]) * (HEAD_DIM ** -0.5)
    k = _seg_rms(z[:, aw:aw + kw], seg_ref[:kw, :kw], kg_ref[...])
    v = z[:, aw + kw:aw + 2 * kw]
    u = z[:, aw + 2 * kw:]
    lower = (lax.broadcasted_iota(jnp.int32, q.shape, 1) % LANES) < HEAD_DIM
    q_lo = jnp.where(lower, q, 0.0)
    q_hi = jnp.where(lower, 0.0, q)
    return q_lo, q_hi, k, v, u


def _ssm_readout(y, u, d_ref, wglu_ref, bglu_ref):
    y = _gelu_tanh(y + d_ref[...] * u)
    gate = jnp.dot(y.astype(BF16), wglu_ref[...], preferred_element_type=F32) + bglu_ref[...]
    return y * jax.nn.sigmoid(gate)


def _softmax_pv(s, sink, v):
    m = jnp.maximum(jnp.max(s, axis=-1, keepdims=True), sink)
    e = jnp.exp(s - m)
    den = jnp.sum(e, axis=-1, keepdims=True) + jnp.exp(sink - m)
    return jnp.dot(e.astype(BF16), v, preferred_element_type=F32) * (1.0 / den)


def _mixer_prompt_kernel(layer, sink_ref, x_ref, seg_ref, perm_ref, g1_ref, w_ref, qg_ref, kg_ref,
                         ar_ref, ai_ref, bx_ref, cm_ref, d_ref, wglu_ref, bglu_ref,
                         attn_ref, ssm_ref, kout_ref, vout_ref, hr_ref, hi_ref,
                         q_s, k_s, v_s, u_s, ub_s, xr_s, xi_s, y_s):
    c = pl.program_id(0)
    nb, tt, _ = x_ref.shape
    rows = nb * tt
    n_tiles = rows // ROW_TILE
    bpt = ROW_TILE // tt
    n_state = ar_ref.shape[-1]
    nj = bx_ref.shape[0]
    gp = n_state // nj

    @pl.when(c == 0)
    def _():
        k_s[:, :tt, :] = jnp.zeros((nb, tt, k_s.shape[-1]), BF16)
        v_s[:, :tt, :] = jnp.zeros((nb, tt, v_s.shape[-1]), BF16)
        hr_ref[...] = jnp.zeros(hr_ref.shape, F32)
        hi_ref[...] = jnp.zeros(hi_ref.shape, F32)

    def project(r, carry):
        x = x_ref[pl.ds(r * bpt, bpt)].reshape(ROW_TILE, x_ref.shape[-1])
        q_lo, q_hi, k, v, u = _project_rows(x, g1_ref, w_ref, seg_ref, qg_ref, kg_ref)
        q_s[0, pl.ds(r * bpt, bpt)] = q_lo.astype(BF16).reshape(bpt, tt, -1)
        q_s[1, pl.ds(r * bpt, bpt)] = q_hi.astype(BF16).reshape(bpt, tt, -1)
        k_s[pl.ds(r * bpt, bpt), tt:, :] = k.astype(BF16).reshape(bpt, tt, -1)
        v_s[pl.ds(r * bpt, bpt), tt:, :] = v.astype(BF16).reshape(bpt, tt, -1)
        kout_ref[pl.ds(r * bpt, bpt)] = k.reshape(bpt, tt, -1)
        vout_ref[pl.ds(r * bpt, bpt)] = v.reshape(bpt, tt, -1)
        u_s[pl.ds(r * ROW_TILE, ROW_TILE), :] = u
        ub_s[pl.ds(r * ROW_TILE, ROW_TILE), :] = u.astype(BF16)
        return carry

    lax.fori_loop(0, n_tiles, project, 0)

    t_idx = lax.broadcasted_iota(jnp.int32, (tt, 2 * tt), 0)
    j_idx = lax.broadcasted_iota(jnp.int32, (tt, 2 * tt), 1)
    dist = t_idx - j_idx + tt
    valid = (dist >= 0) & (dist <= WINDOW) & (j_idx >= jnp.where(c > 0, 0, tt))
    dist_f = dist.astype(F32)
    lower = lax.broadcasted_iota(jnp.int32, (tt, LANES), 1) < HEAD_DIM

    def attend(b, carry):
        kb = k_s[b]
        vb = v_s[b]
        for p in range(N_HEADS // 2):
            outs = []
            for e in range(2):
                h = HEAD_ORDER[2 * p + e]
                qm = q_s[e, b, :, p * LANES:(p + 1) * LANES]
                s = lax.dot_general(qm, kb, (((1,), (1,)), ((), ())), preferred_element_type=F32)
                s = jnp.where(valid, s - SLOPES[h] * dist_f, NEG_INF)
                outs.append(_softmax_pv(s, sink_ref[layer, h], vb))
            attn_ref[b, :, p * LANES:(p + 1) * LANES] = jnp.where(lower, outs[0], outs[1]).astype(BF16)
        return carry

    lax.fori_loop(0, nb, attend, 0)

    k_s[:, :tt, :] = k_s[:, tt:, :]
    v_s[:, :tt, :] = v_s[:, tt:, :]

    def ssm_in(r, carry):
        up = jnp.dot(perm_ref[pl.ds(r * ROW_TILE, ROW_TILE), :], ub_s[...],
                     preferred_element_type=F32).astype(BF16)
        for j in range(nj):
            xj = jnp.dot(up[:, j * LANES:(j + 1) * LANES], bx_ref[j], preferred_element_type=F32)
            xr_s[pl.ds(r * ROW_TILE, ROW_TILE), j * gp:(j + 1) * gp] = xj[:, :gp]
            xi_s[pl.ds(r * ROW_TILE, ROW_TILE), j * gp:(j + 1) * gp] = xj[:, gp:]
        return carry

    lax.fori_loop(0, n_tiles, ssm_in, 0)

    lane_blk = 4 * LANES
    for lb in range(n_state // lane_blk):
        sl = slice(lb * lane_blk, (lb + 1) * lane_blk)
        a_re = jnp.broadcast_to(ar_ref[:, sl], (nb, lane_blk))
        a_im = jnp.broadcast_to(ai_ref[:, sl], (nb, lane_blk))

        def scan(t, h):
            h_re, h_im = h
            row = pl.multiple_of(t * nb, nb)
            n_re = a_re * h_re - a_im * h_im + xr_s[pl.ds(row, nb), sl]
            n_im = a_re * h_im + a_im * h_re + xi_s[pl.ds(row, nb), sl]
            xr_s[pl.ds(row, nb), sl] = n_re
            xi_s[pl.ds(row, nb), sl] = n_im
            return n_re, n_im

        h_re, h_im = lax.fori_loop(0, tt, scan, (hr_ref[:, sl], hi_ref[:, sl]), unroll=8)
        hr_ref[:, sl] = h_re
        hi_ref[:, sl] = h_im

    def ssm_out(r, carry):
        for j in range(nj):
            h_cat = jnp.concatenate(
                [xr_s[pl.ds(r * ROW_TILE, ROW_TILE), j * gp:(j + 1) * gp],
                 xi_s[pl.ds(r * ROW_TILE, ROW_TILE), j * gp:(j + 1) * gp]], axis=1).astype(BF16)
            y_s[j, pl.ds(r * ROW_TILE, ROW_TILE), :] = jnp.dot(h_cat, cm_ref[j], preferred_element_type=F32)
        return carry

    lax.fori_loop(0, n_tiles, ssm_out, 0)

    def readout(b, carry):
        y = jnp.concatenate([y_s[j, pl.ds(b, tt, stride=nb), :] for j in range(nj)], axis=1)
        u = u_s[pl.ds(pl.multiple_of(b * tt, tt), tt), :]
        ssm_ref[b] = _ssm_readout(y, u, d_ref, wglu_ref, bglu_ref).astype(BF16)
        return carry

    lax.fori_loop(0, nb, readout, 0)


def _mixer_prompt(layer, x, sinks, seg, perm, g1, w, qg, kg, ar, ai, bx, cm, d, wglu, bglu):
    nb, T, D = x.shape
    tt = WINDOW
    rows = nb * tt
    aw = N_HEADS * HEAD_DIM
    kw = N_KV_HEADS * HEAD_DIM
    sw = d.shape[-1]
    n_state = ar.shape[-1]
    nj = bx.shape[1]
    chunk = lambda last: pl.BlockSpec((nb, tt, last), lambda c: (0, c, 0))
    whole = lambda last: pl.BlockSpec((nb, tt, last), lambda c: (0, 0, 0))
    state = pl.BlockSpec((nb, n_state), lambda c: (0, 0))
    stacked = (g1, w, qg, kg, ar, ai, bx, cm, d, wglu, bglu)
    return pl.pallas_call(
        functools.partial(_mixer_prompt_kernel, layer),
        grid=(T // tt,),
        in_specs=[pl.BlockSpec(memory_space=pltpu.SMEM), chunk(D), _const_spec(seg), _const_spec(perm)]
                 + [_layer_spec(a, layer) for a in stacked],
        out_specs=[chunk(aw), chunk(sw), whole(kw), whole(kw), state, state],
        out_shape=[jax.ShapeDtypeStruct((nb, T, aw), BF16), jax.ShapeDtypeStruct((nb, T, sw), BF16),
                   jax.ShapeDtypeStruct((nb, tt, kw), F32), jax.ShapeDtypeStruct((nb, tt, kw), F32),
                   jax.ShapeDtypeStruct((nb, n_state), F32), jax.ShapeDtypeStruct((nb, n_state), F32)],
        scratch_shapes=[pltpu.VMEM((2, nb, tt, aw), BF16),
                        pltpu.VMEM((nb, 2 * tt, kw), BF16), pltpu.VMEM((nb, 2 * tt, kw), BF16),
                        pltpu.VMEM((rows, sw), F32), pltpu.VMEM((rows, sw), BF16),
                        pltpu.VMEM((rows, n_state), F32), pltpu.VMEM((rows, n_state), F32),
                        pltpu.VMEM((nj, rows, LANES), F32)],
        compiler_params=pltpu.CompilerParams(dimension_semantics=("arbitrary",), vmem_limit_bytes=VMEM_LIMIT),
        name="mixer_prompt",
    )(sinks, x, seg, perm, *stacked)


def _mixer_sample_kernel(layer, sink_ref, x_ref, ck_ref, cv_ref, h0r_ref, h0i_ref, seg_ref, g1_ref, w_ref, qg_ref,
                         kg_ref, ar_ref, ai_ref, bx_ref, cm_ref, d_ref, wglu_ref, bglu_ref,
                         attn_ref, ssm_ref, kcat_ref, vcat_ref, hr_ref, hi_ref,
                         q_s, o_s, u_s, xr_s, xi_s, knew_s, vnew_s):
    i = pl.program_id(0)
    rows = x_ref.shape[0]
    nbatch = h0r_ref.shape[0]
    ts = rows // nbatch
    past = ck_ref.shape[1]
    gb = ck_ref.shape[0]
    n_tiles = rows // ROW_TILE
    n_state = ar_ref.shape[-1]
    nj = bx_ref.shape[0]
    gp = n_state // nj

    @pl.when(i == 0)
    def _():
        def project(r, carry):
            rs = pl.ds(r * ROW_TILE, ROW_TILE)
            q_lo, q_hi, k, v, u = _project_rows(x_ref[rs, :], g1_ref, w_ref, seg_ref, qg_ref, kg_ref)
            q_s[0, rs, :] = q_lo
            q_s[1, rs, :] = q_hi
            knew_s[rs, :] = k
            vnew_s[rs, :] = v
            u_s[rs, :] = u
            return carry

        lax.fori_loop(0, n_tiles, project, 0)

    n_q = ts * Q_GROUP * gb
    n_keys = gb * past + ts * gb
    r_idx = lax.broadcasted_iota(jnp.int32, (n_q, n_keys), 0)
    c_idx = lax.broadcasted_iota(jnp.int32, (n_q, n_keys), 1)
    q_t = r_idx // (Q_GROUP * gb)
    q_b = r_idx % gb
    is_cache = c_idx < gb * past
    key_b = jnp.where(is_cache, c_idx // past, (c_idx - gb * past) % gb)
    dist = jnp.where(is_cache, past + q_t - c_idx % past, q_t - (c_idx - gb * past) // gb)
    valid = (key_b == q_b) & (dist >= 0) & (dist <= WINDOW)
    dist_f = dist.astype(F32)
    hh = (lax.broadcasted_iota(jnp.int32, (n_q, 1), 0) // gb) % Q_GROUP
    lower = lax.broadcasted_iota(jnp.int32, (gb, LANES), 1) < HEAD_DIM
    base = pl.multiple_of(i * gb, gb)

    for cat_ref, c_ref, new_s in ((kcat_ref, ck_ref, knew_s), (vcat_ref, cv_ref, vnew_s)):
        for b in range(gb):
            cat_ref[b * past:(b + 1) * past - ts, :] = c_ref[b, ts:, :]
        for t in range(ts):
            cat_ref[pl.ds(past - ts + t, gb, stride=past), :] = new_s[pl.ds(base + t * nbatch, gb), :]

    k_all = jnp.concatenate(
        [ck_ref[...].reshape(gb * past, -1)] + [knew_s[pl.ds(base + t * nbatch, gb), :] for t in range(ts)],
        axis=0).astype(BF16)
    v_all = jnp.concatenate(
        [cv_ref[...].reshape(gb * past, -1)] + [vnew_s[pl.ds(base + t * nbatch, gb), :] for t in range(ts)],
        axis=0).astype(BF16)
    outs = []
    for kvh in range(N_KV_HEADS):
        slope = jnp.zeros((n_q, 1), F32)
        sink = jnp.zeros((n_q, 1), F32)
        for g in range(Q_GROUP):
            h = kvh * Q_GROUP + g
            slope = jnp.where(hh == g, SLOPES[h], slope)
            sink = jnp.where(hh == g, sink_ref[layer, h], sink)
        qm = jnp.concatenate(
            [q_s[kvh, pl.ds(base + t * nbatch, gb), g * LANES:(g + 1) * LANES]
             for t in range(ts) for g in range(Q_GROUP)], axis=0).astype(BF16)
        s = lax.dot_general(qm, k_all, (((1,), (1,)), ((), ())), preferred_element_type=F32)
        s = jnp.where(valid, s - slope * dist_f, NEG_INF)
        outs.append(_softmax_pv(s, sink, v_all))
    for t in range(ts):
        for g in range(Q_GROUP):
            r0 = (t * Q_GROUP + g) * gb
            o_s[pl.ds(base + t * nbatch, gb), g * LANES:(g + 1) * LANES] = jnp.where(
                lower, outs[0][r0:r0 + gb], outs[1][r0:r0 + gb])

    @pl.when(i == pl.num_programs(0) - 1)
    def _():
        attn_ref[...] = o_s[...].astype(BF16)

        def ssm_in(r, carry):
            rs = pl.ds(r * ROW_TILE, ROW_TILE)
            ub = u_s[rs, :].astype(BF16)
            for j in range(nj):
                xj = jnp.dot(ub[:, j * LANES:(j + 1) * LANES], bx_ref[j], preferred_element_type=F32)
                xr_s[rs, j * gp:(j + 1) * gp] = xj[:, :gp]
                xi_s[rs, j * gp:(j + 1) * gp] = xj[:, gp:]
            return carry

        lax.fori_loop(0, n_tiles, ssm_in, 0)

        lane_blk = 4 * LANES
        for lb in range(n_state // lane_blk):
            sl = slice(lb * lane_blk, (lb + 1) * lane_blk)
            a_re = jnp.broadcast_to(ar_ref[:, sl], (SUBLANES, lane_blk))
            a_im = jnp.broadcast_to(ai_ref[:, sl], (SUBLANES, lane_blk))

            def scan(bt, carry):
                b0 = pl.multiple_of(bt * SUBLANES, SUBLANES)
                h_re = h0r_ref[pl.ds(b0, SUBLANES), sl]
                h_im = h0i_ref[pl.ds(b0, SUBLANES), sl]
                for t in range(ts):
                    rs = pl.ds(b0 + t * nbatch, SUBLANES)
                    n_re = a_re * h_re - a_im * h_im + xr_s[rs, sl]
                    n_im = a_re * h_im + a_im * h_re + xi_s[rs, sl]
                    xr_s[rs, sl] = n_re
                    xi_s[rs, sl] = n_im
                    h_re, h_im = n_re, n_im
                hr_ref[pl.ds(b0, SUBLANES), sl] = h_re
                hi_ref[pl.ds(b0, SUBLANES), sl] = h_im
                return carry

            lax.fori_loop(0, nbatch // SUBLANES, scan, 0)

        def ssm_out(r, carry):
            rs = pl.ds(r * ROW_TILE, ROW_TILE)
            ys = []
            for j in range(nj):
                h_cat = jnp.concatenate([xr_s[rs, j * gp:(j + 1) * gp], xi_s[rs, j * gp:(j + 1) * gp]],
                                        axis=1).astype(BF16)
                ys.append(jnp.dot(h_cat, cm_ref[j], preferred_element_type=F32))
            y = jnp.concatenate(ys, axis=1)
            ssm_ref[rs, :] = _ssm_readout(y, u_s[rs, :], d_ref, wglu_ref, bglu_ref).astype(BF16)
            return carry

        lax.fori_loop(0, n_tiles, ssm_out, 0)


def _mixer_sample(layer, x, ck, cv, h0r, h0i, sinks, seg, g1, w, qg, kg, ar, ai, bx, cm, d, wglu, bglu):
    rows, D = x.shape
    _, nbatch, past, kw = ck.shape
    aw = N_HEADS * HEAD_DIM
    sw = d.shape[-1]
    n_state = ar.shape[-1]
    gb = SUBLANES
    cache = pl.BlockSpec((None, gb, past, kw), lambda i: (layer, i, 0, 0))
    h0 = pl.BlockSpec((None, nbatch, n_state), lambda i: (layer, 0, 0))
    out2 = lambda last: pl.BlockSpec((rows, last), lambda i: (0, 0))
    cat = pl.BlockSpec((gb * past, kw), lambda i: (i, 0))
    state = pl.BlockSpec((nbatch, n_state), lambda i: (0, 0))
    stacked = (g1, w, qg, kg, ar, ai, bx, cm, d, wglu, bglu)
    return pl.pallas_call(
        functools.partial(_mixer_sample_kernel, layer),
        grid=(nbatch // gb,),
        in_specs=[pl.BlockSpec(memory_space=pltpu.SMEM), pl.BlockSpec(x.shape, lambda i: (0, 0)), cache, cache, h0, h0,
                  _const_spec(seg)] + [_layer_spec(a, layer) for a in stacked],
        out_specs=[out2(aw), out2(sw), cat, cat, state, state],
        out_shape=[jax.ShapeDtypeStruct((rows, aw), BF16), jax.ShapeDtypeStruct((rows, sw), BF16),
                   jax.ShapeDtypeStruct((nbatch * past, kw), F32), jax.ShapeDtypeStruct((nbatch * past, kw), F32),
                   jax.ShapeDtypeStruct((nbatch, n_state), F32), jax.ShapeDtypeStruct((nbatch, n_state), F32)],
        scratch_shapes=[pltpu.VMEM((2, rows, aw), F32), pltpu.VMEM((rows, aw), F32), pltpu.VMEM((rows, sw), F32),
                        pltpu.VMEM((rows, n_state), F32), pltpu.VMEM((rows, n_state), F32),
                        pltpu.VMEM((rows, kw), F32), pltpu.VMEM((rows, kw), F32)],
        compiler_params=pltpu.CompilerParams(dimension_semantics=("arbitrary",), vmem_limit_bytes=VMEM_LIMIT),
        name="mixer_sample",
    )(sinks, x, ck, cv, h0r, h0i, seg, *stacked)


def _mix_ffn_rows(x_ref, attn_ref, ssm_ref, o_ref, g1_ref, wg_ref, bg_ref, wao_ref, wso_ref, wout_ref,
                  g2_ref, wup_ref, wdn_ref):
    x = x_ref[...]
    D = x.shape[-1]
    xn = _rms(x, g1_ref[...]).astype(BF16)
    attn = attn_ref[...]
    ssm = ssm_ref[...]
    o_ref[...] = x
    cw = 2 * LANES
    for cc in range(D // cw):
        cs = slice(cc * cw, (cc + 1) * cw)
        gs = slice(D + cc * cw, D + (cc + 1) * cw)
        g_attn = jax.nn.sigmoid(jnp.dot(xn, wg_ref[:, cs], preferred_element_type=F32) + bg_ref[:, cs])
        g_ssm = jax.nn.sigmoid(jnp.dot(xn, wg_ref[:, gs], preferred_element_type=F32) + bg_ref[:, gs])
        mixed = (g_attn * jnp.dot(attn, wao_ref[:, cs], preferred_element_type=F32)
                 + g_ssm * jnp.dot(ssm, wso_ref[:, cs], preferred_element_type=F32))
        o_ref[...] += jnp.dot(mixed.astype(BF16), wout_ref[cs, :], preferred_element_type=F32)
    xn2 = _rms(o_ref[...], g2_ref[...]).astype(BF16)
    fw = 4 * LANES
    for fc in range(wup_ref.shape[-1] // fw):
        fs = slice(fc * fw, (fc + 1) * fw)
        hdn = jnp.maximum(jnp.dot(xn2, wup_ref[:, fs], preferred_element_type=F32), 0.0)
        o_ref[...] += jnp.dot((hdn * hdn).astype(BF16), wdn_ref[fs, :], preferred_element_type=F32)


def _mix_ffn_kernel(n_prompt, xp_ref, ap_ref, sp_ref, xs_ref, as_ref, ss_ref, *rest):
    w_refs, (op_ref, os_ref) = rest[:-2], rest[-2:]
    i = pl.program_id(0)

    @pl.when(i < n_prompt)
    def _():
        _mix_ffn_rows(xp_ref, ap_ref, sp_ref, op_ref, *w_refs)

    @pl.when(i >= n_prompt)
    def _():
        _mix_ffn_rows(xs_ref, as_ref, ss_ref, os_ref, *w_refs)


def _mix_ffn(layer, xp, attn_p, ssm_p, xs, attn_s, ssm_s, g1, wg, bg, wao, wso, wout, g2, wup, wdn):
    rp, D = xp.shape
    rs = xs.shape[0]
    tp = min(512, rp)
    ts = min(512, rs)
    assert rp % tp == 0 and rs % ts == 0
    n_p, n_s = rp // tp, rs // ts
    prow = lambda last: pl.BlockSpec((tp, last), lambda i: (jnp.minimum(i, n_p - 1), 0))
    srow = lambda last: pl.BlockSpec((ts, last), lambda i: (jnp.maximum(i - n_p, 0), 0))
    stacked = (g1, wg, bg, wao, wso, wout, g2, wup, wdn)
    return pl.pallas_call(
        functools.partial(_mix_ffn_kernel, n_p),
        grid=(n_p + n_s,),
        in_specs=[prow(D), prow(attn_p.shape[-1]), prow(ssm_p.shape[-1]),
                  srow(D), srow(attn_s.shape[-1]), srow(ssm_s.shape[-1])]
                 + [_layer_spec(a, layer) for a in stacked],
        out_specs=[prow(D), srow(D)],
        out_shape=[jax.ShapeDtypeStruct((rp, D), F32), jax.ShapeDtypeStruct((rs, D), F32)],
        compiler_params=pltpu.CompilerParams(dimension_semantics=("arbitrary",), vmem_limit_bytes=VMEM_LIMIT),
        name="mix_ffn",
    )(xp, attn_p, ssm_p, xs, attn_s, ssm_s, *stacked)


def _perm_matrix(nb, tt):
    p = np.zeros((nb * tt, nb * tt), np.float32)
    t, b = np.meshgrid(np.arange(tt), np.arange(nb), indexing="ij")
    p[(t * nb + b).ravel(), (b * tt + t).ravel()] = 1.0
    return jnp.asarray(p, BF16)


def _seg_matrix(width):
    i = np.arange(width) // HEAD_DIM
    return jnp.asarray((i[:, None] == i[None, :]).astype(np.float32) / HEAD_DIM, BF16)


def kernel(x_prompt, x_sample, cache_k, cache_v, state_ssm_re, state_ssm_im, norm1_g, w_in, b_gate, q_norm_g,
           k_norm_g, attn_sinks, lam_re, lam_im, log_step, b_re, b_im, c_re, c_im, d_skip, w_glu, b_glu,
           w_attn_o, w_ssm_o, w_out, norm2_g, w_up, w_down):
    B, T, D = x_prompt.shape
    DB, DS, _ = x_sample.shape
    L = w_in.shape[0]
    aw = N_HEADS * HEAD_DIM
    kw = N_KV_HEADS * HEAD_DIM
    sw = d_skip.shape[-1]
    G, P = lam_re.shape[1:]
    past = cache_k.shape[2]
    assert past == WINDOW and T % WINDOW == 0 and (B * WINDOW) % ROW_TILE == 0 and (DB * DS) % ROW_TILE == 0
    order = np.asarray(HEAD_ORDER)

    ar, ai, bx, cm = _ssm_prep(lam_re, lam_im, log_step, b_re, b_im, c_re, c_im)
    perm = _perm_matrix(B, WINDOW)
    seg = _seg_matrix(aw)
    rows = lambda a: a.reshape(L, 1, -1)
    wq = w_in[:, :, :aw].reshape(L, D, N_HEADS, HEAD_DIM)[:, :, order].reshape(L, D, aw)
    w_qkvu = jnp.concatenate([wq, w_in[:, :, aw:aw + 2 * kw + sw]], axis=2).astype(BF16)
    wg = w_in[:, :, aw + 2 * kw + sw:].astype(BF16)
    wao = w_attn_o.reshape(L, N_HEADS, HEAD_DIM, D)[:, order].reshape(L, aw, D).astype(BF16)
    g1 = rows(norm1_g)
    mixer_w = (g1, w_qkvu, rows(jnp.tile(q_norm_g, (1, N_HEADS))), rows(jnp.tile(k_norm_g, (1, N_KV_HEADS))),
               ar, ai, bx, cm, rows(d_skip), w_glu.astype(BF16), rows(b_glu))
    ffn_w = (g1, wg, rows(b_gate), wao, w_ssm_o.astype(BF16), w_out.astype(BF16), rows(norm2_g),
             w_up.astype(BF16), w_down.astype(BF16))
    ck = cache_k.reshape(L, DB, past, kw)
    cv = cache_v.reshape(L, DB, past, kw)
    h0r = state_ssm_re.reshape(L, DB, G * P)
    h0i = state_ssm_im.reshape(L, DB, G * P)

    yp = x_prompt.reshape(B * T, D)
    ys = x_sample.transpose(1, 0, 2).reshape(DS * DB, D)
    outs = [[] for _ in range(8)]
    for l in range(L):
        attn_p, ssm_p, kp, vp, hrp, hip = _mixer_prompt(l, yp.reshape(B, T, D), attn_sinks, seg, perm, *mixer_w)
        attn_s, ssm_s, kcat, vcat, hrs, his = _mixer_sample(l, ys, ck, cv, h0r, h0i, attn_sinks, seg, *mixer_w)
        yp, ys = _mix_ffn(l, yp, attn_p.reshape(B * T, aw), ssm_p.reshape(B * T, sw), ys, attn_s, ssm_s, *ffn_w)
        vals = (kp.reshape(B, WINDOW, N_KV_HEADS, HEAD_DIM), vp.reshape(B, WINDOW, N_KV_HEADS, HEAD_DIM),
                hrp.reshape(B, G, P), hip.reshape(B, G, P),
                kcat.reshape(DB, past, N_KV_HEADS, HEAD_DIM), vcat.reshape(DB, past, N_KV_HEADS, HEAD_DIM),
                hrs.reshape(DB, G, P), his.reshape(DB, G, P))
        for o, v in zip(outs, vals):
            o.append(v)

    y_sample = ys.reshape(DS, DB, D).transpose(1, 0, 2)
    return (yp.reshape(B, T, D), y_sample) + tuple(jnp.stack(o) for o in outs)
```

```python
import functools
import math

import numpy as np
import jax
import jax.numpy as jnp
from jax import lax
from jax.experimental import pallas as pl
from jax.experimental.pallas import tpu as pltpu

F32 = jnp.float32
BF16 = jnp.bfloat16

HEAD_DIM = 64
N_HEADS = 8
N_KV_HEADS = 2
Q_GROUP = N_HEADS // N_KV_HEADS
WINDOW = 128
GROUP_CH = 16
SSM_STATE = 64
EPS = 1e-6
NEG_INF = -1e30
LANES = 128
SUBLANES = 8
ROW_TILE = 256
VMEM_LIMIT = 56 * 1024 * 1024

WEIGHT_CHUNK_ROWS = 256
SLOPES = tuple(2.0 ** (-8.0 * (h + 1) / N_HEADS) for h in range(N_HEADS))


def _rms(x, g):
    return x * lax.rsqrt(jnp.mean(x * x, axis=-1, keepdims=True) + EPS) * g


def _seg_rms(x, seg, g):
    ms = jnp.dot((x * x).astype(BF16), seg, preferred_element_type=F32)
    return x * lax.rsqrt(ms + EPS) * g


def _gelu_tanh(y):
    return 0.5 * y * (1.0 + jnp.tanh(math.sqrt(2.0 / math.pi) * (y + 0.044715 * (y * y * y))))


def _const_spec(a):
    n = a.ndim
    return pl.BlockSpec(a.shape, lambda *_: (0,) * n, pipeline_mode=pl.Buffered(1))


def _layer_spec(a, layer):
    n = a.ndim - 1
    return pl.BlockSpec((None,) + a.shape[1:], lambda *_: (layer,) + (0,) * n, pipeline_mode=pl.Buffered(1))


def _stage_weights(chunks, stage, sem):
    def copy(k):
        src, _, cols = chunks[k]
        return pltpu.make_async_copy(src, stage.at[k % 2, :, pl.ds(0, cols)], sem.at[k % 2])

    copy(0).start()
    for k, (_, dst, cols) in enumerate(chunks):
        if k + 1 < len(chunks):
            copy(k + 1).start()
        copy(k).wait()
        dst[...] = stage[k % 2, :, :cols].astype(BF16)


def _weight_chunks(src, dst, rows, cols, chunk_rows, chunk_cols, src_col0=0):
    return [(src.at[pl.ds(r, chunk_rows), pl.ds(src_col0 + c, chunk_cols)],
             dst.at[pl.ds(r, chunk_rows), pl.ds(c, chunk_cols)], chunk_cols)
            for r in range(0, rows, chunk_rows) for c in range(0, cols, chunk_cols)]


def _ssm_prep_kernel(lr_ref, li_ref, ls_ref, btr_ref, bti_ref, ctr_ref, cti_ref,
                     ar_ref, ai_ref, bx_ref, cm_ref):
    lr = lr_ref[0]
    li = li_ref[0]
    step = jnp.exp(ls_ref[0])
    mag = jnp.exp(lr * step)
    ar = mag * jnp.cos(li * step)
    ai = mag * jnp.sin(li * step)
    den = lr * lr + li * li
    cr = ((ar - 1.0) * lr + ai * li) / den
    ci = (ai * lr - (ar - 1.0) * li) / den
    ar_ref[0] = ar
    ai_ref[0] = ai
    nj = btr_ref.shape[1]
    gp = SUBLANES * SSM_STATE
    rows_b = lax.broadcasted_iota(jnp.int32, (LANES, gp), 0) // GROUP_CH
    cols_b = lax.broadcasted_iota(jnp.int32, (LANES, gp), 1) // SSM_STATE
    mask_b = rows_b == cols_b
    rows_c = lax.broadcasted_iota(jnp.int32, (gp, LANES), 0) // SSM_STATE
    cols_c = lax.broadcasted_iota(jnp.int32, (gp, LANES), 1) // GROUP_CH
    mask_c = rows_c == cols_c
    for j in range(nj):
        crj = cr[:, j * gp:(j + 1) * gp]
        cij = ci[:, j * gp:(j + 1) * gp]
        b_re = btr_ref[0, j]
        b_im = bti_ref[0, j]
        bb_re = jnp.where(mask_b, crj * b_re - cij * b_im, 0.0)
        bb_im = jnp.where(mask_b, crj * b_im + cij * b_re, 0.0)
        bx_ref[0, j, :, :gp] = bb_re.astype(BF16)
        bx_ref[0, j, :, gp:] = bb_im.astype(BF16)
        cm_ref[0, j, :gp, :] = jnp.where(mask_c, ctr_ref[0, j], 0.0).astype(BF16)
        cm_ref[0, j, gp:, :] = jnp.where(mask_c, -cti_ref[0, j], 0.0).astype(BF16)


def _ssm_prep(lam_re, lam_im, log_step, b_re, b_im, c_re, c_im):
    L, G, P = lam_re.shape
    GC = b_re.shape[-1]
    gpb = LANES // GC
    nj = G // gpb
    gp = gpb * P
    flat = lambda a: a.reshape(L, 1, G * P)
    ls = jnp.repeat(log_step, P, axis=-1).reshape(L, 1, G * P)

    def b_tiles(b):
        t = b.reshape(L, nj, gpb, P, GC).transpose(0, 1, 4, 2, 3).reshape(L, nj, 1, GC, gp)
        return jnp.broadcast_to(t, (L, nj, gpb, GC, gp)).reshape(L, nj, LANES, gp)

    def c_tiles(c):
        t = c.reshape(L, nj, gpb, GC, P).transpose(0, 1, 2, 4, 3).reshape(L, nj, gp, 1, GC)
        return jnp.broadcast_to(t, (L, nj, gp, gpb, GC)).reshape(L, nj, gp, LANES)

    vec = pl.BlockSpec((1, 1, G * P), lambda l: (l, 0, 0))
    bt = pl.BlockSpec((1, nj, LANES, gp), lambda l: (l, 0, 0, 0))
    ct = pl.BlockSpec((1, nj, gp, LANES), lambda l: (l, 0, 0, 0))
    return pl.pallas_call(
        _ssm_prep_kernel,
        grid=(L,),
        in_specs=[vec, vec, vec, bt, bt, ct, ct],
        out_specs=[vec, vec,
                   pl.BlockSpec((1, nj, LANES, 2 * gp), lambda l: (l, 0, 0, 0)),
                   pl.BlockSpec((1, nj, 2 * gp, LANES), lambda l: (l, 0, 0, 0))],
        out_shape=[jax.ShapeDtypeStruct((L, 1, G * P), F32), jax.ShapeDtypeStruct((L, 1, G * P), F32),
                   jax.ShapeDtypeStruct((L, nj, LANES, 2 * gp), BF16),
                   jax.ShapeDtypeStruct((L, nj, 2 * gp, LANES), BF16)],
        name="ssm_prep",
    )(flat(lam_re), flat(lam_im), ls, b_tiles(b_re), b_tiles(b_im), c_tiles(c_re), c_tiles(c_im))


def _project_rows(x, g1_ref, w_ref, seg_ref, qg_ref, kg_ref):
    aw = N_HEADS * HEAD_DIM
    kw = N_KV_HEADS * HEAD_DIM
    xn = _rms(x, g1_ref[...]).astype(BF16)
    z = jnp.dot(xn, w_ref[...], preferred_element_type=F32)
    q = _seg_rms(z[:, :aw], seg_ref[...], qg_ref[...]) * (HEAD_DIM ** -0.5)
    k = _seg_rms(z[:, aw:aw + kw], seg_ref[:kw, :kw], kg_ref[...])
    v = z[:, aw + kw:aw + 2 * kw]
    u = z[:, aw + 2 * kw:]
    lower = (lax.broadcasted_iota(jnp.int32, q.shape, 1) % LANES) < HEAD_DIM
    q_lo = jnp.where(lower, q, 0.0)
    q_hi = jnp.where(lower, 0.0, q)
    return q_lo, q_hi, k, v, u


def _stage_mixer_weights(layer, w_in_hbm, wglu_hbm, w_ref, wglu_ref, stage, sem):
    _stage_weights(
        _weight_chunks(w_in_hbm.at[layer], w_ref, w_ref.shape[0], w_ref.shape[1], WEIGHT_CHUNK_ROWS, w_ref.shape[1])
        + _weight_chunks(wglu_hbm.at[layer], wglu_ref, wglu_ref.shape[0], wglu_ref.shape[1], WEIGHT_CHUNK_ROWS,
                         wglu_ref.shape[1]),
        stage, sem)


def _mixer_weight_scratch(D, in_cols, sw):
    return [pltpu.VMEM((D, in_cols), BF16), pltpu.VMEM((sw, sw), BF16),
            pltpu.VMEM((2, WEIGHT_CHUNK_ROWS, in_cols), F32), pltpu.SemaphoreType.DMA((2,))]


def _ssm_readout(y, u, d_ref, wglu_ref, bglu_ref):
    y = _gelu_tanh(y + d_ref[...] * u)
    gate = jnp.dot(y.astype(BF16), wglu_ref[...], preferred_element_type=F32) + bglu_ref[...]
    return y * jax.nn.sigmoid(gate)


def _softmax_pv(s, sink, v):
    m = jnp.maximum(jnp.max(s, axis=-1, keepdims=True), sink)
    e = jnp.exp(s - m)
    den = jnp.sum(e, axis=-1, keepdims=True) + jnp.exp(sink - m)
    return jnp.dot(e.astype(BF16), v, preferred_element_type=F32) * (1.0 / den)


def _mixer_prompt_kernel(layer, sink_ref, x_ref, seg_ref, perm_ref, w_in_hbm, wglu_hbm, g1_ref, qg_ref, kg_ref,
                         ar_ref, ai_ref, bx_ref, cm_ref, d_ref, bglu_ref,
                         attn_ref, ssm_ref, kout_ref, vout_ref, hr_ref, hi_ref,
                         w_ref, wglu_ref, stage, sem, q_s, k_s, v_s, u_s, ub_s, xr_s, xi_s, y_s):
    c = pl.program_id(0)
    nb, tt, _ = x_ref.shape
    rows = nb * tt
    n_tiles = rows // ROW_TILE
    bpt = ROW_TILE // tt
    n_state = ar_ref.shape[-1]
    nj = bx_ref.shape[0]
    gp = n_state // nj
    kw = kout_ref.shape[-1]

    @pl.when(c == 0)
    def _():
        _stage_mixer_weights(layer, w_in_hbm, wglu_hbm, w_ref, wglu_ref, stage, sem)
        k_s[:, :, :tt, :] = jnp.zeros((2, nb, tt, kw), BF16)
        v_s[:, :, :tt, :] = jnp.zeros((2, nb, tt, kw), BF16)
        hr_ref[...] = jnp.zeros(hr_ref.shape, F32)
        hi_ref[...] = jnp.zeros(hi_ref.shape, F32)

    def project(r, carry):
        x = x_ref[pl.ds(r * bpt, bpt)].reshape(ROW_TILE, x_ref.shape[-1])
        q_lo, q_hi, k, v, u = _project_rows(x, g1_ref, w_ref, seg_ref, qg_ref, kg_ref)
        q_s[0, pl.ds(r * bpt, bpt)] = q_lo.astype(BF16).reshape(bpt, tt, -1)
        q_s[1, pl.ds(r * bpt, bpt)] = q_hi.astype(BF16).reshape(bpt, tt, -1)
        k_s[0, pl.ds(r * bpt, bpt), tt:, :] = k.astype(BF16).reshape(bpt, tt, -1)
        v_s[0, pl.ds(r * bpt, bpt), tt:, :] = v.astype(BF16).reshape(bpt, tt, -1)
        k_s[1, pl.ds(r * bpt, bpt), tt:, :] = pltpu.roll(k, HEAD_DIM, 1).astype(BF16).reshape(bpt, tt, -1)
        v_s[1, pl.ds(r * bpt, bpt), tt:, :] = pltpu.roll(v, HEAD_DIM, 1).astype(BF16).reshape(bpt, tt, -1)
        kout_ref[pl.ds(r * bpt, bpt)] = k.reshape(bpt, tt, -1)
        vout_ref[pl.ds(r * bpt, bpt)] = v.reshape(bpt, tt, -1)
        u_s[pl.ds(r * ROW_TILE, ROW_TILE), :] = u
        ub_s[pl.ds(r * ROW_TILE, ROW_TILE), :] = u.astype(BF16)
        return carry

    lax.fori_loop(0, n_tiles, project, 0)

    t_idx = lax.broadcasted_iota(jnp.int32, (tt, 2 * tt), 0)
    j_idx = lax.broadcasted_iota(jnp.int32, (tt, 2 * tt), 1)
    dist = t_idx - j_idx + tt
    valid = (dist >= 0) & (dist <= WINDOW) & (j_idx >= jnp.where(c > 0, 0, tt))
    dist_f = dist.astype(F32)
    lower = lax.broadcasted_iota(jnp.int32, (tt, LANES), 1) < HEAD_DIM

    def attend(b, carry):
        for p in range(N_HEADS // 2):
            kv = (2 * p) // Q_GROUP
            outs = []
            for e in range(2):
                h = 2 * p + e
                swap = 0 if kv == e else 1
                qm = q_s[e, b, :, p * LANES:(p + 1) * LANES]
                s = lax.dot_general(qm, k_s[swap, b], (((1,), (1,)), ((), ())), preferred_element_type=F32)
                s = jnp.where(valid, s - SLOPES[h] * dist_f, NEG_INF)
                outs.append(_softmax_pv(s, sink_ref[layer, h], v_s[swap, b]))
            attn_ref[b, :, p * LANES:(p + 1) * LANES] = jnp.where(lower, outs[0], outs[1]).astype(BF16)
        return carry

    lax.fori_loop(0, nb, attend, 0)

    k_s[:, :, :tt, :] = k_s[:, :, tt:, :]
    v_s[:, :, :tt, :] = v_s[:, :, tt:, :]

    def ssm_in(r, carry):
        up = jnp.dot(perm_ref[pl.ds(r * ROW_TILE, ROW_TILE), :], ub_s[...],
                     preferred_element_type=F32).astype(BF16)
        for j in range(nj):
            xj = jnp.dot(up[:, j * LANES:(j + 1) * LANES], bx_ref[j], preferred_element_type=F32)
            xr_s[pl.ds(r * ROW_TILE, ROW_TILE), j * gp:(j + 1) * gp] = xj[:, :gp]
            xi_s[pl.ds(r * ROW_TILE, ROW_TILE), j * gp:(j + 1) * gp] = xj[:, gp:]
        return carry

    lax.fori_loop(0, n_tiles, ssm_in, 0)

    lane_blk = 4 * LANES
    for lb in range(n_state // lane_blk):
        sl = slice(lb * lane_blk, (lb + 1) * lane_blk)
        a_re = jnp.broadcast_to(ar_ref[:, sl], (nb, lane_blk))
        a_im = jnp.broadcast_to(ai_ref[:, sl], (nb, lane_blk))

        def scan(t, h):
            h_re, h_im = h
            row = pl.multiple_of(t * nb, nb)
            n_re = a_re * h_re - a_im * h_im + xr_s[pl.ds(row, nb), sl]
            n_im = a_re * h_im + a_im * h_re + xi_s[pl.ds(row, nb), sl]
            xr_s[pl.ds(row, nb), sl] = n_re
            xi_s[pl.ds(row, nb), sl] = n_im
            return n_re, n_im

        h_re, h_im = lax.fori_loop(0, tt, scan, (hr_ref[:, sl], hi_ref[:, sl]), unroll=8)
        hr_ref[:, sl] = h_re
        hi_ref[:, sl] = h_im

    def ssm_out(r, carry):
        for j in range(nj):
            h_cat = jnp.concatenate(
                [xr_s[pl.ds(r * ROW_TILE, ROW_TILE), j * gp:(j + 1) * gp],
                 xi_s[pl.ds(r * ROW_TILE, ROW_TILE), j * gp:(j + 1) * gp]], axis=1).astype(BF16)
            y_s[j, pl.ds(r * ROW_TILE, ROW_TILE), :] = jnp.dot(h_cat, cm_ref[j], preferred_element_type=F32)
        return carry

    lax.fori_loop(0, n_tiles, ssm_out, 0)

    def readout(b, carry):
        y = jnp.concatenate([y_s[j, pl.ds(b, tt, stride=nb), :] for j in range(nj)], axis=1)
        u = u_s[pl.ds(pl.multiple_of(b * tt, tt), tt), :]
        ssm_ref[b] = _ssm_readout(y, u, d_ref, wglu_ref, bglu_ref).astype(BF16)
        return carry

    lax.fori_loop(0, nb, readout, 0)


def _mixer_prompt(layer, x, sinks, seg, perm, w_in, wglu, g1, qg, kg, ar, ai, bx, cm, d, bglu):
    nb, T, D = x.shape
    tt = WINDOW
    rows = nb * tt
    aw = N_HEADS * HEAD_DIM
    kw = N_KV_HEADS * HEAD_DIM
    sw = d.shape[-1]
    n_state = ar.shape[-1]
    nj = bx.shape[1]
    chunk = lambda last: pl.BlockSpec((nb, tt, last), lambda c: (0, c, 0))
    whole = lambda last: pl.BlockSpec((nb, tt, last), lambda c: (0, 0, 0))
    state = pl.BlockSpec((nb, n_state), lambda c: (0, 0))
    hbm = pl.BlockSpec(memory_space=pl.ANY)
    stacked = (g1, qg, kg, ar, ai, bx, cm, d, bglu)
    return pl.pallas_call(
        functools.partial(_mixer_prompt_kernel, layer),
        grid=(T // tt,),
        in_specs=[pl.BlockSpec(memory_space=pltpu.SMEM), chunk(D), _const_spec(seg), _const_spec(perm), hbm, hbm]
                 + [_layer_spec(a, layer) for a in stacked],
        out_specs=[chunk(aw), chunk(sw), whole(kw), whole(kw), state, state],
        out_shape=[jax.ShapeDtypeStruct((nb, T, aw), BF16), jax.ShapeDtypeStruct((nb, T, sw), BF16),
                   jax.ShapeDtypeStruct((nb, tt, kw), F32), jax.ShapeDtypeStruct((nb, tt, kw), F32),
                   jax.ShapeDtypeStruct((nb, n_state), F32), jax.ShapeDtypeStruct((nb, n_state), F32)],
        scratch_shapes=_mixer_weight_scratch(D, aw + 2 * kw + sw, sw)
                       + [pltpu.VMEM((2, nb, tt, aw), BF16),
                          pltpu.VMEM((2, nb, 2 * tt, kw), BF16), pltpu.VMEM((2, nb, 2 * tt, kw), BF16),
                          pltpu.VMEM((rows, sw), F32), pltpu.VMEM((rows, sw), BF16),
                          pltpu.VMEM((rows, n_state), F32), pltpu.VMEM((rows, n_state), F32),
                          pltpu.VMEM((nj, rows, LANES), F32)],
        compiler_params=pltpu.CompilerParams(dimension_semantics=("arbitrary",), vmem_limit_bytes=VMEM_LIMIT),
        name="mixer_prompt",
    )(sinks, x, seg, perm, w_in, wglu, *stacked)


def _mixer_sample_kernel(layer, sink_ref, x_ref, ck_ref, cv_ref, h0r_ref, h0i_ref, seg_ref, w_in_hbm, wglu_hbm,
                         g1_ref, qg_ref, kg_ref, ar_ref, ai_ref, bx_ref, cm_ref, d_ref, bglu_ref,
                         attn_ref, ssm_ref, kcat_ref, vcat_ref, hr_ref, hi_ref,
                         w_ref, wglu_ref, stage, sem, q_s, o_s, u_s, xr_s, xi_s, knew_s, vnew_s):
    i = pl.program_id(0)
    rows = x_ref.shape[0]
    nbatch = h0r_ref.shape[0]
    ts = rows // nbatch
    past = ck_ref.shape[1]
    gb = ck_ref.shape[0]
    n_tiles = rows // ROW_TILE
    n_state = ar_ref.shape[-1]
    nj = bx_ref.shape[0]
    gp = n_state // nj

    @pl.when(i == 0)
    def _():
        _stage_mixer_weights(layer, w_in_hbm, wglu_hbm, w_ref, wglu_ref, stage, sem)

        def project(r, carry):
            rs = pl.ds(r * ROW_TILE, ROW_TILE)
            q_lo, q_hi, k, v, u = _project_rows(x_ref[rs, :], g1_ref, w_ref, seg_ref, qg_ref, kg_ref)
            q_s[0, rs, :] = q_lo
            q_s[1, rs, :] = q_hi
            knew_s[rs, :] = k
            vnew_s[rs, :] = v
            u_s[rs, :] = u
            return carry

        lax.fori_loop(0, n_tiles, project, 0)

    n_q = ts * Q_GROUP * gb
    n_keys = gb * past + ts * gb
    r_idx = lax.broadcasted_iota(jnp.int32, (n_q, n_keys), 0)
    c_idx = lax.broadcasted_iota(jnp.int32, (n_q, n_keys), 1)
    q_t = r_idx // (Q_GROUP * gb)
    q_b = r_idx % gb
    is_cache = c_idx < gb * past
    key_b = jnp.where(is_cache, c_idx // past, (c_idx - gb * past) % gb)
    dist = jnp.where(is_cache, past + q_t - c_idx % past, q_t - (c_idx - gb * past) // gb)
    valid = (key_b == q_b) & (dist >= 0) & (dist <= WINDOW)
    dist_f = dist.astype(F32)
    hh = (lax.broadcasted_iota(jnp.int32, (n_q, 1), 0) // gb) % Q_GROUP
    lower = lax.broadcasted_iota(jnp.int32, (gb, LANES), 1) < HEAD_DIM
    base = pl.multiple_of(i * gb, gb)

    for cat_ref, c_ref, new_s in ((kcat_ref, ck_ref, knew_s), (vcat_ref, cv_ref, vnew_s)):
        for b in range(gb):
            cat_ref[b * past:(b + 1) * past - ts, :] = c_ref[b, ts:, :]
        for t in range(ts):
            cat_ref[pl.ds(past - ts + t, gb, stride=past), :] = new_s[pl.ds(base + t * nbatch, gb), :]

    k_all = jnp.concatenate(
        [ck_ref[...].reshape(gb * past, -1)] + [knew_s[pl.ds(base + t * nbatch, gb), :] for t in range(ts)],
        axis=0).astype(BF16)
    v_all = jnp.concatenate(
        [cv_ref[...].reshape(gb * past, -1)] + [vnew_s[pl.ds(base + t * nbatch, gb), :] for t in range(ts)],
        axis=0).astype(BF16)
    outs = []
    for kvh in range(N_KV_HEADS):
        slope = jnp.zeros((n_q, 1), F32)
        sink = jnp.zeros((n_q, 1), F32)
        for g in range(Q_GROUP):
            h = kvh * Q_GROUP + g
            slope = jnp.where(hh == g, SLOPES[h], slope)
            sink = jnp.where(hh == g, sink_ref[layer, h], sink)
        pieces = []
        for t in range(ts):
            for g in range(Q_GROUP):
                h = kvh * Q_GROUP + g
                piece = q_s[h % 2, pl.ds(base + t * nbatch, gb), (h // 2) * LANES:(h // 2 + 1) * LANES]
                pieces.append(piece if h % 2 == kvh else pltpu.roll(piece, HEAD_DIM, 1))
        qm = jnp.concatenate(pieces, axis=0).astype(BF16)
        s = lax.dot_general(qm, k_all, (((1,), (1,)), ((), ())), preferred_element_type=F32)
        s = jnp.where(valid, s - slope * dist_f, NEG_INF)
        outs.append(_softmax_pv(s, sink, v_all))
    for t in range(ts):
        for p in range(N_HEADS // 2):
            kvh = (2 * p) // Q_GROUP
            r0 = (t * Q_GROUP + 2 * p - kvh * Q_GROUP) * gb
            lo = outs[kvh][r0:r0 + gb]
            hi = outs[kvh][r0 + gb:r0 + 2 * gb]
            if kvh == 1:
                lo = pltpu.roll(lo, HEAD_DIM, 1)
            else:
                hi = pltpu.roll(hi, HEAD_DIM, 1)
            o_s[pl.ds(base + t * nbatch, gb), p * LANES:(p + 1) * LANES] = jnp.where(lower, lo, hi)

    @pl.when(i == pl.num_programs(0) - 1)
    def _():
        attn_ref[...] = o_s[...].astype(BF16)

        def ssm_in(r, carry):
            rs = pl.ds(r * ROW_TILE, ROW_TILE)
            ub = u_s[rs, :].astype(BF16)
            for j in range(nj):
                xj = jnp.dot(ub[:, j * LANES:(j + 1) * LANES], bx_ref[j], preferred_element_type=F32)
                xr_s[rs, j * gp:(j + 1) * gp] = xj[:, :gp]
                xi_s[rs, j * gp:(j + 1) * gp] = xj[:, gp:]
            return carry

        lax.fori_loop(0, n_tiles, ssm_in, 0)

        lane_blk = 4 * LANES
        for lb in range(n_state // lane_blk):
            sl = slice(lb * lane_blk, (lb + 1) * lane_blk)
            a_re = jnp.broadcast_to(ar_ref[:, sl], (SUBLANES, lane_blk))
            a_im = jnp.broadcast_to(ai_ref[:, sl], (SUBLANES, lane_blk))

            def scan(bt, carry):
                b0 = pl.multiple_of(bt * SUBLANES, SUBLANES)
                h_re = h0r_ref[pl.ds(b0, SUBLANES), sl]
                h_im = h0i_ref[pl.ds(b0, SUBLANES), sl]
                for t in range(ts):
                    rs = pl.ds(b0 + t * nbatch, SUBLANES)
                    n_re = a_re * h_re - a_im * h_im + xr_s[rs, sl]
                    n_im = a_re * h_im + a_im * h_re + xi_s[rs, sl]
                    xr_s[rs, sl] = n_re
                    xi_s[rs, sl] = n_im
                    h_re, h_im = n_re, n_im
                hr_ref[pl.ds(b0, SUBLANES), sl] = h_re
                hi_ref[pl.ds(b0, SUBLANES), sl] = h_im
                return carry

            lax.fori_loop(0, nbatch // SUBLANES, scan, 0)

        def ssm_out(r, carry):
            rs = pl.ds(r * ROW_TILE, ROW_TILE)
            ys = []
            for j in range(nj):
                h_cat = jnp.concatenate([xr_s[rs, j * gp:(j + 1) * gp], xi_s[rs, j * gp:(j + 1) * gp]],
                                        axis=1).astype(BF16)
                ys.append(jnp.dot(h_cat, cm_ref[j], preferred_element_type=F32))
            y = jnp.concatenate(ys, axis=1)
            ssm_ref[rs, :] = _ssm_readout(y, u_s[rs, :], d_ref, wglu_ref, bglu_ref).astype(BF16)
            return carry

        lax.fori_loop(0, n_tiles, ssm_out, 0)


def _mixer_sample(layer, x, ck, cv, h0r, h0i, sinks, seg, w_in, wglu, g1, qg, kg, ar, ai, bx, cm, d, bglu):
    rows, D = x.shape
    _, nbatch, past, kw = ck.shape
    aw = N_HEADS * HEAD_DIM
    sw = d.shape[-1]
    n_state = ar.shape[-1]
    gb = SUBLANES
    cache = pl.BlockSpec((None, gb, past, kw), lambda i: (layer, i, 0, 0))
    h0 = pl.BlockSpec((None, nbatch, n_state), lambda i: (layer, 0, 0))
    out2 = lambda last: pl.BlockSpec((rows, last), lambda i: (0, 0))
    cat = pl.BlockSpec((gb * past, kw), lambda i: (i, 0))
    state = pl.BlockSpec((nbatch, n_state), lambda i: (0, 0))
    hbm = pl.BlockSpec(memory_space=pl.ANY)
    stacked = (g1, qg, kg, ar, ai, bx, cm, d, bglu)
    return pl.pallas_call(
        functools.partial(_mixer_sample_kernel, layer),
        grid=(nbatch // gb,),
        in_specs=[pl.BlockSpec(memory_space=pltpu.SMEM), pl.BlockSpec(x.shape, lambda i: (0, 0)), cache, cache, h0, h0,
                  _const_spec(seg), hbm, hbm] + [_layer_spec(a, layer) for a in stacked],
        out_specs=[out2(aw), out2(sw), cat, cat, state, state],
        out_shape=[jax.ShapeDtypeStruct((rows, aw), BF16), jax.ShapeDtypeStruct((rows, sw), BF16),
                   jax.ShapeDtypeStruct((nbatch * past, kw), F32), jax.ShapeDtypeStruct((nbatch * past, kw), F32),
                   jax.ShapeDtypeStruct((nbatch, n_state), F32), jax.ShapeDtypeStruct((nbatch, n_state), F32)],
        scratch_shapes=_mixer_weight_scratch(D, aw + 2 * kw + sw, sw)
                       + [pltpu.VMEM((2, rows, aw), F32), pltpu.VMEM((rows, aw), F32), pltpu.VMEM((rows, sw), F32),
                          pltpu.VMEM((rows, n_state), F32), pltpu.VMEM((rows, n_state), F32),
                          pltpu.VMEM((rows, kw), F32), pltpu.VMEM((rows, kw), F32)],
        compiler_params=pltpu.CompilerParams(dimension_semantics=("arbitrary",), vmem_limit_bytes=VMEM_LIMIT),
        name="mixer_sample",
    )(sinks, x, ck, cv, h0r, h0i, seg, w_in, wglu, *stacked)


def _mix_ffn_rows(x_ref, attn_ref, ssm_ref, o_ref, g1_ref, wg_ref, bg_ref, wao_ref, wso_ref, wout_ref,
                  g2_ref, wup_ref, wdn_ref):
    x = x_ref[...]
    D = x.shape[-1]
    xn = _rms(x, g1_ref[...]).astype(BF16)
    attn = attn_ref[...]
    ssm = ssm_ref[...]
    o_ref[...] = x
    cw = 2 * LANES
    for cc in range(D // cw):
        cs = slice(cc * cw, (cc + 1) * cw)
        gs = slice(D + cc * cw, D + (cc + 1) * cw)
        g_attn = jax.nn.sigmoid(jnp.dot(xn, wg_ref[:, cs], preferred_element_type=F32) + bg_ref[:, cs])
        g_ssm = jax.nn.sigmoid(jnp.dot(xn, wg_ref[:, gs], preferred_element_type=F32) + bg_ref[:, gs])
        mixed = (g_attn * jnp.dot(attn, wao_ref[:, cs], preferred_element_type=F32)
                 + g_ssm * jnp.dot(ssm, wso_ref[:, cs], preferred_element_type=F32))
        o_ref[...] += jnp.dot(mixed.astype(BF16), wout_ref[cs, :], preferred_element_type=F32)
    xn2 = _rms(o_ref[...], g2_ref[...]).astype(BF16)
    fw = 4 * LANES
    for fc in range(wup_ref.shape[-1] // fw):
        fs = slice(fc * fw, (fc + 1) * fw)
        hdn = jnp.maximum(jnp.dot(xn2, wup_ref[:, fs], preferred_element_type=F32), 0.0)
        o_ref[...] += jnp.dot((hdn * hdn).astype(BF16), wdn_ref[fs, :], preferred_element_type=F32)


def _mix_ffn_kernel(layer, n_prompt, xp_ref, ap_ref, sp_ref, xs_ref, as_ref, ss_ref,
                    w_in_hbm, wao_hbm, wso_hbm, wout_hbm, wup_hbm, wdn_hbm, g1_ref, bg_ref, g2_ref,
                    op_ref, os_ref, wg_ref, wao_ref, wso_ref, wout_ref, wup_ref, wdn_ref, stage, sem):
    i = pl.program_id(0)
    cr, cc = stage.shape[1:]

    @pl.when(i == 0)
    def _():
        chunks = []
        chunks += _weight_chunks(w_in_hbm.at[layer], wg_ref, wg_ref.shape[0], wg_ref.shape[1], cr, cc,
                                 src_col0=w_in_hbm.shape[-1] - wg_ref.shape[1])
        for hbm, ref in ((wao_hbm, wao_ref), (wso_hbm, wso_ref), (wout_hbm, wout_ref), (wup_hbm, wup_ref),
                         (wdn_hbm, wdn_ref)):
            chunks += _weight_chunks(hbm.at[layer], ref, ref.shape[0], ref.shape[1], cr, cc)
        _stage_weights(chunks, stage, sem)

    w_refs = (g1_ref, wg_ref, bg_ref, wao_ref, wso_ref, wout_ref, g2_ref, wup_ref, wdn_ref)

    @pl.when(i < n_prompt)
    def _():
        _mix_ffn_rows(xp_ref, ap_ref, sp_ref, op_ref, *w_refs)

    @pl.when(i >= n_prompt)
    def _():
        _mix_ffn_rows(xs_ref, as_ref, ss_ref, os_ref, *w_refs)


def _mix_ffn(layer, xp, attn_p, ssm_p, xs, attn_s, ssm_s, w_in, wao, wso, wout, wup, wdn, g1, bg, g2):
    rp, D = xp.shape
    rs = xs.shape[0]
    tp = min(512, rp)
    ts = min(512, rs)
    assert rp % tp == 0 and rs % ts == 0
    n_p, n_s = rp // tp, rs // ts
    prow = lambda last: pl.BlockSpec((tp, last), lambda i: (jnp.minimum(i, n_p - 1), 0))
    srow = lambda last: pl.BlockSpec((ts, last), lambda i: (jnp.maximum(i - n_p, 0), 0))
    hbm = pl.BlockSpec(memory_space=pl.ANY)
    resident = lambda a: pltpu.VMEM(a.shape[1:], BF16)
    return pl.pallas_call(
        functools.partial(_mix_ffn_kernel, layer, n_p),
        grid=(n_p + n_s,),
        in_specs=[prow(D), prow(attn_p.shape[-1]), prow(ssm_p.shape[-1]),
                  srow(D), srow(attn_s.shape[-1]), srow(ssm_s.shape[-1])]
                 + [hbm] * 6 + [_layer_spec(a, layer) for a in (g1, bg, g2)],
        out_specs=[prow(D), srow(D)],
        out_shape=[jax.ShapeDtypeStruct((rp, D), F32), jax.ShapeDtypeStruct((rs, D), F32)],
        scratch_shapes=[pltpu.VMEM((D, bg.shape[-1]), BF16), resident(wao), resident(wso), resident(wout),
                        resident(wup), resident(wdn),
                        pltpu.VMEM((2, WEIGHT_CHUNK_ROWS, D), F32), pltpu.SemaphoreType.DMA((2,))],
        compiler_params=pltpu.CompilerParams(dimension_semantics=("arbitrary",), vmem_limit_bytes=VMEM_LIMIT),
        name="mix_ffn",
    )(xp, attn_p, ssm_p, xs, attn_s, ssm_s, w_in, wao, wso, wout, wup, wdn, g1, bg, g2)


def _perm_matrix(nb, tt):
    p = np.zeros((nb * tt, nb * tt), np.float32)
    t, b = np.meshgrid(np.arange(tt), np.arange(nb), indexing="ij")
    p[(t * nb + b).ravel(), (b * tt + t).ravel()] = 1.0
    return jnp.asarray(p, BF16)


def _seg_matrix(width):
    i = np.arange(width) // HEAD_DIM
    return jnp.asarray((i[:, None] == i[None, :]).astype(np.float32) / HEAD_DIM, BF16)


def kernel(x_prompt, x_sample, cache_k, cache_v, state_ssm_re, state_ssm_im, norm1_g, w_in, b_gate, q_norm_g,
           k_norm_g, attn_sinks, lam_re, lam_im, log_step, b_re, b_im, c_re, c_im, d_skip, w_glu, b_glu,
           w_attn_o, w_ssm_o, w_out, norm2_g, w_up, w_down):
    B, T, D = x_prompt.shape
    DB, DS, _ = x_sample.shape
    L = w_in.shape[0]
    aw = N_HEADS * HEAD_DIM
    kw = N_KV_HEADS * HEAD_DIM
    sw = d_skip.shape[-1]
    G, P = lam_re.shape[1:]
    past = cache_k.shape[2]
    assert past == WINDOW and T % WINDOW == 0 and (B * WINDOW) % ROW_TILE == 0 and (DB * DS) % ROW_TILE == 0

    ar, ai, bx, cm = _ssm_prep(lam_re, lam_im, log_step, b_re, b_im, c_re, c_im)
    perm = _perm_matrix(B, WINDOW)
    seg = _seg_matrix(aw)
    rows = lambda a: a.reshape(L, 1, -1)
    g1 = rows(norm1_g)
    mixer_w = (w_in, w_glu, g1, rows(jnp.tile(q_norm_g, (1, N_HEADS))), rows(jnp.tile(k_norm_g, (1, N_KV_HEADS))),
               ar, ai, bx, cm, rows(d_skip), rows(b_glu))
    ffn_w = (w_in, w_attn_o, w_ssm_o, w_out, w_up, w_down, g1, rows(b_gate), rows(norm2_g))
    ck = cache_k.reshape(L, DB, past, kw)
    cv = cache_v.reshape(L, DB, past, kw)
    h0r = state_ssm_re.reshape(L, DB, G * P)
    h0i = state_ssm_im.reshape(L, DB, G * P)

    yp = x_prompt.reshape(B * T, D)
    ys = x_sample.transpose(1, 0, 2).reshape(DS * DB, D)
    outs = [[] for _ in range(8)]
    for l in range(L):
        attn_p, ssm_p, kp, vp, hrp, hip = _mixer_prompt(l, yp.reshape(B, T, D), attn_sinks, seg, perm, *mixer_w)
        attn_s, ssm_s, kcat, vcat, hrs, his = _mixer_sample(l, ys, ck, cv, h0r, h0i, attn_sinks, seg, *mixer_w)
        yp, ys = _mix_ffn(l, yp, attn_p.reshape(B * T, aw), ssm_p.reshape(B * T, sw), ys, attn_s, ssm_s, *ffn_w)
        vals = (kp.reshape(B, WINDOW, N_KV_HEADS, HEAD_DIM), vp.reshape(B, WINDOW, N_KV_HEADS, HEAD_DIM),
                hrp.reshape(B, G, P), hip.reshape(B, G, P),
                kcat.reshape(DB, past, N_KV_HEADS, HEAD_DIM), vcat.reshape(DB, past, N_KV_HEADS, HEAD_DIM),
                hrs.reshape(DB, G, P), his.reshape(DB, G, P))
        for o, v in zip(outs, vals):
            o.append(v)

    y_sample = ys.reshape(DS, DB, D).transpose(1, 0, 2)
    return (yp.reshape(B, T, D), y_sample) + tuple(jnp.stack(o) for o in outs)
```

```python
import functools
import math

import numpy as np
import jax
import jax.numpy as jnp
from jax import lax
from jax.experimental import pallas as pl
from jax.experimental.pallas import tpu as pltpu

F32 = jnp.float32
BF16 = jnp.bfloat16

HEAD_DIM = 64
N_HEADS = 8
N_KV_HEADS = 2
Q_GROUP = N_HEADS // N_KV_HEADS
WINDOW = 128
GROUP_CH = 16
SSM_STATE = 64
EPS = 1e-6
NEG_INF = -1e30
LANES = 128
SUBLANES = 8
ROW_TILE = 256
VMEM_LIMIT = 56 * 1024 * 1024

WEIGHT_CHUNK_ROWS = 256
SLOPES = tuple(2.0 ** (-8.0 * (h + 1) / N_HEADS) for h in range(N_HEADS))


def _rms(x, g):
    return x * lax.rsqrt(jnp.mean(x * x, axis=-1, keepdims=True) + EPS) * g


def _seg_rms(x, seg, g):
    ms = jnp.dot((x * x).astype(BF16), seg, preferred_element_type=F32)
    return x * lax.rsqrt(ms + EPS) * g


def _gelu_tanh(y):
    return 0.5 * y * (1.0 + jnp.tanh(math.sqrt(2.0 / math.pi) * (y + 0.044715 * (y * y * y))))


def _const_spec(a):
    n = a.ndim
    return pl.BlockSpec(a.shape, lambda *_: (0,) * n, pipeline_mode=pl.Buffered(1))


def _layer_spec(a, layer):
    n = a.ndim - 1
    return pl.BlockSpec((None,) + a.shape[1:], lambda *_: (layer,) + (0,) * n, pipeline_mode=pl.Buffered(1))


def _stage_weights(chunks, stage, sem):
    def copy(k):
        src, _, cols = chunks[k]
        return pltpu.make_async_copy(src, stage.at[k % 2, :, pl.ds(0, cols)], sem.at[k % 2])

    copy(0).start()
    for k, (_, dst, cols) in enumerate(chunks):
        if k + 1 < len(chunks):
            copy(k + 1).start()
        copy(k).wait()
        dst[...] = stage[k % 2, :, :cols].astype(BF16)


def _weight_chunks(src, dst, rows, cols, chunk_rows, chunk_cols, src_col0=0):
    return [(src.at[pl.ds(r, chunk_rows), pl.ds(src_col0 + c, chunk_cols)],
             dst.at[pl.ds(r, chunk_rows), pl.ds(c, chunk_cols)], chunk_cols)
            for r in range(0, rows, chunk_rows) for c in range(0, cols, chunk_cols)]


def _ssm_prep_kernel(lr_ref, li_ref, ls_ref, btr_ref, bti_ref, ctr_ref, cti_ref,
                     ar_ref, ai_ref, bx_ref, cm_ref):
    lr = lr_ref[0]
    li = li_ref[0]
    step = jnp.exp(ls_ref[0])
    mag = jnp.exp(lr * step)
    ar = mag * jnp.cos(li * step)
    ai = mag * jnp.sin(li * step)
    den = lr * lr + li * li
    cr = ((ar - 1.0) * lr + ai * li) / den
    ci = (ai * lr - (ar - 1.0) * li) / den
    ar_ref[0] = ar
    ai_ref[0] = ai
    nj = btr_ref.shape[1]
    gp = SUBLANES * SSM_STATE
    rows_b = lax.broadcasted_iota(jnp.int32, (LANES, gp), 0) // GROUP_CH
    cols_b = lax.broadcasted_iota(jnp.int32, (LANES, gp), 1) // SSM_STATE
    mask_b = rows_b == cols_b
    rows_c = lax.broadcasted_iota(jnp.int32, (gp, LANES), 0) // SSM_STATE
    cols_c = lax.broadcasted_iota(jnp.int32, (gp, LANES), 1) // GROUP_CH
    mask_c = rows_c == cols_c
    for j in range(nj):
        crj = cr[:, j * gp:(j + 1) * gp]
        cij = ci[:, j * gp:(j + 1) * gp]
        b_re = btr_ref[0, j]
        b_im = bti_ref[0, j]
        bb_re = jnp.where(mask_b, crj * b_re - cij * b_im, 0.0)
        bb_im = jnp.where(mask_b, crj * b_im + cij * b_re, 0.0)
        bx_ref[0, j, :, :gp] = bb_re.astype(BF16)
        bx_ref[0, j, :, gp:] = bb_im.astype(BF16)
        cm_ref[0, j, :gp, :] = jnp.where(mask_c, ctr_ref[0, j], 0.0).astype(BF16)
        cm_ref[0, j, gp:, :] = jnp.where(mask_c, -cti_ref[0, j], 0.0).astype(BF16)


def _ssm_prep(lam_re, lam_im, log_step, b_re, b_im, c_re, c_im):
    L, G, P = lam_re.shape
    GC = b_re.shape[-1]
    gpb = LANES // GC
    nj = G // gpb
    gp = gpb * P
    flat = lambda a: a.reshape(L, 1, G * P)
    ls = jnp.repeat(log_step, P, axis=-1).reshape(L, 1, G * P)

    def b_tiles(b):
        t = b.reshape(L, nj, gpb, P, GC).transpose(0, 1, 4, 2, 3).reshape(L, nj, 1, GC, gp)
        return jnp.broadcast_to(t, (L, nj, gpb, GC, gp)).reshape(L, nj, LANES, gp)

    def c_tiles(c):
        t = c.reshape(L, nj, gpb, GC, P).transpose(0, 1, 2, 4, 3).reshape(L, nj, gp, 1, GC)
        return jnp.broadcast_to(t, (L, nj, gp, gpb, GC)).reshape(L, nj, gp, LANES)

    vec = pl.BlockSpec((1, 1, G * P), lambda l: (l, 0, 0))
    bt = pl.BlockSpec((1, nj, LANES, gp), lambda l: (l, 0, 0, 0))
    ct = pl.BlockSpec((1, nj, gp, LANES), lambda l: (l, 0, 0, 0))
    return pl.pallas_call(
        _ssm_prep_kernel,
        grid=(L,),
        in_specs=[vec, vec, vec, bt, bt, ct, ct],
        out_specs=[vec, vec,
                   pl.BlockSpec((1, nj, LANES, 2 * gp), lambda l: (l, 0, 0, 0)),
                   pl.BlockSpec((1, nj, 2 * gp, LANES), lambda l: (l, 0, 0, 0))],
        out_shape=[jax.ShapeDtypeStruct((L, 1, G * P), F32), jax.ShapeDtypeStruct((L, 1, G * P), F32),
                   jax.ShapeDtypeStruct((L, nj, LANES, 2 * gp), BF16),
                   jax.ShapeDtypeStruct((L, nj, 2 * gp, LANES), BF16)],
        name="ssm_prep",
    )(flat(lam_re), flat(lam_im), ls, b_tiles(b_re), b_tiles(b_im), c_tiles(c_re), c_tiles(c_im))


def _project_rows(x, g1_ref, w_ref, seg_ref, qg_ref, kg_ref):
    aw = N_HEADS * HEAD_DIM
    kw = N_KV_HEADS * HEAD_DIM
    xn = _rms(x, g1_ref[...]).astype(BF16)
    z = jnp.dot(xn, w_ref[...], preferred_element_type=F32)
    q = _seg_rms(z[:, :aw], seg_ref[...], qg_ref[...]) * (HEAD_DIM ** -0.5)
    k = _seg_rms(z[:, aw:aw + kw], seg_ref[:kw, :kw], kg_ref[...])
    v = z[:, aw + kw:aw + 2 * kw]
    u = z[:, aw + 2 * kw:]
    lower = (lax.broadcasted_iota(jnp.int32, q.shape, 1) % LANES) < HEAD_DIM
    q_lo = jnp.where(lower, q, 0.0)
    q_hi = jnp.where(lower, 0.0, q)
    return q_lo, q_hi, k, v, u


def _stage_mixer_weights(layer, w_in_hbm, wglu_hbm, w_ref, wglu_ref, stage, sem):
    _stage_weights(
        _weight_chunks(w_in_hbm.at[layer], w_ref, w_ref.shape[0], w_ref.shape[1], WEIGHT_CHUNK_ROWS, w_ref.shape[1])
        + _weight_chunks(wglu_hbm.at[layer], wglu_ref, wglu_ref.shape[0], wglu_ref.shape[1], WEIGHT_CHUNK_ROWS,
                         wglu_ref.shape[1]),
        stage, sem)


def _mixer_weight_scratch(D, in_cols, sw):
    return [pltpu.VMEM((D, in_cols), BF16), pltpu.VMEM((sw, sw), BF16),
            pltpu.VMEM((2, WEIGHT_CHUNK_ROWS, in_cols), F32), pltpu.SemaphoreType.DMA((2,))]


def _ssm_readout(y, u, d_ref, wglu_ref, bglu_ref):
    y = _gelu_tanh(y + d_ref[...] * u)
    gate = jnp.dot(y.astype(BF16), wglu_ref[...], preferred_element_type=F32) + bglu_ref[...]
    return y * jax.nn.sigmoid(gate)


def _softmax_pv(s, sink, v):
    m = jnp.maximum(jnp.max(s, axis=-1, keepdims=True), sink)
    e = jnp.exp(s - m)
    den = jnp.sum(e, axis=-1, keepdims=True) + jnp.exp(sink - m)
    return jnp.dot(e.astype(BF16), v, preferred_element_type=F32) * (1.0 / den)


def _mixer_prompt_kernel(layer, sink_ref, x_ref, seg_ref, perm_ref, w_in_hbm, wglu_hbm, g1_ref, qg_ref, kg_ref,
                         ar_ref, ai_ref, bx_ref, cm_ref, d_ref, bglu_ref,
                         attn_ref, ssm_ref, kout_ref, vout_ref, hr_ref, hi_ref,
                         w_ref, wglu_ref, stage, sem, q_s, k_s, v_s, u_s, ub_s, y_s, *x_s):
    c = pl.program_id(0)
    nb, tt, _ = x_ref.shape
    rows = nb * tt
    n_tiles = rows // ROW_TILE
    bpt = ROW_TILE // tt
    n_state = ar_ref.shape[-1]
    nj = bx_ref.shape[0]
    gp = n_state // nj
    xr_s, xi_s = x_s[:nj], x_s[nj:]
    kw = kout_ref.shape[-1]

    @pl.when(c == 0)
    def _():
        _stage_mixer_weights(layer, w_in_hbm, wglu_hbm, w_ref, wglu_ref, stage, sem)
        k_s[:, :, :tt, :] = jnp.zeros((2, nb, tt, kw), BF16)
        v_s[:, :, :tt, :] = jnp.zeros((2, nb, tt, kw), BF16)
        hr_ref[...] = jnp.zeros(hr_ref.shape, F32)
        hi_ref[...] = jnp.zeros(hi_ref.shape, F32)

    def project(r, carry):
        x = x_ref[pl.ds(r * bpt, bpt)].reshape(ROW_TILE, x_ref.shape[-1])
        q_lo, q_hi, k, v, u = _project_rows(x, g1_ref, w_ref, seg_ref, qg_ref, kg_ref)
        q_s[0, pl.ds(r * bpt, bpt)] = q_lo.astype(BF16).reshape(bpt, tt, -1)
        q_s[1, pl.ds(r * bpt, bpt)] = q_hi.astype(BF16).reshape(bpt, tt, -1)
        k_s[0, pl.ds(r * bpt, bpt), tt:, :] = k.astype(BF16).reshape(bpt, tt, -1)
        v_s[0, pl.ds(r * bpt, bpt), tt:, :] = v.astype(BF16).reshape(bpt, tt, -1)
        k_s[1, pl.ds(r * bpt, bpt), tt:, :] = pltpu.roll(k, HEAD_DIM, 1).astype(BF16).reshape(bpt, tt, -1)
        v_s[1, pl.ds(r * bpt, bpt), tt:, :] = pltpu.roll(v, HEAD_DIM, 1).astype(BF16).reshape(bpt, tt, -1)
        kout_ref[pl.ds(r * bpt, bpt)] = k.reshape(bpt, tt, -1)
        vout_ref[pl.ds(r * bpt, bpt)] = v.reshape(bpt, tt, -1)
        u_s[pl.ds(r * ROW_TILE, ROW_TILE), :] = u
        ub_s[pl.ds(r * ROW_TILE, ROW_TILE), :] = u.astype(BF16)
        return carry

    lax.fori_loop(0, n_tiles, project, 0)

    t_idx = lax.broadcasted_iota(jnp.int32, (tt, 2 * tt), 0)
    j_idx = lax.broadcasted_iota(jnp.int32, (tt, 2 * tt), 1)
    dist = t_idx - j_idx + tt
    valid = (dist >= 0) & (dist <= WINDOW) & (j_idx >= jnp.where(c > 0, 0, tt))
    dist_f = dist.astype(F32)
    lower = lax.broadcasted_iota(jnp.int32, (tt, LANES), 1) < HEAD_DIM

    def attend(b):
        for p in range(N_HEADS // 2):
            kv = (2 * p) // Q_GROUP
            outs = []
            for e in range(2):
                h = 2 * p + e
                swap = 0 if kv == e else 1
                qm = q_s[e, b, :, p * LANES:(p + 1) * LANES]
                s = lax.dot_general(qm, k_s[swap, b], (((1,), (1,)), ((), ())), preferred_element_type=F32)
                s = jnp.where(valid, s - SLOPES[h] * dist_f, NEG_INF)
                outs.append(_softmax_pv(s, sink_ref[layer, h], v_s[swap, b]))
            attn_ref[b, :, p * LANES:(p + 1) * LANES] = jnp.where(lower, outs[0], outs[1]).astype(BF16)

    def ssm_in_and_attend(r, carry):
        rs = pl.ds(r * ROW_TILE, ROW_TILE)
        up = jnp.dot(perm_ref[rs, :], ub_s[...], preferred_element_type=F32).astype(BF16)
        for j in range(nj):
            xj = jnp.dot(up[:, j * LANES:(j + 1) * LANES], bx_ref[j], preferred_element_type=F32)
            xr_s[j][rs, :] = xj[:, :gp]
            xi_s[j][rs, :] = xj[:, gp:]
        for bb in range(bpt):
            attend(r * bpt + bb)
        return carry

    lax.fori_loop(0, n_tiles, ssm_in_and_attend, 0)

    k_s[:, :, :tt, :] = k_s[:, :, tt:, :]
    v_s[:, :, :tt, :] = v_s[:, :, tt:, :]

    def ssm_out_tile(j, r):
        rs = pl.ds(r * ROW_TILE, ROW_TILE)
        h_cat = jnp.concatenate([xr_s[j][rs, :], xi_s[j][rs, :]], axis=1).astype(BF16)
        y_s[j, rs, :] = jnp.dot(h_cat, cm_ref[j], preferred_element_type=F32)

    for lb in range(nj):
        sl = slice(lb * gp, (lb + 1) * gp)
        a_re = jnp.broadcast_to(ar_ref[:, sl], (nb, gp))
        a_im = jnp.broadcast_to(ai_ref[:, sl], (nb, gp))

        def scan_tile(r, h, lb=lb, a_re=a_re, a_im=a_im):
            if lb > 0:
                ssm_out_tile(lb - 1, r)
            h_re, h_im = h
            for step in range(ROW_TILE // nb):
                row = pl.ds(pl.multiple_of(r * ROW_TILE + step * nb, nb), nb)
                n_re = a_re * h_re - a_im * h_im + xr_s[lb][row, :]
                n_im = a_re * h_im + a_im * h_re + xi_s[lb][row, :]
                xr_s[lb][row, :] = n_re
                xi_s[lb][row, :] = n_im
                h_re, h_im = n_re, n_im
            return h_re, h_im

        h_re, h_im = lax.fori_loop(0, n_tiles, scan_tile, (hr_ref[:, sl], hi_ref[:, sl]))
        hr_ref[:, sl] = h_re
        hi_ref[:, sl] = h_im

    def ssm_out_last(r, carry):
        ssm_out_tile(nj - 1, r)
        return carry

    lax.fori_loop(0, n_tiles, ssm_out_last, 0)

    def readout(b, carry):
        y = jnp.concatenate([y_s[j, pl.ds(b, tt, stride=nb), :] for j in range(nj)], axis=1)
        u = u_s[pl.ds(pl.multiple_of(b * tt, tt), tt), :]
        ssm_ref[b] = _ssm_readout(y, u, d_ref, wglu_ref, bglu_ref).astype(BF16)
        return carry

    lax.fori_loop(0, nb, readout, 0)


def _mixer_prompt(layer, x, sinks, seg, perm, w_in, wglu, g1, qg, kg, ar, ai, bx, cm, d, bglu):
    nb, T, D = x.shape
    tt = WINDOW
    rows = nb * tt
    aw = N_HEADS * HEAD_DIM
    kw = N_KV_HEADS * HEAD_DIM
    sw = d.shape[-1]
    n_state = ar.shape[-1]
    nj = bx.shape[1]
    chunk = lambda last: pl.BlockSpec((nb, tt, last), lambda c: (0, c, 0))
    whole = lambda last: pl.BlockSpec((nb, tt, last), lambda c: (0, 0, 0))
    state = pl.BlockSpec((nb, n_state), lambda c: (0, 0))
    hbm = pl.BlockSpec(memory_space=pl.ANY)
    stacked = (g1, qg, kg, ar, ai, bx, cm, d, bglu)
    return pl.pallas_call(
        functools.partial(_mixer_prompt_kernel, layer),
        grid=(T // tt,),
        in_specs=[pl.BlockSpec(memory_space=pltpu.SMEM), chunk(D), _const_spec(seg), _const_spec(perm), hbm, hbm]
                 + [_layer_spec(a, layer) for a in stacked],
        out_specs=[chunk(aw), chunk(sw), whole(kw), whole(kw), state, state],
        out_shape=[jax.ShapeDtypeStruct((nb, T, aw), BF16), jax.ShapeDtypeStruct((nb, T, sw), BF16),
                   jax.ShapeDtypeStruct((nb, tt, kw), F32), jax.ShapeDtypeStruct((nb, tt, kw), F32),
                   jax.ShapeDtypeStruct((nb, n_state), F32), jax.ShapeDtypeStruct((nb, n_state), F32)],
        scratch_shapes=_mixer_weight_scratch(D, aw + 2 * kw + sw, sw)
                       + [pltpu.VMEM((2, nb, tt, aw), BF16),
                          pltpu.VMEM((2, nb, 2 * tt, kw), BF16), pltpu.VMEM((2, nb, 2 * tt, kw), BF16),
                          pltpu.VMEM((rows, sw), F32), pltpu.VMEM((rows, sw), BF16),
                          pltpu.VMEM((nj, rows, LANES), F32)]
                       + [pltpu.VMEM((rows, n_state // nj), F32)] * (2 * nj),
        compiler_params=pltpu.CompilerParams(dimension_semantics=("arbitrary",), vmem_limit_bytes=VMEM_LIMIT),
        name="mixer_prompt",
    )(sinks, x, seg, perm, w_in, wglu, *stacked)


def _mixer_sample_kernel(layer, sink_ref, x_ref, ck_ref, cv_ref, h0r_ref, h0i_ref, seg_ref, w_in_hbm, wglu_hbm,
                         g1_ref, qg_ref, kg_ref, ar_ref, ai_ref, bx_ref, cm_ref, d_ref, bglu_ref,
                         attn_ref, ssm_ref, kcat_ref, vcat_ref, hr_ref, hi_ref,
                         w_ref, wglu_ref, stage, sem, q_s, o_s, u_s, xr_s, xi_s, knew_s, vnew_s):
    i = pl.program_id(0)
    rows = x_ref.shape[0]
    nbatch = h0r_ref.shape[0]
    ts = rows // nbatch
    past = ck_ref.shape[1]
    gb = ck_ref.shape[0]
    n_tiles = rows // ROW_TILE
    n_state = ar_ref.shape[-1]
    nj = bx_ref.shape[0]
    gp = n_state // nj

    @pl.when(i == 0)
    def _():
        _stage_mixer_weights(layer, w_in_hbm, wglu_hbm, w_ref, wglu_ref, stage, sem)

        def project(r, carry):
            rs = pl.ds(r * ROW_TILE, ROW_TILE)
            q_lo, q_hi, k, v, u = _project_rows(x_ref[rs, :], g1_ref, w_ref, seg_ref, qg_ref, kg_ref)
            q_s[0, rs, :] = q_lo
            q_s[1, rs, :] = q_hi
            knew_s[rs, :] = k
            vnew_s[rs, :] = v
            u_s[rs, :] = u
            return carry

        lax.fori_loop(0, n_tiles, project, 0)

    n_q = ts * Q_GROUP * gb
    n_keys = gb * past + ts * gb
    r_idx = lax.broadcasted_iota(jnp.int32, (n_q, n_keys), 0)
    c_idx = lax.broadcasted_iota(jnp.int32, (n_q, n_keys), 1)
    q_t = r_idx // (Q_GROUP * gb)
    q_b = r_idx % gb
    is_cache = c_idx < gb * past
    key_b = jnp.where(is_cache, c_idx // past, (c_idx - gb * past) % gb)
    dist = jnp.where(is_cache, past + q_t - c_idx % past, q_t - (c_idx - gb * past) // gb)
    valid = (key_b == q_b) & (dist >= 0) & (dist <= WINDOW)
    dist_f = dist.astype(F32)
    hh = (lax.broadcasted_iota(jnp.int32, (n_q, 1), 0) // gb) % Q_GROUP
    lower = lax.broadcasted_iota(jnp.int32, (gb, LANES), 1) < HEAD_DIM
    base = pl.multiple_of(i * gb, gb)

    for cat_ref, c_ref, new_s in ((kcat_ref, ck_ref, knew_s), (vcat_ref, cv_ref, vnew_s)):
        for b in range(gb):
            cat_ref[b * past:(b + 1) * past - ts, :] = c_ref[b, ts:, :]
        for t in range(ts):
            cat_ref[pl.ds(past - ts + t, gb, stride=past), :] = new_s[pl.ds(base + t * nbatch, gb), :]

    k_all = jnp.concatenate(
        [ck_ref[...].reshape(gb * past, -1)] + [knew_s[pl.ds(base + t * nbatch, gb), :] for t in range(ts)],
        axis=0).astype(BF16)
    v_all = jnp.concatenate(
        [cv_ref[...].reshape(gb * past, -1)] + [vnew_s[pl.ds(base + t * nbatch, gb), :] for t in range(ts)],
        axis=0).astype(BF16)
    outs = []
    for kvh in range(N_KV_HEADS):
        slope = jnp.zeros((n_q, 1), F32)
        sink = jnp.zeros((n_q, 1), F32)
        for g in range(Q_GROUP):
            h = kvh * Q_GROUP + g
            slope = jnp.where(hh == g, SLOPES[h], slope)
            sink = jnp.where(hh == g, sink_ref[layer, h], sink)
        pieces = []
        for t in range(ts):
            for g in range(Q_GROUP):
                h = kvh * Q_GROUP + g
                piece = q_s[h % 2, pl.ds(base + t * nbatch, gb), (h // 2) * LANES:(h // 2 + 1) * LANES]
                pieces.append(piece if h % 2 == kvh else pltpu.roll(piece, HEAD_DIM, 1))
        qm = jnp.concatenate(pieces, axis=0).astype(BF16)
        s = lax.dot_general(qm, k_all, (((1,), (1,)), ((), ())), preferred_element_type=F32)
        s = jnp.where(valid, s - slope * dist_f, NEG_INF)
        outs.append(_softmax_pv(s, sink, v_all))
    for t in range(ts):
        for p in range(N_HEADS // 2):
            kvh = (2 * p) // Q_GROUP
            r0 = (t * Q_GROUP + 2 * p - kvh * Q_GROUP) * gb
            lo = outs[kvh][r0:r0 + gb]
            hi = outs[kvh][r0 + gb:r0 + 2 * gb]
            if kvh == 1:
                lo = pltpu.roll(lo, HEAD_DIM, 1)
            else:
                hi = pltpu.roll(hi, HEAD_DIM, 1)
            o_s[pl.ds(base + t * nbatch, gb), p * LANES:(p + 1) * LANES] = jnp.where(lower, lo, hi)

    @pl.when(i == pl.num_programs(0) - 1)
    def _():
        attn_ref[...] = o_s[...].astype(BF16)

        def ssm_in(r, carry):
            rs = pl.ds(r * ROW_TILE, ROW_TILE)
            ub = u_s[rs, :].astype(BF16)
            for j in range(nj):
                xj = jnp.dot(ub[:, j * LANES:(j + 1) * LANES], bx_ref[j], preferred_element_type=F32)
                xr_s[rs, j * gp:(j + 1) * gp] = xj[:, :gp]
                xi_s[rs, j * gp:(j + 1) * gp] = xj[:, gp:]
            return carry

        lax.fori_loop(0, n_tiles, ssm_in, 0)

        lane_blk = 4 * LANES
        for lb in range(n_state // lane_blk):
            sl = slice(lb * lane_blk, (lb + 1) * lane_blk)
            a_re = jnp.broadcast_to(ar_ref[:, sl], (SUBLANES, lane_blk))
            a_im = jnp.broadcast_to(ai_ref[:, sl], (SUBLANES, lane_blk))

            def scan(bt, carry):
                b0 = pl.multiple_of(bt * SUBLANES, SUBLANES)
                h_re = h0r_ref[pl.ds(b0, SUBLANES), sl]
                h_im = h0i_ref[pl.ds(b0, SUBLANES), sl]
                for t in range(ts):
                    rs = pl.ds(b0 + t * nbatch, SUBLANES)
                    n_re = a_re * h_re - a_im * h_im + xr_s[rs, sl]
                    n_im = a_re * h_im + a_im * h_re + xi_s[rs, sl]
                    xr_s[rs, sl] = n_re
                    xi_s[rs, sl] = n_im
                    h_re, h_im = n_re, n_im
                hr_ref[pl.ds(b0, SUBLANES), sl] = h_re
                hi_ref[pl.ds(b0, SUBLANES), sl] = h_im
                return carry

            lax.fori_loop(0, nbatch // SUBLANES, scan, 0)

        def ssm_out(r, carry):
            rs = pl.ds(r * ROW_TILE, ROW_TILE)
            ys = []
            for j in range(nj):
                h_cat = jnp.concatenate([xr_s[rs, j * gp:(j + 1) * gp], xi_s[rs, j * gp:(j + 1) * gp]],
                                        axis=1).astype(BF16)
                ys.append(jnp.dot(h_cat, cm_ref[j], preferred_element_type=F32))
            y = jnp.concatenate(ys, axis=1)
            ssm_ref[rs, :] = _ssm_readout(y, u_s[rs, :], d_ref, wglu_ref, bglu_ref).astype(BF16)
            return carry

        lax.fori_loop(0, n_tiles, ssm_out, 0)


def _mixer_sample(layer, x, ck, cv, h0r, h0i, sinks, seg, w_in, wglu, g1, qg, kg, ar, ai, bx, cm, d, bglu):
    rows, D = x.shape
    _, nbatch, past, kw = ck.shape
    aw = N_HEADS * HEAD_DIM
    sw = d.shape[-1]
    n_state = ar.shape[-1]
    gb = SUBLANES
    cache = pl.BlockSpec((None, gb, past, kw), lambda i: (layer, i, 0, 0))
    h0 = pl.BlockSpec((None, nbatch, n_state), lambda i: (layer, 0, 0))
    out2 = lambda last: pl.BlockSpec((rows, last), lambda i: (0, 0))
    cat = pl.BlockSpec((gb * past, kw), lambda i: (i, 0))
    state = pl.BlockSpec((nbatch, n_state), lambda i: (0, 0))
    hbm = pl.BlockSpec(memory_space=pl.ANY)
    stacked = (g1, qg, kg, ar, ai, bx, cm, d, bglu)
    return pl.pallas_call(
        functools.partial(_mixer_sample_kernel, layer),
        grid=(nbatch // gb,),
        in_specs=[pl.BlockSpec(memory_space=pltpu.SMEM), pl.BlockSpec(x.shape, lambda i: (0, 0)), cache, cache, h0, h0,
                  _const_spec(seg), hbm, hbm] + [_layer_spec(a, layer) for a in stacked],
        out_specs=[out2(aw), out2(sw), cat, cat, state, state],
        out_shape=[jax.ShapeDtypeStruct((rows, aw), BF16), jax.ShapeDtypeStruct((rows, sw), BF16),
                   jax.ShapeDtypeStruct((nbatch * past, kw), F32), jax.ShapeDtypeStruct((nbatch * past, kw), F32),
                   jax.ShapeDtypeStruct((nbatch, n_state), F32), jax.ShapeDtypeStruct((nbatch, n_state), F32)],
        scratch_shapes=_mixer_weight_scratch(D, aw + 2 * kw + sw, sw)
                       + [pltpu.VMEM((2, rows, aw), F32), pltpu.VMEM((rows, aw), F32), pltpu.VMEM((rows, sw), F32),
                          pltpu.VMEM((rows, n_state), F32), pltpu.VMEM((rows, n_state), F32),
                          pltpu.VMEM((rows, kw), F32), pltpu.VMEM((rows, kw), F32)],
        compiler_params=pltpu.CompilerParams(dimension_semantics=("arbitrary",), vmem_limit_bytes=VMEM_LIMIT),
        name="mixer_sample",
    )(sinks, x, ck, cv, h0r, h0i, seg, w_in, wglu, *stacked)


def _mix_ffn_rows(x_ref, attn_ref, ssm_ref, o_ref, g1_ref, wg_ref, bg_ref, wao_ref, wso_ref, wout_ref,
                  g2_ref, wup_ref, wdn_ref):
    x = x_ref[...]
    D = x.shape[-1]
    xn = _rms(x, g1_ref[...]).astype(BF16)
    attn = attn_ref[...]
    ssm = ssm_ref[...]
    o_ref[...] = x
    cw = 2 * LANES
    for cc in range(D // cw):
        cs = slice(cc * cw, (cc + 1) * cw)
        gs = slice(D + cc * cw, D + (cc + 1) * cw)
        g_attn = jax.nn.sigmoid(jnp.dot(xn, wg_ref[:, cs], preferred_element_type=F32) + bg_ref[:, cs])
        g_ssm = jax.nn.sigmoid(jnp.dot(xn, wg_ref[:, gs], preferred_element_type=F32) + bg_ref[:, gs])
        mixed = (g_attn * jnp.dot(attn, wao_ref[:, cs], preferred_element_type=F32)
                 + g_ssm * jnp.dot(ssm, wso_ref[:, cs], preferred_element_type=F32))
        o_ref[...] += jnp.dot(mixed.astype(BF16), wout_ref[cs, :], preferred_element_type=F32)
    xn2 = _rms(o_ref[...], g2_ref[...]).astype(BF16)
    fw = 4 * LANES
    for fc in range(wup_ref.shape[-1] // fw):
        fs = slice(fc * fw, (fc + 1) * fw)
        hdn = jnp.maximum(jnp.dot(xn2, wup_ref[:, fs], preferred_element_type=F32), 0.0)
        o_ref[...] += jnp.dot((hdn * hdn).astype(BF16), wdn_ref[fs, :], preferred_element_type=F32)


def _mix_ffn_kernel(layer, n_prompt, xp_ref, ap_ref, sp_ref, xs_ref, as_ref, ss_ref,
                    w_in_hbm, wao_hbm, wso_hbm, wout_hbm, wup_hbm, wdn_hbm, g1_ref, bg_ref, g2_ref,
                    op_ref, os_ref, wg_ref, wao_ref, wso_ref, wout_ref, wup_ref, wdn_ref, stage, sem):
    i = pl.program_id(0)
    cr, cc = stage.shape[1:]

    @pl.when(i == 0)
    def _():
        chunks = []
        chunks += _weight_chunks(w_in_hbm.at[layer], wg_ref, wg_ref.shape[0], wg_ref.shape[1], cr, cc,
                                 src_col0=w_in_hbm.shape[-1] - wg_ref.shape[1])
        for hbm, ref in ((wao_hbm, wao_ref), (wso_hbm, wso_ref), (wout_hbm, wout_ref), (wup_hbm, wup_ref),
                         (wdn_hbm, wdn_ref)):
            chunks += _weight_chunks(hbm.at[layer], ref, ref.shape[0], ref.shape[1], cr, cc)
        _stage_weights(chunks, stage, sem)

    w_refs = (g1_ref, wg_ref, bg_ref, wao_ref, wso_ref, wout_ref, g2_ref, wup_ref, wdn_ref)

    @pl.when(i < n_prompt)
    def _():
        _mix_ffn_rows(xp_ref, ap_ref, sp_ref, op_ref, *w_refs)

    @pl.when(i >= n_prompt)
    def _():
        _mix_ffn_rows(xs_ref, as_ref, ss_ref, os_ref, *w_refs)


def _mix_ffn(layer, xp, attn_p, ssm_p, xs, attn_s, ssm_s, w_in, wao, wso, wout, wup, wdn, g1, bg, g2):
    rp, D = xp.shape
    rs = xs.shape[0]
    tp = min(512, rp)
    ts = min(512, rs)
    assert rp % tp == 0 and rs % ts == 0
    n_p, n_s = rp // tp, rs // ts
    prow = lambda last: pl.BlockSpec((tp, last), lambda i: (jnp.minimum(i, n_p - 1), 0))
    srow = lambda last: pl.BlockSpec((ts, last), lambda i: (jnp.maximum(i - n_p, 0), 0))
    hbm = pl.BlockSpec(memory_space=pl.ANY)
    resident = lambda a: pltpu.VMEM(a.shape[1:], BF16)
    return pl.pallas_call(
        functools.partial(_mix_ffn_kernel, layer, n_p),
        grid=(n_p + n_s,),
        in_specs=[prow(D), prow(attn_p.shape[-1]), prow(ssm_p.shape[-1]),
                  srow(D), srow(attn_s.shape[-1]), srow(ssm_s.shape[-1])]
                 + [hbm] * 6 + [_layer_spec(a, layer) for a in (g1, bg, g2)],
        out_specs=[prow(D), srow(D)],
        out_shape=[jax.ShapeDtypeStruct((rp, D), F32), jax.ShapeDtypeStruct((rs, D), F32)],
        scratch_shapes=[pltpu.VMEM((D, bg.shape[-1]), BF16), resident(wao), resident(wso), resident(wout),
                        resident(wup), resident(wdn),
                        pltpu.VMEM((2, WEIGHT_CHUNK_ROWS, D), F32), pltpu.SemaphoreType.DMA((2,))],
        compiler_params=pltpu.CompilerParams(dimension_semantics=("arbitrary",), vmem_limit_bytes=VMEM_LIMIT),
        name="mix_ffn",
    )(xp, attn_p, ssm_p, xs, attn_s, ssm_s, w_in, wao, wso, wout, wup, wdn, g1, bg, g2)


def _perm_matrix(nb, tt):
    p = np.zeros((nb * tt, nb * tt), np.float32)
    t, b = np.meshgrid(np.arange(tt), np.arange(nb), indexing="ij")
    p[(t * nb + b).ravel(), (b * tt + t).ravel()] = 1.0
    return jnp.asarray(p, BF16)


def _seg_matrix(width):
    i = np.arange(width) // HEAD_DIM
    return jnp.asarray((i[:, None] == i[None, :]).astype(np.float32) / HEAD_DIM, BF16)


def kernel(x_prompt, x_sample, cache_k, cache_v, state_ssm_re, state_ssm_im, norm1_g, w_in, b_gate, q_norm_g,
           k_norm_g, attn_sinks, lam_re, lam_im, log_step, b_re, b_im, c_re, c_im, d_skip, w_glu, b_glu,
           w_attn_o, w_ssm_o, w_out, norm2_g, w_up, w_down):
    B, T, D = x_prompt.shape
    DB, DS, _ = x_sample.shape
    L = w_in.shape[0]
    aw = N_HEADS * HEAD_DIM
    kw = N_KV_HEADS * HEAD_DIM
    sw = d_skip.shape[-1]
    G, P = lam_re.shape[1:]
    past = cache_k.shape[2]
    assert past == WINDOW and T % WINDOW == 0 and (B * WINDOW) % ROW_TILE == 0 and (DB * DS) % ROW_TILE == 0

    ar, ai, bx, cm = _ssm_prep(lam_re, lam_im, log_step, b_re, b_im, c_re, c_im)
    perm = _perm_matrix(B, WINDOW)
    seg = _seg_matrix(aw)
    rows = lambda a: a.reshape(L, 1, -1)
    g1 = rows(norm1_g)
    mixer_w = (w_in, w_glu, g1, rows(jnp.tile(q_norm_g, (1, N_HEADS))), rows(jnp.tile(k_norm_g, (1, N_KV_HEADS))),
               ar, ai, bx, cm, rows(d_skip), rows(b_glu))
    ffn_w = (w_in, w_attn_o, w_ssm_o, w_out, w_up, w_down, g1, rows(b_gate), rows(norm2_g))
    ck = cache_k.reshape(L, DB, past, kw)
    cv = cache_v.reshape(L, DB, past, kw)
    h0r = state_ssm_re.reshape(L, DB, G * P)
    h0i = state_ssm_im.reshape(L, DB, G * P)

    yp = x_prompt.reshape(B * T, D)
    ys = x_sample.transpose(1, 0, 2).reshape(DS * DB, D)
    outs = [[] for _ in range(8)]
    for l in range(L):
        attn_p, ssm_p, kp, vp, hrp, hip = _mixer_prompt(l, yp.reshape(B, T, D), attn_sinks, seg, perm, *mixer_w)
        attn_s, ssm_s, kcat, vcat, hrs, his = _mixer_sample(l, ys, ck, cv, h0r, h0i, attn_sinks, seg, *mixer_w)
        yp, ys = _mix_ffn(l, yp, attn_p.reshape(B * T, aw), ssm_p.reshape(B * T, sw), ys, attn_s, ssm_s, *ffn_w)
        vals = (kp.reshape(B, WINDOW, N_KV_HEADS, HEAD_DIM), vp.reshape(B, WINDOW, N_KV_HEADS, HEAD_DIM),
                hrp.reshape(B, G, P), hip.reshape(B, G, P),
                kcat.reshape(DB, past, N_KV_HEADS, HEAD_DIM), vcat.reshape(DB, past, N_KV_HEADS, HEAD_DIM),
                hrs.reshape(DB, G, P), his.reshape(DB, G, P))
        for o, v in zip(outs, vals):
            o.append(v)

    y_sample = ys.reshape(DS, DB, D).transpose(1, 0, 2)
    return (yp.reshape(B, T, D), y_sample) + tuple(jnp.stack(o) for o in outs)
```

```python
import functools
import math

import numpy as np
import jax
import jax.numpy as jnp
from jax import lax
from jax.experimental import pallas as pl
from jax.experimental.pallas import tpu as pltpu

F32 = jnp.float32
BF16 = jnp.bfloat16

HEAD_DIM = 64
N_HEADS = 8
N_KV_HEADS = 2
Q_GROUP = N_HEADS // N_KV_HEADS
WINDOW = 128
GROUP_CH = 16
SSM_STATE = 64
EPS = 1e-6
NEG_INF = -1e30
LOG2E = math.log2(math.e)
LANES = 128
SUBLANES = 8
ROW_TILE = 256
VMEM_LIMIT = 56 * 1024 * 1024

WEIGHT_SLOTS = 4
SLOPES = tuple(2.0 ** (-8.0 * (h + 1) / N_HEADS) for h in range(N_HEADS))


def _rms(x, g):
    return x * lax.rsqrt(jnp.mean(x * x, axis=-1, keepdims=True) + EPS) * g


def _seg_rms(x, seg, g):
    ms = jnp.dot((x * x).astype(BF16), seg, preferred_element_type=F32)
    return x * lax.rsqrt(ms + EPS) * g


def _gelu_tanh(y):
    return 0.5 * y * (1.0 + jnp.tanh(math.sqrt(2.0 / math.pi) * (y + 0.044715 * (y * y * y))))


def _const_spec(a):
    n = a.ndim
    return pl.BlockSpec(a.shape, lambda *_: (0,) * n, pipeline_mode=pl.Buffered(1))


def _layer_spec(a, layer):
    n = a.ndim - 1
    return pl.BlockSpec((None,) + a.shape[1:], lambda *_: (layer,) + (0,) * n, pipeline_mode=pl.Buffered(1))


def _stage_weights(chunks, stage, sem):
    slots = stage.shape[0]

    def copy(k):
        src, _, cols = chunks[k]
        return pltpu.make_async_copy(src, stage.at[k % slots, :, pl.ds(0, cols)], sem.at[k % slots])

    for k in range(min(slots - 1, len(chunks))):
        copy(k).start()
    for k, (_, dst, cols) in enumerate(chunks):
        if k + slots - 1 < len(chunks):
            copy(k + slots - 1).start()
        copy(k).wait()
        dst[...] = stage[k % slots, :, :cols].astype(BF16)


def _weight_chunks(src, dst, rows, cols, chunk_rows, chunk_cols, src_col0=0):
    return [(src.at[pl.ds(r, chunk_rows), pl.ds(src_col0 + c, chunk_cols)],
             dst.at[pl.ds(r, chunk_rows), pl.ds(c, chunk_cols)], chunk_cols)
            for r in range(0, rows, chunk_rows) for c in range(0, cols, chunk_cols)]


def _ssm_prep_kernel(lr_ref, li_ref, ls_ref, btr_ref, bti_ref, ctr_ref, cti_ref,
                     ar_ref, ai_ref, bx_ref, cm_ref):
    lr = lr_ref[0]
    li = li_ref[0]
    step = jnp.exp(ls_ref[0])
    mag = jnp.exp(lr * step)
    ar = mag * jnp.cos(li * step)
    ai = mag * jnp.sin(li * step)
    den = lr * lr + li * li
    cr = ((ar - 1.0) * lr + ai * li) / den
    ci = (ai * lr - (ar - 1.0) * li) / den
    ar_ref[0] = ar
    ai_ref[0] = ai
    nj = btr_ref.shape[1]
    gp = SUBLANES * SSM_STATE
    rows_b = lax.broadcasted_iota(jnp.int32, (LANES, gp), 0) // GROUP_CH
    cols_b = lax.broadcasted_iota(jnp.int32, (LANES, gp), 1) // SSM_STATE
    mask_b = rows_b == cols_b
    rows_c = lax.broadcasted_iota(jnp.int32, (gp, LANES), 0) // SSM_STATE
    cols_c = lax.broadcasted_iota(jnp.int32, (gp, LANES), 1) // GROUP_CH
    mask_c = rows_c == cols_c
    for j in range(nj):
        crj = cr[:, j * gp:(j + 1) * gp]
        cij = ci[:, j * gp:(j + 1) * gp]
        b_re = btr_ref[0, j]
        b_im = bti_ref[0, j]
        bb_re = jnp.where(mask_b, crj * b_re - cij * b_im, 0.0)
        bb_im = jnp.where(mask_b, crj * b_im + cij * b_re, 0.0)
        bx_ref[0, j, :, :gp] = bb_re.astype(BF16)
        bx_ref[0, j, :, gp:] = bb_im.astype(BF16)
        cm_ref[0, j, :gp, :] = jnp.where(mask_c, ctr_ref[0, j], 0.0).astype(BF16)
        cm_ref[0, j, gp:, :] = jnp.where(mask_c, -cti_ref[0, j], 0.0).astype(BF16)


def _ssm_prep(lam_re, lam_im, log_step, b_re, b_im, c_re, c_im):
    L, G, P = lam_re.shape
    GC = b_re.shape[-1]
    gpb = LANES // GC
    nj = G // gpb
    gp = gpb * P
    flat = lambda a: a.reshape(L, 1, G * P)
    ls = jnp.repeat(log_step, P, axis=-1).reshape(L, 1, G * P)

    def b_tiles(b):
        t = b.reshape(L, nj, gpb, P, GC).transpose(0, 1, 4, 2, 3).reshape(L, nj, 1, GC, gp)
        return jnp.broadcast_to(t, (L, nj, gpb, GC, gp)).reshape(L, nj, LANES, gp)

    def c_tiles(c):
        t = c.reshape(L, nj, gpb, GC, P).transpose(0, 1, 2, 4, 3).reshape(L, nj, gp, 1, GC)
        return jnp.broadcast_to(t, (L, nj, gp, gpb, GC)).reshape(L, nj, gp, LANES)

    vec = pl.BlockSpec((1, 1, G * P), lambda l: (l, 0, 0))
    bt = pl.BlockSpec((1, nj, LANES, gp), lambda l: (l, 0, 0, 0))
    ct = pl.BlockSpec((1, nj, gp, LANES), lambda l: (l, 0, 0, 0))
    return pl.pallas_call(
        _ssm_prep_kernel,
        grid=(L,),
        in_specs=[vec, vec, vec, bt, bt, ct, ct],
        out_specs=[vec, vec,
                   pl.BlockSpec((1, nj, LANES, 2 * gp), lambda l: (l, 0, 0, 0)),
                   pl.BlockSpec((1, nj, 2 * gp, LANES), lambda l: (l, 0, 0, 0))],
        out_shape=[jax.ShapeDtypeStruct((L, 1, G * P), F32), jax.ShapeDtypeStruct((L, 1, G * P), F32),
                   jax.ShapeDtypeStruct((L, nj, LANES, 2 * gp), BF16),
                   jax.ShapeDtypeStruct((L, nj, 2 * gp, LANES), BF16)],
        name="ssm_prep",
    )(flat(lam_re), flat(lam_im), ls, b_tiles(b_re), b_tiles(b_im), c_tiles(c_re), c_tiles(c_im))


def _project_rows(x, g1_ref, w_ref, seg_ref, qg_ref, kg_ref):
    aw = N_HEADS * HEAD_DIM
    kw = N_KV_HEADS * HEAD_DIM
    xn = _rms(x, g1_ref[...]).astype(BF16)
    z = jnp.dot(xn, w_ref[...], preferred_element_type=F32)
    q = _seg_rms(z[:, :aw], seg_ref[...], qg_ref[...]) * (HEAD_DIM ** -0.5 * LOG2E)
    k = _seg_rms(z[:, aw:aw + kw], seg_ref[:kw, :kw], kg_ref[...])
    v = z[:, aw + kw:aw + 2 * kw]
    u = z[:, aw + 2 * kw:]
    lower = (lax.broadcasted_iota(jnp.int32, q.shape, 1) % LANES) < HEAD_DIM
    q_lo = jnp.where(lower, q, 0.0)
    q_hi = jnp.where(lower, 0.0, q)
    return q_lo, q_hi, k, v, u


def _stage_mixer_weights(layer, w_in_hbm, wglu_hbm, w_ref, wglu_ref, stage, sem):
    _stage_weights(
        _weight_chunks(w_in_hbm.at[layer], w_ref, w_ref.shape[0], w_ref.shape[1], stage.shape[1], w_ref.shape[1])
        + _weight_chunks(wglu_hbm.at[layer], wglu_ref, wglu_ref.shape[0], wglu_ref.shape[1], stage.shape[1],
                         wglu_ref.shape[1]),
        stage, sem)


def _mixer_weight_scratch(D, in_cols, sw):
    return [pltpu.VMEM((D, in_cols), BF16), pltpu.VMEM((sw, sw), BF16),
            pltpu.VMEM((WEIGHT_SLOTS, LANES, in_cols), F32), pltpu.SemaphoreType.DMA((WEIGHT_SLOTS,))]


def _ssm_readout(y, u, d_ref, wglu_ref, bglu_ref):
    y = _gelu_tanh(y + d_ref[...] * u)
    gate = jnp.dot(y.astype(BF16), wglu_ref[...], preferred_element_type=F32) + bglu_ref[...]
    return y * jax.nn.sigmoid(gate)


def _softmax_pv(s, sink, v):
    m = jnp.maximum(jnp.max(s, axis=-1, keepdims=True), sink)
    e = jnp.exp2(s - m)
    den = jnp.sum(e, axis=-1, keepdims=True) + jnp.exp2(sink - m)
    return jnp.dot(e.astype(BF16), v, preferred_element_type=F32) * (1.0 / den)


def _mixer_prompt_kernel(layer, sink_ref, x_ref, seg_ref, perm_ref, w_in_hbm, wglu_hbm, g1_ref, qg_ref, kg_ref,
                         ar_ref, ai_ref, bx_ref, cm_ref, d_ref, bglu_ref,
                         attn_ref, ssm_ref, kout_ref, vout_ref, hr_ref, hi_ref,
                         w_ref, wglu_ref, stage, sem, q_s, k_s, v_s, u_s, ub_s, y_s, *x_s):
    c = pl.program_id(0)
    nb, tt, _ = x_ref.shape
    rows = nb * tt
    n_tiles = rows // ROW_TILE
    bpt = ROW_TILE // tt
    n_state = ar_ref.shape[-1]
    nj = bx_ref.shape[0]
    gp = n_state // nj
    xr_s, xi_s = x_s[:nj], x_s[nj:]
    kw = kout_ref.shape[-1]

    @pl.when(c == 0)
    def _():
        _stage_mixer_weights(layer, w_in_hbm, wglu_hbm, w_ref, wglu_ref, stage, sem)
        k_s[:, :, :tt, :] = jnp.zeros((2, nb, tt, kw), BF16)
        v_s[:, :, :tt, :] = jnp.zeros((2, nb, tt, kw), BF16)
        hr_ref[...] = jnp.zeros(hr_ref.shape, F32)
        hi_ref[...] = jnp.zeros(hi_ref.shape, F32)

    def project(r, carry):
        x = x_ref[pl.ds(r * bpt, bpt)].reshape(ROW_TILE, x_ref.shape[-1])
        q_lo, q_hi, k, v, u = _project_rows(x, g1_ref, w_ref, seg_ref, qg_ref, kg_ref)
        q_s[0, pl.ds(r * bpt, bpt)] = q_lo.astype(BF16).reshape(bpt, tt, -1)
        q_s[1, pl.ds(r * bpt, bpt)] = q_hi.astype(BF16).reshape(bpt, tt, -1)
        k_s[0, pl.ds(r * bpt, bpt), tt:, :] = k.astype(BF16).reshape(bpt, tt, -1)
        v_s[0, pl.ds(r * bpt, bpt), tt:, :] = v.astype(BF16).reshape(bpt, tt, -1)
        k_s[1, pl.ds(r * bpt, bpt), tt:, :] = pltpu.roll(k, HEAD_DIM, 1).astype(BF16).reshape(bpt, tt, -1)
        v_s[1, pl.ds(r * bpt, bpt), tt:, :] = pltpu.roll(v, HEAD_DIM, 1).astype(BF16).reshape(bpt, tt, -1)
        kout_ref[pl.ds(r * bpt, bpt)] = k.reshape(bpt, tt, -1)
        vout_ref[pl.ds(r * bpt, bpt)] = v.reshape(bpt, tt, -1)
        u_s[pl.ds(r * ROW_TILE, ROW_TILE), :] = u
        ub_s[pl.ds(r * ROW_TILE, ROW_TILE), :] = u.astype(BF16)
        return carry

    lax.fori_loop(0, n_tiles, project, 0, unroll=2)

    t_idx = lax.broadcasted_iota(jnp.int32, (tt, 2 * tt), 0)
    j_idx = lax.broadcasted_iota(jnp.int32, (tt, 2 * tt), 1)
    dist = t_idx - j_idx + tt
    valid = (dist >= 0) & (dist <= WINDOW) & (j_idx >= jnp.where(c > 0, 0, tt))
    key_pos = lax.broadcasted_iota(jnp.int32, (1, 2 * tt), 1).astype(F32)
    row_pos = (lax.broadcasted_iota(jnp.int32, (tt, 1), 0) + tt).astype(F32)
    lower = lax.broadcasted_iota(jnp.int32, (tt, LANES), 1) < HEAD_DIM

    def attend(b):
        for p in range(N_HEADS // 2):
            kv = (2 * p) // Q_GROUP
            outs = []
            for e in range(2):
                h = 2 * p + e
                swap = 0 if kv == e else 1
                qm = q_s[e, b, :, p * LANES:(p + 1) * LANES]
                s = lax.dot_general(qm, k_s[swap, b], (((1,), (1,)), ((), ())), preferred_element_type=F32)
                slope = SLOPES[h] * LOG2E
                s = jnp.where(valid, s + slope * key_pos, NEG_INF)
                sink = sink_ref[layer, h] * LOG2E + slope * row_pos
                outs.append(_softmax_pv(s, sink, v_s[swap, b]))
            attn_ref[b, :, p * LANES:(p + 1) * LANES] = jnp.where(lower, outs[0], outs[1]).astype(BF16)

    def ssm_in_and_attend(r, carry):
        rs = pl.ds(r * ROW_TILE, ROW_TILE)
        up = jnp.dot(perm_ref[rs, :], ub_s[...], preferred_element_type=F32).astype(BF16)
        for j in range(nj):
            xj = jnp.dot(up[:, j * LANES:(j + 1) * LANES], bx_ref[j], preferred_element_type=F32)
            xr_s[j][rs, :] = xj[:, :gp]
            xi_s[j][rs, :] = xj[:, gp:]
        for bb in range(bpt):
            attend(r * bpt + bb)
        return carry

    lax.fori_loop(0, n_tiles, ssm_in_and_attend, 0, unroll=2)

    k_s[:, :, :tt, :] = k_s[:, :, tt:, :]
    v_s[:, :, :tt, :] = v_s[:, :, tt:, :]

    def ssm_out_tile(j, r):
        rs = pl.ds(r * ROW_TILE, ROW_TILE)
        y_s[j, rs, :] = jnp.dot(xr_s[j][rs, :].astype(BF16), cm_ref[j, :gp, :], preferred_element_type=F32)
        y_s[j, rs, :] = y_s[j, rs, :] + jnp.dot(xi_s[j][rs, :].astype(BF16), cm_ref[j, gp:, :],
                                                preferred_element_type=F32)

    for lb in range(nj):
        sl = slice(lb * gp, (lb + 1) * gp)
        a_re = jnp.broadcast_to(ar_ref[:, sl], (nb, gp))
        a_im = jnp.broadcast_to(ai_ref[:, sl], (nb, gp))

        def scan_tile(r, h, lb=lb, a_re=a_re, a_im=a_im):
            if lb > 0:
                ssm_out_tile(lb - 1, r)
            h_re, h_im = h
            for step in range(ROW_TILE // nb):
                row = pl.ds(pl.multiple_of(r * ROW_TILE + step * nb, nb), nb)
                n_re = a_re * h_re - a_im * h_im + xr_s[lb][row, :]
                n_im = a_re * h_im + a_im * h_re + xi_s[lb][row, :]
                xr_s[lb][row, :] = n_re
                xi_s[lb][row, :] = n_im
                h_re, h_im = n_re, n_im
            return h_re, h_im

        h_re, h_im = lax.fori_loop(0, n_tiles, scan_tile, (hr_ref[:, sl], hi_ref[:, sl]), unroll=True)
        hr_ref[:, sl] = h_re
        hi_ref[:, sl] = h_im

    def ssm_out_last(r, carry):
        ssm_out_tile(nj - 1, r)
        return carry

    lax.fori_loop(0, n_tiles, ssm_out_last, 0, unroll=True)

    def readout(b, carry):
        y = jnp.concatenate([y_s[j, pl.ds(b, tt, stride=nb), :] for j in range(nj)], axis=1)
        u = u_s[pl.ds(pl.multiple_of(b * tt, tt), tt), :]
        ssm_ref[b] = _ssm_readout(y, u, d_ref, wglu_ref, bglu_ref).astype(BF16)
        return carry

    lax.fori_loop(0, nb, readout, 0, unroll=2)


def _mixer_prompt(layer, x, sinks, seg, perm, w_in, wglu, g1, qg, kg, ar, ai, bx, cm, d, bglu):
    nb, T, D = x.shape
    tt = WINDOW
    rows = nb * tt
    aw = N_HEADS * HEAD_DIM
    kw = N_KV_HEADS * HEAD_DIM
    sw = d.shape[-1]
    n_state = ar.shape[-1]
    nj = bx.shape[1]
    chunk = lambda last: pl.BlockSpec((nb, tt, last), lambda c: (0, c, 0))
    whole = lambda last: pl.BlockSpec((nb, tt, last), lambda c: (0, 0, 0))
    state = pl.BlockSpec((nb, n_state), lambda c: (0, 0))
    hbm = pl.BlockSpec(memory_space=pl.ANY)
    stacked = (g1, qg, kg, ar, ai, bx, cm, d, bglu)
    return pl.pallas_call(
        functools.partial(_mixer_prompt_kernel, layer),
        grid=(T // tt,),
        in_specs=[pl.BlockSpec(memory_space=pltpu.SMEM), chunk(D), _const_spec(seg), _const_spec(perm), hbm, hbm]
                 + [_layer_spec(a, layer) for a in stacked],
        out_specs=[chunk(aw), chunk(sw), whole(kw), whole(kw), state, state],
        out_shape=[jax.ShapeDtypeStruct((nb, T, aw), BF16), jax.ShapeDtypeStruct((nb, T, sw), BF16),
                   jax.ShapeDtypeStruct((nb, tt, kw), F32), jax.ShapeDtypeStruct((nb, tt, kw), F32),
                   jax.ShapeDtypeStruct((nb, n_state), F32), jax.ShapeDtypeStruct((nb, n_state), F32)],
        scratch_shapes=_mixer_weight_scratch(D, aw + 2 * kw + sw, sw)
                       + [pltpu.VMEM((2, nb, tt, aw), BF16),
                          pltpu.VMEM((2, nb, 2 * tt, kw), BF16), pltpu.VMEM((2, nb, 2 * tt, kw), BF16),
                          pltpu.VMEM((rows, sw), F32), pltpu.VMEM((rows, sw), BF16),
                          pltpu.VMEM((nj, rows, LANES), F32)]
                       + [pltpu.VMEM((rows, n_state // nj), F32)] * (2 * nj),
        compiler_params=pltpu.CompilerParams(dimension_semantics=("arbitrary",), vmem_limit_bytes=VMEM_LIMIT),
        name="mixer_prompt",
    )(sinks, x, seg, perm, w_in, wglu, *stacked)


def _mixer_sample_kernel(layer, sink_ref, x_ref, ck_ref, cv_ref, h0r_ref, h0i_ref, seg_ref, w_in_hbm, wglu_hbm,
                         g1_ref, qg_ref, kg_ref, ar_ref, ai_ref, bx_ref, cm_ref, d_ref, bglu_ref,
                         attn_ref, ssm_ref, kcat_ref, vcat_ref, hr_ref, hi_ref,
                         w_ref, wglu_ref, stage, sem, q_s, o_s, u_s, xr_s, xi_s, knew_s, vnew_s):
    i = pl.program_id(0)
    rows = x_ref.shape[0]
    nbatch = h0r_ref.shape[0]
    ts = rows // nbatch
    past = ck_ref.shape[1]
    gb = ck_ref.shape[0]
    n_tiles = rows // ROW_TILE
    n_state = ar_ref.shape[-1]
    nj = bx_ref.shape[0]
    gp = n_state // nj

    @pl.when(i == 0)
    def _():
        _stage_mixer_weights(layer, w_in_hbm, wglu_hbm, w_ref, wglu_ref, stage, sem)

        def project(r, carry):
            rs = pl.ds(r * ROW_TILE, ROW_TILE)
            q_lo, q_hi, k, v, u = _project_rows(x_ref[rs, :], g1_ref, w_ref, seg_ref, qg_ref, kg_ref)
            q_s[0, rs, :] = q_lo
            q_s[1, rs, :] = q_hi
            knew_s[rs, :] = k
            vnew_s[rs, :] = v
            u_s[rs, :] = u
            return carry

        lax.fori_loop(0, n_tiles, project, 0)

    n_q = ts * Q_GROUP * gb
    n_keys = gb * past + ts * gb
    r_idx = lax.broadcasted_iota(jnp.int32, (n_q, n_keys), 0)
    c_idx = lax.broadcasted_iota(jnp.int32, (n_q, n_keys), 1)
    q_t = r_idx // (Q_GROUP * gb)
    q_b = r_idx % gb
    is_cache = c_idx < gb * past
    key_b = jnp.where(is_cache, c_idx // past, (c_idx - gb * past) % gb)
    dist = jnp.where(is_cache, past + q_t - c_idx % past, q_t - (c_idx - gb * past) // gb)
    valid = (key_b == q_b) & (dist >= 0) & (dist <= WINDOW)
    dist_f = dist.astype(F32)
    hh = (lax.broadcasted_iota(jnp.int32, (n_q, 1), 0) // gb) % Q_GROUP
    lower = lax.broadcasted_iota(jnp.int32, (gb, LANES), 1) < HEAD_DIM
    base = pl.multiple_of(i * gb, gb)

    for cat_ref, c_ref, new_s in ((kcat_ref, ck_ref, knew_s), (vcat_ref, cv_ref, vnew_s)):
        for b in range(gb):
            cat_ref[b * past:(b + 1) * past - ts, :] = c_ref[b, ts:, :]
        for t in range(ts):
            cat_ref[pl.ds(past - ts + t, gb, stride=past), :] = new_s[pl.ds(base + t * nbatch, gb), :]

    k_all = jnp.concatenate(
        [ck_ref[...].reshape(gb * past, -1)] + [knew_s[pl.ds(base + t * nbatch, gb), :] for t in range(ts)],
        axis=0).astype(BF16)
    v_all = jnp.concatenate(
        [cv_ref[...].reshape(gb * past, -1)] + [vnew_s[pl.ds(base + t * nbatch, gb), :] for t in range(ts)],
        axis=0).astype(BF16)
    outs = []
    for kvh in range(N_KV_HEADS):
        slope = jnp.zeros((n_q, 1), F32)
        sink = jnp.zeros((n_q, 1), F32)
        for g in range(Q_GROUP):
            h = kvh * Q_GROUP + g
            slope = jnp.where(hh == g, SLOPES[h] * LOG2E, slope)
            sink = jnp.where(hh == g, sink_ref[layer, h] * LOG2E, sink)
        pieces = []
        for t in range(ts):
            for g in range(Q_GROUP):
                h = kvh * Q_GROUP + g
                piece = q_s[h % 2, pl.ds(base + t * nbatch, gb), (h // 2) * LANES:(h // 2 + 1) * LANES]
                pieces.append(piece if h % 2 == kvh else pltpu.roll(piece, HEAD_DIM, 1))
        qm = jnp.concatenate(pieces, axis=0).astype(BF16)
        s = lax.dot_general(qm, k_all, (((1,), (1,)), ((), ())), preferred_element_type=F32)
        s = jnp.where(valid, s - slope * dist_f, NEG_INF)
        outs.append(_softmax_pv(s, sink, v_all))
    for t in range(ts):
        for p in range(N_HEADS // 2):
            kvh = (2 * p) // Q_GROUP
            r0 = (t * Q_GROUP + 2 * p - kvh * Q_GROUP) * gb
            lo = outs[kvh][r0:r0 + gb]
            hi = outs[kvh][r0 + gb:r0 + 2 * gb]
            if kvh == 1:
                lo = pltpu.roll(lo, HEAD_DIM, 1)
            else:
                hi = pltpu.roll(hi, HEAD_DIM, 1)
            o_s[pl.ds(base + t * nbatch, gb), p * LANES:(p + 1) * LANES] = jnp.where(lower, lo, hi)

    @pl.when(i == pl.num_programs(0) - 1)
    def _():
        attn_ref[...] = o_s[...].astype(BF16)

        def ssm_in(r, carry):
            rs = pl.ds(r * ROW_TILE, ROW_TILE)
            ub = u_s[rs, :].astype(BF16)
            for j in range(nj):
                xj = jnp.dot(ub[:, j * LANES:(j + 1) * LANES], bx_ref[j], preferred_element_type=F32)
                xr_s[rs, j * gp:(j + 1) * gp] = xj[:, :gp]
                xi_s[rs, j * gp:(j + 1) * gp] = xj[:, gp:]
            return carry

        lax.fori_loop(0, n_tiles, ssm_in, 0)

        lane_blk = 4 * LANES
        for lb in range(n_state // lane_blk):
            sl = slice(lb * lane_blk, (lb + 1) * lane_blk)
            a_re = jnp.broadcast_to(ar_ref[:, sl], (SUBLANES, lane_blk))
            a_im = jnp.broadcast_to(ai_ref[:, sl], (SUBLANES, lane_blk))

            def scan(bt, carry):
                b0 = pl.multiple_of(bt * SUBLANES, SUBLANES)
                h_re = h0r_ref[pl.ds(b0, SUBLANES), sl]
                h_im = h0i_ref[pl.ds(b0, SUBLANES), sl]
                for t in range(ts):
                    rs = pl.ds(b0 + t * nbatch, SUBLANES)
                    n_re = a_re * h_re - a_im * h_im + xr_s[rs, sl]
                    n_im = a_re * h_im + a_im * h_re + xi_s[rs, sl]
                    xr_s[rs, sl] = n_re
                    xi_s[rs, sl] = n_im
                    h_re, h_im = n_re, n_im
                hr_ref[pl.ds(b0, SUBLANES), sl] = h_re
                hi_ref[pl.ds(b0, SUBLANES), sl] = h_im
                return carry

            lax.fori_loop(0, nbatch // SUBLANES, scan, 0)

        def ssm_out(r, carry):
            rs = pl.ds(r * ROW_TILE, ROW_TILE)
            ys = []
            for j in range(nj):
                h_cat = jnp.concatenate([xr_s[rs, j * gp:(j + 1) * gp], xi_s[rs, j * gp:(j + 1) * gp]],
                                        axis=1).astype(BF16)
                ys.append(jnp.dot(h_cat, cm_ref[j], preferred_element_type=F32))
            y = jnp.concatenate(ys, axis=1)
            ssm_ref[rs, :] = _ssm_readout(y, u_s[rs, :], d_ref, wglu_ref, bglu_ref).astype(BF16)
            return carry

        lax.fori_loop(0, n_tiles, ssm_out, 0)


def _mixer_sample(layer, x, ck, cv, h0r, h0i, sinks, seg, w_in, wglu, g1, qg, kg, ar, ai, bx, cm, d, bglu):
    rows, D = x.shape
    _, nbatch, past, kw = ck.shape
    aw = N_HEADS * HEAD_DIM
    sw = d.shape[-1]
    n_state = ar.shape[-1]
    gb = SUBLANES
    cache = pl.BlockSpec((None, gb, past, kw), lambda i: (layer, i, 0, 0))
    h0 = pl.BlockSpec((None, nbatch, n_state), lambda i: (layer, 0, 0))
    out2 = lambda last: pl.BlockSpec((rows, last), lambda i: (0, 0))
    cat = pl.BlockSpec((gb * past, kw), lambda i: (i, 0))
    state = pl.BlockSpec((nbatch, n_state), lambda i: (0, 0))
    hbm = pl.BlockSpec(memory_space=pl.ANY)
    stacked = (g1, qg, kg, ar, ai, bx, cm, d, bglu)
    return pl.pallas_call(
        functools.partial(_mixer_sample_kernel, layer),
        grid=(nbatch // gb,),
        in_specs=[pl.BlockSpec(memory_space=pltpu.SMEM), pl.BlockSpec(x.shape, lambda i: (0, 0)), cache, cache, h0, h0,
                  _const_spec(seg), hbm, hbm] + [_layer_spec(a, layer) for a in stacked],
        out_specs=[out2(aw), out2(sw), cat, cat, state, state],
        out_shape=[jax.ShapeDtypeStruct((rows, aw), BF16), jax.ShapeDtypeStruct((rows, sw), BF16),
                   jax.ShapeDtypeStruct((nbatch * past, kw), F32), jax.ShapeDtypeStruct((nbatch * past, kw), F32),
                   jax.ShapeDtypeStruct((nbatch, n_state), F32), jax.ShapeDtypeStruct((nbatch, n_state), F32)],
        scratch_shapes=_mixer_weight_scratch(D, aw + 2 * kw + sw, sw)
                       + [pltpu.VMEM((2, rows, aw), F32), pltpu.VMEM((rows, aw), F32), pltpu.VMEM((rows, sw), F32),
                          pltpu.VMEM((rows, n_state), F32), pltpu.VMEM((rows, n_state), F32),
                          pltpu.VMEM((rows, kw), F32), pltpu.VMEM((rows, kw), F32)],
        compiler_params=pltpu.CompilerParams(dimension_semantics=("arbitrary",), vmem_limit_bytes=VMEM_LIMIT),
        name="mixer_sample",
    )(sinks, x, ck, cv, h0r, h0i, seg, w_in, wglu, *stacked)


def _mix_ffn_rows(x_ref, attn_ref, ssm_ref, o_ref, g1_ref, wg_ref, bg_ref, wao_ref, wso_ref, wout_ref,
                  g2_ref, wup_ref, wdn_ref):
    x = x_ref[...]
    D = x.shape[-1]
    xn = _rms(x, g1_ref[...]).astype(BF16)
    attn = attn_ref[...]
    ssm = ssm_ref[...]
    o_ref[...] = x
    cw = 2 * LANES
    for cc in range(D // cw):
        cs = slice(cc * cw, (cc + 1) * cw)
        gs = slice(D + cc * cw, D + (cc + 1) * cw)
        g_attn = jax.nn.sigmoid(jnp.dot(xn, wg_ref[:, cs], preferred_element_type=F32) + bg_ref[:, cs])
        g_ssm = jax.nn.sigmoid(jnp.dot(xn, wg_ref[:, gs], preferred_element_type=F32) + bg_ref[:, gs])
        mixed = (g_attn * jnp.dot(attn, wao_ref[:, cs], preferred_element_type=F32)
                 + g_ssm * jnp.dot(ssm, wso_ref[:, cs], preferred_element_type=F32))
        o_ref[...] += jnp.dot(mixed.astype(BF16), wout_ref[cs, :], preferred_element_type=F32)
    xn2 = _rms(o_ref[...], g2_ref[...]).astype(BF16)
    fw = 4 * LANES
    for fc in range(wup_ref.shape[-1] // fw):
        fs = slice(fc * fw, (fc + 1) * fw)
        hdn = jnp.maximum(jnp.dot(xn2, wup_ref[:, fs], preferred_element_type=F32), 0.0)
        o_ref[...] += jnp.dot((hdn * hdn).astype(BF16), wdn_ref[fs, :], preferred_element_type=F32)


def _mix_ffn_kernel(layer, n_prompt, xp_ref, ap_ref, sp_ref, xs_ref, as_ref, ss_ref,
                    w_in_hbm, wao_hbm, wso_hbm, wout_hbm, wup_hbm, wdn_hbm, g1_ref, bg_ref, g2_ref,
                    op_ref, os_ref, wg_ref, wao_ref, wso_ref, wout_ref, wup_ref, wdn_ref, stage, sem):
    i = pl.program_id(0)
    cr, cc = stage.shape[1:]

    @pl.when(i == 0)
    def _():
        chunks = []
        chunks += _weight_chunks(w_in_hbm.at[layer], wg_ref, wg_ref.shape[0], wg_ref.shape[1], cr, cc,
                                 src_col0=w_in_hbm.shape[-1] - wg_ref.shape[1])
        for hbm, ref in ((wao_hbm, wao_ref), (wso_hbm, wso_ref), (wout_hbm, wout_ref), (wup_hbm, wup_ref),
                         (wdn_hbm, wdn_ref)):
            chunks += _weight_chunks(hbm.at[layer], ref, ref.shape[0], ref.shape[1], cr, cc)
        _stage_weights(chunks, stage, sem)

    w_refs = (g1_ref, wg_ref, bg_ref, wao_ref, wso_ref, wout_ref, g2_ref, wup_ref, wdn_ref)

    @pl.when(i < n_prompt)
    def _():
        _mix_ffn_rows(xp_ref, ap_ref, sp_ref, op_ref, *w_refs)

    @pl.when(i >= n_prompt)
    def _():
        _mix_ffn_rows(xs_ref, as_ref, ss_ref, os_ref, *w_refs)


def _mix_ffn(layer, xp, attn_p, ssm_p, xs, attn_s, ssm_s, w_in, wao, wso, wout, wup, wdn, g1, bg, g2):
    rp, D = xp.shape
    rs = xs.shape[0]
    tp = min(512, rp)
    ts = min(512, rs)
    assert rp % tp == 0 and rs % ts == 0
    n_p, n_s = rp // tp, rs // ts
    prow = lambda last: pl.BlockSpec((tp, last), lambda i: (jnp.minimum(i, n_p - 1), 0))
    srow = lambda last: pl.BlockSpec((ts, last), lambda i: (jnp.maximum(i - n_p, 0), 0))
    hbm = pl.BlockSpec(memory_space=pl.ANY)
    resident = lambda a: pltpu.VMEM(a.shape[1:], BF16)
    return pl.pallas_call(
        functools.partial(_mix_ffn_kernel, layer, n_p),
        grid=(n_p + n_s,),
        in_specs=[prow(D), prow(attn_p.shape[-1]), prow(ssm_p.shape[-1]),
                  srow(D), srow(attn_s.shape[-1]), srow(ssm_s.shape[-1])]
                 + [hbm] * 6 + [_layer_spec(a, layer) for a in (g1, bg, g2)],
        out_specs=[prow(D), srow(D)],
        out_shape=[jax.ShapeDtypeStruct((rp, D), F32), jax.ShapeDtypeStruct((rs, D), F32)],
        scratch_shapes=[pltpu.VMEM((D, bg.shape[-1]), BF16), resident(wao), resident(wso), resident(wout),
                        resident(wup), resident(wdn),
                        pltpu.VMEM((WEIGHT_SLOTS, 2 * LANES, D), F32), pltpu.SemaphoreType.DMA((WEIGHT_SLOTS,))],
        compiler_params=pltpu.CompilerParams(dimension_semantics=("arbitrary",), vmem_limit_bytes=VMEM_LIMIT),
        name="mix_ffn",
    )(xp, attn_p, ssm_p, xs, attn_s, ssm_s, w_in, wao, wso, wout, wup, wdn, g1, bg, g2)


def _perm_matrix(nb, tt):
    p = np.zeros((nb * tt, nb * tt), np.float32)
    t, b = np.meshgrid(np.arange(tt), np.arange(nb), indexing="ij")
    p[(t * nb + b).ravel(), (b * tt + t).ravel()] = 1.0
    return jnp.asarray(p, BF16)


def _seg_matrix(width):
    i = np.arange(width) // HEAD_DIM
    return jnp.asarray((i[:, None] == i[None, :]).astype(np.float32) / HEAD_DIM, BF16)


def kernel(x_prompt, x_sample, cache_k, cache_v, state_ssm_re, state_ssm_im, norm1_g, w_in, b_gate, q_norm_g,
           k_norm_g, attn_sinks, lam_re, lam_im, log_step, b_re, b_im, c_re, c_im, d_skip, w_glu, b_glu,
           w_attn_o, w_ssm_o, w_out, norm2_g, w_up, w_down):
    B, T, D = x_prompt.shape
    DB, DS, _ = x_sample.shape
    L = w_in.shape[0]
    aw = N_HEADS * HEAD_DIM
    kw = N_KV_HEADS * HEAD_DIM
    sw = d_skip.shape[-1]
    G, P = lam_re.shape[1:]
    past = cache_k.shape[2]
    assert past == WINDOW and T % WINDOW == 0 and (B * WINDOW) % ROW_TILE == 0 and (DB * DS) % ROW_TILE == 0

    ar, ai, bx, cm = _ssm_prep(lam_re, lam_im, log_step, b_re, b_im, c_re, c_im)
    perm = _perm_matrix(B, WINDOW)
    seg = _seg_matrix(aw)
    rows = lambda a: a.reshape(L, 1, -1)
    g1 = rows(norm1_g)
    mixer_w = (w_in, w_glu, g1, rows(jnp.tile(q_norm_g, (1, N_HEADS))), rows(jnp.tile(k_norm_g, (1, N_KV_HEADS))),
               ar, ai, bx, cm, rows(d_skip), rows(b_glu))
    ffn_w = (w_in, w_attn_o, w_ssm_o, w_out, w_up, w_down, g1, rows(b_gate), rows(norm2_g))
    ck = cache_k.reshape(L, DB, past, kw)
    cv = cache_v.reshape(L, DB, past, kw)
    h0r = state_ssm_re.reshape(L, DB, G * P)
    h0i = state_ssm_im.reshape(L, DB, G * P)

    yp = x_prompt.reshape(B * T, D)
    ys = x_sample.transpose(1, 0, 2).reshape(DS * DB, D)
    outs = [[] for _ in range(8)]
    for l in range(L):
        attn_p, ssm_p, kp, vp, hrp, hip = _mixer_prompt(l, yp.reshape(B, T, D), attn_sinks, seg, perm, *mixer_w)
        attn_s, ssm_s, kcat, vcat, hrs, his = _mixer_sample(l, ys, ck, cv, h0r, h0i, attn_sinks, seg, *mixer_w)
        yp, ys = _mix_ffn(l, yp, attn_p.reshape(B * T, aw), ssm_p.reshape(B * T, sw), ys, attn_s, ssm_s, *ffn_w)
        vals = (kp.reshape(B, WINDOW, N_KV_HEADS, HEAD_DIM), vp.reshape(B, WINDOW, N_KV_HEADS, HEAD_DIM),
                hrp.reshape(B, G, P), hip.reshape(B, G, P),
                kcat.reshape(DB, past, N_KV_HEADS, HEAD_DIM), vcat.reshape(DB, past, N_KV_HEADS, HEAD_DIM),
                hrs.reshape(DB, G, P), his.reshape(DB, G, P))
        for o, v in zip(outs, vals):
            o.append(v)

    y_sample = ys.reshape(DS, DB, D).transpose(1, 0, 2)
    return (yp.reshape(B, T, D), y_sample) + tuple(jnp.stack(o) for o in outs)
```

```python
import functools
import math

import numpy as np
import jax
import jax.numpy as jnp
from jax import lax
from jax.experimental import pallas as pl
from jax.experimental.pallas import tpu as pltpu

F32 = jnp.float32
BF16 = jnp.bfloat16

HEAD_DIM = 64
N_HEADS = 8
N_KV_HEADS = 2
Q_GROUP = N_HEADS // N_KV_HEADS
WINDOW = 128
GROUP_CH = 16
SSM_STATE = 64
EPS = 1e-6
NEG_INF = -1e30
LOG2E = math.log2(math.e)
LANES = 128
SUBLANES = 8
ROW_TILE = 256
VMEM_LIMIT = 56 * 1024 * 1024

WEIGHT_SLOTS = 4
SLOPES = tuple(2.0 ** (-8.0 * (h + 1) / N_HEADS) for h in range(N_HEADS))


def _rms(x, g):
    return x * lax.rsqrt(jnp.mean(x * x, axis=-1, keepdims=True) + EPS) * g


def _seg_rms(x, seg, g):
    ms = jnp.dot((x * x).astype(BF16), seg, preferred_element_type=F32)
    return x * lax.rsqrt(ms + EPS) * g


def _gelu_tanh(y):
    return 0.5 * y * (1.0 + jnp.tanh(math.sqrt(2.0 / math.pi) * (y + 0.044715 * (y * y * y))))


def _const_spec(a):
    n = a.ndim
    return pl.BlockSpec(a.shape, lambda *_: (0,) * n, pipeline_mode=pl.Buffered(1))


def _layer_spec(a, layer):
    n = a.ndim - 1
    return pl.BlockSpec((None,) + a.shape[1:], lambda *_: (layer,) + (0,) * n, pipeline_mode=pl.Buffered(1))


def _stage_weights(chunks, stage, sem):
    slots = stage.shape[0]

    def copy(k):
        src, _, cols = chunks[k]
        return pltpu.make_async_copy(src, stage.at[k % slots, :, pl.ds(0, cols)], sem.at[k % slots])

    for k in range(min(slots - 1, len(chunks))):
        copy(k).start()
    for k, (_, dst, cols) in enumerate(chunks):
        if k + slots - 1 < len(chunks):
            copy(k + slots - 1).start()
        copy(k).wait()
        dst[...] = stage[k % slots, :, :cols].astype(BF16)


def _weight_chunks(src, dst, rows, cols, chunk_rows, chunk_cols, src_col0=0):
    return [(src.at[pl.ds(r, chunk_rows), pl.ds(src_col0 + c, chunk_cols)],
             dst.at[pl.ds(r, chunk_rows), pl.ds(c, chunk_cols)], chunk_cols)
            for r in range(0, rows, chunk_rows) for c in range(0, cols, chunk_cols)]


def _ssm_prep_kernel(lr_ref, li_ref, ls_ref, btr_ref, bti_ref, ctr_ref, cti_ref,
                     ar_ref, ai_ref, bx_ref, cm_ref):
    lr = lr_ref[0]
    li = li_ref[0]
    step = jnp.exp(ls_ref[0])
    mag = jnp.exp(lr * step)
    ar = mag * jnp.cos(li * step)
    ai = mag * jnp.sin(li * step)
    den = lr * lr + li * li
    cr = ((ar - 1.0) * lr + ai * li) / den
    ci = (ai * lr - (ar - 1.0) * li) / den
    ar_ref[0] = ar
    ai_ref[0] = ai
    nj = btr_ref.shape[1]
    gp = SUBLANES * SSM_STATE
    rows_b = lax.broadcasted_iota(jnp.int32, (LANES, gp), 0) // GROUP_CH
    cols_b = lax.broadcasted_iota(jnp.int32, (LANES, gp), 1) // SSM_STATE
    mask_b = rows_b == cols_b
    rows_c = lax.broadcasted_iota(jnp.int32, (gp, LANES), 0) // SSM_STATE
    cols_c = lax.broadcasted_iota(jnp.int32, (gp, LANES), 1) // GROUP_CH
    mask_c = rows_c == cols_c
    for j in range(nj):
        crj = cr[:, j * gp:(j + 1) * gp]
        cij = ci[:, j * gp:(j + 1) * gp]
        b_re = btr_ref[0, j]
        b_im = bti_ref[0, j]
        bb_re = jnp.where(mask_b, crj * b_re - cij * b_im, 0.0)
        bb_im = jnp.where(mask_b, crj * b_im + cij * b_re, 0.0)
        bx_ref[0, j, :, :gp] = bb_re.astype(BF16)
        bx_ref[0, j, :, gp:] = bb_im.astype(BF16)
        cm_ref[0, j, :gp, :] = jnp.where(mask_c, ctr_ref[0, j], 0.0).astype(BF16)
        cm_ref[0, j, gp:, :] = jnp.where(mask_c, -cti_ref[0, j], 0.0).astype(BF16)


def _ssm_prep(lam_re, lam_im, log_step, b_re, b_im, c_re, c_im):
    L, G, P = lam_re.shape
    GC = b_re.shape[-1]
    gpb = LANES // GC
    nj = G // gpb
    gp = gpb * P
    flat = lambda a: a.reshape(L, 1, G * P)
    ls = jnp.repeat(log_step, P, axis=-1).reshape(L, 1, G * P)

    def b_tiles(b):
        t = b.reshape(L, nj, gpb, P, GC).transpose(0, 1, 4, 2, 3).reshape(L, nj, 1, GC, gp)
        return jnp.broadcast_to(t, (L, nj, gpb, GC, gp)).reshape(L, nj, LANES, gp)

    def c_tiles(c):
        t = c.reshape(L, nj, gpb, GC, P).transpose(0, 1, 2, 4, 3).reshape(L, nj, gp, 1, GC)
        return jnp.broadcast_to(t, (L, nj, gp, gpb, GC)).reshape(L, nj, gp, LANES)

    vec = pl.BlockSpec((1, 1, G * P), lambda l: (l, 0, 0))
    bt = pl.BlockSpec((1, nj, LANES, gp), lambda l: (l, 0, 0, 0))
    ct = pl.BlockSpec((1, nj, gp, LANES), lambda l: (l, 0, 0, 0))
    return pl.pallas_call(
        _ssm_prep_kernel,
        grid=(L,),
        in_specs=[vec, vec, vec, bt, bt, ct, ct],
        out_specs=[vec, vec,
                   pl.BlockSpec((1, nj, LANES, 2 * gp), lambda l: (l, 0, 0, 0)),
                   pl.BlockSpec((1, nj, 2 * gp, LANES), lambda l: (l, 0, 0, 0))],
        out_shape=[jax.ShapeDtypeStruct((L, 1, G * P), F32), jax.ShapeDtypeStruct((L, 1, G * P), F32),
                   jax.ShapeDtypeStruct((L, nj, LANES, 2 * gp), BF16),
                   jax.ShapeDtypeStruct((L, nj, 2 * gp, LANES), BF16)],
        name="ssm_prep",
    )(flat(lam_re), flat(lam_im), ls, b_tiles(b_re), b_tiles(b_im), c_tiles(c_re), c_tiles(c_im))


def _project_rows(x, g1_ref, w_ref, seg_ref, qg_ref, kg_ref):
    aw = N_HEADS * HEAD_DIM
    kw = N_KV_HEADS * HEAD_DIM
    xn = _rms(x, g1_ref[...]).astype(BF16)
    z = jnp.dot(xn, w_ref[...], preferred_element_type=F32)
    q = _seg_rms(z[:, :aw], seg_ref[...], qg_ref[...]) * (HEAD_DIM ** -0.5 * LOG2E)
    k = _seg_rms(z[:, aw:aw + kw], seg_ref[:kw, :kw], kg_ref[...])
    v = z[:, aw + kw:aw + 2 * kw]
    u = z[:, aw + 2 * kw:]
    lower = (lax.broadcasted_iota(jnp.int32, q.shape, 1) % LANES) < HEAD_DIM
    q_lo = jnp.where(lower, q, 0.0)
    q_hi = jnp.where(lower, 0.0, q)
    return q_lo, q_hi, k, v, u


def _stage_mixer_weights(layer, w_in_hbm, wglu_hbm, w_ref, wglu_ref, stage, sem):
    _stage_weights(
        _weight_chunks(w_in_hbm.at[layer], w_ref, w_ref.shape[0], w_ref.shape[1], stage.shape[1], w_ref.shape[1])
        + _weight_chunks(wglu_hbm.at[layer], wglu_ref, wglu_ref.shape[0], wglu_ref.shape[1], stage.shape[1],
                         wglu_ref.shape[1]),
        stage, sem)


def _mixer_weight_scratch(D, in_cols, sw):
    return [pltpu.VMEM((D, in_cols), BF16), pltpu.VMEM((sw, sw), BF16),
            pltpu.VMEM((WEIGHT_SLOTS, LANES, in_cols), F32), pltpu.SemaphoreType.DMA((WEIGHT_SLOTS,))]


def _ssm_readout(y, u, d_ref, wglu_ref, bglu_ref):
    y = _gelu_tanh(y + d_ref[...] * u)
    gate = jnp.dot(y.astype(BF16), wglu_ref[...], preferred_element_type=F32) + bglu_ref[...]
    return y * jax.nn.sigmoid(gate)


def _softmax_pv(s, sink, v):
    m = jnp.maximum(jnp.max(s, axis=-1, keepdims=True), sink)
    e = jnp.exp2(s - m)
    den = jnp.sum(e, axis=-1, keepdims=True) + jnp.exp2(sink - m)
    return jnp.dot(e.astype(BF16), v, preferred_element_type=F32) * (1.0 / den)


def _mixer_prompt_kernel(layer, sink_ref, x_ref, seg_ref, perm_ref, w_in_hbm, wglu_hbm, g1_ref, qg_ref, kg_ref,
                         ar_ref, ai_ref, bx_ref, cm_ref, d_ref, bglu_ref,
                         attn_ref, ssm_ref, kout_ref, vout_ref, hr_ref, hi_ref,
                         w_ref, wglu_ref, stage, sem, q_s, k_s, v_s, u_s, ub_s, y_s, *x_s):
    c = pl.program_id(0)
    nb, tt, _ = x_ref.shape
    rows = nb * tt
    n_tiles = rows // ROW_TILE
    bpt = ROW_TILE // tt
    n_state = ar_ref.shape[-1]
    nj = bx_ref.shape[0]
    gp = n_state // nj
    xr_s, xi_s = x_s[:nj], x_s[nj:]
    kw = kout_ref.shape[1]

    @pl.when(c == 0)
    def _():
        _stage_mixer_weights(layer, w_in_hbm, wglu_hbm, w_ref, wglu_ref, stage, sem)
        k_s[:, :, :tt, :] = jnp.zeros((2, nb, tt, kw), BF16)
        v_s[:, :, :tt, :] = jnp.zeros((2, nb, tt, kw), BF16)
        hr_ref[...] = jnp.zeros(hr_ref.shape, F32)
        hi_ref[...] = jnp.zeros(hi_ref.shape, F32)

    def project(r, carry):
        x = x_ref[pl.ds(r * bpt, bpt)].reshape(ROW_TILE, x_ref.shape[-1])
        q_lo, q_hi, k, v, u = _project_rows(x, g1_ref, w_ref, seg_ref, qg_ref, kg_ref)
        q_s[0, pl.ds(r * bpt, bpt)] = q_lo.astype(BF16).reshape(bpt, tt, -1)
        q_s[1, pl.ds(r * bpt, bpt)] = q_hi.astype(BF16).reshape(bpt, tt, -1)
        k_s[0, pl.ds(r * bpt, bpt), tt:, :] = k.astype(BF16).reshape(bpt, tt, -1)
        v_s[0, pl.ds(r * bpt, bpt), tt:, :] = v.astype(BF16).reshape(bpt, tt, -1)
        k_s[1, pl.ds(r * bpt, bpt), tt:, :] = pltpu.roll(k, HEAD_DIM, 1).astype(BF16).reshape(bpt, tt, -1)
        v_s[1, pl.ds(r * bpt, bpt), tt:, :] = pltpu.roll(v, HEAD_DIM, 1).astype(BF16).reshape(bpt, tt, -1)
        for bb in range(bpt):
            kout_ref[r * bpt + bb] = k[bb * tt:(bb + 1) * tt, :].T
            vout_ref[r * bpt + bb] = v[bb * tt:(bb + 1) * tt, :].T
        u_s[pl.ds(r * ROW_TILE, ROW_TILE), :] = u
        ub_s[pl.ds(r * ROW_TILE, ROW_TILE), :] = u.astype(BF16)
        return carry

    lax.fori_loop(0, n_tiles, project, 0, unroll=2)

    t_idx = lax.broadcasted_iota(jnp.int32, (tt, 2 * tt), 0)
    j_idx = lax.broadcasted_iota(jnp.int32, (tt, 2 * tt), 1)
    dist = t_idx - j_idx + tt
    valid = (dist >= 0) & (dist <= WINDOW) & (j_idx >= jnp.where(c > 0, 0, tt))
    key_pos = lax.broadcasted_iota(jnp.int32, (1, 2 * tt), 1).astype(F32)
    row_pos = (lax.broadcasted_iota(jnp.int32, (tt, 1), 0) + tt).astype(F32)
    lower = lax.broadcasted_iota(jnp.int32, (tt, LANES), 1) < HEAD_DIM

    def attend(b):
        for p in range(N_HEADS // 2):
            kv = (2 * p) // Q_GROUP
            outs = []
            for e in range(2):
                h = 2 * p + e
                swap = 0 if kv == e else 1
                qm = q_s[e, b, :, p * LANES:(p + 1) * LANES]
                s = lax.dot_general(qm, k_s[swap, b], (((1,), (1,)), ((), ())), preferred_element_type=F32)
                slope = SLOPES[h] * LOG2E
                s = jnp.where(valid, s + slope * key_pos, NEG_INF)
                sink = sink_ref[layer, h] * LOG2E + slope * row_pos
                outs.append(_softmax_pv(s, sink, v_s[swap, b]))
            attn_ref[b, :, p * LANES:(p + 1) * LANES] = jnp.where(lower, outs[0], outs[1]).astype(BF16)

    def ssm_in_and_attend(r, carry):
        rs = pl.ds(r * ROW_TILE, ROW_TILE)
        up = jnp.dot(perm_ref[rs, :], ub_s[...], preferred_element_type=F32).astype(BF16)
        for j in range(nj):
            xj = jnp.dot(up[:, j * LANES:(j + 1) * LANES], bx_ref[j], preferred_element_type=F32)
            xr_s[j][rs, :] = xj[:, :gp]
            xi_s[j][rs, :] = xj[:, gp:]
        for bb in range(bpt):
            attend(r * bpt + bb)
        return carry

    lax.fori_loop(0, n_tiles, ssm_in_and_attend, 0, unroll=2)

    k_s[:, :, :tt, :] = k_s[:, :, tt:, :]
    v_s[:, :, :tt, :] = v_s[:, :, tt:, :]

    def ssm_out_tile(j, r):
        rs = pl.ds(r * ROW_TILE, ROW_TILE)
        y_s[j, rs, :] = jnp.dot(xr_s[j][rs, :].astype(BF16), cm_ref[j, :gp, :], preferred_element_type=F32)
        y_s[j, rs, :] = y_s[j, rs, :] + jnp.dot(xi_s[j][rs, :].astype(BF16), cm_ref[j, gp:, :],
                                                preferred_element_type=F32)

    for lb in range(nj):
        sl = slice(lb * gp, (lb + 1) * gp)
        a_re = jnp.broadcast_to(ar_ref[:, sl], (nb, gp))
        a_im = jnp.broadcast_to(ai_ref[:, sl], (nb, gp))

        def scan_tile(r, h, lb=lb, a_re=a_re, a_im=a_im):
            if lb > 0:
                ssm_out_tile(lb - 1, r)
            h_re, h_im = h
            for step in range(ROW_TILE // nb):
                row = pl.ds(pl.multiple_of(r * ROW_TILE + step * nb, nb), nb)
                n_re = a_re * h_re - a_im * h_im + xr_s[lb][row, :]
                n_im = a_re * h_im + a_im * h_re + xi_s[lb][row, :]
                xr_s[lb][row, :] = n_re
                xi_s[lb][row, :] = n_im
                h_re, h_im = n_re, n_im
            return h_re, h_im

        h_re, h_im = lax.fori_loop(0, n_tiles, scan_tile, (hr_ref[:, sl], hi_ref[:, sl]), unroll=True)
        hr_ref[:, sl] = h_re
        hi_ref[:, sl] = h_im

    def ssm_out_last(r, carry):
        ssm_out_tile(nj - 1, r)
        return carry

    lax.fori_loop(0, n_tiles, ssm_out_last, 0, unroll=True)

    def readout(b, carry):
        y = jnp.concatenate([y_s[j, pl.ds(b, tt, stride=nb), :] for j in range(nj)], axis=1)
        u = u_s[pl.ds(pl.multiple_of(b * tt, tt), tt), :]
        ssm_ref[b] = _ssm_readout(y, u, d_ref, wglu_ref, bglu_ref).astype(BF16)
        return carry

    lax.fori_loop(0, nb, readout, 0, unroll=2)


def _mixer_prompt(layer, x, sinks, seg, perm, w_in, wglu, g1, qg, kg, ar, ai, bx, cm, d, bglu):
    nb, T, D = x.shape
    tt = WINDOW
    rows = nb * tt
    aw = N_HEADS * HEAD_DIM
    kw = N_KV_HEADS * HEAD_DIM
    sw = d.shape[-1]
    n_state = ar.shape[-1]
    nj = bx.shape[1]
    chunk = lambda last: pl.BlockSpec((nb, tt, last), lambda c: (0, c, 0))
    whole = pl.BlockSpec((nb, kw, tt), lambda c: (0, 0, 0))
    state = pl.BlockSpec((nb, n_state), lambda c: (0, 0))
    hbm = pl.BlockSpec(memory_space=pl.ANY)
    stacked = (g1, qg, kg, ar, ai, bx, cm, d, bglu)
    return pl.pallas_call(
        functools.partial(_mixer_prompt_kernel, layer),
        grid=(T // tt,),
        in_specs=[pl.BlockSpec(memory_space=pltpu.SMEM), chunk(D), _const_spec(seg), _const_spec(perm), hbm, hbm]
                 + [_layer_spec(a, layer) for a in stacked],
        out_specs=[chunk(aw), chunk(sw), whole, whole, state, state],
        out_shape=[jax.ShapeDtypeStruct((nb, T, aw), BF16), jax.ShapeDtypeStruct((nb, T, sw), BF16),
                   jax.ShapeDtypeStruct((nb, kw, tt), F32), jax.ShapeDtypeStruct((nb, kw, tt), F32),
                   jax.ShapeDtypeStruct((nb, n_state), F32), jax.ShapeDtypeStruct((nb, n_state), F32)],
        scratch_shapes=_mixer_weight_scratch(D, aw + 2 * kw + sw, sw)
                       + [pltpu.VMEM((2, nb, tt, aw), BF16),
                          pltpu.VMEM((2, nb, 2 * tt, kw), BF16), pltpu.VMEM((2, nb, 2 * tt, kw), BF16),
                          pltpu.VMEM((rows, sw), F32), pltpu.VMEM((rows, sw), BF16),
                          pltpu.VMEM((nj, rows, LANES), F32)]
                       + [pltpu.VMEM((rows, n_state // nj), F32)] * (2 * nj),
        compiler_params=pltpu.CompilerParams(dimension_semantics=("arbitrary",), vmem_limit_bytes=VMEM_LIMIT),
        name="mixer_prompt",
    )(sinks, x, seg, perm, w_in, wglu, *stacked)


def _mixer_sample_kernel(layer, sink_ref, x_ref, ckt_ref, cvt_ref, h0rt_ref, h0it_ref, *refs):
    if layer:
        prevk_ref, prevv_ref, *refs = refs
    (seg_ref, w_in_hbm, wglu_hbm, g1_ref, qg_ref, kg_ref, ar_ref, ai_ref, bx_ref, cm_ref, d_ref, bglu_ref,
     attn_ref, ssm_ref, kcat_ref, vcat_ref, hrt_ref, hit_ref,
     w_ref, wglu_ref, stage, sem, q_s, o_s, u_s, xr_s, xi_s, knew_s, vnew_s,
     ck_ref, cv_ref, kroll_s, vroll_s, h0r_ref, h0i_ref, hr_ref, hi_ref) = refs
    i = pl.program_id(0)
    rows = x_ref.shape[0]
    nbatch = h0rt_ref.shape[1]
    ts = rows // nbatch
    gb, kw, past = ckt_ref.shape
    n_tiles = rows // ROW_TILE
    n_state = ar_ref.shape[-1]
    nj = bx_ref.shape[0]
    gp = n_state // nj

    @pl.when(i == 0)
    def _():
        _stage_mixer_weights(layer, w_in_hbm, wglu_hbm, w_ref, wglu_ref, stage, sem)

        def project(r, carry):
            rs = pl.ds(r * ROW_TILE, ROW_TILE)
            q_lo, q_hi, k, v, u = _project_rows(x_ref[rs, :], g1_ref, w_ref, seg_ref, qg_ref, kg_ref)
            q_s[0, rs, :] = q_lo
            q_s[1, rs, :] = q_hi
            knew_s[rs, :] = k
            vnew_s[rs, :] = v
            u_s[rs, :] = u
            return carry

        lax.fori_loop(0, n_tiles, project, 0)

    n_q = ts * Q_GROUP * gb
    n_keys = gb * past + ts * gb
    r_idx = lax.broadcasted_iota(jnp.int32, (n_q, n_keys), 0)
    c_idx = lax.broadcasted_iota(jnp.int32, (n_q, n_keys), 1)
    q_t = r_idx // (Q_GROUP * gb)
    q_b = r_idx % gb
    is_cache = c_idx < gb * past
    key_b = jnp.where(is_cache, c_idx // past, (c_idx - gb * past) % gb)
    dist = jnp.where(is_cache, past + q_t - c_idx % past, q_t - (c_idx - gb * past) // gb)
    valid = (key_b == q_b) & (dist >= 0) & (dist <= WINDOW)
    dist_f = dist.astype(F32)
    hh = (lax.broadcasted_iota(jnp.int32, (n_q, 1), 0) // gb) % Q_GROUP
    lower = lax.broadcasted_iota(jnp.int32, (gb, LANES), 1) < HEAD_DIM
    base = pl.multiple_of(i * gb, gb)

    for cat_ref, ct_ref, c_ref, roll_s, new_s in ((kcat_ref, ckt_ref, ck_ref, kroll_s, knew_s),
                                                  (vcat_ref, cvt_ref, cv_ref, vroll_s, vnew_s)):
        for b in range(gb):
            c_ref[b * past:(b + 1) * past, :] = ct_ref[b].T
        for b in range(gb):
            roll_s[b * past:(b + 1) * past - ts, :] = c_ref[b * past + ts:(b + 1) * past, :]
        for t in range(ts):
            roll_s[pl.ds(past - ts + t, gb, stride=past), :] = new_s[pl.ds(base + t * nbatch, gb), :]
        for b in range(gb):
            cat_ref[layer, b] = roll_s[b * past:(b + 1) * past, :].T
    if layer:
        kcat_ref[:layer] = prevk_ref[...]
        vcat_ref[:layer] = prevv_ref[...]

    k_all = jnp.concatenate(
        [ck_ref[...]] + [knew_s[pl.ds(base + t * nbatch, gb), :] for t in range(ts)], axis=0).astype(BF16)
    v_all = jnp.concatenate(
        [cv_ref[...]] + [vnew_s[pl.ds(base + t * nbatch, gb), :] for t in range(ts)], axis=0).astype(BF16)
    outs = []
    for kvh in range(N_KV_HEADS):
        slope = jnp.zeros((n_q, 1), F32)
        sink = jnp.zeros((n_q, 1), F32)
        for g in range(Q_GROUP):
            h = kvh * Q_GROUP + g
            slope = jnp.where(hh == g, SLOPES[h] * LOG2E, slope)
            sink = jnp.where(hh == g, sink_ref[layer, h] * LOG2E, sink)
        pieces = []
        for t in range(ts):
            for g in range(Q_GROUP):
                h = kvh * Q_GROUP + g
                piece = q_s[h % 2, pl.ds(base + t * nbatch, gb), (h // 2) * LANES:(h // 2 + 1) * LANES]
                pieces.append(piece if h % 2 == kvh else pltpu.roll(piece, HEAD_DIM, 1))
        qm = jnp.concatenate(pieces, axis=0).astype(BF16)
        s = lax.dot_general(qm, k_all, (((1,), (1,)), ((), ())), preferred_element_type=F32)
        s = jnp.where(valid, s - slope * dist_f, NEG_INF)
        outs.append(_softmax_pv(s, sink, v_all))
    for t in range(ts):
        for p in range(N_HEADS // 2):
            kvh = (2 * p) // Q_GROUP
            r0 = (t * Q_GROUP + 2 * p - kvh * Q_GROUP) * gb
            lo = outs[kvh][r0:r0 + gb]
            hi = outs[kvh][r0 + gb:r0 + 2 * gb]
            if kvh == 1:
                lo = pltpu.roll(lo, HEAD_DIM, 1)
            else:
                hi = pltpu.roll(hi, HEAD_DIM, 1)
            o_s[pl.ds(base + t * nbatch, gb), p * LANES:(p + 1) * LANES] = jnp.where(lower, lo, hi)

    @pl.when(i == pl.num_programs(0) - 1)
    def _():
        attn_ref[...] = o_s[...].astype(BF16)

        def ssm_in(r, carry):
            rs = pl.ds(r * ROW_TILE, ROW_TILE)
            ub = u_s[rs, :].astype(BF16)
            for j in range(nj):
                xj = jnp.dot(ub[:, j * LANES:(j + 1) * LANES], bx_ref[j], preferred_element_type=F32)
                xr_s[rs, j * gp:(j + 1) * gp] = xj[:, :gp]
                xi_s[rs, j * gp:(j + 1) * gp] = xj[:, gp:]
            return carry

        lax.fori_loop(0, n_tiles, ssm_in, 0)

        for blk in range(n_state // LANES):
            bs = slice(blk * LANES, (blk + 1) * LANES)
            h0r_ref[:, bs] = h0rt_ref[bs, :].T
            h0i_ref[:, bs] = h0it_ref[bs, :].T

        lane_blk = 4 * LANES
        for lb in range(n_state // lane_blk):
            sl = slice(lb * lane_blk, (lb + 1) * lane_blk)
            a_re = jnp.broadcast_to(ar_ref[:, sl], (SUBLANES, lane_blk))
            a_im = jnp.broadcast_to(ai_ref[:, sl], (SUBLANES, lane_blk))

            def scan(bt, carry):
                b0 = pl.multiple_of(bt * SUBLANES, SUBLANES)
                h_re = h0r_ref[pl.ds(b0, SUBLANES), sl]
                h_im = h0i_ref[pl.ds(b0, SUBLANES), sl]
                for t in range(ts):
                    rs = pl.ds(b0 + t * nbatch, SUBLANES)
                    n_re = a_re * h_re - a_im * h_im + xr_s[rs, sl]
                    n_im = a_re * h_im + a_im * h_re + xi_s[rs, sl]
                    xr_s[rs, sl] = n_re
                    xi_s[rs, sl] = n_im
                    h_re, h_im = n_re, n_im
                hr_ref[pl.ds(b0, SUBLANES), sl] = h_re
                hi_ref[pl.ds(b0, SUBLANES), sl] = h_im
                return carry

            lax.fori_loop(0, nbatch // SUBLANES, scan, 0)

        for blk in range(n_state // LANES):
            bs = slice(blk * LANES, (blk + 1) * LANES)
            hrt_ref[bs, :] = hr_ref[:, bs].T
            hit_ref[bs, :] = hi_ref[:, bs].T

        def ssm_out(r, carry):
            rs = pl.ds(r * ROW_TILE, ROW_TILE)
            ys = []
            for j in range(nj):
                h_cat = jnp.concatenate([xr_s[rs, j * gp:(j + 1) * gp], xi_s[rs, j * gp:(j + 1) * gp]],
                                        axis=1).astype(BF16)
                ys.append(jnp.dot(h_cat, cm_ref[j], preferred_element_type=F32))
            y = jnp.concatenate(ys, axis=1)
            ssm_ref[rs, :] = _ssm_readout(y, u_s[rs, :], d_ref, wglu_ref, bglu_ref).astype(BF16)
            return carry

        lax.fori_loop(0, n_tiles, ssm_out, 0)


def _mixer_sample(layer, x, ckt, cvt, h0rt, h0it, prev_k, prev_v, sinks, seg, w_in, wglu,
                  g1, qg, kg, ar, ai, bx, cm, d, bglu):
    rows, D = x.shape
    _, nbatch, kw, past = ckt.shape
    aw = N_HEADS * HEAD_DIM
    sw = d.shape[-1]
    n_state = ar.shape[-1]
    gb = SUBLANES
    assert nbatch % LANES == 0 and past == LANES and kw == LANES
    cache = pl.BlockSpec((None, gb, kw, past), lambda i: (layer, i, 0, 0))
    h0 = pl.BlockSpec((None, n_state, nbatch), lambda i: (layer, 0, 0))
    prev = pl.BlockSpec((layer, gb, kw, past), lambda i: (0, i, 0, 0))
    out2 = lambda last: pl.BlockSpec((rows, last), lambda i: (0, 0))
    cat = pl.BlockSpec((layer + 1, gb, kw, past), lambda i: (0, i, 0, 0))
    state = pl.BlockSpec((n_state, nbatch), lambda i: (0, 0))
    hbm = pl.BlockSpec(memory_space=pl.ANY)
    stacked = (g1, qg, kg, ar, ai, bx, cm, d, bglu)
    prevs = (prev_k, prev_v) if layer else ()
    return pl.pallas_call(
        functools.partial(_mixer_sample_kernel, layer),
        grid=(nbatch // gb,),
        in_specs=[pl.BlockSpec(memory_space=pltpu.SMEM), pl.BlockSpec(x.shape, lambda i: (0, 0)), cache, cache, h0, h0]
                 + [prev] * len(prevs) + [_const_spec(seg), hbm, hbm] + [_layer_spec(a, layer) for a in stacked],
        out_specs=[out2(aw), out2(sw), cat, cat, state, state],
        out_shape=[jax.ShapeDtypeStruct((rows, aw), BF16), jax.ShapeDtypeStruct((rows, sw), BF16),
                   jax.ShapeDtypeStruct((layer + 1, nbatch, kw, past), F32),
                   jax.ShapeDtypeStruct((layer + 1, nbatch, kw, past), F32),
                   jax.ShapeDtypeStruct((n_state, nbatch), F32), jax.ShapeDtypeStruct((n_state, nbatch), F32)],
        scratch_shapes=_mixer_weight_scratch(D, aw + 2 * kw + sw, sw)
                       + [pltpu.VMEM((2, rows, aw), F32), pltpu.VMEM((rows, aw), F32), pltpu.VMEM((rows, sw), F32),
                          pltpu.VMEM((rows, n_state), F32), pltpu.VMEM((rows, n_state), F32),
                          pltpu.VMEM((rows, kw), F32), pltpu.VMEM((rows, kw), F32)]
                       + [pltpu.VMEM((gb * past, kw), F32)] * 4 + [pltpu.VMEM((nbatch, n_state), F32)] * 4,
        compiler_params=pltpu.CompilerParams(dimension_semantics=("arbitrary",), vmem_limit_bytes=VMEM_LIMIT),
        name="mixer_sample",
    )(sinks, x, ckt, cvt, h0rt, h0it, *prevs, seg, w_in, wglu, *stacked)


def _mix_ffn_rows(x_ref, attn_ref, ssm_ref, o_ref, g1_ref, wg_ref, bg_ref, wao_ref, wso_ref, wout_ref,
                  g2_ref, wup_ref, wdn_ref):
    x = x_ref[...]
    D = x.shape[-1]
    xn = _rms(x, g1_ref[...]).astype(BF16)
    attn = attn_ref[...]
    ssm = ssm_ref[...]
    o_ref[...] = x
    cw = 2 * LANES
    for cc in range(D // cw):
        cs = slice(cc * cw, (cc + 1) * cw)
        gs = slice(D + cc * cw, D + (cc + 1) * cw)
        g_attn = jax.nn.sigmoid(jnp.dot(xn, wg_ref[:, cs], preferred_element_type=F32) + bg_ref[:, cs])
        g_ssm = jax.nn.sigmoid(jnp.dot(xn, wg_ref[:, gs], preferred_element_type=F32) + bg_ref[:, gs])
        mixed = (g_attn * jnp.dot(attn, wao_ref[:, cs], preferred_element_type=F32)
                 + g_ssm * jnp.dot(ssm, wso_ref[:, cs], preferred_element_type=F32))
        o_ref[...] += jnp.dot(mixed.astype(BF16), wout_ref[cs, :], preferred_element_type=F32)
    xn2 = _rms(o_ref[...], g2_ref[...]).astype(BF16)
    fw = 4 * LANES
    for fc in range(wup_ref.shape[-1] // fw):
        fs = slice(fc * fw, (fc + 1) * fw)
        hdn = jnp.maximum(jnp.dot(xn2, wup_ref[:, fs], preferred_element_type=F32), 0.0)
        o_ref[...] += jnp.dot((hdn * hdn).astype(BF16), wdn_ref[fs, :], preferred_element_type=F32)


def _mix_ffn_kernel(layer, n_prompt, xp_ref, ap_ref, sp_ref, xs_ref, as_ref, ss_ref,
                    w_in_hbm, wao_hbm, wso_hbm, wout_hbm, wup_hbm, wdn_hbm, g1_ref, bg_ref, g2_ref,
                    op_ref, os_ref, wg_ref, wao_ref, wso_ref, wout_ref, wup_ref, wdn_ref, stage, sem):
    i = pl.program_id(0)
    cr, cc = stage.shape[1:]

    @pl.when(i == 0)
    def _():
        chunks = []
        chunks += _weight_chunks(w_in_hbm.at[layer], wg_ref, wg_ref.shape[0], wg_ref.shape[1], cr, cc,
                                 src_col0=w_in_hbm.shape[-1] - wg_ref.shape[1])
        for hbm, ref in ((wao_hbm, wao_ref), (wso_hbm, wso_ref), (wout_hbm, wout_ref), (wup_hbm, wup_ref),
                         (wdn_hbm, wdn_ref)):
            chunks += _weight_chunks(hbm.at[layer], ref, ref.shape[0], ref.shape[1], cr, cc)
        _stage_weights(chunks, stage, sem)

    w_refs = (g1_ref, wg_ref, bg_ref, wao_ref, wso_ref, wout_ref, g2_ref, wup_ref, wdn_ref)

    @pl.when(i < n_prompt)
    def _():
        _mix_ffn_rows(xp_ref, ap_ref, sp_ref, op_ref, *w_refs)

    @pl.when(i >= n_prompt)
    def _():
        _mix_ffn_rows(xs_ref, as_ref, ss_ref, os_ref, *w_refs)


def _mix_ffn(layer, xp, attn_p, ssm_p, xs, attn_s, ssm_s, w_in, wao, wso, wout, wup, wdn, g1, bg, g2):
    rp, D = xp.shape
    rs = xs.shape[0]
    tp = min(512, rp)
    ts = min(512, rs)
    assert rp % tp == 0 and rs % ts == 0
    n_p, n_s = rp // tp, rs // ts
    prow = lambda last: pl.BlockSpec((tp, last), lambda i: (jnp.minimum(i, n_p - 1), 0))
    srow = lambda last: pl.BlockSpec((ts, last), lambda i: (jnp.maximum(i - n_p, 0), 0))
    hbm = pl.BlockSpec(memory_space=pl.ANY)
    resident = lambda a: pltpu.VMEM(a.shape[1:], BF16)
    return pl.pallas_call(
        functools.partial(_mix_ffn_kernel, layer, n_p),
        grid=(n_p + n_s,),
        in_specs=[prow(D), prow(attn_p.shape[-1]), prow(ssm_p.shape[-1]),
                  srow(D), srow(attn_s.shape[-1]), srow(ssm_s.shape[-1])]
                 + [hbm] * 6 + [_layer_spec(a, layer) for a in (g1, bg, g2)],
        out_specs=[prow(D), srow(D)],
        out_shape=[jax.ShapeDtypeStruct((rp, D), F32), jax.ShapeDtypeStruct((rs, D), F32)],
        scratch_shapes=[pltpu.VMEM((D, bg.shape[-1]), BF16), resident(wao), resident(wso), resident(wout),
                        resident(wup), resident(wdn),
                        pltpu.VMEM((WEIGHT_SLOTS, 2 * LANES, D), F32), pltpu.SemaphoreType.DMA((WEIGHT_SLOTS,))],
        compiler_params=pltpu.CompilerParams(dimension_semantics=("arbitrary",), vmem_limit_bytes=VMEM_LIMIT),
        name="mix_ffn",
    )(xp, attn_p, ssm_p, xs, attn_s, ssm_s, w_in, wao, wso, wout, wup, wdn, g1, bg, g2)


def _perm_matrix(nb, tt):
    p = np.zeros((nb * tt, nb * tt), np.float32)
    t, b = np.meshgrid(np.arange(tt), np.arange(nb), indexing="ij")
    p[(t * nb + b).ravel(), (b * tt + t).ravel()] = 1.0
    return jnp.asarray(p, BF16)


def _seg_matrix(width):
    i = np.arange(width) // HEAD_DIM
    return jnp.asarray((i[:, None] == i[None, :]).astype(np.float32) / HEAD_DIM, BF16)


def kernel(x_prompt, x_sample, cache_k, cache_v, state_ssm_re, state_ssm_im, norm1_g, w_in, b_gate, q_norm_g,
           k_norm_g, attn_sinks, lam_re, lam_im, log_step, b_re, b_im, c_re, c_im, d_skip, w_glu, b_glu,
           w_attn_o, w_ssm_o, w_out, norm2_g, w_up, w_down):
    B, T, D = x_prompt.shape
    DB, DS, _ = x_sample.shape
    L = w_in.shape[0]
    aw = N_HEADS * HEAD_DIM
    kw = N_KV_HEADS * HEAD_DIM
    sw = d_skip.shape[-1]
    G, P = lam_re.shape[1:]
    past = cache_k.shape[2]
    assert past == WINDOW and T % WINDOW == 0 and (B * WINDOW) % ROW_TILE == 0 and (DB * DS) % ROW_TILE == 0

    ar, ai, bx, cm = _ssm_prep(lam_re, lam_im, log_step, b_re, b_im, c_re, c_im)
    perm = _perm_matrix(B, WINDOW)
    seg = _seg_matrix(aw)
    rows = lambda a: a.reshape(L, 1, -1)
    g1 = rows(norm1_g)
    mixer_w = (w_in, w_glu, g1, rows(jnp.tile(q_norm_g, (1, N_HEADS))), rows(jnp.tile(k_norm_g, (1, N_KV_HEADS))),
               ar, ai, bx, cm, rows(d_skip), rows(b_glu))
    ffn_w = (w_in, w_attn_o, w_ssm_o, w_out, w_up, w_down, g1, rows(b_gate), rows(norm2_g))
    kv_t = lambda a: a.transpose(0, 1, 3, 4, 2).reshape(L, DB, kw, past)
    st_t = lambda a: a.transpose(0, 2, 3, 1).reshape(L, G * P, DB)
    ckt, cvt, h0rt, h0it = kv_t(cache_k), kv_t(cache_v), st_t(state_ssm_re), st_t(state_ssm_im)

    yp = x_prompt.reshape(B * T, D)
    ys = x_sample.transpose(1, 0, 2).reshape(DS * DB, D)
    outs = [[] for _ in range(6)]
    kcat = vcat = None
    for l in range(L):
        attn_p, ssm_p, kp, vp, hrp, hip = _mixer_prompt(l, yp.reshape(B, T, D), attn_sinks, seg, perm, *mixer_w)
        attn_s, ssm_s, kcat, vcat, hrs, his = _mixer_sample(l, ys, ckt, cvt, h0rt, h0it, kcat, vcat, attn_sinks, seg,
                                                            *mixer_w)
        yp, ys = _mix_ffn(l, yp, attn_p.reshape(B * T, aw), ssm_p.reshape(B * T, sw), ys, attn_s, ssm_s, *ffn_w)
        for o, v in zip(outs, (kp, vp, hrp, hip, hrs, his)):
            o.append(v)

    kp, vp, hrp, hip, hrs, his = (jnp.stack(o) for o in outs)
    kv_back = lambda a: a.reshape(a.shape[:2] + (N_KV_HEADS, HEAD_DIM, a.shape[-1])).transpose(0, 1, 4, 2, 3)
    st_back = lambda a: a.reshape(L, G, P, DB).transpose(0, 3, 1, 2)
    y_sample = ys.reshape(DS, DB, D).transpose(1, 0, 2)
    return (yp.reshape(B, T, D), y_sample, kv_back(kp), kv_back(vp), hrp.reshape(L, B, G, P), hip.reshape(L, B, G, P),
            kv_back(kcat), kv_back(vcat), st_back(hrs), st_back(his))
```

```python
import functools
import math

import numpy as np
import jax
import jax.numpy as jnp
from jax import lax
from jax.experimental import pallas as pl
from jax.experimental.pallas import tpu as pltpu

F32 = jnp.float32
BF16 = jnp.bfloat16

HEAD_DIM = 64
N_HEADS = 8
N_KV_HEADS = 2
Q_GROUP = N_HEADS // N_KV_HEADS
WINDOW = 128
GROUP_CH = 16
SSM_STATE = 64
EPS = 1e-6
NEG_INF = -1e30
LOG2E = math.log2(math.e)
LANES = 128
SUBLANES = 8
ROW_TILE = 256
VMEM_LIMIT = 60 * 1024 * 1024

WEIGHT_SLOTS = 4
FFN_WEIGHT_SLOTS = 8
SLOPES = tuple(2.0 ** (-8.0 * (h + 1) / N_HEADS) for h in range(N_HEADS))


def _rms(x, g):
    return x * lax.rsqrt(jnp.mean(x * x, axis=-1, keepdims=True) + EPS) * g


def _seg_rms(x, seg, g):
    ms = jnp.dot((x * x).astype(BF16), seg, preferred_element_type=F32)
    return x * lax.rsqrt(ms + EPS) * g


def _gelu_tanh(y):
    return 0.5 * y * (1.0 + jnp.tanh(math.sqrt(2.0 / math.pi) * (y + 0.044715 * (y * y * y))))


def _const_spec(a):
    n = a.ndim
    return pl.BlockSpec(a.shape, lambda *_: (0,) * n, pipeline_mode=pl.Buffered(1))


def _layer_spec(a, layer):
    n = a.ndim - 1
    return pl.BlockSpec((None,) + a.shape[1:], lambda *_: (layer,) + (0,) * n, pipeline_mode=pl.Buffered(1))


def _stage_weights(chunks, stage, sem):
    slots = stage.shape[0]

    def copy(k):
        src, _, cols = chunks[k]
        return pltpu.make_async_copy(src, stage.at[k % slots, :, pl.ds(0, cols)], sem.at[k % slots])

    for k in range(min(slots - 1, len(chunks))):
        copy(k).start()
    for k, (_, dst, cols) in enumerate(chunks):
        if k + slots - 1 < len(chunks):
            copy(k + slots - 1).start()
        copy(k).wait()
        dst[...] = stage[k % slots, :, :cols].astype(BF16)


def _weight_chunks(src, dst, rows, cols, chunk_rows, chunk_cols, src_col0=0):
    return [(src.at[pl.ds(r, chunk_rows), pl.ds(src_col0 + c, chunk_cols)],
             dst.at[pl.ds(r, chunk_rows), pl.ds(c, chunk_cols)], chunk_cols)
            for r in range(0, rows, chunk_rows) for c in range(0, cols, chunk_cols)]


def _ssm_prep_kernel(lr_ref, li_ref, ls_ref, btr_ref, bti_ref, ctr_ref, cti_ref,
                     ar_ref, ai_ref, bx_ref, cm_ref):
    lr = lr_ref[0]
    li = li_ref[0]
    step = jnp.exp(ls_ref[0])
    mag = jnp.exp(lr * step)
    ar = mag * jnp.cos(li * step)
    ai = mag * jnp.sin(li * step)
    den = lr * lr + li * li
    cr = ((ar - 1.0) * lr + ai * li) / den
    ci = (ai * lr - (ar - 1.0) * li) / den
    ar_ref[0] = ar
    ai_ref[0] = ai
    nj = btr_ref.shape[1]
    gp = SUBLANES * SSM_STATE
    rows_b = lax.broadcasted_iota(jnp.int32, (LANES, gp), 0) // GROUP_CH
    cols_b = lax.broadcasted_iota(jnp.int32, (LANES, gp), 1) // SSM_STATE
    mask_b = rows_b == cols_b
    rows_c = lax.broadcasted_iota(jnp.int32, (gp, LANES), 0) // SSM_STATE
    cols_c = lax.broadcasted_iota(jnp.int32, (gp, LANES), 1) // GROUP_CH
    mask_c = rows_c == cols_c
    for j in range(nj):
        crj = cr[:, j * gp:(j + 1) * gp]
        cij = ci[:, j * gp:(j + 1) * gp]
        b_re = btr_ref[0, j]
        b_im = bti_ref[0, j]
        bb_re = jnp.where(mask_b, crj * b_re - cij * b_im, 0.0)
        bb_im = jnp.where(mask_b, crj * b_im + cij * b_re, 0.0)
        bx_ref[0, j, :, :gp] = bb_re.astype(BF16)
        bx_ref[0, j, :, gp:] = bb_im.astype(BF16)
        cm_ref[0, j, :gp, :] = jnp.where(mask_c, ctr_ref[0, j], 0.0).astype(BF16)
        cm_ref[0, j, gp:, :] = jnp.where(mask_c, -cti_ref[0, j], 0.0).astype(BF16)


def _ssm_prep(lam_re, lam_im, log_step, b_re, b_im, c_re, c_im):
    L, G, P = lam_re.shape
    GC = b_re.shape[-1]
    gpb = LANES // GC
    nj = G // gpb
    gp = gpb * P
    flat = lambda a: a.reshape(L, 1, G * P)
    ls = jnp.repeat(log_step, P, axis=-1).reshape(L, 1, G * P)

    def b_tiles(b):
        t = b.reshape(L, nj, gpb, P, GC).transpose(0, 1, 4, 2, 3).reshape(L, nj, 1, GC, gp)
        return jnp.broadcast_to(t, (L, nj, gpb, GC, gp)).reshape(L, nj, LANES, gp)

    def c_tiles(c):
        t = c.reshape(L, nj, gpb, GC, P).transpose(0, 1, 2, 4, 3).reshape(L, nj, gp, 1, GC)
        return jnp.broadcast_to(t, (L, nj, gp, gpb, GC)).reshape(L, nj, gp, LANES)

    vec = pl.BlockSpec((1, 1, G * P), lambda l: (l, 0, 0))
    bt = pl.BlockSpec((1, nj, LANES, gp), lambda l: (l, 0, 0, 0))
    ct = pl.BlockSpec((1, nj, gp, LANES), lambda l: (l, 0, 0, 0))
    return pl.pallas_call(
        _ssm_prep_kernel,
        grid=(L,),
        in_specs=[vec, vec, vec, bt, bt, ct, ct],
        out_specs=[vec, vec,
                   pl.BlockSpec((1, nj, LANES, 2 * gp), lambda l: (l, 0, 0, 0)),
                   pl.BlockSpec((1, nj, 2 * gp, LANES), lambda l: (l, 0, 0, 0))],
        out_shape=[jax.ShapeDtypeStruct((L, 1, G * P), F32), jax.ShapeDtypeStruct((L, 1, G * P), F32),
                   jax.ShapeDtypeStruct((L, nj, LANES, 2 * gp), BF16),
                   jax.ShapeDtypeStruct((L, nj, 2 * gp, LANES), BF16)],
        name="ssm_prep",
    )(flat(lam_re), flat(lam_im), ls, b_tiles(b_re), b_tiles(b_im), c_tiles(c_re), c_tiles(c_im))


def _project_rows(x, g1_ref, w_ref, seg_ref, qg_ref, kg_ref):
    aw = N_HEADS * HEAD_DIM
    kw = N_KV_HEADS * HEAD_DIM
    xn = _rms(x, g1_ref[...]).astype(BF16)
    z = jnp.dot(xn, w_ref[...], preferred_element_type=F32)
    q = _seg_rms(z[:, :aw], seg_ref[...], qg_ref[...]) * (HEAD_DIM ** -0.5 * LOG2E)
    k = _seg_rms(z[:, aw:aw + kw], seg_ref[:kw, :kw], kg_ref[...])
    v = z[:, aw + kw:aw + 2 * kw]
    u = z[:, aw + 2 * kw:]
    lower = (lax.broadcasted_iota(jnp.int32, q.shape, 1) % LANES) < HEAD_DIM
    q_lo = jnp.where(lower, q, 0.0)
    q_hi = jnp.where(lower, 0.0, q)
    return q_lo, q_hi, k, v, u


def _stage_mixer_weights(layer, w_in_hbm, wglu_hbm, w_ref, wglu_ref, stage, sem):
    _stage_weights(
        _weight_chunks(w_in_hbm.at[layer], w_ref, w_ref.shape[0], w_ref.shape[1], stage.shape[1], w_ref.shape[1])
        + _weight_chunks(wglu_hbm.at[layer], wglu_ref, wglu_ref.shape[0], wglu_ref.shape[1], stage.shape[1],
                         wglu_ref.shape[1]),
        stage, sem)


def _mixer_weight_scratch(D, in_cols, sw):
    return [pltpu.VMEM((D, in_cols), BF16), pltpu.VMEM((sw, sw), BF16),
            pltpu.VMEM((WEIGHT_SLOTS, LANES, in_cols), F32), pltpu.SemaphoreType.DMA((WEIGHT_SLOTS,))]


def _ssm_readout(y, u, d_ref, wglu_ref, bglu_ref):
    y = _gelu_tanh(y + d_ref[...] * u)
    gate = jnp.dot(y.astype(BF16), wglu_ref[...], preferred_element_type=F32) + bglu_ref[...]
    return y * jax.nn.sigmoid(gate)


def _softmax_pv(s, sink, v):
    m = jnp.maximum(jnp.max(s, axis=-1, keepdims=True), sink)
    e = jnp.exp2(s - m)
    den = jnp.sum(e, axis=-1, keepdims=True) + jnp.exp2(sink - m)
    return jnp.dot(e.astype(BF16), v, preferred_element_type=F32) * (1.0 / den)


def _mixer_prompt_kernel(layer, sink_ref, x_ref, seg_ref, perm_ref, w_in_hbm, wglu_hbm, g1_ref, qg_ref, kg_ref,
                         ar_ref, ai_ref, bx_ref, cm_ref, d_ref, bglu_ref,
                         attn_ref, ssm_ref, kout_ref, vout_ref, hr_ref, hi_ref,
                         w_ref, wglu_ref, stage, sem, q_s, k_s, v_s, u_s, ub_s, y_s, *x_s):
    c = pl.program_id(0)
    nb, tt, _ = x_ref.shape
    rows = nb * tt
    n_tiles = rows // ROW_TILE
    bpt = ROW_TILE // tt
    n_state = ar_ref.shape[-1]
    nj = bx_ref.shape[0]
    gp = n_state // nj
    xr_s, xi_s = x_s[:nj], x_s[nj:]
    kw = kout_ref.shape[1]

    @pl.when(c == 0)
    def _():
        _stage_mixer_weights(layer, w_in_hbm, wglu_hbm, w_ref, wglu_ref, stage, sem)
        k_s[:, :, :tt, :] = jnp.zeros((2, nb, tt, kw), BF16)
        v_s[:, :, :tt, :] = jnp.zeros((2, nb, tt, kw), BF16)
        hr_ref[...] = jnp.zeros(hr_ref.shape, F32)
        hi_ref[...] = jnp.zeros(hi_ref.shape, F32)

    def project(r, carry):
        x = x_ref[pl.ds(r * bpt, bpt)].reshape(ROW_TILE, x_ref.shape[-1])
        q_lo, q_hi, k, v, u = _project_rows(x, g1_ref, w_ref, seg_ref, qg_ref, kg_ref)
        q_s[0, pl.ds(r * bpt, bpt)] = q_lo.astype(BF16).reshape(bpt, tt, -1)
        q_s[1, pl.ds(r * bpt, bpt)] = q_hi.astype(BF16).reshape(bpt, tt, -1)
        k_s[0, pl.ds(r * bpt, bpt), tt:, :] = k.astype(BF16).reshape(bpt, tt, -1)
        v_s[0, pl.ds(r * bpt, bpt), tt:, :] = v.astype(BF16).reshape(bpt, tt, -1)
        k_s[1, pl.ds(r * bpt, bpt), tt:, :] = pltpu.roll(k, HEAD_DIM, 1).astype(BF16).reshape(bpt, tt, -1)
        v_s[1, pl.ds(r * bpt, bpt), tt:, :] = pltpu.roll(v, HEAD_DIM, 1).astype(BF16).reshape(bpt, tt, -1)
        for bb in range(bpt):
            kout_ref[r * bpt + bb] = k[bb * tt:(bb + 1) * tt, :].T
            vout_ref[r * bpt + bb] = v[bb * tt:(bb + 1) * tt, :].T
        u_s[pl.ds(r * ROW_TILE, ROW_TILE), :] = u
        ub_s[pl.ds(r * ROW_TILE, ROW_TILE), :] = u.astype(BF16)
        return carry

    lax.fori_loop(0, n_tiles, project, 0, unroll=2)

    t_idx = lax.broadcasted_iota(jnp.int32, (tt, 2 * tt), 0)
    j_idx = lax.broadcasted_iota(jnp.int32, (tt, 2 * tt), 1)
    dist = t_idx - j_idx + tt
    valid = (dist >= 0) & (dist <= WINDOW) & (j_idx >= jnp.where(c > 0, 0, tt))
    key_pos = lax.broadcasted_iota(jnp.int32, (1, 2 * tt), 1).astype(F32)
    row_pos = (lax.broadcasted_iota(jnp.int32, (tt, 1), 0) + tt).astype(F32)
    lower = lax.broadcasted_iota(jnp.int32, (tt, LANES), 1) < HEAD_DIM

    def attend(b):
        for p in range(N_HEADS // 2):
            kv = (2 * p) // Q_GROUP
            outs = []
            for e in range(2):
                h = 2 * p + e
                swap = 0 if kv == e else 1
                qm = q_s[e, b, :, p * LANES:(p + 1) * LANES]
                s = lax.dot_general(qm, k_s[swap, b], (((1,), (1,)), ((), ())), preferred_element_type=F32)
                slope = SLOPES[h] * LOG2E
                s = jnp.where(valid, s + slope * key_pos, NEG_INF)
                sink = sink_ref[layer, h] * LOG2E + slope * row_pos
                outs.append(_softmax_pv(s, sink, v_s[swap, b]))
            attn_ref[b, :, p * LANES:(p + 1) * LANES] = jnp.where(lower, outs[0], outs[1]).astype(BF16)

    def ssm_in_and_attend(r, carry):
        rs = pl.ds(r * ROW_TILE, ROW_TILE)
        up = jnp.dot(perm_ref[rs, :], ub_s[...], preferred_element_type=F32).astype(BF16)
        for j in range(nj):
            xj = jnp.dot(up[:, j * LANES:(j + 1) * LANES], bx_ref[j], preferred_element_type=F32)
            xr_s[j][rs, :] = xj[:, :gp]
            xi_s[j][rs, :] = xj[:, gp:]
        for bb in range(bpt):
            attend(r * bpt + bb)
        return carry

    lax.fori_loop(0, n_tiles, ssm_in_and_attend, 0, unroll=2)

    k_s[:, :, :tt, :] = k_s[:, :, tt:, :]
    v_s[:, :, :tt, :] = v_s[:, :, tt:, :]

    def ssm_out_tile(j, r):
        rs = pl.ds(r * ROW_TILE, ROW_TILE)
        y_s[j, rs, :] = jnp.dot(xr_s[j][rs, :].astype(BF16), cm_ref[j, :gp, :], preferred_element_type=F32)
        y_s[j, rs, :] = y_s[j, rs, :] + jnp.dot(xi_s[j][rs, :].astype(BF16), cm_ref[j, gp:, :],
                                                preferred_element_type=F32)

    for lb in range(nj):
        sl = slice(lb * gp, (lb + 1) * gp)
        a_re = jnp.broadcast_to(ar_ref[:, sl], (nb, gp))
        a_im = jnp.broadcast_to(ai_ref[:, sl], (nb, gp))

        def scan_tile(r, h, lb=lb, a_re=a_re, a_im=a_im):
            if lb > 0:
                ssm_out_tile(lb - 1, r)
            h_re, h_im = h
            for step in range(ROW_TILE // nb):
                row = pl.ds(pl.multiple_of(r * ROW_TILE + step * nb, nb), nb)
                n_re = a_re * h_re - a_im * h_im + xr_s[lb][row, :]
                n_im = a_re * h_im + a_im * h_re + xi_s[lb][row, :]
                xr_s[lb][row, :] = n_re
                xi_s[lb][row, :] = n_im
                h_re, h_im = n_re, n_im
            return h_re, h_im

        h_re, h_im = lax.fori_loop(0, n_tiles, scan_tile, (hr_ref[:, sl], hi_ref[:, sl]), unroll=True)
        hr_ref[:, sl] = h_re
        hi_ref[:, sl] = h_im

    def ssm_out_last(r, carry):
        ssm_out_tile(nj - 1, r)
        return carry

    lax.fori_loop(0, n_tiles, ssm_out_last, 0, unroll=True)

    def readout(b, carry):
        y = jnp.concatenate([y_s[j, pl.ds(b, tt, stride=nb), :] for j in range(nj)], axis=1)
        u = u_s[pl.ds(pl.multiple_of(b * tt, tt), tt), :]
        ssm_ref[b] = _ssm_readout(y, u, d_ref, wglu_ref, bglu_ref).astype(BF16)
        return carry

    lax.fori_loop(0, nb, readout, 0, unroll=2)


def _mixer_prompt(layer, x, sinks, seg, perm, w_in, wglu, g1, qg, kg, ar, ai, bx, cm, d, bglu):
    nb, T, D = x.shape
    tt = WINDOW
    rows = nb * tt
    aw = N_HEADS * HEAD_DIM
    kw = N_KV_HEADS * HEAD_DIM
    sw = d.shape[-1]
    n_state = ar.shape[-1]
    nj = bx.shape[1]
    chunk = lambda last: pl.BlockSpec((nb, tt, last), lambda c: (0, c, 0))
    whole = pl.BlockSpec((nb, kw, tt), lambda c: (0, 0, 0))
    state = pl.BlockSpec((nb, n_state), lambda c: (0, 0))
    hbm = pl.BlockSpec(memory_space=pl.ANY)
    stacked = (g1, qg, kg, ar, ai, bx, cm, d, bglu)
    return pl.pallas_call(
        functools.partial(_mixer_prompt_kernel, layer),
        grid=(T // tt,),
        in_specs=[pl.BlockSpec(memory_space=pltpu.SMEM), chunk(D), _const_spec(seg), _const_spec(perm), hbm, hbm]
                 + [_layer_spec(a, layer) for a in stacked],
        out_specs=[chunk(aw), chunk(sw), whole, whole, state, state],
        out_shape=[jax.ShapeDtypeStruct((nb, T, aw), BF16), jax.ShapeDtypeStruct((nb, T, sw), BF16),
                   jax.ShapeDtypeStruct((nb, kw, tt), F32), jax.ShapeDtypeStruct((nb, kw, tt), F32),
                   jax.ShapeDtypeStruct((nb, n_state), F32), jax.ShapeDtypeStruct((nb, n_state), F32)],
        scratch_shapes=_mixer_weight_scratch(D, aw + 2 * kw + sw, sw)
                       + [pltpu.VMEM((2, nb, tt, aw), BF16),
                          pltpu.VMEM((2, nb, 2 * tt, kw), BF16), pltpu.VMEM((2, nb, 2 * tt, kw), BF16),
                          pltpu.VMEM((rows, sw), F32), pltpu.VMEM((rows, sw), BF16),
                          pltpu.VMEM((nj, rows, LANES), F32)]
                       + [pltpu.VMEM((rows, n_state // nj), F32)] * (2 * nj),
        compiler_params=pltpu.CompilerParams(dimension_semantics=("arbitrary",), vmem_limit_bytes=VMEM_LIMIT),
        name="mixer_prompt",
    )(sinks, x, seg, perm, w_in, wglu, *stacked)


def _mixer_sample_kernel(layer, sink_ref, x_ref, ckt_ref, cvt_ref, h0rt_ref, h0it_ref, *refs):
    if layer:
        prevk_ref, prevv_ref, *refs = refs
    (seg_ref, w_in_hbm, wglu_hbm, g1_ref, qg_ref, kg_ref, ar_ref, ai_ref, bx_ref, cm_ref, d_ref, bglu_ref,
     attn_ref, ssm_ref, kcat_ref, vcat_ref, hrt_ref, hit_ref,
     w_ref, wglu_ref, stage, sem, q_s, o_s, u_s, xr_s, xi_s, knew_s, vnew_s,
     ck_ref, cv_ref, kroll_s, vroll_s, h0r_ref, h0i_ref, hr_ref, hi_ref) = refs
    i = pl.program_id(0)
    rows = x_ref.shape[0]
    nbatch = h0rt_ref.shape[1]
    ts = rows // nbatch
    gb, kw, past = ckt_ref.shape
    n_tiles = rows // ROW_TILE
    n_state = ar_ref.shape[-1]
    nj = bx_ref.shape[0]
    gp = n_state // nj

    @pl.when(i == 0)
    def _():
        _stage_mixer_weights(layer, w_in_hbm, wglu_hbm, w_ref, wglu_ref, stage, sem)

        def project(r, carry):
            rs = pl.ds(r * ROW_TILE, ROW_TILE)
            q_lo, q_hi, k, v, u = _project_rows(x_ref[rs, :], g1_ref, w_ref, seg_ref, qg_ref, kg_ref)
            q_s[0, rs, :] = q_lo
            q_s[1, rs, :] = q_hi
            knew_s[rs, :] = k
            vnew_s[rs, :] = v
            u_s[rs, :] = u
            return carry

        lax.fori_loop(0, n_tiles, project, 0)

    n_q = ts * Q_GROUP * gb
    n_keys = gb * past + ts * gb
    r_idx = lax.broadcasted_iota(jnp.int32, (n_q, n_keys), 0)
    c_idx = lax.broadcasted_iota(jnp.int32, (n_q, n_keys), 1)
    q_t = r_idx // (Q_GROUP * gb)
    q_b = r_idx % gb
    is_cache = c_idx < gb * past
    key_b = jnp.where(is_cache, c_idx // past, (c_idx - gb * past) % gb)
    dist = jnp.where(is_cache, past + q_t - c_idx % past, q_t - (c_idx - gb * past) // gb)
    valid = (key_b == q_b) & (dist >= 0) & (dist <= WINDOW)
    dist_f = dist.astype(F32)
    hh = (lax.broadcasted_iota(jnp.int32, (n_q, 1), 0) // gb) % Q_GROUP
    lower = lax.broadcasted_iota(jnp.int32, (gb, LANES), 1) < HEAD_DIM
    base = pl.multiple_of(i * gb, gb)

    for cat_ref, ct_ref, c_ref, roll_s, new_s in ((kcat_ref, ckt_ref, ck_ref, kroll_s, knew_s),
                                                  (vcat_ref, cvt_ref, cv_ref, vroll_s, vnew_s)):
        for b in range(gb):
            c_ref[b * past:(b + 1) * past, :] = ct_ref[b].T
        for b in range(gb):
            roll_s[b * past:(b + 1) * past - ts, :] = c_ref[b * past + ts:(b + 1) * past, :]
        for t in range(ts):
            roll_s[pl.ds(past - ts + t, gb, stride=past), :] = new_s[pl.ds(base + t * nbatch, gb), :]
        for b in range(gb):
            cat_ref[layer, b] = roll_s[b * past:(b + 1) * past, :].T
    if layer:
        kcat_ref[:layer] = prevk_ref[...]
        vcat_ref[:layer] = prevv_ref[...]

    k_all = jnp.concatenate(
        [ck_ref[...]] + [knew_s[pl.ds(base + t * nbatch, gb), :] for t in range(ts)], axis=0).astype(BF16)
    v_all = jnp.concatenate(
        [cv_ref[...]] + [vnew_s[pl.ds(base + t * nbatch, gb), :] for t in range(ts)], axis=0).astype(BF16)
    outs = []
    for kvh in range(N_KV_HEADS):
        slope = jnp.zeros((n_q, 1), F32)
        sink = jnp.zeros((n_q, 1), F32)
        for g in range(Q_GROUP):
            h = kvh * Q_GROUP + g
            slope = jnp.where(hh == g, SLOPES[h] * LOG2E, slope)
            sink = jnp.where(hh == g, sink_ref[layer, h] * LOG2E, sink)
        pieces = []
        for t in range(ts):
            for g in range(Q_GROUP):
                h = kvh * Q_GROUP + g
                piece = q_s[h % 2, pl.ds(base + t * nbatch, gb), (h // 2) * LANES:(h // 2 + 1) * LANES]
                pieces.append(piece if h % 2 == kvh else pltpu.roll(piece, HEAD_DIM, 1))
        qm = jnp.concatenate(pieces, axis=0).astype(BF16)
        s = lax.dot_general(qm, k_all, (((1,), (1,)), ((), ())), preferred_element_type=F32)
        s = jnp.where(valid, s - slope * dist_f, NEG_INF)
        outs.append(_softmax_pv(s, sink, v_all))
    for t in range(ts):
        for p in range(N_HEADS // 2):
            kvh = (2 * p) // Q_GROUP
            r0 = (t * Q_GROUP + 2 * p - kvh * Q_GROUP) * gb
            lo = outs[kvh][r0:r0 + gb]
            hi = outs[kvh][r0 + gb:r0 + 2 * gb]
            if kvh == 1:
                lo = pltpu.roll(lo, HEAD_DIM, 1)
            else:
                hi = pltpu.roll(hi, HEAD_DIM, 1)
            o_s[pl.ds(base + t * nbatch, gb), p * LANES:(p + 1) * LANES] = jnp.where(lower, lo, hi)

    @pl.when(i == pl.num_programs(0) - 1)
    def _():
        attn_ref[...] = o_s[...].astype(BF16)

        def ssm_in(r, carry):
            rs = pl.ds(r * ROW_TILE, ROW_TILE)
            ub = u_s[rs, :].astype(BF16)
            for j in range(nj):
                xj = jnp.dot(ub[:, j * LANES:(j + 1) * LANES], bx_ref[j], preferred_element_type=F32)
                xr_s[rs, j * gp:(j + 1) * gp] = xj[:, :gp]
                xi_s[rs, j * gp:(j + 1) * gp] = xj[:, gp:]
            return carry

        lax.fori_loop(0, n_tiles, ssm_in, 0)

        for blk in range(n_state // LANES):
            bs = slice(blk * LANES, (blk + 1) * LANES)
            h0r_ref[:, bs] = h0rt_ref[bs, :].T
            h0i_ref[:, bs] = h0it_ref[bs, :].T

        lane_blk = 4 * LANES
        for lb in range(n_state // lane_blk):
            sl = slice(lb * lane_blk, (lb + 1) * lane_blk)
            a_re = jnp.broadcast_to(ar_ref[:, sl], (SUBLANES, lane_blk))
            a_im = jnp.broadcast_to(ai_ref[:, sl], (SUBLANES, lane_blk))

            def scan(bt, carry):
                b0 = pl.multiple_of(bt * SUBLANES, SUBLANES)
                h_re = h0r_ref[pl.ds(b0, SUBLANES), sl]
                h_im = h0i_ref[pl.ds(b0, SUBLANES), sl]
                for t in range(ts):
                    rs = pl.ds(b0 + t * nbatch, SUBLANES)
                    n_re = a_re * h_re - a_im * h_im + xr_s[rs, sl]
                    n_im = a_re * h_im + a_im * h_re + xi_s[rs, sl]
                    xr_s[rs, sl] = n_re
                    xi_s[rs, sl] = n_im
                    h_re, h_im = n_re, n_im
                hr_ref[pl.ds(b0, SUBLANES), sl] = h_re
                hi_ref[pl.ds(b0, SUBLANES), sl] = h_im
                return carry

            lax.fori_loop(0, nbatch // SUBLANES, scan, 0)

        for blk in range(n_state // LANES):
            bs = slice(blk * LANES, (blk + 1) * LANES)
            hrt_ref[bs, :] = hr_ref[:, bs].T
            hit_ref[bs, :] = hi_ref[:, bs].T

        def ssm_out(r, carry):
            rs = pl.ds(r * ROW_TILE, ROW_TILE)
            ys = []
            for j in range(nj):
                h_cat = jnp.concatenate([xr_s[rs, j * gp:(j + 1) * gp], xi_s[rs, j * gp:(j + 1) * gp]],
                                        axis=1).astype(BF16)
                ys.append(jnp.dot(h_cat, cm_ref[j], preferred_element_type=F32))
            y = jnp.concatenate(ys, axis=1)
            ssm_ref[rs, :] = _ssm_readout(y, u_s[rs, :], d_ref, wglu_ref, bglu_ref).astype(BF16)
            return carry

        lax.fori_loop(0, n_tiles, ssm_out, 0)


def _mixer_sample(layer, x, ckt, cvt, h0rt, h0it, prev_k, prev_v, sinks, seg, w_in, wglu,
                  g1, qg, kg, ar, ai, bx, cm, d, bglu):
    rows, D = x.shape
    _, nbatch, kw, past = ckt.shape
    aw = N_HEADS * HEAD_DIM
    sw = d.shape[-1]
    n_state = ar.shape[-1]
    gb = SUBLANES
    assert nbatch % LANES == 0 and past == LANES and kw == LANES
    cache = pl.BlockSpec((None, gb, kw, past), lambda i: (layer, i, 0, 0))
    h0 = pl.BlockSpec((None, n_state, nbatch), lambda i: (layer, 0, 0))
    prev = pl.BlockSpec((layer, gb, kw, past), lambda i: (0, i, 0, 0))
    out2 = lambda last: pl.BlockSpec((rows, last), lambda i: (0, 0))
    cat = pl.BlockSpec((layer + 1, gb, kw, past), lambda i: (0, i, 0, 0))
    state = pl.BlockSpec((n_state, nbatch), lambda i: (0, 0))
    hbm = pl.BlockSpec(memory_space=pl.ANY)
    stacked = (g1, qg, kg, ar, ai, bx, cm, d, bglu)
    prevs = (prev_k, prev_v) if layer else ()
    return pl.pallas_call(
        functools.partial(_mixer_sample_kernel, layer),
        grid=(nbatch // gb,),
        in_specs=[pl.BlockSpec(memory_space=pltpu.SMEM), pl.BlockSpec(x.shape, lambda i: (0, 0)), cache, cache, h0, h0]
                 + [prev] * len(prevs) + [_const_spec(seg), hbm, hbm] + [_layer_spec(a, layer) for a in stacked],
        out_specs=[out2(aw), out2(sw), cat, cat, state, state],
        out_shape=[jax.ShapeDtypeStruct((rows, aw), BF16), jax.ShapeDtypeStruct((rows, sw), BF16),
                   jax.ShapeDtypeStruct((layer + 1, nbatch, kw, past), F32),
                   jax.ShapeDtypeStruct((layer + 1, nbatch, kw, past), F32),
                   jax.ShapeDtypeStruct((n_state, nbatch), F32), jax.ShapeDtypeStruct((n_state, nbatch), F32)],
        scratch_shapes=_mixer_weight_scratch(D, aw + 2 * kw + sw, sw)
                       + [pltpu.VMEM((2, rows, aw), F32), pltpu.VMEM((rows, aw), F32), pltpu.VMEM((rows, sw), F32),
                          pltpu.VMEM((rows, n_state), F32), pltpu.VMEM((rows, n_state), F32),
                          pltpu.VMEM((rows, kw), F32), pltpu.VMEM((rows, kw), F32)]
                       + [pltpu.VMEM((gb * past, kw), F32)] * 4 + [pltpu.VMEM((nbatch, n_state), F32)] * 4,
        compiler_params=pltpu.CompilerParams(dimension_semantics=("arbitrary",), vmem_limit_bytes=VMEM_LIMIT),
        name="mixer_sample",
    )(sinks, x, ckt, cvt, h0rt, h0it, *prevs, seg, w_in, wglu, *stacked)


def _mix_ffn_rows(x_ref, attn_ref, ssm_ref, o_ref, g1_ref, wg_ref, bg_ref, wao_ref, wso_ref, wout_ref,
                  g2_ref, wup_ref, wdn_ref):
    x = x_ref[...]
    D = x.shape[-1]
    xn = _rms(x, g1_ref[...]).astype(BF16)
    attn = attn_ref[...]
    ssm = ssm_ref[...]
    o_ref[...] = x
    cw = 2 * LANES
    for cc in range(D // cw):
        cs = slice(cc * cw, (cc + 1) * cw)
        gs = slice(D + cc * cw, D + (cc + 1) * cw)
        g_attn = jax.nn.sigmoid(jnp.dot(xn, wg_ref[:, cs], preferred_element_type=F32) + bg_ref[:, cs])
        g_ssm = jax.nn.sigmoid(jnp.dot(xn, wg_ref[:, gs], preferred_element_type=F32) + bg_ref[:, gs])
        mixed = (g_attn * jnp.dot(attn, wao_ref[:, cs], preferred_element_type=F32)
                 + g_ssm * jnp.dot(ssm, wso_ref[:, cs], preferred_element_type=F32))
        o_ref[...] += jnp.dot(mixed.astype(BF16), wout_ref[cs, :], preferred_element_type=F32)
    xn2 = _rms(o_ref[...], g2_ref[...]).astype(BF16)
    fw = 4 * LANES
    for fc in range(wup_ref.shape[-1] // fw):
        fs = slice(fc * fw, (fc + 1) * fw)
        hdn = jnp.maximum(jnp.dot(xn2, wup_ref[:, fs], preferred_element_type=F32), 0.0)
        o_ref[...] += jnp.dot((hdn * hdn).astype(BF16), wdn_ref[fs, :], preferred_element_type=F32)


def _mix_ffn_kernel(layer, n_prompt, xp_ref, ap_ref, sp_ref, xs_ref, as_ref, ss_ref,
                    w_in_hbm, wao_hbm, wso_hbm, wout_hbm, wup_hbm, wdn_hbm, g1_ref, bg_ref, g2_ref,
                    op_ref, os_ref, wg_ref, wao_ref, wso_ref, wout_ref, wup_ref, wdn_ref, stage, sem):
    i = pl.program_id(0)
    cr, cc = stage.shape[1:]

    @pl.when(i == 0)
    def _():
        chunks = []
        chunks += _weight_chunks(w_in_hbm.at[layer], wg_ref, wg_ref.shape[0], wg_ref.shape[1], cr, cc,
                                 src_col0=w_in_hbm.shape[-1] - wg_ref.shape[1])
        for hbm, ref in ((wao_hbm, wao_ref), (wso_hbm, wso_ref), (wout_hbm, wout_ref), (wup_hbm, wup_ref),
                         (wdn_hbm, wdn_ref)):
            chunks += _weight_chunks(hbm.at[layer], ref, ref.shape[0], ref.shape[1], cr, cc)
        _stage_weights(chunks, stage, sem)

    w_refs = (g1_ref, wg_ref, bg_ref, wao_ref, wso_ref, wout_ref, g2_ref, wup_ref, wdn_ref)

    @pl.when(i < n_prompt)
    def _():
        _mix_ffn_rows(xp_ref, ap_ref, sp_ref, op_ref, *w_refs)

    @pl.when(i >= n_prompt)
    def _():
        _mix_ffn_rows(xs_ref, as_ref, ss_ref, os_ref, *w_refs)


def _mix_ffn(layer, xp, attn_p, ssm_p, xs, attn_s, ssm_s, w_in, wao, wso, wout, wup, wdn, g1, bg, g2):
    rp, D = xp.shape
    rs = xs.shape[0]
    tp = min(512, rp)
    ts = min(512, rs)
    assert rp % tp == 0 and rs % ts == 0
    n_p, n_s = rp // tp, rs // ts
    prow = lambda last: pl.BlockSpec((tp, last), lambda i: (jnp.minimum(i, n_p - 1), 0))
    srow = lambda last: pl.BlockSpec((ts, last), lambda i: (jnp.maximum(i - n_p, 0), 0))
    hbm = pl.BlockSpec(memory_space=pl.ANY)
    resident = lambda a: pltpu.VMEM(a.shape[1:], BF16)
    return pl.pallas_call(
        functools.partial(_mix_ffn_kernel, layer, n_p),
        grid=(n_p + n_s,),
        in_specs=[prow(D), prow(attn_p.shape[-1]), prow(ssm_p.shape[-1]),
                  srow(D), srow(attn_s.shape[-1]), srow(ssm_s.shape[-1])]
                 + [hbm] * 6 + [_layer_spec(a, layer) for a in (g1, bg, g2)],
        out_specs=[prow(D), srow(D)],
        out_shape=[jax.ShapeDtypeStruct((rp, D), F32), jax.ShapeDtypeStruct((rs, D), F32)],
        scratch_shapes=[pltpu.VMEM((D, bg.shape[-1]), BF16), resident(wao), resident(wso), resident(wout),
                        resident(wup), resident(wdn),
                        pltpu.VMEM((FFN_WEIGHT_SLOTS, 2 * LANES, D), F32),
                        pltpu.SemaphoreType.DMA((FFN_WEIGHT_SLOTS,))],
        compiler_params=pltpu.CompilerParams(dimension_semantics=("arbitrary",), vmem_limit_bytes=VMEM_LIMIT),
        name="mix_ffn",
    )(xp, attn_p, ssm_p, xs, attn_s, ssm_s, w_in, wao, wso, wout, wup, wdn, g1, bg, g2)


def _perm_matrix(nb, tt):
    p = np.zeros((nb * tt, nb * tt), np.float32)
    t, b = np.meshgrid(np.arange(tt), np.arange(nb), indexing="ij")
    p[(t * nb + b).ravel(), (b * tt + t).ravel()] = 1.0
    return jnp.asarray(p, BF16)


def _seg_matrix(width):
    i = np.arange(width) // HEAD_DIM
    return jnp.asarray((i[:, None] == i[None, :]).astype(np.float32) / HEAD_DIM, BF16)


def kernel(x_prompt, x_sample, cache_k, cache_v, state_ssm_re, state_ssm_im, norm1_g, w_in, b_gate, q_norm_g,
           k_norm_g, attn_sinks, lam_re, lam_im, log_step, b_re, b_im, c_re, c_im, d_skip, w_glu, b_glu,
           w_attn_o, w_ssm_o, w_out, norm2_g, w_up, w_down):
    B, T, D = x_prompt.shape
    DB, DS, _ = x_sample.shape
    L = w_in.shape[0]
    aw = N_HEADS * HEAD_DIM
    kw = N_KV_HEADS * HEAD_DIM
    sw = d_skip.shape[-1]
    G, P = lam_re.shape[1:]
    past = cache_k.shape[2]
    assert past == WINDOW and T % WINDOW == 0 and (B * WINDOW) % ROW_TILE == 0 and (DB * DS) % ROW_TILE == 0

    ar, ai, bx, cm = _ssm_prep(lam_re, lam_im, log_step, b_re, b_im, c_re, c_im)
    perm = _perm_matrix(B, WINDOW)
    seg = _seg_matrix(aw)
    rows = lambda a: a.reshape(L, 1, -1)
    g1 = rows(norm1_g)
    mixer_w = (w_in, w_glu, g1, rows(jnp.tile(q_norm_g, (1, N_HEADS))), rows(jnp.tile(k_norm_g, (1, N_KV_HEADS))),
               ar, ai, bx, cm, rows(d_skip), rows(b_glu))
    ffn_w = (w_in, w_attn_o, w_ssm_o, w_out, w_up, w_down, g1, rows(b_gate), rows(norm2_g))
    kv_t = lambda a: a.transpose(0, 1, 3, 4, 2).reshape(L, DB, kw, past)
    st_t = lambda a: a.transpose(0, 2, 3, 1).reshape(L, G * P, DB)
    ckt, cvt, h0rt, h0it = kv_t(cache_k), kv_t(cache_v), st_t(state_ssm_re), st_t(state_ssm_im)

    yp = x_prompt.reshape(B * T, D)
    ys = x_sample.transpose(1, 0, 2).reshape(DS * DB, D)
    outs = [[] for _ in range(6)]
    kcat = vcat = None
    for l in range(L):
        attn_p, ssm_p, kp, vp, hrp, hip = _mixer_prompt(l, yp.reshape(B, T, D), attn_sinks, seg, perm, *mixer_w)
        attn_s, ssm_s, kcat, vcat, hrs, his = _mixer_sample(l, ys, ckt, cvt, h0rt, h0it, kcat, vcat, attn_sinks, seg,
                                                            *mixer_w)
        yp, ys = _mix_ffn(l, yp, attn_p.reshape(B * T, aw), ssm_p.reshape(B * T, sw), ys, attn_s, ssm_s, *ffn_w)
        for o, v in zip(outs, (kp, vp, hrp, hip, hrs, his)):
            o.append(v)

    kp, vp, hrp, hip, hrs, his = (jnp.stack(o) for o in outs)
    kv_back = lambda a: a.reshape(a.shape[:2] + (N_KV_HEADS, HEAD_DIM, a.shape[-1])).transpose(0, 1, 4, 2, 3)
    st_back = lambda a: a.reshape(L, G, P, DB).transpose(0, 3, 1, 2)
    y_sample = ys.reshape(DS, DB, D).transpose(1, 0, 2)
    return (yp.reshape(B, T, D), y_sample, kv_back(kp), kv_back(vp), hrp.reshape(L, B, G, P), hip.reshape(L, B, G, P),
            kv_back(kcat), kv_back(vcat), st_back(hrs), st_back(his))
```

```python
import functools
import math

import numpy as np
import jax
import jax.numpy as jnp
from jax import lax
from jax.experimental import pallas as pl
from jax.experimental.pallas import tpu as pltpu

F32 = jnp.float32
BF16 = jnp.bfloat16

HEAD_DIM = 64
N_HEADS = 8
N_KV_HEADS = 2
Q_GROUP = N_HEADS // N_KV_HEADS
WINDOW = 128
GROUP_CH = 16
SSM_STATE = 64
EPS = 1e-6
NEG_INF = -1e30
LOG2E = math.log2(math.e)
LANES = 128
SUBLANES = 8
ROW_TILE = 256
U_PITCH = WINDOW + SUBLANES
VMEM_LIMIT = 60 * 1024 * 1024

WEIGHT_SLOTS = 4
FFN_WEIGHT_SLOTS = 8
SLOPES = tuple(2.0 ** (-8.0 * (h + 1) / N_HEADS) for h in range(N_HEADS))


def _rms(x, g):
    return x * lax.rsqrt(jnp.mean(x * x, axis=-1, keepdims=True) + EPS) * g


def _seg_rms(x, seg, g):
    ms = jnp.dot((x * x).astype(BF16), seg, preferred_element_type=F32)
    return x * lax.rsqrt(ms + EPS) * g


def _gelu_tanh(y):
    return 0.5 * y * (1.0 + jnp.tanh(math.sqrt(2.0 / math.pi) * (y + 0.044715 * (y * y * y))))


def _const_spec(a):
    n = a.ndim
    return pl.BlockSpec(a.shape, lambda *_: (0,) * n, pipeline_mode=pl.Buffered(1))


def _layer_spec(a, layer):
    n = a.ndim - 1
    return pl.BlockSpec((None,) + a.shape[1:], lambda *_: (layer,) + (0,) * n, pipeline_mode=pl.Buffered(1))


def _stage_weights(chunks, stage, sem):
    slots = stage.shape[0]

    def copy(k):
        src, _, cols = chunks[k]
        return pltpu.make_async_copy(src, stage.at[k % slots, :, pl.ds(0, cols)], sem.at[k % slots])

    for k in range(min(slots - 1, len(chunks))):
        copy(k).start()
    for k, (_, dst, cols) in enumerate(chunks):
        if k + slots - 1 < len(chunks):
            copy(k + slots - 1).start()
        copy(k).wait()
        dst[...] = stage[k % slots, :, :cols].astype(BF16)


def _weight_chunks(src, dst, rows, cols, chunk_rows, chunk_cols, src_col0=0):
    return [(src.at[pl.ds(r, chunk_rows), pl.ds(src_col0 + c, chunk_cols)],
             dst.at[pl.ds(r, chunk_rows), pl.ds(c, chunk_cols)], chunk_cols)
            for r in range(0, rows, chunk_rows) for c in range(0, cols, chunk_cols)]


def _ssm_prep_kernel(lr_ref, li_ref, ls_ref, btr_ref, bti_ref, ctr_ref, cti_ref,
                     ar_ref, ai_ref, bx_ref, cm_ref):
    lr = lr_ref[0]
    li = li_ref[0]
    step = jnp.exp(ls_ref[0])
    mag = jnp.exp(lr * step)
    ar = mag * jnp.cos(li * step)
    ai = mag * jnp.sin(li * step)
    den = lr * lr + li * li
    cr = ((ar - 1.0) * lr + ai * li) / den
    ci = (ai * lr - (ar - 1.0) * li) / den
    ar_ref[0] = ar
    ai_ref[0] = ai
    nj = btr_ref.shape[1]
    gp = SUBLANES * SSM_STATE
    rows_b = lax.broadcasted_iota(jnp.int32, (LANES, gp), 0) // GROUP_CH
    cols_b = lax.broadcasted_iota(jnp.int32, (LANES, gp), 1) // SSM_STATE
    mask_b = rows_b == cols_b
    rows_c = lax.broadcasted_iota(jnp.int32, (gp, LANES), 0) // SSM_STATE
    cols_c = lax.broadcasted_iota(jnp.int32, (gp, LANES), 1) // GROUP_CH
    mask_c = rows_c == cols_c
    for j in range(nj):
        crj = cr[:, j * gp:(j + 1) * gp]
        cij = ci[:, j * gp:(j + 1) * gp]
        b_re = btr_ref[0, j]
        b_im = bti_ref[0, j]
        bb_re = jnp.where(mask_b, crj * b_re - cij * b_im, 0.0)
        bb_im = jnp.where(mask_b, crj * b_im + cij * b_re, 0.0)
        bx_ref[0, j, :, :gp] = bb_re.astype(BF16)
        bx_ref[0, j, :, gp:] = bb_im.astype(BF16)
        cm_ref[0, j, :gp, :] = jnp.where(mask_c, ctr_ref[0, j], 0.0).astype(BF16)
        cm_ref[0, j, gp:, :] = jnp.where(mask_c, -cti_ref[0, j], 0.0).astype(BF16)


def _ssm_prep(lam_re, lam_im, log_step, b_re, b_im, c_re, c_im):
    L, G, P = lam_re.shape
    GC = b_re.shape[-1]
    gpb = LANES // GC
    nj = G // gpb
    gp = gpb * P
    flat = lambda a: a.reshape(L, 1, G * P)
    ls = jnp.repeat(log_step, P, axis=-1).reshape(L, 1, G * P)

    def b_tiles(b):
        t = b.reshape(L, nj, gpb, P, GC).transpose(0, 1, 4, 2, 3).reshape(L, nj, 1, GC, gp)
        return jnp.broadcast_to(t, (L, nj, gpb, GC, gp)).reshape(L, nj, LANES, gp)

    def c_tiles(c):
        t = c.reshape(L, nj, gpb, GC, P).transpose(0, 1, 2, 4, 3).reshape(L, nj, gp, 1, GC)
        return jnp.broadcast_to(t, (L, nj, gp, gpb, GC)).reshape(L, nj, gp, LANES)

    vec = pl.BlockSpec((1, 1, G * P), lambda l: (l, 0, 0))
    bt = pl.BlockSpec((1, nj, LANES, gp), lambda l: (l, 0, 0, 0))
    ct = pl.BlockSpec((1, nj, gp, LANES), lambda l: (l, 0, 0, 0))
    return pl.pallas_call(
        _ssm_prep_kernel,
        grid=(L,),
        in_specs=[vec, vec, vec, bt, bt, ct, ct],
        out_specs=[vec, vec,
                   pl.BlockSpec((1, nj, LANES, 2 * gp), lambda l: (l, 0, 0, 0)),
                   pl.BlockSpec((1, nj, 2 * gp, LANES), lambda l: (l, 0, 0, 0))],
        out_shape=[jax.ShapeDtypeStruct((L, 1, G * P), F32), jax.ShapeDtypeStruct((L, 1, G * P), F32),
                   jax.ShapeDtypeStruct((L, nj, LANES, 2 * gp), BF16),
                   jax.ShapeDtypeStruct((L, nj, 2 * gp, LANES), BF16)],
        name="ssm_prep",
    )(flat(lam_re), flat(lam_im), ls, b_tiles(b_re), b_tiles(b_im), c_tiles(c_re), c_tiles(c_im))


def _project_rows(x, g1_ref, w_ref, seg_ref, qg_ref, kg_ref):
    aw = N_HEADS * HEAD_DIM
    kw = N_KV_HEADS * HEAD_DIM
    xn = _rms(x, g1_ref[...]).astype(BF16)
    z = jnp.dot(xn, w_ref[...], preferred_element_type=F32)
    q = _seg_rms(z[:, :aw], seg_ref[...], qg_ref[...]) * (HEAD_DIM ** -0.5 * LOG2E)
    k = _seg_rms(z[:, aw:aw + kw], seg_ref[:kw, :kw], kg_ref[...])
    v = z[:, aw + kw:aw + 2 * kw]
    u = z[:, aw + 2 * kw:]
    lower = (lax.broadcasted_iota(jnp.int32, q.shape, 1) % LANES) < HEAD_DIM
    q_lo = jnp.where(lower, q, 0.0)
    q_hi = jnp.where(lower, 0.0, q)
    return q_lo, q_hi, k, v, u


def _stage_mixer_weights(layer, w_in_hbm, wglu_hbm, w_ref, wglu_ref, stage, sem):
    _stage_weights(
        _weight_chunks(w_in_hbm.at[layer], w_ref, w_ref.shape[0], w_ref.shape[1], stage.shape[1], w_ref.shape[1])
        + _weight_chunks(wglu_hbm.at[layer], wglu_ref, wglu_ref.shape[0], wglu_ref.shape[1], stage.shape[1],
                         wglu_ref.shape[1]),
        stage, sem)


def _mixer_weight_scratch(D, in_cols, sw):
    return [pltpu.VMEM((D, in_cols), BF16), pltpu.VMEM((sw, sw), BF16),
            pltpu.VMEM((WEIGHT_SLOTS, LANES, in_cols), F32), pltpu.SemaphoreType.DMA((WEIGHT_SLOTS,))]


def _ssm_readout(y, u, d_ref, wglu_ref, bglu_ref):
    y = _gelu_tanh(y + d_ref[...] * u)
    gate = jnp.dot(y.astype(BF16), wglu_ref[...], preferred_element_type=F32) + bglu_ref[...]
    return y * jax.nn.sigmoid(gate)


def _softmax_pv(s, sink, v):
    m = jnp.maximum(jnp.max(s, axis=-1, keepdims=True), sink)
    e = jnp.exp2(s - m)
    den = jnp.sum(e, axis=-1, keepdims=True) + jnp.exp2(sink - m)
    return jnp.dot(e.astype(BF16), v, preferred_element_type=F32) * (1.0 / den)


def _mixer_prompt_kernel(layer, sink_ref, x_ref, seg_ref, w_in_hbm, wglu_hbm, g1_ref, qg_ref, kg_ref,
                         ar_ref, ai_ref, bx_ref, cm_ref, d_ref, bglu_ref,
                         attn_ref, ssm_ref, kout_ref, vout_ref, hr_ref, hi_ref,
                         w_ref, wglu_ref, stage, sem, q_s, k_s, v_s, u_s, y_s, *x_s):
    c = pl.program_id(0)
    nb, tt, _ = x_ref.shape
    rows = nb * tt
    n_tiles = rows // ROW_TILE
    bpt = ROW_TILE // tt
    n_state = ar_ref.shape[-1]
    nj = bx_ref.shape[0]
    gp = n_state // nj
    xr_s, xi_s = x_s[:nj], x_s[nj:]
    kw = kout_ref.shape[1]

    @pl.when(c == 0)
    def _():
        _stage_mixer_weights(layer, w_in_hbm, wglu_hbm, w_ref, wglu_ref, stage, sem)
        k_s[:, :, :tt, :] = jnp.zeros((2, nb, tt, kw), BF16)
        v_s[:, :, :tt, :] = jnp.zeros((2, nb, tt, kw), BF16)
        hr_ref[...] = jnp.zeros(hr_ref.shape, F32)
        hi_ref[...] = jnp.zeros(hi_ref.shape, F32)

    def project(r, carry):
        x = x_ref[pl.ds(r * bpt, bpt)].reshape(ROW_TILE, x_ref.shape[-1])
        q_lo, q_hi, k, v, u = _project_rows(x, g1_ref, w_ref, seg_ref, qg_ref, kg_ref)
        q_s[0, pl.ds(r * bpt, bpt)] = q_lo.astype(BF16).reshape(bpt, tt, -1)
        q_s[1, pl.ds(r * bpt, bpt)] = q_hi.astype(BF16).reshape(bpt, tt, -1)
        k_s[0, pl.ds(r * bpt, bpt), tt:, :] = k.astype(BF16).reshape(bpt, tt, -1)
        v_s[0, pl.ds(r * bpt, bpt), tt:, :] = v.astype(BF16).reshape(bpt, tt, -1)
        k_s[1, pl.ds(r * bpt, bpt), tt:, :] = pltpu.roll(k, HEAD_DIM, 1).astype(BF16).reshape(bpt, tt, -1)
        v_s[1, pl.ds(r * bpt, bpt), tt:, :] = pltpu.roll(v, HEAD_DIM, 1).astype(BF16).reshape(bpt, tt, -1)
        for bb in range(bpt):
            kout_ref[r * bpt + bb] = k[bb * tt:(bb + 1) * tt, :].T
            vout_ref[r * bpt + bb] = v[bb * tt:(bb + 1) * tt, :].T
        for bb in range(bpt):
            row0 = pl.multiple_of((r * bpt + bb) * U_PITCH, SUBLANES)
            for j in range(nj):
                u_s[j, pl.ds(row0, tt), :] = u[bb * tt:(bb + 1) * tt, j * LANES:(j + 1) * LANES]
        return carry

    lax.fori_loop(0, n_tiles, project, 0, unroll=2)

    t_idx = lax.broadcasted_iota(jnp.int32, (tt, 2 * tt), 0)
    j_idx = lax.broadcasted_iota(jnp.int32, (tt, 2 * tt), 1)
    dist = t_idx - j_idx + tt
    valid = (dist >= 0) & (dist <= WINDOW) & (j_idx >= jnp.where(c > 0, 0, tt))
    key_pos = lax.broadcasted_iota(jnp.int32, (1, 2 * tt), 1).astype(F32)
    row_pos = (lax.broadcasted_iota(jnp.int32, (tt, 1), 0) + tt).astype(F32)
    lower = lax.broadcasted_iota(jnp.int32, (tt, LANES), 1) < HEAD_DIM

    def attend(b):
        for p in range(N_HEADS // 2):
            kv = (2 * p) // Q_GROUP
            outs = []
            for e in range(2):
                h = 2 * p + e
                swap = 0 if kv == e else 1
                qm = q_s[e, b, :, p * LANES:(p + 1) * LANES]
                s = lax.dot_general(qm, k_s[swap, b], (((1,), (1,)), ((), ())), preferred_element_type=F32)
                slope = SLOPES[h] * LOG2E
                s = jnp.where(valid, s + slope * key_pos, NEG_INF)
                sink = sink_ref[layer, h] * LOG2E + slope * row_pos
                outs.append(_softmax_pv(s, sink, v_s[swap, b]))
            attn_ref[b, :, p * LANES:(p + 1) * LANES] = jnp.where(lower, outs[0], outs[1]).astype(BF16)

    def ssm_in_and_attend(r, carry):
        rs = pl.ds(r * ROW_TILE, ROW_TILE)
        t0 = r * (ROW_TILE // nb)
        for j in range(nj):
            up = jnp.concatenate([u_s[j, pl.ds(t0 + step, nb, stride=U_PITCH), :]
                                  for step in range(ROW_TILE // nb)], axis=0).astype(BF16)
            xj = jnp.dot(up, bx_ref[j], preferred_element_type=F32)
            xr_s[j][rs, :] = xj[:, :gp]
            xi_s[j][rs, :] = xj[:, gp:]
        for bb in range(bpt):
            attend(r * bpt + bb)
        return carry

    lax.fori_loop(0, n_tiles, ssm_in_and_attend, 0, unroll=2)

    k_s[:, :, :tt, :] = k_s[:, :, tt:, :]
    v_s[:, :, :tt, :] = v_s[:, :, tt:, :]

    def ssm_out_tile(j, r):
        rs = pl.ds(r * ROW_TILE, ROW_TILE)
        y_s[j, rs, :] = jnp.dot(xr_s[j][rs, :].astype(BF16), cm_ref[j, :gp, :], preferred_element_type=F32)
        y_s[j, rs, :] = y_s[j, rs, :] + jnp.dot(xi_s[j][rs, :].astype(BF16), cm_ref[j, gp:, :],
                                                preferred_element_type=F32)

    for lb in range(nj):
        sl = slice(lb * gp, (lb + 1) * gp)
        a_re = jnp.broadcast_to(ar_ref[:, sl], (nb, gp))
        a_im = jnp.broadcast_to(ai_ref[:, sl], (nb, gp))

        def scan_tile(r, h, lb=lb, a_re=a_re, a_im=a_im):
            if lb > 0:
                ssm_out_tile(lb - 1, r)
            h_re, h_im = h
            for step in range(ROW_TILE // nb):
                row = pl.ds(pl.multiple_of(r * ROW_TILE + step * nb, nb), nb)
                n_re = a_re * h_re - a_im * h_im + xr_s[lb][row, :]
                n_im = a_re * h_im + a_im * h_re + xi_s[lb][row, :]
                xr_s[lb][row, :] = n_re
                xi_s[lb][row, :] = n_im
                h_re, h_im = n_re, n_im
            return h_re, h_im

        h_re, h_im = lax.fori_loop(0, n_tiles, scan_tile, (hr_ref[:, sl], hi_ref[:, sl]), unroll=True)
        hr_ref[:, sl] = h_re
        hi_ref[:, sl] = h_im

    def ssm_out_last(r, carry):
        ssm_out_tile(nj - 1, r)
        return carry

    lax.fori_loop(0, n_tiles, ssm_out_last, 0, unroll=True)

    def readout(b, carry):
        y = jnp.concatenate([y_s[j, pl.ds(b, tt, stride=nb), :] for j in range(nj)], axis=1)
        row0 = pl.multiple_of(b * U_PITCH, SUBLANES)
        u = jnp.concatenate([u_s[j, pl.ds(row0, tt), :] for j in range(nj)], axis=1)
        ssm_ref[b] = _ssm_readout(y, u, d_ref, wglu_ref, bglu_ref).astype(BF16)
        return carry

    lax.fori_loop(0, nb, readout, 0, unroll=2)


def _mixer_prompt(layer, x, sinks, seg, w_in, wglu, g1, qg, kg, ar, ai, bx, cm, d, bglu):
    nb, T, D = x.shape
    tt = WINDOW
    rows = nb * tt
    aw = N_HEADS * HEAD_DIM
    kw = N_KV_HEADS * HEAD_DIM
    sw = d.shape[-1]
    n_state = ar.shape[-1]
    nj = bx.shape[1]
    chunk = lambda last: pl.BlockSpec((nb, tt, last), lambda c: (0, c, 0))
    whole = pl.BlockSpec((nb, kw, tt), lambda c: (0, 0, 0))
    state = pl.BlockSpec((nb, n_state), lambda c: (0, 0))
    hbm = pl.BlockSpec(memory_space=pl.ANY)
    stacked = (g1, qg, kg, ar, ai, bx, cm, d, bglu)
    return pl.pallas_call(
        functools.partial(_mixer_prompt_kernel, layer),
        grid=(T // tt,),
        in_specs=[pl.BlockSpec(memory_space=pltpu.SMEM), chunk(D), _const_spec(seg), hbm, hbm]
                 + [_layer_spec(a, layer) for a in stacked],
        out_specs=[chunk(aw), chunk(sw), whole, whole, state, state],
        out_shape=[jax.ShapeDtypeStruct((nb, T, aw), BF16), jax.ShapeDtypeStruct((nb, T, sw), BF16),
                   jax.ShapeDtypeStruct((nb, kw, tt), F32), jax.ShapeDtypeStruct((nb, kw, tt), F32),
                   jax.ShapeDtypeStruct((nb, n_state), F32), jax.ShapeDtypeStruct((nb, n_state), F32)],
        scratch_shapes=_mixer_weight_scratch(D, aw + 2 * kw + sw, sw)
                       + [pltpu.VMEM((2, nb, tt, aw), BF16),
                          pltpu.VMEM((2, nb, 2 * tt, kw), BF16), pltpu.VMEM((2, nb, 2 * tt, kw), BF16),
                          pltpu.VMEM((sw // LANES, nb * U_PITCH, LANES), F32),
                          pltpu.VMEM((nj, rows, LANES), F32)]
                       + [pltpu.VMEM((rows, n_state // nj), F32)] * (2 * nj),
        compiler_params=pltpu.CompilerParams(dimension_semantics=("arbitrary",), vmem_limit_bytes=VMEM_LIMIT),
        name="mixer_prompt",
    )(sinks, x, seg, w_in, wglu, *stacked)


def _mixer_sample_kernel(layer, sink_ref, x_ref, ckt_ref, cvt_ref, h0rt_ref, h0it_ref, *refs):
    if layer:
        prevk_ref, prevv_ref, *refs = refs
    (seg_ref, w_in_hbm, wglu_hbm, g1_ref, qg_ref, kg_ref, ar_ref, ai_ref, bx_ref, cm_ref, d_ref, bglu_ref,
     attn_ref, ssm_ref, kcat_ref, vcat_ref, hrt_ref, hit_ref,
     w_ref, wglu_ref, stage, sem, q_s, o_s, u_s, xr_s, xi_s, knew_s, vnew_s,
     ck_ref, cv_ref, kroll_s, vroll_s, h0r_ref, h0i_ref, hr_ref, hi_ref) = refs
    i = pl.program_id(0)
    rows = x_ref.shape[0]
    nbatch = h0rt_ref.shape[1]
    ts = rows // nbatch
    gb, kw, past = ckt_ref.shape
    n_tiles = rows // ROW_TILE
    n_state = ar_ref.shape[-1]
    nj = bx_ref.shape[0]
    gp = n_state // nj

    @pl.when(i == 0)
    def _():
        _stage_mixer_weights(layer, w_in_hbm, wglu_hbm, w_ref, wglu_ref, stage, sem)

        def project(r, carry):
            rs = pl.ds(r * ROW_TILE, ROW_TILE)
            q_lo, q_hi, k, v, u = _project_rows(x_ref[rs, :], g1_ref, w_ref, seg_ref, qg_ref, kg_ref)
            q_s[0, rs, :] = q_lo
            q_s[1, rs, :] = q_hi
            knew_s[rs, :] = k
            vnew_s[rs, :] = v
            u_s[rs, :] = u
            return carry

        lax.fori_loop(0, n_tiles, project, 0)

    n_q = ts * Q_GROUP * gb
    n_keys = gb * past + ts * gb
    r_idx = lax.broadcasted_iota(jnp.int32, (n_q, n_keys), 0)
    c_idx = lax.broadcasted_iota(jnp.int32, (n_q, n_keys), 1)
    q_t = r_idx // (Q_GROUP * gb)
    q_b = r_idx % gb
    is_cache = c_idx < gb * past
    key_b = jnp.where(is_cache, c_idx // past, (c_idx - gb * past) % gb)
    dist = jnp.where(is_cache, past + q_t - c_idx % past, q_t - (c_idx - gb * past) // gb)
    valid = (key_b == q_b) & (dist >= 0) & (dist <= WINDOW)
    dist_f = dist.astype(F32)
    hh = (lax.broadcasted_iota(jnp.int32, (n_q, 1), 0) // gb) % Q_GROUP
    lower = lax.broadcasted_iota(jnp.int32, (gb, LANES), 1) < HEAD_DIM
    base = pl.multiple_of(i * gb, gb)

    for cat_ref, ct_ref, c_ref, roll_s, new_s in ((kcat_ref, ckt_ref, ck_ref, kroll_s, knew_s),
                                                  (vcat_ref, cvt_ref, cv_ref, vroll_s, vnew_s)):
        for b in range(gb):
            c_ref[b * past:(b + 1) * past, :] = ct_ref[b].T
        for b in range(gb):
            roll_s[b * past:(b + 1) * past - ts, :] = c_ref[b * past + ts:(b + 1) * past, :]
        for t in range(ts):
            roll_s[pl.ds(past - ts + t, gb, stride=past), :] = new_s[pl.ds(base + t * nbatch, gb), :]
        for b in range(gb):
            cat_ref[layer, b] = roll_s[b * past:(b + 1) * past, :].T
    if layer:
        kcat_ref[:layer] = prevk_ref[...]
        vcat_ref[:layer] = prevv_ref[...]

    k_all = jnp.concatenate(
        [ck_ref[...]] + [knew_s[pl.ds(base + t * nbatch, gb), :] for t in range(ts)], axis=0).astype(BF16)
    v_all = jnp.concatenate(
        [cv_ref[...]] + [vnew_s[pl.ds(base + t * nbatch, gb), :] for t in range(ts)], axis=0).astype(BF16)
    outs = []
    for kvh in range(N_KV_HEADS):
        slope = jnp.zeros((n_q, 1), F32)
        sink = jnp.zeros((n_q, 1), F32)
        for g in range(Q_GROUP):
            h = kvh * Q_GROUP + g
            slope = jnp.where(hh == g, SLOPES[h] * LOG2E, slope)
            sink = jnp.where(hh == g, sink_ref[layer, h] * LOG2E, sink)
        pieces = []
        for t in range(ts):
            for g in range(Q_GROUP):
                h = kvh * Q_GROUP + g
                piece = q_s[h % 2, pl.ds(base + t * nbatch, gb), (h // 2) * LANES:(h // 2 + 1) * LANES]
                pieces.append(piece if h % 2 == kvh else pltpu.roll(piece, HEAD_DIM, 1))
        qm = jnp.concatenate(pieces, axis=0).astype(BF16)
        s = lax.dot_general(qm, k_all, (((1,), (1,)), ((), ())), preferred_element_type=F32)
        s = jnp.where(valid, s - slope * dist_f, NEG_INF)
        outs.append(_softmax_pv(s, sink, v_all))
    for t in range(ts):
        for p in range(N_HEADS // 2):
            kvh = (2 * p) // Q_GROUP
            r0 = (t * Q_GROUP + 2 * p - kvh * Q_GROUP) * gb
            lo = outs[kvh][r0:r0 + gb]
            hi = outs[kvh][r0 + gb:r0 + 2 * gb]
            if kvh == 1:
                lo = pltpu.roll(lo, HEAD_DIM, 1)
            else:
                hi = pltpu.roll(hi, HEAD_DIM, 1)
            o_s[pl.ds(base + t * nbatch, gb), p * LANES:(p + 1) * LANES] = jnp.where(lower, lo, hi)

    @pl.when(i == pl.num_programs(0) - 1)
    def _():
        attn_ref[...] = o_s[...].astype(BF16)

        def ssm_in(r, carry):
            rs = pl.ds(r * ROW_TILE, ROW_TILE)
            ub = u_s[rs, :].astype(BF16)
            for j in range(nj):
                xj = jnp.dot(ub[:, j * LANES:(j + 1) * LANES], bx_ref[j], preferred_element_type=F32)
                xr_s[rs, j * gp:(j + 1) * gp] = xj[:, :gp]
                xi_s[rs, j * gp:(j + 1) * gp] = xj[:, gp:]
            return carry

        lax.fori_loop(0, n_tiles, ssm_in, 0)

        for blk in range(n_state // LANES):
            bs = slice(blk * LANES, (blk + 1) * LANES)
            h0r_ref[:, bs] = h0rt_ref[bs, :].T
            h0i_ref[:, bs] = h0it_ref[bs, :].T

        lane_blk = 4 * LANES
        for lb in range(n_state // lane_blk):
            sl = slice(lb * lane_blk, (lb + 1) * lane_blk)
            a_re = jnp.broadcast_to(ar_ref[:, sl], (SUBLANES, lane_blk))
            a_im = jnp.broadcast_to(ai_ref[:, sl], (SUBLANES, lane_blk))

            def scan(bt, carry):
                b0 = pl.multiple_of(bt * SUBLANES, SUBLANES)
                h_re = h0r_ref[pl.ds(b0, SUBLANES), sl]
                h_im = h0i_ref[pl.ds(b0, SUBLANES), sl]
                for t in range(ts):
                    rs = pl.ds(b0 + t * nbatch, SUBLANES)
                    n_re = a_re * h_re - a_im * h_im + xr_s[rs, sl]
                    n_im = a_re * h_im + a_im * h_re + xi_s[rs, sl]
                    xr_s[rs, sl] = n_re
                    xi_s[rs, sl] = n_im
                    h_re, h_im = n_re, n_im
                hr_ref[pl.ds(b0, SUBLANES), sl] = h_re
                hi_ref[pl.ds(b0, SUBLANES), sl] = h_im
                return carry

            lax.fori_loop(0, nbatch // SUBLANES, scan, 0)

        for blk in range(n_state // LANES):
            bs = slice(blk * LANES, (blk + 1) * LANES)
            hrt_ref[bs, :] = hr_ref[:, bs].T
            hit_ref[bs, :] = hi_ref[:, bs].T

        def ssm_out(r, carry):
            rs = pl.ds(r * ROW_TILE, ROW_TILE)
            ys = []
            for j in range(nj):
                h_cat = jnp.concatenate([xr_s[rs, j * gp:(j + 1) * gp], xi_s[rs, j * gp:(j + 1) * gp]],
                                        axis=1).astype(BF16)
                ys.append(jnp.dot(h_cat, cm_ref[j], preferred_element_type=F32))
            y = jnp.concatenate(ys, axis=1)
            ssm_ref[rs, :] = _ssm_readout(y, u_s[rs, :], d_ref, wglu_ref, bglu_ref).astype(BF16)
            return carry

        lax.fori_loop(0, n_tiles, ssm_out, 0)


def _mixer_sample(layer, x, ckt, cvt, h0rt, h0it, prev_k, prev_v, sinks, seg, w_in, wglu,
                  g1, qg, kg, ar, ai, bx, cm, d, bglu):
    rows, D = x.shape
    _, nbatch, kw, past = ckt.shape
    aw = N_HEADS * HEAD_DIM
    sw = d.shape[-1]
    n_state = ar.shape[-1]
    gb = SUBLANES
    assert nbatch % LANES == 0 and past == LANES and kw == LANES
    cache = pl.BlockSpec((None, gb, kw, past), lambda i: (layer, i, 0, 0))
    h0 = pl.BlockSpec((None, n_state, nbatch), lambda i: (layer, 0, 0))
    prev = pl.BlockSpec((layer, gb, kw, past), lambda i: (0, i, 0, 0))
    out2 = lambda last: pl.BlockSpec((rows, last), lambda i: (0, 0))
    cat = pl.BlockSpec((layer + 1, gb, kw, past), lambda i: (0, i, 0, 0))
    state = pl.BlockSpec((n_state, nbatch), lambda i: (0, 0))
    hbm = pl.BlockSpec(memory_space=pl.ANY)
    stacked = (g1, qg, kg, ar, ai, bx, cm, d, bglu)
    prevs = (prev_k, prev_v) if layer else ()
    return pl.pallas_call(
        functools.partial(_mixer_sample_kernel, layer),
        grid=(nbatch // gb,),
        in_specs=[pl.BlockSpec(memory_space=pltpu.SMEM), pl.BlockSpec(x.shape, lambda i: (0, 0)), cache, cache, h0, h0]
                 + [prev] * len(prevs) + [_const_spec(seg), hbm, hbm] + [_layer_spec(a, layer) for a in stacked],
        out_specs=[out2(aw), out2(sw), cat, cat, state, state],
        out_shape=[jax.ShapeDtypeStruct((rows, aw), BF16), jax.ShapeDtypeStruct((rows, sw), BF16),
                   jax.ShapeDtypeStruct((layer + 1, nbatch, kw, past), F32),
                   jax.ShapeDtypeStruct((layer + 1, nbatch, kw, past), F32),
                   jax.ShapeDtypeStruct((n_state, nbatch), F32), jax.ShapeDtypeStruct((n_state, nbatch), F32)],
        scratch_shapes=_mixer_weight_scratch(D, aw + 2 * kw + sw, sw)
                       + [pltpu.VMEM((2, rows, aw), F32), pltpu.VMEM((rows, aw), F32), pltpu.VMEM((rows, sw), F32),
                          pltpu.VMEM((rows, n_state), F32), pltpu.VMEM((rows, n_state), F32),
                          pltpu.VMEM((rows, kw), F32), pltpu.VMEM((rows, kw), F32)]
                       + [pltpu.VMEM((gb * past, kw), F32)] * 4 + [pltpu.VMEM((nbatch, n_state), F32)] * 4,
        compiler_params=pltpu.CompilerParams(dimension_semantics=("arbitrary",), vmem_limit_bytes=VMEM_LIMIT),
        name="mixer_sample",
    )(sinks, x, ckt, cvt, h0rt, h0it, *prevs, seg, w_in, wglu, *stacked)


def _mix_ffn_rows(x_ref, attn_ref, ssm_ref, o_ref, g1_ref, wg_ref, bg_ref, wao_ref, wso_ref, wout_ref,
                  g2_ref, wup_ref, wdn_ref):
    x = x_ref[...]
    D = x.shape[-1]
    xn = _rms(x, g1_ref[...]).astype(BF16)
    attn = attn_ref[...]
    ssm = ssm_ref[...]
    o_ref[...] = x
    cw = 2 * LANES
    for cc in range(D // cw):
        cs = slice(cc * cw, (cc + 1) * cw)
        gs = slice(D + cc * cw, D + (cc + 1) * cw)
        g_attn = jax.nn.sigmoid(jnp.dot(xn, wg_ref[:, cs], preferred_element_type=F32) + bg_ref[:, cs])
        g_ssm = jax.nn.sigmoid(jnp.dot(xn, wg_ref[:, gs], preferred_element_type=F32) + bg_ref[:, gs])
        mixed = (g_attn * jnp.dot(attn, wao_ref[:, cs], preferred_element_type=F32)
                 + g_ssm * jnp.dot(ssm, wso_ref[:, cs], preferred_element_type=F32))
        o_ref[...] += jnp.dot(mixed.astype(BF16), wout_ref[cs, :], preferred_element_type=F32)
    xn2 = _rms(o_ref[...], g2_ref[...]).astype(BF16)
    fw = 4 * LANES
    for fc in range(wup_ref.shape[-1] // fw):
        fs = slice(fc * fw, (fc + 1) * fw)
        hdn = jnp.maximum(jnp.dot(xn2, wup_ref[:, fs], preferred_element_type=F32), 0.0)
        o_ref[...] += jnp.dot((hdn * hdn).astype(BF16), wdn_ref[fs, :], preferred_element_type=F32)


def _mix_ffn_kernel(layer, n_prompt, xp_ref, ap_ref, sp_ref, xs_ref, as_ref, ss_ref,
                    w_in_hbm, wao_hbm, wso_hbm, wout_hbm, wup_hbm, wdn_hbm, g1_ref, bg_ref, g2_ref,
                    op_ref, os_ref, wg_ref, wao_ref, wso_ref, wout_ref, wup_ref, wdn_ref, stage, sem):
    i = pl.program_id(0)
    cr, cc = stage.shape[1:]

    @pl.when(i == 0)
    def _():
        chunks = []
        chunks += _weight_chunks(w_in_hbm.at[layer], wg_ref, wg_ref.shape[0], wg_ref.shape[1], cr, cc,
                                 src_col0=w_in_hbm.shape[-1] - wg_ref.shape[1])
        for hbm, ref in ((wao_hbm, wao_ref), (wso_hbm, wso_ref), (wout_hbm, wout_ref), (wup_hbm, wup_ref),
                         (wdn_hbm, wdn_ref)):
            chunks += _weight_chunks(hbm.at[layer], ref, ref.shape[0], ref.shape[1], cr, cc)
        _stage_weights(chunks, stage, sem)

    w_refs = (g1_ref, wg_ref, bg_ref, wao_ref, wso_ref, wout_ref, g2_ref, wup_ref, wdn_ref)

    @pl.when(i < n_prompt)
    def _():
        _mix_ffn_rows(xp_ref, ap_ref, sp_ref, op_ref, *w_refs)

    @pl.when(i >= n_prompt)
    def _():
        _mix_ffn_rows(xs_ref, as_ref, ss_ref, os_ref, *w_refs)


def _mix_ffn(layer, xp, attn_p, ssm_p, xs, attn_s, ssm_s, w_in, wao, wso, wout, wup, wdn, g1, bg, g2):
    rp, D = xp.shape
    rs = xs.shape[0]
    tp = min(512, rp)
    ts = min(512, rs)
    assert rp % tp == 0 and rs % ts == 0
    n_p, n_s = rp // tp, rs // ts
    prow = lambda last: pl.BlockSpec((tp, last), lambda i: (jnp.minimum(i, n_p - 1), 0))
    srow = lambda last: pl.BlockSpec((ts, last), lambda i: (jnp.maximum(i - n_p, 0), 0))
    hbm = pl.BlockSpec(memory_space=pl.ANY)
    resident = lambda a: pltpu.VMEM(a.shape[1:], BF16)
    return pl.pallas_call(
        functools.partial(_mix_ffn_kernel, layer, n_p),
        grid=(n_p + n_s,),
        in_specs=[prow(D), prow(attn_p.shape[-1]), prow(ssm_p.shape[-1]),
                  srow(D), srow(attn_s.shape[-1]), srow(ssm_s.shape[-1])]
                 + [hbm] * 6 + [_layer_spec(a, layer) for a in (g1, bg, g2)],
        out_specs=[prow(D), srow(D)],
        out_shape=[jax.ShapeDtypeStruct((rp, D), F32), jax.ShapeDtypeStruct((rs, D), F32)],
        scratch_shapes=[pltpu.VMEM((D, bg.shape[-1]), BF16), resident(wao), resident(wso), resident(wout),
                        resident(wup), resident(wdn),
                        pltpu.VMEM((FFN_WEIGHT_SLOTS, 2 * LANES, D), F32),
                        pltpu.SemaphoreType.DMA((FFN_WEIGHT_SLOTS,))],
        compiler_params=pltpu.CompilerParams(dimension_semantics=("arbitrary",), vmem_limit_bytes=VMEM_LIMIT),
        name="mix_ffn",
    )(xp, attn_p, ssm_p, xs, attn_s, ssm_s, w_in, wao, wso, wout, wup, wdn, g1, bg, g2)


def _seg_matrix(width):
    i = np.arange(width) // HEAD_DIM
    return jnp.asarray((i[:, None] == i[None, :]).astype(np.float32) / HEAD_DIM, BF16)


def kernel(x_prompt, x_sample, cache_k, cache_v, state_ssm_re, state_ssm_im, norm1_g, w_in, b_gate, q_norm_g,
           k_norm_g, attn_sinks, lam_re, lam_im, log_step, b_re, b_im, c_re, c_im, d_skip, w_glu, b_glu,
           w_attn_o, w_ssm_o, w_out, norm2_g, w_up, w_down):
    B, T, D = x_prompt.shape
    DB, DS, _ = x_sample.shape
    L = w_in.shape[0]
    aw = N_HEADS * HEAD_DIM
    kw = N_KV_HEADS * HEAD_DIM
    sw = d_skip.shape[-1]
    G, P = lam_re.shape[1:]
    past = cache_k.shape[2]
    assert past == WINDOW and T % WINDOW == 0 and (B * WINDOW) % ROW_TILE == 0 and (DB * DS) % ROW_TILE == 0

    ar, ai, bx, cm = _ssm_prep(lam_re, lam_im, log_step, b_re, b_im, c_re, c_im)
    seg = _seg_matrix(aw)
    rows = lambda a: a.reshape(L, 1, -1)
    g1 = rows(norm1_g)
    mixer_w = (w_in, w_glu, g1, rows(jnp.tile(q_norm_g, (1, N_HEADS))), rows(jnp.tile(k_norm_g, (1, N_KV_HEADS))),
               ar, ai, bx, cm, rows(d_skip), rows(b_glu))
    ffn_w = (w_in, w_attn_o, w_ssm_o, w_out, w_up, w_down, g1, rows(b_gate), rows(norm2_g))
    kv_t = lambda a: a.transpose(0, 1, 3, 4, 2).reshape(L, DB, kw, past)
    st_t = lambda a: a.transpose(0, 2, 3, 1).reshape(L, G * P, DB)
    ckt, cvt, h0rt, h0it = kv_t(cache_k), kv_t(cache_v), st_t(state_ssm_re), st_t(state_ssm_im)

    yp = x_prompt.reshape(B * T, D)
    ys = x_sample.transpose(1, 0, 2).reshape(DS * DB, D)
    outs = [[] for _ in range(6)]
    kcat = vcat = None
    for l in range(L):
        attn_p, ssm_p, kp, vp, hrp, hip = _mixer_prompt(l, yp.reshape(B, T, D), attn_sinks, seg, *mixer_w)
        attn_s, ssm_s, kcat, vcat, hrs, his = _mixer_sample(l, ys, ckt, cvt, h0rt, h0it, kcat, vcat, attn_sinks, seg,
                                                            *mixer_w)
        yp, ys = _mix_ffn(l, yp, attn_p.reshape(B * T, aw), ssm_p.reshape(B * T, sw), ys, attn_s, ssm_s, *ffn_w)
        for o, v in zip(outs, (kp, vp, hrp, hip, hrs, his)):
            o.append(v)

    kp, vp, hrp, hip, hrs, his = (jnp.stack(o) for o in outs)
    kv_back = lambda a: a.reshape(a.shape[:2] + (N_KV_HEADS, HEAD_DIM, a.shape[-1])).transpose(0, 1, 4, 2, 3)
    st_back = lambda a: a.reshape(L, G, P, DB).transpose(0, 3, 1, 2)
    y_sample = ys.reshape(DS, DB, D).transpose(1, 0, 2)
    return (yp.reshape(B, T, D), y_sample, kv_back(kp), kv_back(vp), hrp.reshape(L, B, G, P), hip.reshape(L, B, G, P),
            kv_back(kcat), kv_back(vcat), st_back(hrs), st_back(his))
```

```python
import functools
import math

import numpy as np
import jax
import jax.numpy as jnp
from jax import lax
from jax.experimental import pallas as pl
from jax.experimental.pallas import tpu as pltpu

F32 = jnp.float32
BF16 = jnp.bfloat16

HEAD_DIM = 64
N_HEADS = 8
N_KV_HEADS = 2
Q_GROUP = N_HEADS // N_KV_HEADS
WINDOW = 128
GROUP_CH = 16
SSM_STATE = 64
EPS = 1e-6
NEG_INF = -1e30
LOG2E = math.log2(math.e)
LANES = 128
SUBLANES = 8
ROW_TILE = 256
U_PITCH = WINDOW + SUBLANES
VMEM_LIMIT = 60 * 1024 * 1024

WEIGHT_SLOTS = 4
SLOPES = tuple(2.0 ** (-8.0 * (h + 1) / N_HEADS) for h in range(N_HEADS))


def _rms(x, g):
    return x * lax.rsqrt(jnp.mean(x * x, axis=-1, keepdims=True) + EPS) * g


def _seg_rms(x, seg, g):
    ms = jnp.dot((x * x).astype(BF16), seg, preferred_element_type=F32)
    return x * lax.rsqrt(ms + EPS) * g


def _gelu_tanh(y):
    return 0.5 * y * (1.0 + jnp.tanh(math.sqrt(2.0 / math.pi) * (y + 0.044715 * (y * y * y))))


def _const_spec(a):
    n = a.ndim
    return pl.BlockSpec(a.shape, lambda *_: (0,) * n, pipeline_mode=pl.Buffered(1))


def _layer_spec(a, layer):
    n = a.ndim - 1
    return pl.BlockSpec((None,) + a.shape[1:], lambda *_: (layer,) + (0,) * n, pipeline_mode=pl.Buffered(1))


def _stage_weights(chunks, stage, sem):
    slots = stage.shape[0]

    def copy(k):
        src, _, cols = chunks[k]
        return pltpu.make_async_copy(src, stage.at[k % slots, :, pl.ds(0, cols)], sem.at[k % slots])

    for k in range(min(slots - 1, len(chunks))):
        copy(k).start()
    for k, (_, dst, cols) in enumerate(chunks):
        if k + slots - 1 < len(chunks):
            copy(k + slots - 1).start()
        copy(k).wait()
        dst[...] = stage[k % slots, :, :cols].astype(BF16)


def _weight_chunks(src, dst, rows, cols, chunk_rows, chunk_cols, src_col0=0):
    return [(src.at[pl.ds(r, chunk_rows), pl.ds(src_col0 + c, chunk_cols)],
             dst.at[pl.ds(r, chunk_rows), pl.ds(c, chunk_cols)], chunk_cols)
            for r in range(0, rows, chunk_rows) for c in range(0, cols, chunk_cols)]


def _ssm_prep_kernel(lr_ref, li_ref, ls_ref, btr_ref, bti_ref, ctr_ref, cti_ref,
                     ar_ref, ai_ref, bx_ref, cm_ref):
    lr = lr_ref[0]
    li = li_ref[0]
    step = jnp.exp(ls_ref[0])
    mag = jnp.exp(lr * step)
    ar = mag * jnp.cos(li * step)
    ai = mag * jnp.sin(li * step)
    den = lr * lr + li * li
    cr = ((ar - 1.0) * lr + ai * li) / den
    ci = (ai * lr - (ar - 1.0) * li) / den
    ar_ref[0] = ar
    ai_ref[0] = ai
    nj = btr_ref.shape[1]
    gp = SUBLANES * SSM_STATE
    rows_b = lax.broadcasted_iota(jnp.int32, (LANES, gp), 0) // GROUP_CH
    cols_b = lax.broadcasted_iota(jnp.int32, (LANES, gp), 1) // SSM_STATE
    mask_b = rows_b == cols_b
    rows_c = lax.broadcasted_iota(jnp.int32, (gp, LANES), 0) // SSM_STATE
    cols_c = lax.broadcasted_iota(jnp.int32, (gp, LANES), 1) // GROUP_CH
    mask_c = rows_c == cols_c
    for j in range(nj):
        crj = cr[:, j * gp:(j + 1) * gp]
        cij = ci[:, j * gp:(j + 1) * gp]
        b_re = btr_ref[0, j]
        b_im = bti_ref[0, j]
        bb_re = jnp.where(mask_b, crj * b_re - cij * b_im, 0.0)
        bb_im = jnp.where(mask_b, crj * b_im + cij * b_re, 0.0)
        bx_ref[0, j, :, :gp] = bb_re.astype(BF16)
        bx_ref[0, j, :, gp:] = bb_im.astype(BF16)
        cm_ref[0, j, :gp, :] = jnp.where(mask_c, ctr_ref[0, j], 0.0).astype(BF16)
        cm_ref[0, j, gp:, :] = jnp.where(mask_c, -cti_ref[0, j], 0.0).astype(BF16)


def _ssm_prep(lam_re, lam_im, log_step, b_re, b_im, c_re, c_im):
    L, G, P = lam_re.shape
    GC = b_re.shape[-1]
    gpb = LANES // GC
    nj = G // gpb
    gp = gpb * P
    flat = lambda a: a.reshape(L, 1, G * P)
    ls = jnp.repeat(log_step, P, axis=-1).reshape(L, 1, G * P)

    def b_tiles(b):
        t = b.reshape(L, nj, gpb, P, GC).transpose(0, 1, 4, 2, 3).reshape(L, nj, 1, GC, gp)
        return jnp.broadcast_to(t, (L, nj, gpb, GC, gp)).reshape(L, nj, LANES, gp)

    def c_tiles(c):
        t = c.reshape(L, nj, gpb, GC, P).transpose(0, 1, 2, 4, 3).reshape(L, nj, gp, 1, GC)
        return jnp.broadcast_to(t, (L, nj, gp, gpb, GC)).reshape(L, nj, gp, LANES)

    vec = pl.BlockSpec((1, 1, G * P), lambda l: (l, 0, 0))
    bt = pl.BlockSpec((1, nj, LANES, gp), lambda l: (l, 0, 0, 0))
    ct = pl.BlockSpec((1, nj, gp, LANES), lambda l: (l, 0, 0, 0))
    return pl.pallas_call(
        _ssm_prep_kernel,
        grid=(L,),
        in_specs=[vec, vec, vec, bt, bt, ct, ct],
        out_specs=[vec, vec,
                   pl.BlockSpec((1, nj, LANES, 2 * gp), lambda l: (l, 0, 0, 0)),
                   pl.BlockSpec((1, nj, 2 * gp, LANES), lambda l: (l, 0, 0, 0))],
        out_shape=[jax.ShapeDtypeStruct((L, 1, G * P), F32), jax.ShapeDtypeStruct((L, 1, G * P), F32),
                   jax.ShapeDtypeStruct((L, nj, LANES, 2 * gp), BF16),
                   jax.ShapeDtypeStruct((L, nj, 2 * gp, LANES), BF16)],
        name="ssm_prep",
    )(flat(lam_re), flat(lam_im), ls, b_tiles(b_re), b_tiles(b_im), c_tiles(c_re), c_tiles(c_im))


def _project_rows(x, g1_ref, w_ref, seg_ref, qg_ref, kg_ref):
    aw = N_HEADS * HEAD_DIM
    kw = N_KV_HEADS * HEAD_DIM
    xn = _rms(x, g1_ref[...]).astype(BF16)
    z = jnp.dot(xn, w_ref[...], preferred_element_type=F32)
    q = _seg_rms(z[:, :aw], seg_ref[...], qg_ref[...]) * (HEAD_DIM ** -0.5 * LOG2E)
    k = _seg_rms(z[:, aw:aw + kw], seg_ref[:kw, :kw], kg_ref[...])
    v = z[:, aw + kw:aw + 2 * kw]
    u = z[:, aw + 2 * kw:]
    lower = (lax.broadcasted_iota(jnp.int32, q.shape, 1) % LANES) < HEAD_DIM
    q_lo = jnp.where(lower, q, 0.0)
    q_hi = jnp.where(lower, 0.0, q)
    return q_lo, q_hi, k, v, u


def _stage_mixer_weights(layer, w_in_hbm, wglu_hbm, w_ref, wglu_ref, stage, sem):
    _stage_weights(
        _weight_chunks(w_in_hbm.at[layer], w_ref, w_ref.shape[0], w_ref.shape[1], stage.shape[1], w_ref.shape[1])
        + _weight_chunks(wglu_hbm.at[layer], wglu_ref, wglu_ref.shape[0], wglu_ref.shape[1], stage.shape[1],
                         wglu_ref.shape[1]),
        stage, sem)


def _mixer_weight_scratch(D, in_cols, sw):
    return [pltpu.VMEM((D, in_cols), BF16), pltpu.VMEM((sw, sw), BF16),
            pltpu.VMEM((WEIGHT_SLOTS, LANES, in_cols), F32), pltpu.SemaphoreType.DMA((WEIGHT_SLOTS,))]


def _ssm_readout(y, u, d_ref, wglu_ref, bglu_ref):
    y = _gelu_tanh(y + d_ref[...] * u)
    gate = jnp.dot(y.astype(BF16), wglu_ref[...], preferred_element_type=F32) + bglu_ref[...]
    return y * jax.nn.sigmoid(gate)


def _softmax_pv(s, sink, v):
    m = jnp.maximum(jnp.max(s, axis=-1, keepdims=True), sink)
    e = jnp.exp2(s - m)
    den = jnp.sum(e, axis=-1, keepdims=True) + jnp.exp2(sink - m)
    return jnp.dot(e.astype(BF16), v, preferred_element_type=F32) * (1.0 / den)


def _mixer_prompt_kernel(layer, sink_ref, x_ref, seg_ref, w_in_hbm, wglu_hbm, g1_ref, qg_ref, kg_ref,
                         ar_ref, ai_ref, bx_ref, cm_ref, d_ref, bglu_ref,
                         attn_ref, ssm_ref, kout_ref, vout_ref, hr_ref, hi_ref,
                         w_ref, wglu_ref, stage, sem, q_s, k_s, v_s, u_s, y_s, *x_s):
    c = pl.program_id(0)
    nb, tt, _ = x_ref.shape
    rows = nb * tt
    n_tiles = rows // ROW_TILE
    bpt = ROW_TILE // tt
    n_state = ar_ref.shape[-1]
    nj = bx_ref.shape[0]
    gp = n_state // nj
    xr_s, xi_s = x_s[:nj], x_s[nj:]
    kw = kout_ref.shape[1]

    @pl.when(c == 0)
    def _():
        _stage_mixer_weights(layer, w_in_hbm, wglu_hbm, w_ref, wglu_ref, stage, sem)
        k_s[:, :, :tt, :] = jnp.zeros((2, nb, tt, kw), BF16)
        v_s[:, :, :tt, :] = jnp.zeros((2, nb, tt, kw), BF16)
        hr_ref[...] = jnp.zeros(hr_ref.shape, F32)
        hi_ref[...] = jnp.zeros(hi_ref.shape, F32)

    def project(r, carry):
        x = x_ref[pl.ds(r * bpt, bpt)].reshape(ROW_TILE, x_ref.shape[-1])
        q_lo, q_hi, k, v, u = _project_rows(x, g1_ref, w_ref, seg_ref, qg_ref, kg_ref)
        q_s[0, pl.ds(r * bpt, bpt)] = q_lo.astype(BF16).reshape(bpt, tt, -1)
        q_s[1, pl.ds(r * bpt, bpt)] = q_hi.astype(BF16).reshape(bpt, tt, -1)
        k_s[0, pl.ds(r * bpt, bpt), tt:, :] = k.astype(BF16).reshape(bpt, tt, -1)
        v_s[0, pl.ds(r * bpt, bpt), tt:, :] = v.astype(BF16).reshape(bpt, tt, -1)
        k_s[1, pl.ds(r * bpt, bpt), tt:, :] = pltpu.roll(k, HEAD_DIM, 1).astype(BF16).reshape(bpt, tt, -1)
        v_s[1, pl.ds(r * bpt, bpt), tt:, :] = pltpu.roll(v, HEAD_DIM, 1).astype(BF16).reshape(bpt, tt, -1)
        for bb in range(bpt):
            kout_ref[r * bpt + bb] = k[bb * tt:(bb + 1) * tt, :].T
            vout_ref[r * bpt + bb] = v[bb * tt:(bb + 1) * tt, :].T
        for bb in range(bpt):
            row0 = pl.multiple_of((r * bpt + bb) * U_PITCH, SUBLANES)
            for j in range(nj):
                u_s[j, pl.ds(row0, tt), :] = u[bb * tt:(bb + 1) * tt, j * LANES:(j + 1) * LANES]
        return carry

    lax.fori_loop(0, n_tiles, project, 0, unroll=True)

    t_idx = lax.broadcasted_iota(jnp.int32, (tt, 2 * tt), 0)
    j_idx = lax.broadcasted_iota(jnp.int32, (tt, 2 * tt), 1)
    dist = t_idx - j_idx + tt
    valid = (dist >= 0) & (dist <= WINDOW) & (j_idx >= jnp.where(c > 0, 0, tt))
    key_pos = lax.broadcasted_iota(jnp.int32, (1, 2 * tt), 1).astype(F32)
    row_pos = (lax.broadcasted_iota(jnp.int32, (tt, 1), 0) + tt).astype(F32)
    lower = lax.broadcasted_iota(jnp.int32, (tt, LANES), 1) < HEAD_DIM

    def attend(b):
        for p in range(N_HEADS // 2):
            kv = (2 * p) // Q_GROUP
            outs = []
            for e in range(2):
                h = 2 * p + e
                swap = 0 if kv == e else 1
                qm = q_s[e, b, :, p * LANES:(p + 1) * LANES]
                s = lax.dot_general(qm, k_s[swap, b], (((1,), (1,)), ((), ())), preferred_element_type=F32)
                slope = SLOPES[h] * LOG2E
                s = jnp.where(valid, s + slope * key_pos, NEG_INF)
                sink = sink_ref[layer, h] * LOG2E + slope * row_pos
                outs.append(_softmax_pv(s, sink, v_s[swap, b]))
            attn_ref[b, :, p * LANES:(p + 1) * LANES] = jnp.where(lower, outs[0], outs[1]).astype(BF16)

    def ssm_in_and_attend(r, carry):
        rs = pl.ds(r * ROW_TILE, ROW_TILE)
        t0 = r * (ROW_TILE // nb)
        for j in range(nj):
            up = jnp.concatenate([u_s[j, pl.ds(t0 + step, nb, stride=U_PITCH), :]
                                  for step in range(ROW_TILE // nb)], axis=0).astype(BF16)
            xj = jnp.dot(up, bx_ref[j], preferred_element_type=F32)
            xr_s[j][rs, :] = xj[:, :gp]
            xi_s[j][rs, :] = xj[:, gp:]
        for bb in range(bpt):
            attend(r * bpt + bb)
        return carry

    lax.fori_loop(0, n_tiles, ssm_in_and_attend, 0, unroll=2)

    k_s[:, :, :tt, :] = k_s[:, :, tt:, :]
    v_s[:, :, :tt, :] = v_s[:, :, tt:, :]

    def ssm_out_tile(j, r):
        rs = pl.ds(r * ROW_TILE, ROW_TILE)
        y_s[j, rs, :] = jnp.dot(xr_s[j][rs, :].astype(BF16), cm_ref[j, :gp, :], preferred_element_type=F32)
        y_s[j, rs, :] = y_s[j, rs, :] + jnp.dot(xi_s[j][rs, :].astype(BF16), cm_ref[j, gp:, :],
                                                preferred_element_type=F32)

    for lb in range(nj):
        sl = slice(lb * gp, (lb + 1) * gp)
        a_re = jnp.broadcast_to(ar_ref[:, sl], (nb, gp))
        a_im = jnp.broadcast_to(ai_ref[:, sl], (nb, gp))

        def scan_tile(r, h, lb=lb, a_re=a_re, a_im=a_im):
            if lb > 0:
                ssm_out_tile(lb - 1, r)
            h_re, h_im = h
            for step in range(ROW_TILE // nb):
                row = pl.ds(pl.multiple_of(r * ROW_TILE + step * nb, nb), nb)
                n_re = a_re * h_re - a_im * h_im + xr_s[lb][row, :]
                n_im = a_re * h_im + a_im * h_re + xi_s[lb][row, :]
                xr_s[lb][row, :] = n_re
                xi_s[lb][row, :] = n_im
                h_re, h_im = n_re, n_im
            return h_re, h_im

        h_re, h_im = lax.fori_loop(0, n_tiles, scan_tile, (hr_ref[:, sl], hi_ref[:, sl]), unroll=True)
        hr_ref[:, sl] = h_re
        hi_ref[:, sl] = h_im

    def ssm_out_last(r, carry):
        ssm_out_tile(nj - 1, r)
        return carry

    lax.fori_loop(0, n_tiles, ssm_out_last, 0, unroll=True)

    def readout(b, carry):
        y = jnp.concatenate([y_s[j, pl.ds(b, tt, stride=nb), :] for j in range(nj)], axis=1)
        row0 = pl.multiple_of(b * U_PITCH, SUBLANES)
        u = jnp.concatenate([u_s[j, pl.ds(row0, tt), :] for j in range(nj)], axis=1)
        ssm_ref[b] = _ssm_readout(y, u, d_ref, wglu_ref, bglu_ref).astype(BF16)
        return carry

    lax.fori_loop(0, nb, readout, 0, unroll=4)


def _mixer_prompt(layer, x, sinks, seg, w_in, wglu, g1, qg, kg, ar, ai, bx, cm, d, bglu):
    nb, T, D = x.shape
    tt = WINDOW
    rows = nb * tt
    aw = N_HEADS * HEAD_DIM
    kw = N_KV_HEADS * HEAD_DIM
    sw = d.shape[-1]
    n_state = ar.shape[-1]
    nj = bx.shape[1]
    chunk = lambda last: pl.BlockSpec((nb, tt, last), lambda c: (0, c, 0))
    whole = pl.BlockSpec((nb, kw, tt), lambda c: (0, 0, 0))
    state = pl.BlockSpec((nb, n_state), lambda c: (0, 0))
    hbm = pl.BlockSpec(memory_space=pl.ANY)
    stacked = (g1, qg, kg, ar, ai, bx, cm, d, bglu)
    return pl.pallas_call(
        functools.partial(_mixer_prompt_kernel, layer),
        grid=(T // tt,),
        in_specs=[pl.BlockSpec(memory_space=pltpu.SMEM), chunk(D), _const_spec(seg), hbm, hbm]
                 + [_layer_spec(a, layer) for a in stacked],
        out_specs=[chunk(aw), chunk(sw), whole, whole, state, state],
        out_shape=[jax.ShapeDtypeStruct((nb, T, aw), BF16), jax.ShapeDtypeStruct((nb, T, sw), BF16),
                   jax.ShapeDtypeStruct((nb, kw, tt), F32), jax.ShapeDtypeStruct((nb, kw, tt), F32),
                   jax.ShapeDtypeStruct((nb, n_state), F32), jax.ShapeDtypeStruct((nb, n_state), F32)],
        scratch_shapes=_mixer_weight_scratch(D, aw + 2 * kw + sw, sw)
                       + [pltpu.VMEM((2, nb, tt, aw), BF16),
                          pltpu.VMEM((2, nb, 2 * tt, kw), BF16), pltpu.VMEM((2, nb, 2 * tt, kw), BF16),
                          pltpu.VMEM((sw // LANES, nb * U_PITCH, LANES), F32),
                          pltpu.VMEM((nj, rows, LANES), F32)]
                       + [pltpu.VMEM((rows, n_state // nj), F32)] * (2 * nj),
        compiler_params=pltpu.CompilerParams(dimension_semantics=("arbitrary",), vmem_limit_bytes=VMEM_LIMIT),
        name="mixer_prompt",
    )(sinks, x, seg, w_in, wglu, *stacked)


def _mixer_sample_kernel(layer, sink_ref, x_ref, ckt_ref, cvt_ref, h0rt_ref, h0it_ref, *refs):
    if layer:
        prevk_ref, prevv_ref, *refs = refs
    (seg_ref, w_in_hbm, wglu_hbm, g1_ref, qg_ref, kg_ref, ar_ref, ai_ref, bx_ref, cm_ref, d_ref, bglu_ref,
     attn_ref, ssm_ref, kcat_ref, vcat_ref, hrt_ref, hit_ref,
     w_ref, wglu_ref, stage, sem, q_s, o_s, u_s, xr_s, xi_s, knew_s, vnew_s,
     h0r_ref, h0i_ref, hr_ref, hi_ref, *group_s) = refs
    i = pl.program_id(0)
    rows = x_ref.shape[0]
    nbatch = h0rt_ref.shape[1]
    ts = rows // nbatch
    gb_step, kw, past = ckt_ref.shape
    gb = SUBLANES
    n_tiles = rows // ROW_TILE
    n_state = ar_ref.shape[-1]
    nj = bx_ref.shape[0]
    gp = n_state // nj

    @pl.when(i == 0)
    def _():
        _stage_mixer_weights(layer, w_in_hbm, wglu_hbm, w_ref, wglu_ref, stage, sem)

        def project(r, carry):
            rs = pl.ds(r * ROW_TILE, ROW_TILE)
            q_lo, q_hi, k, v, u = _project_rows(x_ref[rs, :], g1_ref, w_ref, seg_ref, qg_ref, kg_ref)
            q_s[0, rs, :] = q_lo
            q_s[1, rs, :] = q_hi
            knew_s[rs, :] = k
            vnew_s[rs, :] = v
            u_s[rs, :] = u
            return carry

        lax.fori_loop(0, n_tiles, project, 0)

    n_q = ts * Q_GROUP * gb
    n_keys = gb * past + ts * gb
    r_idx = lax.broadcasted_iota(jnp.int32, (n_q, n_keys), 0)
    c_idx = lax.broadcasted_iota(jnp.int32, (n_q, n_keys), 1)
    q_t = r_idx // (Q_GROUP * gb)
    q_b = r_idx % gb
    is_cache = c_idx < gb * past
    key_b = jnp.where(is_cache, c_idx // past, (c_idx - gb * past) % gb)
    dist = jnp.where(is_cache, past + q_t - c_idx % past, q_t - (c_idx - gb * past) // gb)
    valid = (key_b == q_b) & (dist >= 0) & (dist <= WINDOW)
    dist_f = dist.astype(F32)
    hh = (lax.broadcasted_iota(jnp.int32, (n_q, 1), 0) // gb) % Q_GROUP
    lower = lax.broadcasted_iota(jnp.int32, (gb, LANES), 1) < HEAD_DIM
    def attend_group(sg):
        ck_ref, cv_ref, kroll_s, vroll_s = group_s[4 * sg:4 * sg + 4]
        sbase = pl.multiple_of(i * gb_step + sg * gb, gb)
        for cat_ref, ct_ref, c_ref, roll_s, new_s in ((kcat_ref, ckt_ref, ck_ref, kroll_s, knew_s),
                                                      (vcat_ref, cvt_ref, cv_ref, vroll_s, vnew_s)):
            for b in range(gb):
                c_ref[b * past:(b + 1) * past, :] = ct_ref[sg * gb + b].T
            for b in range(gb):
                roll_s[b * past:(b + 1) * past - ts, :] = c_ref[b * past + ts:(b + 1) * past, :]
            for t in range(ts):
                roll_s[pl.ds(past - ts + t, gb, stride=past), :] = new_s[pl.ds(sbase + t * nbatch, gb), :]
            for b in range(gb):
                cat_ref[layer, sg * gb + b] = roll_s[b * past:(b + 1) * past, :].T

        k_all = jnp.concatenate(
            [ck_ref[...]] + [knew_s[pl.ds(sbase + t * nbatch, gb), :] for t in range(ts)], axis=0).astype(BF16)
        v_all = jnp.concatenate(
            [cv_ref[...]] + [vnew_s[pl.ds(sbase + t * nbatch, gb), :] for t in range(ts)], axis=0).astype(BF16)
        outs = []
        for kvh in range(N_KV_HEADS):
            slope = jnp.zeros((n_q, 1), F32)
            sink = jnp.zeros((n_q, 1), F32)
            for g in range(Q_GROUP):
                h = kvh * Q_GROUP + g
                slope = jnp.where(hh == g, SLOPES[h] * LOG2E, slope)
                sink = jnp.where(hh == g, sink_ref[layer, h] * LOG2E, sink)
            pieces = []
            for t in range(ts):
                for g in range(Q_GROUP):
                    h = kvh * Q_GROUP + g
                    piece = q_s[h % 2, pl.ds(sbase + t * nbatch, gb), (h // 2) * LANES:(h // 2 + 1) * LANES]
                    pieces.append(piece if h % 2 == kvh else pltpu.roll(piece, HEAD_DIM, 1))
            qm = jnp.concatenate(pieces, axis=0).astype(BF16)
            s = lax.dot_general(qm, k_all, (((1,), (1,)), ((), ())), preferred_element_type=F32)
            s = jnp.where(valid, s - slope * dist_f, NEG_INF)
            outs.append(_softmax_pv(s, sink, v_all))
        for t in range(ts):
            for p in range(N_HEADS // 2):
                kvh = (2 * p) // Q_GROUP
                r0 = (t * Q_GROUP + 2 * p - kvh * Q_GROUP) * gb
                lo = outs[kvh][r0:r0 + gb]
                hi = outs[kvh][r0 + gb:r0 + 2 * gb]
                if kvh == 1:
                    lo = pltpu.roll(lo, HEAD_DIM, 1)
                else:
                    hi = pltpu.roll(hi, HEAD_DIM, 1)
                o_s[pl.ds(sbase + t * nbatch, gb), p * LANES:(p + 1) * LANES] = jnp.where(lower, lo, hi)

    for sg in range(gb_step // gb):
        attend_group(sg)
    if layer:
        kcat_ref[:layer] = prevk_ref[...]
        vcat_ref[:layer] = prevv_ref[...]

    @pl.when(i == pl.num_programs(0) - 1)
    def _():
        attn_ref[...] = o_s[...].astype(BF16)

        def ssm_in(r, carry):
            rs = pl.ds(r * ROW_TILE, ROW_TILE)
            ub = u_s[rs, :].astype(BF16)
            for j in range(nj):
                xj = jnp.dot(ub[:, j * LANES:(j + 1) * LANES], bx_ref[j], preferred_element_type=F32)
                xr_s[rs, j * gp:(j + 1) * gp] = xj[:, :gp]
                xi_s[rs, j * gp:(j + 1) * gp] = xj[:, gp:]
            return carry

        lax.fori_loop(0, n_tiles, ssm_in, 0)

        for blk in range(n_state // LANES):
            bs = slice(blk * LANES, (blk + 1) * LANES)
            h0r_ref[:, bs] = h0rt_ref[bs, :].T
            h0i_ref[:, bs] = h0it_ref[bs, :].T

        lane_blk = 4 * LANES
        for lb in range(n_state // lane_blk):
            sl = slice(lb * lane_blk, (lb + 1) * lane_blk)
            a_re = jnp.broadcast_to(ar_ref[:, sl], (SUBLANES, lane_blk))
            a_im = jnp.broadcast_to(ai_ref[:, sl], (SUBLANES, lane_blk))

            def scan(bt, carry):
                b0 = pl.multiple_of(bt * SUBLANES, SUBLANES)
                h_re = h0r_ref[pl.ds(b0, SUBLANES), sl]
                h_im = h0i_ref[pl.ds(b0, SUBLANES), sl]
                for t in range(ts):
                    rs = pl.ds(b0 + t * nbatch, SUBLANES)
                    n_re = a_re * h_re - a_im * h_im + xr_s[rs, sl]
                    n_im = a_re * h_im + a_im * h_re + xi_s[rs, sl]
                    xr_s[rs, sl] = n_re
                    xi_s[rs, sl] = n_im
                    h_re, h_im = n_re, n_im
                hr_ref[pl.ds(b0, SUBLANES), sl] = h_re
                hi_ref[pl.ds(b0, SUBLANES), sl] = h_im
                return carry

            lax.fori_loop(0, nbatch // SUBLANES, scan, 0)

        for blk in range(n_state // LANES):
            bs = slice(blk * LANES, (blk + 1) * LANES)
            hrt_ref[bs, :] = hr_ref[:, bs].T
            hit_ref[bs, :] = hi_ref[:, bs].T

        def ssm_out(r, carry):
            rs = pl.ds(r * ROW_TILE, ROW_TILE)
            ys = []
            for j in range(nj):
                h_cat = jnp.concatenate([xr_s[rs, j * gp:(j + 1) * gp], xi_s[rs, j * gp:(j + 1) * gp]],
                                        axis=1).astype(BF16)
                ys.append(jnp.dot(h_cat, cm_ref[j], preferred_element_type=F32))
            y = jnp.concatenate(ys, axis=1)
            ssm_ref[rs, :] = _ssm_readout(y, u_s[rs, :], d_ref, wglu_ref, bglu_ref).astype(BF16)
            return carry

        lax.fori_loop(0, n_tiles, ssm_out, 0)


def _mixer_sample(layer, x, ckt, cvt, h0rt, h0it, prev_k, prev_v, sinks, seg, w_in, wglu,
                  g1, qg, kg, ar, ai, bx, cm, d, bglu):
    rows, D = x.shape
    _, nbatch, kw, past = ckt.shape
    aw = N_HEADS * HEAD_DIM
    sw = d.shape[-1]
    n_state = ar.shape[-1]
    gb = SUBLANES
    assert nbatch % LANES == 0 and past == LANES and kw == LANES
    cache = pl.BlockSpec((None, gb, kw, past), lambda i: (layer, i, 0, 0))
    h0 = pl.BlockSpec((None, n_state, nbatch), lambda i: (layer, 0, 0))
    prev = pl.BlockSpec((layer, gb, kw, past), lambda i: (0, i, 0, 0))
    out2 = lambda last: pl.BlockSpec((rows, last), lambda i: (0, 0))
    cat = pl.BlockSpec((layer + 1, gb, kw, past), lambda i: (0, i, 0, 0))
    state = pl.BlockSpec((n_state, nbatch), lambda i: (0, 0))
    hbm = pl.BlockSpec(memory_space=pl.ANY)
    stacked = (g1, qg, kg, ar, ai, bx, cm, d, bglu)
    prevs = (prev_k, prev_v) if layer else ()
    return pl.pallas_call(
        functools.partial(_mixer_sample_kernel, layer),
        grid=(nbatch // gb,),
        in_specs=[pl.BlockSpec(memory_space=pltpu.SMEM), pl.BlockSpec(x.shape, lambda i: (0, 0)), cache, cache, h0, h0]
                 + [prev] * len(prevs) + [_const_spec(seg), hbm, hbm] + [_layer_spec(a, layer) for a in stacked],
        out_specs=[out2(aw), out2(sw), cat, cat, state, state],
        out_shape=[jax.ShapeDtypeStruct((rows, aw), BF16), jax.ShapeDtypeStruct((rows, sw), BF16),
                   jax.ShapeDtypeStruct((layer + 1, nbatch, kw, past), F32),
                   jax.ShapeDtypeStruct((layer + 1, nbatch, kw, past), F32),
                   jax.ShapeDtypeStruct((n_state, nbatch), F32), jax.ShapeDtypeStruct((n_state, nbatch), F32)],
        scratch_shapes=_mixer_weight_scratch(D, aw + 2 * kw + sw, sw)
                       + [pltpu.VMEM((2, rows, aw), F32), pltpu.VMEM((rows, aw), F32), pltpu.VMEM((rows, sw), F32),
                          pltpu.VMEM((rows, n_state), F32), pltpu.VMEM((rows, n_state), F32),
                          pltpu.VMEM((rows, kw), F32), pltpu.VMEM((rows, kw), F32)]
                       + [pltpu.VMEM((nbatch, n_state), F32)] * 4
                       + [pltpu.VMEM((SUBLANES * past, kw), F32)] * (4 * gb // SUBLANES),
        compiler_params=pltpu.CompilerParams(dimension_semantics=("arbitrary",), vmem_limit_bytes=VMEM_LIMIT),
        name="mixer_sample",
    )(sinks, x, ckt, cvt, h0rt, h0it, *prevs, seg, w_in, wglu, *stacked)


def _mix_ffn_rows(x_ref, attn_ref, ssm_ref, o_ref, g1_ref, wg_ref, bg_ref, wao_ref, wso_ref, wout_ref,
                  g2_ref, wup_ref, wdn_ref):
    x = x_ref[...]
    D = x.shape[-1]
    xn = _rms(x, g1_ref[...]).astype(BF16)
    attn = attn_ref[...]
    ssm = ssm_ref[...]
    o_ref[...] = x
    cw = 2 * LANES
    for cc in range(D // cw):
        cs = slice(cc * cw, (cc + 1) * cw)
        gs = slice(D + cc * cw, D + (cc + 1) * cw)
        g_attn = jax.nn.sigmoid(jnp.dot(xn, wg_ref[:, cs], preferred_element_type=F32) + bg_ref[:, cs])
        g_ssm = jax.nn.sigmoid(jnp.dot(xn, wg_ref[:, gs], preferred_element_type=F32) + bg_ref[:, gs])
        mixed = (g_attn * jnp.dot(attn, wao_ref[:, cs], preferred_element_type=F32)
                 + g_ssm * jnp.dot(ssm, wso_ref[:, cs], preferred_element_type=F32))
        o_ref[...] += jnp.dot(mixed.astype(BF16), wout_ref[cs, :], preferred_element_type=F32)
    xn2 = _rms(o_ref[...], g2_ref[...]).astype(BF16)
    fw = 4 * LANES
    for fc in range(wup_ref.shape[-1] // fw):
        fs = slice(fc * fw, (fc + 1) * fw)
        hdn = jnp.maximum(jnp.dot(xn2, wup_ref[:, fs], preferred_element_type=F32), 0.0)
        o_ref[...] += jnp.dot((hdn * hdn).astype(BF16), wdn_ref[fs, :], preferred_element_type=F32)


def _mix_ffn_kernel(layer, n_prompt, xp_ref, ap_ref, sp_ref, xs_ref, as_ref, ss_ref,
                    w_in_hbm, wao_hbm, wso_hbm, wout_hbm, wup_hbm, wdn_hbm, g1_ref, bg_ref, g2_ref,
                    op_ref, os_ref, wg_ref, wao_ref, wso_ref, wout_ref, wup_ref, wdn_ref, stage, sem):
    i = pl.program_id(0)
    cr, cc = stage.shape[1:]

    @pl.when(i == 0)
    def _():
        chunks = []
        chunks += _weight_chunks(w_in_hbm.at[layer], wg_ref, wg_ref.shape[0], wg_ref.shape[1], cr, cc,
                                 src_col0=w_in_hbm.shape[-1] - wg_ref.shape[1])
        for hbm, ref in ((wao_hbm, wao_ref), (wso_hbm, wso_ref), (wout_hbm, wout_ref), (wup_hbm, wup_ref),
                         (wdn_hbm, wdn_ref)):
            chunks += _weight_chunks(hbm.at[layer], ref, ref.shape[0], ref.shape[1], cr, cc)
        _stage_weights(chunks, stage, sem)

    w_refs = (g1_ref, wg_ref, bg_ref, wao_ref, wso_ref, wout_ref, g2_ref, wup_ref, wdn_ref)

    @pl.when(i < n_prompt)
    def _():
        _mix_ffn_rows(xp_ref, ap_ref, sp_ref, op_ref, *w_refs)

    @pl.when(i >= n_prompt)
    def _():
        _mix_ffn_rows(xs_ref, as_ref, ss_ref, os_ref, *w_refs)


def _mix_ffn(layer, xp, attn_p, ssm_p, xs, attn_s, ssm_s, w_in, wao, wso, wout, wup, wdn, g1, bg, g2):
    rp, D = xp.shape
    rs = xs.shape[0]
    tp = min(512, rp)
    ts = min(512, rs)
    assert rp % tp == 0 and rs % ts == 0
    n_p, n_s = rp // tp, rs // ts
    prow = lambda last: pl.BlockSpec((tp, last), lambda i: (jnp.minimum(i, n_p - 1), 0))
    srow = lambda last: pl.BlockSpec((ts, last), lambda i: (jnp.maximum(i - n_p, 0), 0))
    hbm = pl.BlockSpec(memory_space=pl.ANY)
    resident = lambda a: pltpu.VMEM(a.shape[1:], BF16)
    return pl.pallas_call(
        functools.partial(_mix_ffn_kernel, layer, n_p),
        grid=(n_p + n_s,),
        in_specs=[prow(D), prow(attn_p.shape[-1]), prow(ssm_p.shape[-1]),
                  srow(D), srow(attn_s.shape[-1]), srow(ssm_s.shape[-1])]
                 + [hbm] * 6 + [_layer_spec(a, layer) for a in (g1, bg, g2)],
        out_specs=[prow(D), srow(D)],
        out_shape=[jax.ShapeDtypeStruct((rp, D), F32), jax.ShapeDtypeStruct((rs, D), F32)],
        scratch_shapes=[pltpu.VMEM((D, bg.shape[-1]), BF16), resident(wao), resident(wso), resident(wout),
                        resident(wup), resident(wdn),
                        pltpu.VMEM((WEIGHT_SLOTS, 2 * LANES, D), F32), pltpu.SemaphoreType.DMA((WEIGHT_SLOTS,))],
        compiler_params=pltpu.CompilerParams(dimension_semantics=("arbitrary",), vmem_limit_bytes=VMEM_LIMIT),
        name="mix_ffn",
    )(xp, attn_p, ssm_p, xs, attn_s, ssm_s, w_in, wao, wso, wout, wup, wdn, g1, bg, g2)


def _seg_matrix(width):
    i = np.arange(width) // HEAD_DIM
    return jnp.asarray((i[:, None] == i[None, :]).astype(np.float32) / HEAD_DIM, BF16)


def kernel(x_prompt, x_sample, cache_k, cache_v, state_ssm_re, state_ssm_im, norm1_g, w_in, b_gate, q_norm_g,
           k_norm_g, attn_sinks, lam_re, lam_im, log_step, b_re, b_im, c_re, c_im, d_skip, w_glu, b_glu,
           w_attn_o, w_ssm_o, w_out, norm2_g, w_up, w_down):
    B, T, D = x_prompt.shape
    DB, DS, _ = x_sample.shape
    L = w_in.shape[0]
    aw = N_HEADS * HEAD_DIM
    kw = N_KV_HEADS * HEAD_DIM
    sw = d_skip.shape[-1]
    G, P = lam_re.shape[1:]
    past = cache_k.shape[2]
    assert past == WINDOW and T % WINDOW == 0 and (B * WINDOW) % ROW_TILE == 0 and (DB * DS) % ROW_TILE == 0

    ar, ai, bx, cm = _ssm_prep(lam_re, lam_im, log_step, b_re, b_im, c_re, c_im)
    seg = _seg_matrix(aw)
    rows = lambda a: a.reshape(L, 1, -1)
    g1 = rows(norm1_g)
    mixer_w = (w_in, w_glu, g1, rows(jnp.tile(q_norm_g, (1, N_HEADS))), rows(jnp.tile(k_norm_g, (1, N_KV_HEADS))),
               ar, ai, bx, cm, rows(d_skip), rows(b_glu))
    ffn_w = (w_in, w_attn_o, w_ssm_o, w_out, w_up, w_down, g1, rows(b_gate), rows(norm2_g))
    kv_t = lambda a: a.transpose(0, 1, 3, 4, 2).reshape(L, DB, kw, past)
    st_t = lambda a: a.transpose(0, 2, 3, 1).reshape(L, G * P, DB)
    ckt, cvt, h0rt, h0it = kv_t(cache_k), kv_t(cache_v), st_t(state_ssm_re), st_t(state_ssm_im)

    yp = x_prompt.reshape(B * T, D)
    ys = x_sample.transpose(1, 0, 2).reshape(DS * DB, D)
    outs = [[] for _ in range(6)]
    kcat = vcat = None
    for l in range(L):
        attn_p, ssm_p, kp, vp, hrp, hip = _mixer_prompt(l, yp.reshape(B, T, D), attn_sinks, seg, *mixer_w)
        attn_s, ssm_s, kcat, vcat, hrs, his = _mixer_sample(l, ys, ckt, cvt, h0rt, h0it, kcat, vcat, attn_sinks, seg,
                                                            *mixer_w)
        yp, ys = _mix_ffn(l, yp, attn_p.reshape(B * T, aw), ssm_p.reshape(B * T, sw), ys, attn_s, ssm_s, *ffn_w)
        for o, v in zip(outs, (kp, vp, hrp, hip, hrs, his)):
            o.append(v)

    kp, vp, hrp, hip, hrs, his = (jnp.stack(o) for o in outs)
    kv_back = lambda a: a.reshape(a.shape[:2] + (N_KV_HEADS, HEAD_DIM, a.shape[-1])).transpose(0, 1, 4, 2, 3)
    st_back = lambda a: a.reshape(L, G, P, DB).transpose(0, 3, 1, 2)
    y_sample = ys.reshape(DS, DB, D).transpose(1, 0, 2)
    return (yp.reshape(B, T, D), y_sample, kv_back(kp), kv_back(vp), hrp.reshape(L, B, G, P), hip.reshape(L, B, G, P),
            kv_back(kcat), kv_back(vcat), st_back(hrs), st_back(his))
```

```python
import functools
import math

import jax
import jax.numpy as jnp
from jax import lax
from jax.experimental import pallas as pl
from jax.experimental.pallas import tpu as pltpu

F32 = jnp.float32
BF16 = jnp.bfloat16

HEAD_DIM = 64
N_HEADS = 8
N_KV_HEADS = 2
Q_GROUP = N_HEADS // N_KV_HEADS
WINDOW = 128
GROUP_CH = 16
SSM_STATE = 64
EPS = 1e-6
NEG_INF = -1e30
LOG2E = math.log2(math.e)
LANES = 128
SUBLANES = 8
ROW_TILE = 256
U_PITCH = WINDOW + SUBLANES
VMEM_LIMIT = 60 * 1024 * 1024

WEIGHT_SLOTS = 4
SLOPES = tuple(2.0 ** (-8.0 * (h + 1) / N_HEADS) for h in range(N_HEADS))


def _rms(x, g):
    return x * lax.rsqrt(jnp.mean(x * x, axis=-1, keepdims=True) + EPS) * g


def _seg_rms(x, g):
    lower = lax.broadcasted_iota(jnp.int32, (x.shape[0], LANES), 1) < HEAD_DIM
    normed = []
    for p in range(x.shape[1] // LANES):
        xb = x[:, p * LANES:(p + 1) * LANES]
        sq = xb * xb
        lo = jnp.sum(jnp.where(lower, sq, 0.0), axis=-1, keepdims=True)
        hi = jnp.sum(jnp.where(lower, 0.0, sq), axis=-1, keepdims=True)
        ms = jnp.where(lower, lo, hi) * (1.0 / HEAD_DIM)
        normed.append(xb * lax.rsqrt(ms + EPS))
    return jnp.concatenate(normed, axis=1) * g


def _gelu_tanh(y):
    return 0.5 * y * (1.0 + jnp.tanh(math.sqrt(2.0 / math.pi) * (y + 0.044715 * (y * y * y))))


def _layer_spec(a, layer):
    n = a.ndim - 1
    return pl.BlockSpec((None,) + a.shape[1:], lambda *_: (layer,) + (0,) * n, pipeline_mode=pl.Buffered(1))


def _stage_weights(chunks, stage, sem):
    slots = stage.shape[0]

    def copy(k):
        src, _, cols = chunks[k]
        return pltpu.make_async_copy(src, stage.at[k % slots, :, pl.ds(0, cols)], sem.at[k % slots])

    for k in range(min(slots - 1, len(chunks))):
        copy(k).start()
    for k, (_, dst, cols) in enumerate(chunks):
        if k + slots - 1 < len(chunks):
            copy(k + slots - 1).start()
        copy(k).wait()
        dst[...] = stage[k % slots, :, :cols].astype(BF16)


def _weight_chunks(src, dst, rows, cols, chunk_rows, chunk_cols, src_col0=0):
    return [(src.at[pl.ds(r, chunk_rows), pl.ds(src_col0 + c, chunk_cols)],
             dst.at[pl.ds(r, chunk_rows), pl.ds(c, chunk_cols)], chunk_cols)
            for r in range(0, rows, chunk_rows) for c in range(0, cols, chunk_cols)]


def _ssm_prep_kernel(lr_ref, li_ref, ls_ref, btr_ref, bti_ref, ctr_ref, cti_ref,
                     ar_ref, ai_ref, bx_ref, cm_ref):
    lr = lr_ref[0]
    li = li_ref[0]
    step = jnp.exp(ls_ref[0])
    mag = jnp.exp(lr * step)
    ar = mag * jnp.cos(li * step)
    ai = mag * jnp.sin(li * step)
    den = lr * lr + li * li
    cr = ((ar - 1.0) * lr + ai * li) / den
    ci = (ai * lr - (ar - 1.0) * li) / den
    ar_ref[0] = ar
    ai_ref[0] = ai
    nj = btr_ref.shape[1]
    gp = SUBLANES * SSM_STATE
    rows_b = lax.broadcasted_iota(jnp.int32, (LANES, gp), 0) // GROUP_CH
    cols_b = lax.broadcasted_iota(jnp.int32, (LANES, gp), 1) // SSM_STATE
    mask_b = rows_b == cols_b
    rows_c = lax.broadcasted_iota(jnp.int32, (gp, LANES), 0) // SSM_STATE
    cols_c = lax.broadcasted_iota(jnp.int32, (gp, LANES), 1) // GROUP_CH
    mask_c = rows_c == cols_c
    for j in range(nj):
        crj = cr[:, j * gp:(j + 1) * gp]
        cij = ci[:, j * gp:(j + 1) * gp]
        b_re = btr_ref[0, j]
        b_im = bti_ref[0, j]
        bb_re = jnp.where(mask_b, crj * b_re - cij * b_im, 0.0)
        bb_im = jnp.where(mask_b, crj * b_im + cij * b_re, 0.0)
        bx_ref[0, j, :, :gp] = bb_re.astype(BF16)
        bx_ref[0, j, :, gp:] = bb_im.astype(BF16)
        cm_ref[0, j, :gp, :] = jnp.where(mask_c, ctr_ref[0, j], 0.0).astype(BF16)
        cm_ref[0, j, gp:, :] = jnp.where(mask_c, -cti_ref[0, j], 0.0).astype(BF16)


def _ssm_prep(lam_re, lam_im, log_step, b_re, b_im, c_re, c_im):
    L, G, P = lam_re.shape
    GC = b_re.shape[-1]
    gpb = LANES // GC
    nj = G // gpb
    gp = gpb * P
    flat = lambda a: a.reshape(L, 1, G * P)
    ls = jnp.repeat(log_step, P, axis=-1).reshape(L, 1, G * P)

    def b_tiles(b):
        t = b.reshape(L, nj, gpb, P, GC).transpose(0, 1, 4, 2, 3).reshape(L, nj, 1, GC, gp)
        return jnp.broadcast_to(t, (L, nj, gpb, GC, gp)).reshape(L, nj, LANES, gp)

    def c_tiles(c):
        t = c.reshape(L, nj, gpb, GC, P).transpose(0, 1, 2, 4, 3).reshape(L, nj, gp, 1, GC)
        return jnp.broadcast_to(t, (L, nj, gp, gpb, GC)).reshape(L, nj, gp, LANES)

    vec = pl.BlockSpec((1, 1, G * P), lambda l: (l, 0, 0))
    bt = pl.BlockSpec((1, nj, LANES, gp), lambda l: (l, 0, 0, 0))
    ct = pl.BlockSpec((1, nj, gp, LANES), lambda l: (l, 0, 0, 0))
    return pl.pallas_call(
        _ssm_prep_kernel,
        grid=(L,),
        in_specs=[vec, vec, vec, bt, bt, ct, ct],
        out_specs=[vec, vec,
                   pl.BlockSpec((1, nj, LANES, 2 * gp), lambda l: (l, 0, 0, 0)),
                   pl.BlockSpec((1, nj, 2 * gp, LANES), lambda l: (l, 0, 0, 0))],
        out_shape=[jax.ShapeDtypeStruct((L, 1, G * P), F32), jax.ShapeDtypeStruct((L, 1, G * P), F32),
                   jax.ShapeDtypeStruct((L, nj, LANES, 2 * gp), BF16),
                   jax.ShapeDtypeStruct((L, nj, 2 * gp, LANES), BF16)],
        name="ssm_prep",
    )(flat(lam_re), flat(lam_im), ls, b_tiles(b_re), b_tiles(b_im), c_tiles(c_re), c_tiles(c_im))


def _project_rows(x, g1_ref, w_ref, qg_ref, kg_ref):
    aw = N_HEADS * HEAD_DIM
    kw = N_KV_HEADS * HEAD_DIM
    xn = _rms(x, g1_ref[...]).astype(BF16)
    z = jnp.dot(xn, w_ref[...], preferred_element_type=F32)
    q = _seg_rms(z[:, :aw], qg_ref[...]) * (HEAD_DIM ** -0.5 * LOG2E)
    k = _seg_rms(z[:, aw:aw + kw], kg_ref[...])
    v = z[:, aw + kw:aw + 2 * kw]
    u = z[:, aw + 2 * kw:]
    lower = (lax.broadcasted_iota(jnp.int32, q.shape, 1) % LANES) < HEAD_DIM
    q_lo = jnp.where(lower, q, 0.0)
    q_hi = jnp.where(lower, 0.0, q)
    return q_lo, q_hi, k, v, u


def _stage_mixer_weights(layer, w_in_hbm, wglu_hbm, w_ref, wglu_ref, stage, sem):
    _stage_weights(
        _weight_chunks(w_in_hbm.at[layer], w_ref, w_ref.shape[0], w_ref.shape[1], stage.shape[1], w_ref.shape[1])
        + _weight_chunks(wglu_hbm.at[layer], wglu_ref, wglu_ref.shape[0], wglu_ref.shape[1], stage.shape[1],
                         wglu_ref.shape[1]),
        stage, sem)


def _mixer_weight_scratch(D, in_cols, sw):
    return [pltpu.VMEM((D, in_cols), BF16), pltpu.VMEM((sw, sw), BF16),
            pltpu.VMEM((WEIGHT_SLOTS, LANES, in_cols), F32), pltpu.SemaphoreType.DMA((WEIGHT_SLOTS,))]


def _ssm_readout(y, u, d_ref, wglu_ref, bglu_ref):
    y = _gelu_tanh(y + d_ref[...] * u)
    gate = jnp.dot(y.astype(BF16), wglu_ref[...], preferred_element_type=F32) + bglu_ref[...]
    return y * jax.nn.sigmoid(gate)


def _softmax_pv(s, sink, v):
    m = jnp.maximum(jnp.max(s, axis=-1, keepdims=True), sink)
    e = jnp.exp2(s - m)
    den = jnp.sum(e, axis=-1, keepdims=True) + jnp.exp2(sink - m)
    return jnp.dot(e.astype(BF16), v, preferred_element_type=F32) * (1.0 / den)


def _mixer_prompt_kernel(layer, sink_ref, x_ref, w_in_hbm, wglu_hbm, g1_ref, qg_ref, kg_ref,
                         ar_ref, ai_ref, bx_ref, cm_ref, d_ref, bglu_ref,
                         attn_ref, ssm_ref, kout_ref, vout_ref, hr_ref, hi_ref,
                         w_ref, wglu_ref, stage, sem, q_s, k_s, v_s, u_s, y_s, *x_s):
    c = pl.program_id(0)
    nb, tt, _ = x_ref.shape
    rows = nb * tt
    n_tiles = rows // ROW_TILE
    bpt = ROW_TILE // tt
    n_state = ar_ref.shape[-1]
    nj = bx_ref.shape[0]
    gp = n_state // nj
    xr_s, xi_s = x_s[:nj], x_s[nj:]
    kw = kout_ref.shape[1]

    @pl.when(c == 0)
    def _():
        _stage_mixer_weights(layer, w_in_hbm, wglu_hbm, w_ref, wglu_ref, stage, sem)
        k_s[:, :, :tt, :] = jnp.zeros((2, nb, tt, kw), BF16)
        v_s[:, :, :tt, :] = jnp.zeros((2, nb, tt, kw), BF16)
        hr_ref[...] = jnp.zeros(hr_ref.shape, F32)
        hi_ref[...] = jnp.zeros(hi_ref.shape, F32)

    def project(r, carry):
        x = x_ref[pl.ds(r * bpt, bpt)].reshape(ROW_TILE, x_ref.shape[-1])
        q_lo, q_hi, k, v, u = _project_rows(x, g1_ref, w_ref, qg_ref, kg_ref)
        q_s[0, pl.ds(r * bpt, bpt)] = q_lo.astype(BF16).reshape(bpt, tt, -1)
        q_s[1, pl.ds(r * bpt, bpt)] = q_hi.astype(BF16).reshape(bpt, tt, -1)
        k_s[0, pl.ds(r * bpt, bpt), tt:, :] = k.astype(BF16).reshape(bpt, tt, -1)
        v_s[0, pl.ds(r * bpt, bpt), tt:, :] = v.astype(BF16).reshape(bpt, tt, -1)
        k_s[1, pl.ds(r * bpt, bpt), tt:, :] = pltpu.roll(k, HEAD_DIM, 1).astype(BF16).reshape(bpt, tt, -1)
        v_s[1, pl.ds(r * bpt, bpt), tt:, :] = pltpu.roll(v, HEAD_DIM, 1).astype(BF16).reshape(bpt, tt, -1)
        for bb in range(bpt):
            kout_ref[r * bpt + bb] = k[bb * tt:(bb + 1) * tt, :].T
            vout_ref[r * bpt + bb] = v[bb * tt:(bb + 1) * tt, :].T
        for bb in range(bpt):
            row0 = pl.multiple_of((r * bpt + bb) * U_PITCH, SUBLANES)
            for j in range(nj):
                u_s[j, pl.ds(row0, tt), :] = u[bb * tt:(bb + 1) * tt, j * LANES:(j + 1) * LANES]
        return carry

    lax.fori_loop(0, n_tiles, project, 0, unroll=True)

    t_idx = lax.broadcasted_iota(jnp.int32, (tt, 2 * tt), 0)
    j_idx = lax.broadcasted_iota(jnp.int32, (tt, 2 * tt), 1)
    dist = t_idx - j_idx + tt
    valid = (dist >= 0) & (dist <= WINDOW) & (j_idx >= jnp.where(c > 0, 0, tt))
    key_pos = lax.broadcasted_iota(jnp.int32, (1, 2 * tt), 1).astype(F32)
    row_pos = (lax.broadcasted_iota(jnp.int32, (tt, 1), 0) + tt).astype(F32)
    lower = lax.broadcasted_iota(jnp.int32, (tt, LANES), 1) < HEAD_DIM

    def attend(b):
        for p in range(N_HEADS // 2):
            kv = (2 * p) // Q_GROUP
            outs = []
            for e in range(2):
                h = 2 * p + e
                swap = 0 if kv == e else 1
                qm = q_s[e, b, :, p * LANES:(p + 1) * LANES]
                s = lax.dot_general(qm, k_s[swap, b], (((1,), (1,)), ((), ())), preferred_element_type=F32)
                slope = SLOPES[h] * LOG2E
                s = jnp.where(valid, s + slope * key_pos, NEG_INF)
                sink = sink_ref[layer, h] * LOG2E + slope * row_pos
                outs.append(_softmax_pv(s, sink, v_s[swap, b]))
            attn_ref[b, :, p * LANES:(p + 1) * LANES] = jnp.where(lower, outs[0], outs[1]).astype(BF16)

    def ssm_in_and_attend(r, carry):
        rs = pl.ds(r * ROW_TILE, ROW_TILE)
        t0 = r * (ROW_TILE // nb)
        for j in range(nj):
            up = jnp.concatenate([u_s[j, pl.ds(t0 + step, nb, stride=U_PITCH), :]
                                  for step in range(ROW_TILE // nb)], axis=0).astype(BF16)
            xj = jnp.dot(up, bx_ref[j], preferred_element_type=F32)
            xr_s[j][rs, :] = xj[:, :gp]
            xi_s[j][rs, :] = xj[:, gp:]
        for bb in range(bpt):
            attend(r * bpt + bb)
        return carry

    lax.fori_loop(0, n_tiles, ssm_in_and_attend, 0, unroll=2)

    k_s[:, :, :tt, :] = k_s[:, :, tt:, :]
    v_s[:, :, :tt, :] = v_s[:, :, tt:, :]

    def ssm_out_tile(j, r):
        rs = pl.ds(r * ROW_TILE, ROW_TILE)
        h_cat =jnp.concatenate([xr_s[j][rs, :], xi_s[j][rs, :]], axis=1).astype(BF16)
        y_s[j, rs, :] = jnp.dot(h_cat, cm_ref[j], preferred_element_type=F32)

    for lb in range(nj):
        sl = slice(lb * gp, (lb + 1) * gp)
        a_re = jnp.broadcast_to(ar_ref[:, sl], (nb, gp))
        a_im = jnp.broadcast_to(ai_ref[:, sl], (nb, gp))

        def scan_tile(r, h, lb=lb, a_re=a_re, a_im=a_im):
            if lb > 0:
                ssm_out_tile(lb - 1, r)
            h_re, h_im = h
            for step in range(ROW_TILE // nb):
                row = pl.ds(pl.multiple_of(r * ROW_TILE + step * nb, nb), nb)
                n_re = a_re * h_re - a_im * h_im + xr_s[lb][row, :]
                n_im = a_re * h_im + a_im * h_re + xi_s[lb][row, :]
                xr_s[lb][row, :] = n_re
                xi_s[lb][row, :] = n_im
                h_re, h_im = n_re, n_im
            return h_re, h_im

        h_re, h_im = lax.fori_loop(0, n_tiles, scan_tile, (hr_ref[:, sl], hi_ref[:, sl]), unroll=True)
        hr_ref[:, sl] = h_re
        hi_ref[:, sl] = h_im

    def ssm_out_last(r, carry):
        ssm_out_tile(nj - 1, r)
        return carry

    lax.fori_loop(0, n_tiles, ssm_out_last, 0, unroll=True)

    def readout(b, carry):
        y = jnp.concatenate([y_s[j, pl.ds(b, tt, stride=nb), :] for j in range(nj)], axis=1)
        row0 = pl.multiple_of(b * U_PITCH, SUBLANES)
        u = jnp.concatenate([u_s[j, pl.ds(row0, tt), :] for j in range(nj)], axis=1)
        ssm_ref[b] = _ssm_readout(y, u, d_ref, wglu_ref, bglu_ref).astype(BF16)
        return carry

    lax.fori_loop(0, nb, readout, 0, unroll=4)


def _mixer_prompt(layer, x, sinks, w_in, wglu, g1, qg, kg, ar, ai, bx, cm, d, bglu):
    nb, T, D = x.shape
    tt = WINDOW
    rows = nb * tt
    aw = N_HEADS * HEAD_DIM
    kw = N_KV_HEADS * HEAD_DIM
    sw = d.shape[-1]
    n_state = ar.shape[-1]
    nj = bx.shape[1]
    chunk = lambda last: pl.BlockSpec((nb, tt, last), lambda c: (0, c, 0))
    whole = pl.BlockSpec((nb, kw, tt), lambda c: (0, 0, 0))
    state = pl.BlockSpec((nb, n_state), lambda c: (0, 0))
    hbm = pl.BlockSpec(memory_space=pl.ANY)
    stacked = (g1, qg, kg, ar, ai, bx, cm, d, bglu)
    return pl.pallas_call(
        functools.partial(_mixer_prompt_kernel, layer),
        grid=(T // tt,),
        in_specs=[pl.BlockSpec(memory_space=pltpu.SMEM), chunk(D), hbm, hbm]
                 + [_layer_spec(a, layer) for a in stacked],
        out_specs=[chunk(aw), chunk(sw), whole, whole, state, state],
        out_shape=[jax.ShapeDtypeStruct((nb, T, aw), BF16), jax.ShapeDtypeStruct((nb, T, sw), BF16),
                   jax.ShapeDtypeStruct((nb, kw, tt), F32), jax.ShapeDtypeStruct((nb, kw, tt), F32),
                   jax.ShapeDtypeStruct((nb, n_state), F32), jax.ShapeDtypeStruct((nb, n_state), F32)],
        scratch_shapes=_mixer_weight_scratch(D, aw + 2 * kw + sw, sw)
                       + [pltpu.VMEM((2, nb, tt, aw), BF16),
                          pltpu.VMEM((2, nb, 2 * tt, kw), BF16), pltpu.VMEM((2, nb, 2 * tt, kw), BF16),
                          pltpu.VMEM((sw // LANES, nb * U_PITCH, LANES), F32),
                          pltpu.VMEM((nj, rows, LANES), F32)]
                       + [pltpu.VMEM((rows, n_state // nj), F32)] * (2 * nj),
        compiler_params=pltpu.CompilerParams(dimension_semantics=("arbitrary",), vmem_limit_bytes=VMEM_LIMIT),
        name="mixer_prompt",
    )(sinks, x, w_in, wglu, *stacked)


def _mixer_sample_kernel(layer, sink_ref, x_ref, ckt_ref, cvt_ref, h0rt_ref, h0it_ref, *refs):
    if layer:
        prevk_ref, prevv_ref, *refs = refs
    (w_in_hbm, wglu_hbm, g1_ref, qg_ref, kg_ref, ar_ref, ai_ref, bx_ref, cm_ref, d_ref, bglu_ref,
     attn_ref, ssm_ref, kcat_ref, vcat_ref, hrt_ref, hit_ref,
     w_ref, wglu_ref, stage, sem, q_s, o_s, u_s, xr_s, xi_s, knew_s, vnew_s,
     h0r_ref, h0i_ref, hr_ref, hi_ref, *group_s) = refs
    i = pl.program_id(0)
    rows = x_ref.shape[0]
    nbatch = h0rt_ref.shape[1]
    ts = rows // nbatch
    gb_step, kw, past = ckt_ref.shape
    gb = SUBLANES
    n_tiles = rows // ROW_TILE
    n_state = ar_ref.shape[-1]
    nj = bx_ref.shape[0]
    gp = n_state // nj

    @pl.when(i == 0)
    def _():
        _stage_mixer_weights(layer, w_in_hbm, wglu_hbm, w_ref, wglu_ref, stage, sem)

        def project(r, carry):
            rs = pl.ds(r * ROW_TILE, ROW_TILE)
            q_lo, q_hi, k, v, u = _project_rows(x_ref[rs, :], g1_ref, w_ref, qg_ref, kg_ref)
            q_s[0, rs, :] = q_lo
            q_s[1, rs, :] = q_hi
            knew_s[rs, :] = k
            vnew_s[rs, :] = v
            u_s[rs, :] = u
            return carry

        lax.fori_loop(0, n_tiles, project, 0)

    n_q = ts * Q_GROUP * gb
    n_keys = gb * past + ts * gb
    r_idx = lax.broadcasted_iota(jnp.int32, (n_q, n_keys), 0)
    c_idx = lax.broadcasted_iota(jnp.int32, (n_q, n_keys), 1)
    q_t = r_idx // (Q_GROUP * gb)
    q_b = r_idx % gb
    is_cache = c_idx < gb * past
    key_b = jnp.where(is_cache, c_idx // past, (c_idx - gb * past) % gb)
    dist = jnp.where(is_cache, past + q_t - c_idx % past, q_t - (c_idx - gb * past) // gb)
    valid = (key_b == q_b) & (dist >= 0) & (dist <= WINDOW)
    dist_f = dist.astype(F32)
    hh = (lax.broadcasted_iota(jnp.int32, (n_q, 1), 0) // gb) % Q_GROUP
    lower = lax.broadcasted_iota(jnp.int32, (gb, LANES), 1) < HEAD_DIM
    def attend_group(sg):
        ck_ref, cv_ref, kroll_s, vroll_s = group_s[4 * sg:4 * sg + 4]
        sbase = pl.multiple_of(i * gb_step + sg * gb, gb)
        for cat_ref, ct_ref, c_ref, roll_s, new_s in ((kcat_ref, ckt_ref, ck_ref, kroll_s, knew_s),
                                                      (vcat_ref, cvt_ref, cv_ref, vroll_s, vnew_s)):
            for b in range(gb):
                c_ref[b * past:(b + 1) * past, :] = ct_ref[sg * gb + b].T
            for b in range(gb):
                roll_s[b * past:(b + 1) * past - ts, :] = c_ref[b * past + ts:(b + 1) * past, :]
            for t in range(ts):
                roll_s[pl.ds(past - ts + t, gb, stride=past), :] = new_s[pl.ds(sbase + t * nbatch, gb), :]
            for b in range(gb):
                cat_ref[layer, sg * gb + b] = roll_s[b * past:(b + 1) * past, :].T

        k_all = jnp.concatenate(
            [ck_ref[...]] + [knew_s[pl.ds(sbase + t * nbatch, gb), :] for t in range(ts)], axis=0).astype(BF16)
        v_all = jnp.concatenate(
            [cv_ref[...]] + [vnew_s[pl.ds(sbase + t * nbatch, gb), :] for t in range(ts)], axis=0).astype(BF16)
        outs = []
        for kvh in range(N_KV_HEADS):
            slope = jnp.zeros((n_q, 1), F32)
            sink = jnp.zeros((n_q, 1), F32)
            for g in range(Q_GROUP):
                h = kvh * Q_GROUP + g
                slope = jnp.where(hh == g, SLOPES[h] * LOG2E, slope)
                sink = jnp.where(hh == g, sink_ref[layer, h] * LOG2E, sink)
            pieces = []
            for t in range(ts):
                for g in range(Q_GROUP):
                    h = kvh * Q_GROUP + g
                    piece = q_s[h % 2, pl.ds(sbase + t * nbatch, gb), (h // 2) * LANES:(h // 2 + 1) * LANES]
                    pieces.append(piece if h % 2 == kvh else pltpu.roll(piece, HEAD_DIM, 1))
            qm = jnp.concatenate(pieces, axis=0).astype(BF16)
            s = lax.dot_general(qm, k_all, (((1,), (1,)), ((), ())), preferred_element_type=F32)
            s = jnp.where(valid, s - slope * dist_f, NEG_INF)
            outs.append(_softmax_pv(s, sink, v_all))
        for t in range(ts):
            for p in range(N_HEADS // 2):
                kvh = (2 * p) // Q_GROUP
                r0 = (t * Q_GROUP + 2 * p - kvh * Q_GROUP) * gb
                lo = outs[kvh][r0:r0 + gb]
                hi = outs[kvh][r0 + gb:r0 + 2 * gb]
                if kvh == 1:
                    lo = pltpu.roll(lo, HEAD_DIM, 1)
                else:
                    hi = pltpu.roll(hi, HEAD_DIM, 1)
                o_s[pl.ds(sbase + t * nbatch, gb), p * LANES:(p + 1) * LANES] = jnp.where(lower, lo, hi)

    for sg in range(gb_step // gb):
        attend_group(sg)
    if layer:
        kcat_ref[:layer] = prevk_ref[...]
        vcat_ref[:layer] = prevv_ref[...]

    @pl.when(i == pl.num_programs(0) - 1)
    def _():
        attn_ref[...] = o_s[...].astype(BF16)

        def ssm_in(r, carry):
            rs = pl.ds(r * ROW_TILE, ROW_TILE)
            ub = u_s[rs, :].astype(BF16)
            for j in range(nj):
                xj = jnp.dot(ub[:, j * LANES:(j + 1) * LANES], bx_ref[j], preferred_element_type=F32)
                xr_s[rs, j * gp:(j + 1) * gp] = xj[:, :gp]
                xi_s[rs, j * gp:(j + 1) * gp] = xj[:, gp:]
            return carry

        lax.fori_loop(0, n_tiles, ssm_in, 0)

        for blk in range(n_state // LANES):
            bs = slice(blk * LANES, (blk + 1) * LANES)
            h0r_ref[:, bs] = h0rt_ref[bs, :].T
            h0i_ref[:, bs] = h0it_ref[bs, :].T

        lane_blk = 4 * LANES
        for lb in range(n_state // lane_blk):
            sl = slice(lb * lane_blk, (lb + 1) * lane_blk)
            a_re = jnp.broadcast_to(ar_ref[:, sl], (SUBLANES, lane_blk))
            a_im = jnp.broadcast_to(ai_ref[:, sl], (SUBLANES, lane_blk))

            def scan(bt, carry):
                b0 = pl.multiple_of(bt * SUBLANES, SUBLANES)
                h_re = h0r_ref[pl.ds(b0, SUBLANES), sl]
                h_im = h0i_ref[pl.ds(b0, SUBLANES), sl]
                for t in range(ts):
                    rs = pl.ds(b0 + t * nbatch, SUBLANES)
                    n_re = a_re * h_re - a_im * h_im + xr_s[rs, sl]
                    n_im = a_re * h_im + a_im * h_re + xi_s[rs, sl]
                    xr_s[rs, sl] = n_re
                    xi_s[rs, sl] = n_im
                    h_re, h_im = n_re, n_im
                hr_ref[pl.ds(b0, SUBLANES), sl] = h_re
                hi_ref[pl.ds(b0, SUBLANES), sl] = h_im
                return carry

            lax.fori_loop(0, nbatch // SUBLANES, scan, 0)

        for blk in range(n_state // LANES):
            bs = slice(blk * LANES, (blk + 1) * LANES)
            hrt_ref[bs, :] = hr_ref[:, bs].T
            hit_ref[bs, :] = hi_ref[:, bs].T

        def ssm_out(r, carry):
            rs = pl.ds(r * ROW_TILE, ROW_TILE)
            ys = []
            for j in range(nj):
                h_cat = jnp.concatenate([xr_s[rs, j * gp:(j + 1) * gp], xi_s[rs, j * gp:(j + 1) * gp]],
                                        axis=1).astype(BF16)
                ys.append(jnp.dot(h_cat, cm_ref[j], preferred_element_type=F32))
            y = jnp.concatenate(ys, axis=1)
            ssm_ref[rs, :] = _ssm_readout(y, u_s[rs, :], d_ref, wglu_ref, bglu_ref).astype(BF16)
            return carry

        lax.fori_loop(0, n_tiles, ssm_out, 0)


def _mixer_sample(layer, x, ckt, cvt, h0rt, h0it, prev_k, prev_v, sinks, w_in, wglu,
                  g1, qg, kg, ar, ai, bx, cm, d, bglu):
    rows, D = x.shape
    _, nbatch, kw, past = ckt.shape
    aw = N_HEADS * HEAD_DIM
    sw = d.shape[-1]
    n_state = ar.shape[-1]
    gb = SUBLANES
    assert nbatch % LANES == 0 and past == LANES and kw == LANES
    cache = pl.BlockSpec((None, gb, kw, past), lambda i: (layer, i, 0, 0))
    h0 = pl.BlockSpec((None, n_state, nbatch), lambda i: (layer, 0, 0))
    prev = pl.BlockSpec((layer, gb, kw, past), lambda i: (0, i, 0, 0))
    out2 = lambda last: pl.BlockSpec((rows, last), lambda i: (0, 0))
    cat = pl.BlockSpec((layer + 1, gb, kw, past), lambda i: (0, i, 0, 0))
    state = pl.BlockSpec((n_state, nbatch), lambda i: (0, 0))
    hbm = pl.BlockSpec(memory_space=pl.ANY)
    stacked = (g1, qg, kg, ar, ai, bx, cm, d, bglu)
    prevs = (prev_k, prev_v) if layer else ()
    return pl.pallas_call(
        functools.partial(_mixer_sample_kernel, layer),
        grid=(nbatch // gb,),
        in_specs=[pl.BlockSpec(memory_space=pltpu.SMEM), pl.BlockSpec(x.shape, lambda i: (0, 0)), cache, cache, h0, h0]
                 + [prev] * len(prevs) + [hbm, hbm] + [_layer_spec(a, layer) for a in stacked],
        out_specs=[out2(aw), out2(sw), cat, cat, state, state],
        out_shape=[jax.ShapeDtypeStruct((rows, aw), BF16), jax.ShapeDtypeStruct((rows, sw), BF16),
                   jax.ShapeDtypeStruct((layer + 1, nbatch, kw, past), F32),
                   jax.ShapeDtypeStruct((layer + 1, nbatch, kw, past), F32),
                   jax.ShapeDtypeStruct((n_state, nbatch), F32), jax.ShapeDtypeStruct((n_state, nbatch), F32)],
        scratch_shapes=_mixer_weight_scratch(D, aw + 2 * kw + sw, sw)
                       + [pltpu.VMEM((2, rows, aw), F32), pltpu.VMEM((rows, aw), F32), pltpu.VMEM((rows, sw), F32),
                          pltpu.VMEM((rows, n_state), F32), pltpu.VMEM((rows, n_state), F32),
                          pltpu.VMEM((rows, kw), F32), pltpu.VMEM((rows, kw), F32)]
                       + [pltpu.VMEM((nbatch, n_state), F32)] * 4
                       + [pltpu.VMEM((SUBLANES * past, kw), F32)] * (4 * gb // SUBLANES),
        compiler_params=pltpu.CompilerParams(dimension_semantics=("arbitrary",), vmem_limit_bytes=VMEM_LIMIT),
        name="mixer_sample",
    )(sinks, x, ckt, cvt, h0rt, h0it, *prevs, w_in, wglu, *stacked)


def _mix_ffn_rows(x_ref, attn_ref, ssm_ref, o_ref, g1_ref, wg_ref, bg_ref, wao_ref, wso_ref, wout_ref,
                  g2_ref, wup_ref, wdn_ref):
    x = x_ref[...]
    D = x.shape[-1]
    xn = _rms(x, g1_ref[...]).astype(BF16)
    attn = attn_ref[...]
    ssm = ssm_ref[...]
    o_ref[...] = x
    cw = 2 * LANES
    for cc in range(D // cw):
        cs = slice(cc * cw, (cc + 1) * cw)
        gs = slice(D + cc * cw, D + (cc + 1) * cw)
        g_attn = jax.nn.sigmoid(jnp.dot(xn, wg_ref[:, cs], preferred_element_type=F32) + bg_ref[:, cs])
        g_ssm = jax.nn.sigmoid(jnp.dot(xn, wg_ref[:, gs], preferred_element_type=F32) + bg_ref[:, gs])
        mixed = (g_attn * jnp.dot(attn, wao_ref[:, cs], preferred_element_type=F32)
                 + g_ssm * jnp.dot(ssm, wso_ref[:, cs], preferred_element_type=F32))
        o_ref[...] += jnp.dot(mixed.astype(BF16), wout_ref[cs, :], preferred_element_type=F32)
    xn2 = _rms(o_ref[...], g2_ref[...]).astype(BF16)
    fw = 4 * LANES
    for fc in range(wup_ref.shape[-1] // fw):
        fs = slice(fc * fw, (fc + 1) * fw)
        hdn = jnp.maximum(jnp.dot(xn2, wup_ref[:, fs], preferred_element_type=F32), 0.0)
        o_ref[...] += jnp.dot((hdn * hdn).astype(BF16), wdn_ref[fs, :], preferred_element_type=F32)


def _mix_ffn_kernel(layer, n_prompt, xp_ref, ap_ref, sp_ref, xs_ref, as_ref, ss_ref,
                    w_in_hbm, wao_hbm, wso_hbm, wout_hbm, wup_hbm, wdn_hbm, g1_ref, bg_ref, g2_ref,
                    op_ref, os_ref, wg_ref, wao_ref, wso_ref, wout_ref, wup_ref, wdn_ref, stage, sem):
    i = pl.program_id(0)
    cr, cc = stage.shape[1:]

    @pl.when(i == 0)
    def _():
        chunks = []
        chunks += _weight_chunks(w_in_hbm.at[layer], wg_ref, wg_ref.shape[0], wg_ref.shape[1], cr, cc,
                                 src_col0=w_in_hbm.shape[-1] - wg_ref.shape[1])
        for hbm, ref in ((wao_hbm, wao_ref), (wso_hbm, wso_ref), (wout_hbm, wout_ref), (wup_hbm, wup_ref),
                         (wdn_hbm, wdn_ref)):
            chunks += _weight_chunks(hbm.at[layer], ref, ref.shape[0], ref.shape[1], cr, cc)
        _stage_weights(chunks, stage, sem)

    w_refs = (g1_ref, wg_ref, bg_ref, wao_ref, wso_ref, wout_ref, g2_ref, wup_ref, wdn_ref)

    @pl.when(i < n_prompt)
    def _():
        _mix_ffn_rows(xp_ref, ap_ref, sp_ref, op_ref, *w_refs)

    @pl.when(i >= n_prompt)
    def _():
        _mix_ffn_rows(xs_ref, as_ref, ss_ref, os_ref, *w_refs)


def _mix_ffn(layer, xp, attn_p, ssm_p, xs, attn_s, ssm_s, w_in, wao, wso, wout, wup, wdn, g1, bg, g2):
    rp, D = xp.shape
    rs = xs.shape[0]
    tp = min(512, rp)
    ts = min(512, rs)
    assert rp % tp == 0 and rs % ts == 0
    n_p, n_s = rp // tp, rs // ts
    prow = lambda last: pl.BlockSpec((tp, last), lambda i: (jnp.minimum(i, n_p - 1), 0))
    srow = lambda last: pl.BlockSpec((ts, last), lambda i: (jnp.maximum(i - n_p, 0), 0))
    hbm = pl.BlockSpec(memory_space=pl.ANY)
    resident = lambda a: pltpu.VMEM(a.shape[1:], BF16)
    return pl.pallas_call(
        functools.partial(_mix_ffn_kernel, layer, n_p),
        grid=(n_p + n_s,),
        in_specs=[prow(D), prow(attn_p.shape[-1]), prow(ssm_p.shape[-1]),
                  srow(D), srow(attn_s.shape[-1]), srow(ssm_s.shape[-1])]
                 + [hbm] * 6 + [_layer_spec(a, layer) for a in (g1, bg, g2)],
        out_specs=[prow(D), srow(D)],
        out_shape=[jax.ShapeDtypeStruct((rp, D), F32), jax.ShapeDtypeStruct((rs, D), F32)],
        scratch_shapes=[pltpu.VMEM((D, bg.shape[-1]), BF16), resident(wao), resident(wso), resident(wout),
                        resident(wup), resident(wdn),
                        pltpu.VMEM((WEIGHT_SLOTS, 2 * LANES, D), F32), pltpu.SemaphoreType.DMA((WEIGHT_SLOTS,))],
        compiler_params=pltpu.CompilerParams(dimension_semantics=("arbitrary",), vmem_limit_bytes=VMEM_LIMIT),
        name="mix_ffn",
    )(xp, attn_p, ssm_p, xs, attn_s, ssm_s, w_in, wao, wso, wout, wup, wdn, g1, bg, g2)


def kernel(x_prompt, x_sample, cache_k, cache_v, state_ssm_re, state_ssm_im, norm1_g, w_in, b_gate, q_norm_g,
           k_norm_g, attn_sinks, lam_re, lam_im, log_step, b_re, b_im, c_re, c_im, d_skip, w_glu, b_glu,
           w_attn_o, w_ssm_o, w_out, norm2_g, w_up, w_down):
    B, T, D = x_prompt.shape
    DB, DS, _ = x_sample.shape
    L = w_in.shape[0]
    aw = N_HEADS * HEAD_DIM
    kw = N_KV_HEADS * HEAD_DIM
    sw = d_skip.shape[-1]
    G, P = lam_re.shape[1:]
    past = cache_k.shape[2]
    assert past == WINDOW and T % WINDOW == 0 and (B * WINDOW) % ROW_TILE == 0 and (DB * DS) % ROW_TILE == 0

    ar, ai, bx, cm = _ssm_prep(lam_re, lam_im, log_step, b_re, b_im, c_re, c_im)
    rows = lambda a: a.reshape(L, 1, -1)
    g1 = rows(norm1_g)
    mixer_w = (w_in, w_glu, g1, rows(jnp.tile(q_norm_g, (1, N_HEADS))), rows(jnp.tile(k_norm_g, (1, N_KV_HEADS))),
               ar, ai, bx, cm, rows(d_skip), rows(b_glu))
    ffn_w = (w_in, w_attn_o, w_ssm_o, w_out, w_up, w_down, g1, rows(b_gate), rows(norm2_g))
    kv_t = lambda a: a.transpose(0, 1, 3, 4, 2).reshape(L, DB, kw, past)
    st_t = lambda a: a.transpose(0, 2, 3, 1).reshape(L, G * P, DB)
    ckt, cvt, h0rt, h0it = kv_t(cache_k), kv_t(cache_v), st_t(state_ssm_re), st_t(state_ssm_im)

    yp = x_prompt.reshape(B * T, D)
    ys = x_sample.transpose(1, 0, 2).reshape(DS * DB, D)
    outs = [[] for _ in range(6)]
    kcat = vcat = None
    for l in range(L):
        attn_p, ssm_p, kp, vp, hrp, hip = _mixer_prompt(l, yp.reshape(B, T, D), attn_sinks, *mixer_w)
        attn_s, ssm_s, kcat, vcat, hrs, his = _mixer_sample(l, ys, ckt, cvt, h0rt, h0it, kcat, vcat, attn_sinks,
                                                            *mixer_w)
        yp, ys = _mix_ffn(l, yp, attn_p.reshape(B * T, aw), ssm_p.reshape(B * T, sw), ys, attn_s, ssm_s, *ffn_w)
        for o, v in zip(outs, (kp, vp, hrp, hip, hrs, his)):
            o.append(v)

    kp, vp, hrp, hip, hrs, his = (jnp.stack(o) for o in outs)
    kv_back = lambda a: a.reshape(a.shape[:2] + (N_KV_HEADS, HEAD_DIM, a.shape[-1])).transpose(0, 1, 4, 2, 3)
    st_back = lambda a: a.reshape(L, G, P, DB).transpose(0, 3, 1, 2)
    y_sample = ys.reshape(DS, DB, D).transpose(1, 0, 2)
    return (yp.reshape(B, T, D), y_sample, kv_back(kp), kv_back(vp), hrp.reshape(L, B, G, P), hip.reshape(L, B, G, P),
            kv_back(kcat), kv_back(vcat), st_back(hrs), st_back(his))
```

```python
import functools
import math

import jax
import jax.numpy as jnp
from jax import lax
from jax.experimental import pallas as pl
from jax.experimental.pallas import tpu as pltpu

F32 = jnp.float32
BF16 = jnp.bfloat16

HEAD_DIM = 64
N_HEADS = 8
N_KV_HEADS = 2
Q_GROUP = N_HEADS // N_KV_HEADS
WINDOW = 128
GROUP_CH = 16
SSM_STATE = 64
EPS = 1e-6
NEG_INF = -1e30
LOG2E = math.log2(math.e)
LANES = 128
SUBLANES = 8
ROW_TILE = 256
U_PITCH = WINDOW + SUBLANES
VMEM_LIMIT = 60 * 1024 * 1024

WEIGHT_SLOTS = 4
SLOPES = tuple(2.0 ** (-8.0 * (h + 1) / N_HEADS) for h in range(N_HEADS))


def _rms(x, g):
    return x * lax.rsqrt(jnp.mean(x * x, axis=-1, keepdims=True) + EPS) * g


def _seg_rms(x, g):
    lower = lax.broadcasted_iota(jnp.int32, (x.shape[0], LANES), 1) < HEAD_DIM
    normed = []
    for p in range(x.shape[1] // LANES):
        xb = x[:, p * LANES:(p + 1) * LANES]
        sq = xb * xb
        lo = jnp.sum(jnp.where(lower, sq, 0.0), axis=-1, keepdims=True)
        hi = jnp.sum(jnp.where(lower, 0.0, sq), axis=-1, keepdims=True)
        ms = jnp.where(lower, lo, hi) * (1.0 / HEAD_DIM)
        normed.append(xb * lax.rsqrt(ms + EPS))
    return jnp.concatenate(normed, axis=1) * g


def _gelu_tanh(y):
    return 0.5 * y * (1.0 + jnp.tanh(math.sqrt(2.0 / math.pi) * (y + 0.044715 * (y * y * y))))


def _layer_spec(a, layer):
    n = a.ndim - 1
    return pl.BlockSpec((None,) + a.shape[1:], lambda *_: (layer,) + (0,) * n, pipeline_mode=pl.Buffered(1))


def _stage_weights(chunks, stage, sem):
    slots = stage.shape[0]

    def copy(k):
        src, _, cols = chunks[k]
        return pltpu.make_async_copy(src, stage.at[k % slots, :, pl.ds(0, cols)], sem.at[k % slots])

    for k in range(min(slots - 1, len(chunks))):
        copy(k).start()
    for k, (_, dst, cols) in enumerate(chunks):
        if k + slots - 1 < len(chunks):
            copy(k + slots - 1).start()
        copy(k).wait()
        dst[...] = stage[k % slots, :, :cols].astype(BF16)


def _weight_chunks(src, dst, rows, cols, chunk_rows, chunk_cols, src_col0=0):
    return [(src.at[pl.ds(r, chunk_rows), pl.ds(src_col0 + c, chunk_cols)],
             dst.at[pl.ds(r, chunk_rows), pl.ds(c, chunk_cols)], chunk_cols)
            for r in range(0, rows, chunk_rows) for c in range(0, cols, chunk_cols)]


def _ssm_prep_kernel(lr_ref, li_ref, ls_ref, btr_ref, bti_ref, ctr_ref, cti_ref,
                     ar_ref, ai_ref, bx_ref, cm_ref):
    lr = lr_ref[0]
    li = li_ref[0]
    step = jnp.exp(ls_ref[0])
    mag = jnp.exp(lr * step)
    ar = mag * jnp.cos(li * step)
    ai = mag * jnp.sin(li * step)
    den = lr * lr + li * li
    cr = ((ar - 1.0) * lr + ai * li) / den
    ci = (ai * lr - (ar - 1.0) * li) / den
    ar_ref[0] = ar
    ai_ref[0] = ai
    nj = btr_ref.shape[1]
    gp = SUBLANES * SSM_STATE
    rows_b = lax.broadcasted_iota(jnp.int32, (LANES, gp), 0) // GROUP_CH
    cols_b = lax.broadcasted_iota(jnp.int32, (LANES, gp), 1) // SSM_STATE
    mask_b = rows_b == cols_b
    rows_c = lax.broadcasted_iota(jnp.int32, (gp, LANES), 0) // SSM_STATE
    cols_c = lax.broadcasted_iota(jnp.int32, (gp, LANES), 1) // GROUP_CH
    mask_c = rows_c == cols_c
    spread = (lax.broadcasted_iota(jnp.int32, (GROUP_CH, LANES), 1) % GROUP_CH
              == lax.broadcasted_iota(jnp.int32, (GROUP_CH, LANES), 0)).astype(BF16)
    groups = LANES // GROUP_CH
    for j in range(nj):
        crj = cr[:, j * gp:(j + 1) * gp]
        cij = ci[:, j * gp:(j + 1) * gp]
        b_re = jnp.concatenate([btr_ref[0, j]] * groups, axis=0)
        b_im = jnp.concatenate([bti_ref[0, j]] * groups, axis=0)
        bb_re = jnp.where(mask_b, crj * b_re - cij * b_im, 0.0)
        bb_im = jnp.where(mask_b, crj * b_im + cij * b_re, 0.0)
        bx_ref[0, j, :, :gp] = bb_re.astype(BF16)
        bx_ref[0, j, :, gp:] = bb_im.astype(BF16)
        c_re = jnp.dot(ctr_ref[0, j].astype(BF16), spread, preferred_element_type=F32)
        c_im = jnp.dot(cti_ref[0, j].astype(BF16), spread, preferred_element_type=F32)
        cm_ref[0, j, :gp, :] = jnp.where(mask_c, c_re, 0.0).astype(BF16)
        cm_ref[0, j, gp:, :] = jnp.where(mask_c, -c_im, 0.0).astype(BF16)


def _ssm_prep(lam_re, lam_im, log_step, b_re, b_im, c_re, c_im):
    L, G, P = lam_re.shape
    GC = b_re.shape[-1]
    gpb = LANES // GC
    nj = G // gpb
    gp = gpb * P
    flat = lambda a: a.reshape(L, 1, G * P)
    ls = jnp.repeat(log_step, P, axis=-1).reshape(L, 1, G * P)

    def b_tiles(b):
        return b.reshape(L, nj, gpb, P, GC).transpose(0, 1, 4, 2, 3).reshape(L, nj, GC, gp)

    def c_tiles(c):
        return c.reshape(L, nj, gpb, GC, P).transpose(0, 1, 2, 4, 3).reshape(L, nj, gp, GC)

    vec = pl.BlockSpec((1, 1, G * P), lambda l: (l, 0, 0))
    bt = pl.BlockSpec((1, nj, GC, gp), lambda l: (l, 0, 0, 0))
    ct = pl.BlockSpec((1, nj, gp, GC), lambda l: (l, 0, 0, 0))
    return pl.pallas_call(
        _ssm_prep_kernel,
        grid=(L,),
        in_specs=[vec, vec, vec, bt, bt, ct, ct],
        out_specs=[vec, vec,
                   pl.BlockSpec((1, nj, LANES, 2 * gp), lambda l: (l, 0, 0, 0)),
                   pl.BlockSpec((1, nj, 2 * gp, LANES), lambda l: (l, 0, 0, 0))],
        out_shape=[jax.ShapeDtypeStruct((L, 1, G * P), F32), jax.ShapeDtypeStruct((L, 1, G * P), F32),
                   jax.ShapeDtypeStruct((L, nj, LANES, 2 * gp), BF16),
                   jax.ShapeDtypeStruct((L, nj, 2 * gp, LANES), BF16)],
        name="ssm_prep",
    )(flat(lam_re), flat(lam_im), ls, b_tiles(b_re), b_tiles(b_im), c_tiles(c_re), c_tiles(c_im))


def _project_rows(x, g1_ref, w_ref, qg_ref, kg_ref):
    aw = N_HEADS * HEAD_DIM
    kw = N_KV_HEADS * HEAD_DIM
    xn = _rms(x, g1_ref[...]).astype(BF16)
    z = jnp.dot(xn, w_ref[...], preferred_element_type=F32)
    q = _seg_rms(z[:, :aw], qg_ref[...]) * (HEAD_DIM ** -0.5 * LOG2E)
    k = _seg_rms(z[:, aw:aw + kw], kg_ref[...])
    v = z[:, aw + kw:aw + 2 * kw]
    u = z[:, aw + 2 * kw:]
    lower = (lax.broadcasted_iota(jnp.int32, q.shape, 1) % LANES) < HEAD_DIM
    q_lo = jnp.where(lower, q, 0.0)
    q_hi = jnp.where(lower, 0.0, q)
    return q_lo, q_hi, k, v, u


def _stage_mixer_weights(layer, w_in_hbm, wglu_hbm, w_ref, wglu_ref, stage, sem):
    _stage_weights(
        _weight_chunks(w_in_hbm.at[layer], w_ref, w_ref.shape[0], w_ref.shape[1], stage.shape[1], w_ref.shape[1])
        + _weight_chunks(wglu_hbm.at[layer], wglu_ref, wglu_ref.shape[0], wglu_ref.shape[1], stage.shape[1],
                         wglu_ref.shape[1]),
        stage, sem)


def _mixer_weight_scratch(D, in_cols, sw):
    return [pltpu.VMEM((D, in_cols), BF16), pltpu.VMEM((sw, sw), BF16),
            pltpu.VMEM((WEIGHT_SLOTS, LANES, in_cols), F32), pltpu.SemaphoreType.DMA((WEIGHT_SLOTS,))]


def _ssm_readout(y, u, d_ref, wglu_ref, bglu_ref):
    y = _gelu_tanh(y + d_ref[...] * u)
    gate = jnp.dot(y.astype(BF16), wglu_ref[...], preferred_element_type=F32) + bglu_ref[...]
    return y * jax.nn.sigmoid(gate)


def _softmax_pv(s, sink, v):
    m = jnp.maximum(jnp.max(s, axis=-1, keepdims=True), sink)
    e = jnp.exp2(s - m)
    den = jnp.sum(e, axis=-1, keepdims=True) + jnp.exp2(sink - m)
    return jnp.dot(e.astype(BF16), v, preferred_element_type=F32) * (1.0 / den)


def _mixer_prompt_kernel(layer, sink_ref, x_ref, w_in_hbm, wglu_hbm, g1_ref, qg_ref, kg_ref,
                         ar_ref, ai_ref, bx_ref, cm_ref, d_ref, bglu_ref,
                         attn_ref, ssm_ref, kout_ref, vout_ref, hr_ref, hi_ref,
                         w_ref, wglu_ref, stage, sem, q_s, k_s, v_s, u_s, y_s, *x_s):
    c = pl.program_id(0)
    nb, tt, _ = x_ref.shape
    rows = nb * tt
    n_tiles = rows // ROW_TILE
    bpt = ROW_TILE // tt
    n_state = ar_ref.shape[-1]
    nj = bx_ref.shape[0]
    gp = n_state // nj
    xr_s, xi_s = x_s[:nj], x_s[nj:]
    kw = kout_ref.shape[1]

    @pl.when(c == 0)
    def _():
        _stage_mixer_weights(layer, w_in_hbm, wglu_hbm, w_ref, wglu_ref, stage, sem)
        k_s[:, :, :tt, :] = jnp.zeros((2, nb, tt, kw), BF16)
        v_s[:, :, :tt, :] = jnp.zeros((2, nb, tt, kw), BF16)
        hr_ref[...] = jnp.zeros(hr_ref.shape, F32)
        hi_ref[...] = jnp.zeros(hi_ref.shape, F32)

    def project(r, carry):
        x = x_ref[pl.ds(r * bpt, bpt)].reshape(ROW_TILE, x_ref.shape[-1])
        q_lo, q_hi, k, v, u = _project_rows(x, g1_ref, w_ref, qg_ref, kg_ref)
        q_s[0, pl.ds(r * bpt, bpt)] = q_lo.astype(BF16).reshape(bpt, tt, -1)
        q_s[1, pl.ds(r * bpt, bpt)] = q_hi.astype(BF16).reshape(bpt, tt, -1)
        k_s[0, pl.ds(r * bpt, bpt), tt:, :] = k.astype(BF16).reshape(bpt, tt, -1)
        v_s[0, pl.ds(r * bpt, bpt), tt:, :] = v.astype(BF16).reshape(bpt, tt, -1)
        k_s[1, pl.ds(r * bpt, bpt), tt:, :] = pltpu.roll(k, HEAD_DIM, 1).astype(BF16).reshape(bpt, tt, -1)
        v_s[1, pl.ds(r * bpt, bpt), tt:, :] = pltpu.roll(v, HEAD_DIM, 1).astype(BF16).reshape(bpt, tt, -1)
        for bb in range(bpt):
            kout_ref[r * bpt + bb] = k[bb * tt:(bb + 1) * tt, :].T
            vout_ref[r * bpt + bb] = v[bb * tt:(bb + 1) * tt, :].T
        for bb in range(bpt):
            row0 = pl.multiple_of((r * bpt + bb) * U_PITCH, SUBLANES)
            for j in range(nj):
                u_s[j, pl.ds(row0, tt), :] = u[bb * tt:(bb + 1) * tt, j * LANES:(j + 1) * LANES]
        return carry

    lax.fori_loop(0, n_tiles, project, 0, unroll=True)

    t_idx = lax.broadcasted_iota(jnp.int32, (tt, 2 * tt), 0)
    j_idx = lax.broadcasted_iota(jnp.int32, (tt, 2 * tt), 1)
    dist = t_idx - j_idx + tt
    valid = (dist >= 0) & (dist <= WINDOW) & (j_idx >= jnp.where(c > 0, 0, tt))
    key_pos = lax.broadcasted_iota(jnp.int32, (1, 2 * tt), 1).astype(F32)
    row_pos = (lax.broadcasted_iota(jnp.int32, (tt, 1), 0) + tt).astype(F32)
    lower = lax.broadcasted_iota(jnp.int32, (tt, LANES), 1) < HEAD_DIM

    def attend(b):
        for p in range(N_HEADS // 2):
            kv = (2 * p) // Q_GROUP
            outs = []
            for e in range(2):
                h = 2 * p + e
                swap = 0 if kv == e else 1
                qm = q_s[e, b, :, p * LANES:(p + 1) * LANES]
                s = lax.dot_general(qm, k_s[swap, b], (((1,), (1,)), ((), ())), preferred_element_type=F32)
                slope = SLOPES[h] * LOG2E
                s = jnp.where(valid, s + slope * key_pos, NEG_INF)
                sink = sink_ref[layer, h] * LOG2E + slope * row_pos
                outs.append(_softmax_pv(s, sink, v_s[swap, b]))
            attn_ref[b, :, p * LANES:(p + 1) * LANES] = jnp.where(lower, outs[0], outs[1]).astype(BF16)

    def ssm_in_and_attend(r, carry):
        rs = pl.ds(r * ROW_TILE, ROW_TILE)
        t0 = r * (ROW_TILE // nb)
        for j in range(nj):
            up = jnp.concatenate([u_s[j, pl.ds(t0 + step, nb, stride=U_PITCH), :]
                                  for step in range(ROW_TILE // nb)], axis=0).astype(BF16)
            xj = jnp.dot(up, bx_ref[j], preferred_element_type=F32)
            xr_s[j][rs, :] = xj[:, :gp]
            xi_s[j][rs, :] = xj[:, gp:]
        for bb in range(bpt):
            attend(r * bpt + bb)
        return carry

    lax.fori_loop(0, n_tiles, ssm_in_and_attend, 0, unroll=2)

    k_s[:, :, :tt, :] = k_s[:, :, tt:, :]
    v_s[:, :, :tt, :] = v_s[:, :, tt:, :]

    def ssm_out_tile(j, r):
        rs = pl.ds(r * ROW_TILE, ROW_TILE)
        h_cat = jnp.concatenate([xr_s[j][rs, :], xi_s[j][rs, :]], axis=1).astype(BF16)
        y_s[j, rs, :] = jnp.dot(h_cat, cm_ref[j], preferred_element_type=F32)

    for lb in range(nj):
        sl = slice(lb * gp, (lb + 1) * gp)
        a_re = jnp.broadcast_to(ar_ref[:, sl], (nb, gp))
        a_im = jnp.broadcast_to(ai_ref[:, sl], (nb, gp))

        def scan_tile(r, h, lb=lb, a_re=a_re, a_im=a_im):
            if lb > 0:
                ssm_out_tile(lb - 1, r)
            h_re, h_im = h
            for step in range(ROW_TILE // nb):
                row = pl.ds(pl.multiple_of(r * ROW_TILE + step * nb, nb), nb)
                n_re = a_re * h_re - a_im * h_im + xr_s[lb][row, :]
                n_im = a_re * h_im + a_im * h_re + xi_s[lb][row, :]
                xr_s[lb][row, :] = n_re
                xi_s[lb][row, :] = n_im
                h_re, h_im = n_re, n_im
            return h_re, h_im

        h_re, h_im = lax.fori_loop(0, n_tiles, scan_tile, (hr_ref[:, sl], hi_ref[:, sl]), unroll=True)
        hr_ref[:, sl] = h_re
        hi_ref[:, sl] = h_im

    def ssm_out_last(r, carry):
        ssm_out_tile(nj - 1, r)
        return carry

    lax.fori_loop(0, n_tiles, ssm_out_last, 0, unroll=True)

    def readout(b, carry):
        y = jnp.concatenate([y_s[j, pl.ds(b, tt, stride=nb), :] for j in range(nj)], axis=1)
        row0 = pl.multiple_of(b * U_PITCH, SUBLANES)
        u = jnp.concatenate([u_s[j, pl.ds(row0, tt), :] for j in range(nj)], axis=1)
        ssm_ref[b] = _ssm_readout(y, u, d_ref, wglu_ref, bglu_ref).astype(BF16)
        return carry

    lax.fori_loop(0, nb, readout, 0, unroll=4)


def _mixer_prompt(layer, x, sinks, w_in, wglu, g1, qg, kg, ar, ai, bx, cm, d, bglu):
    nb, T, D = x.shape
    tt = WINDOW
    rows = nb * tt
    aw = N_HEADS * HEAD_DIM
    kw = N_KV_HEADS * HEAD_DIM
    sw = d.shape[-1]
    n_state = ar.shape[-1]
    nj = bx.shape[1]
    chunk = lambda last: pl.BlockSpec((nb, tt, last), lambda c: (0, c, 0))
    whole = pl.BlockSpec((nb, kw, tt), lambda c: (0, 0, 0))
    state = pl.BlockSpec((nb, n_state), lambda c: (0, 0))
    hbm = pl.BlockSpec(memory_space=pl.ANY)
    stacked = (g1, qg, kg, ar, ai, bx, cm, d, bglu)
    return pl.pallas_call(
        functools.partial(_mixer_prompt_kernel, layer),
        grid=(T // tt,),
        in_specs=[pl.BlockSpec(memory_space=pltpu.SMEM), chunk(D), hbm, hbm]
                 + [_layer_spec(a, layer) for a in stacked],
        out_specs=[chunk(aw), chunk(sw), whole, whole, state, state],
        out_shape=[jax.ShapeDtypeStruct((nb, T, aw), BF16), jax.ShapeDtypeStruct((nb, T, sw), BF16),
                   jax.ShapeDtypeStruct((nb, kw, tt), F32), jax.ShapeDtypeStruct((nb, kw, tt), F32),
                   jax.ShapeDtypeStruct((nb, n_state), F32), jax.ShapeDtypeStruct((nb, n_state), F32)],
        scratch_shapes=_mixer_weight_scratch(D, aw + 2 * kw + sw, sw)
                       + [pltpu.VMEM((2, nb, tt, aw), BF16),
                          pltpu.VMEM((2, nb, 2 * tt, kw), BF16), pltpu.VMEM((2, nb, 2 * tt, kw), BF16),
                          pltpu.VMEM((sw // LANES, nb * U_PITCH, LANES), F32),
                          pltpu.VMEM((nj, rows, LANES), F32)]
                       + [pltpu.VMEM((rows, n_state // nj), F32)] * (2 * nj),
        compiler_params=pltpu.CompilerParams(dimension_semantics=("arbitrary",), vmem_limit_bytes=VMEM_LIMIT),
        name="mixer_prompt",
    )(sinks, x, w_in, wglu, *stacked)


def _mixer_sample_kernel(layer, sink_ref, x_ref, ckt_ref, cvt_ref, h0rt_ref, h0it_ref, *refs):
    if layer:
        prevk_ref, prevv_ref, *refs = refs
    (w_in_hbm, wglu_hbm, g1_ref, qg_ref, kg_ref, ar_ref, ai_ref, bx_ref, cm_ref, d_ref, bglu_ref,
     attn_ref, ssm_ref, kcat_ref, vcat_ref, hrt_ref, hit_ref,
     w_ref, wglu_ref, stage, sem, q_s, o_s, u_s, xr_s, xi_s, knew_s, vnew_s,
     h0r_ref, h0i_ref, hr_ref, hi_ref, *group_s) = refs
    i = pl.program_id(0)
    rows = x_ref.shape[0]
    nbatch = h0rt_ref.shape[1]
    ts = rows // nbatch
    gb_step, kw, past = ckt_ref.shape
    gb = SUBLANES
    n_tiles = rows // ROW_TILE
    n_state = ar_ref.shape[-1]
    nj = bx_ref.shape[0]
    gp = n_state // nj

    @pl.when(i == 0)
    def _():
        _stage_mixer_weights(layer, w_in_hbm, wglu_hbm, w_ref, wglu_ref, stage, sem)

        def project(r, carry):
            rs = pl.ds(r * ROW_TILE, ROW_TILE)
            q_lo, q_hi, k, v, u = _project_rows(x_ref[rs, :], g1_ref, w_ref, qg_ref, kg_ref)
            q_s[0, rs, :] = q_lo
            q_s[1, rs, :] = q_hi
            knew_s[rs, :] = k
            vnew_s[rs, :] = v
            u_s[rs, :] = u
            return carry

        lax.fori_loop(0, n_tiles, project, 0)

    n_q = ts * Q_GROUP * gb
    n_keys = gb * past + ts * gb
    r_idx = lax.broadcasted_iota(jnp.int32, (n_q, n_keys), 0)
    c_idx = lax.broadcasted_iota(jnp.int32, (n_q, n_keys), 1)
    q_t = r_idx // (Q_GROUP * gb)
    q_b = r_idx % gb
    is_cache = c_idx < gb * past
    key_b = jnp.where(is_cache, c_idx // past, (c_idx - gb * past) % gb)
    dist = jnp.where(is_cache, past + q_t - c_idx % past, q_t - (c_idx - gb * past) // gb)
    valid = (key_b == q_b) & (dist >= 0) & (dist <= WINDOW)
    dist_f = dist.astype(F32)
    hh = (lax.broadcasted_iota(jnp.int32, (n_q, 1), 0) // gb) % Q_GROUP
    lower = lax.broadcasted_iota(jnp.int32, (gb, LANES), 1) < HEAD_DIM
    def attend_group(sg):
        ck_ref, cv_ref, kroll_s, vroll_s = group_s[4 * sg:4 * sg + 4]
        sbase = pl.multiple_of(i * gb_step + sg * gb, gb)
        for cat_ref, ct_ref, c_ref, roll_s, new_s in ((kcat_ref, ckt_ref, ck_ref, kroll_s, knew_s),
                                                      (vcat_ref, cvt_ref, cv_ref, vroll_s, vnew_s)):
            for b in range(gb):
                c_ref[b * past:(b + 1) * past, :] = ct_ref[sg * gb + b].T
            for b in range(gb):
                roll_s[b * past:(b + 1) * past - ts, :] = c_ref[b * past + ts:(b + 1) * past, :]
            for t in range(ts):
                roll_s[pl.ds(past - ts + t, gb, stride=past), :] = new_s[pl.ds(sbase + t * nbatch, gb), :]
            for b in range(gb):
                cat_ref[layer, sg * gb + b] = roll_s[b * past:(b + 1) * past, :].T

        k_all = jnp.concatenate(
            [ck_ref[...]] + [knew_s[pl.ds(sbase + t * nbatch, gb), :] for t in range(ts)], axis=0).astype(BF16)
        v_all = jnp.concatenate(
            [cv_ref[...]] + [vnew_s[pl.ds(sbase + t * nbatch, gb), :] for t in range(ts)], axis=0).astype(BF16)
        outs = []
        for kvh in range(N_KV_HEADS):
            slope = jnp.zeros((n_q, 1), F32)
            sink = jnp.zeros((n_q, 1), F32)
            for g in range(Q_GROUP):
                h = kvh * Q_GROUP + g
                slope = jnp.where(hh == g, SLOPES[h] * LOG2E, slope)
                sink = jnp.where(hh == g, sink_ref[layer, h] * LOG2E, sink)
            pieces = []
            for t in range(ts):
                for g in range(Q_GROUP):
                    h = kvh * Q_GROUP + g
                    piece = q_s[h % 2, pl.ds(sbase + t * nbatch, gb), (h // 2) * LANES:(h // 2 + 1) * LANES]
                    pieces.append(piece if h % 2 == kvh else pltpu.roll(piece, HEAD_DIM, 1))
            qm = jnp.concatenate(pieces, axis=0).astype(BF16)
            s = lax.dot_general(qm, k_all, (((1,), (1,)), ((), ())), preferred_element_type=F32)
            s = jnp.where(valid, s - slope * dist_f, NEG_INF)
            outs.append(_softmax_pv(s, sink, v_all))
        for t in range(ts):
            for p in range(N_HEADS // 2):
                kvh = (2 * p) // Q_GROUP
                r0 = (t * Q_GROUP + 2 * p - kvh * Q_GROUP) * gb
                lo = outs[kvh][r0:r0 + gb]
                hi = outs[kvh][r0 + gb:r0 + 2 * gb]
                if kvh == 1:
                    lo = pltpu.roll(lo, HEAD_DIM, 1)
                else:
                    hi = pltpu.roll(hi, HEAD_DIM, 1)
                o_s[pl.ds(sbase + t * nbatch, gb), p * LANES:(p + 1) * LANES] = jnp.where(lower, lo, hi)

    for sg in range(gb_step // gb):
        attend_group(sg)
    if layer:
        kcat_ref[:layer] = prevk_ref[...]
        vcat_ref[:layer] = prevv_ref[...]

    @pl.when(i == pl.num_programs(0) - 1)
    def _():
        attn_ref[...] = o_s[...].astype(BF16)

        def ssm_in(r, carry):
            rs = pl.ds(r * ROW_TILE, ROW_TILE)
            ub = u_s[rs, :].astype(BF16)
            for j in range(nj):
                xj = jnp.dot(ub[:, j * LANES:(j + 1) * LANES], bx_ref[j], preferred_element_type=F32)
                xr_s[rs, j * gp:(j + 1) * gp] = xj[:, :gp]
                xi_s[rs, j * gp:(j + 1) * gp] = xj[:, gp:]
            return carry

        lax.fori_loop(0, n_tiles, ssm_in, 0)

        for blk in range(n_state // LANES):
            bs = slice(blk * LANES, (blk + 1) * LANES)
            h0r_ref[:, bs] = h0rt_ref[bs, :].T
            h0i_ref[:, bs] = h0it_ref[bs, :].T

        lane_blk = 4 * LANES
        for lb in range(n_state // lane_blk):
            sl = slice(lb * lane_blk, (lb + 1) * lane_blk)
            a_re = jnp.broadcast_to(ar_ref[:, sl], (SUBLANES, lane_blk))
            a_im = jnp.broadcast_to(ai_ref[:, sl], (SUBLANES, lane_blk))

            def scan(bt, carry):
                b0 = pl.multiple_of(bt * SUBLANES, SUBLANES)
                h_re = h0r_ref[pl.ds(b0, SUBLANES), sl]
                h_im = h0i_ref[pl.ds(b0, SUBLANES), sl]
                for t in range(ts):
                    rs = pl.ds(b0 + t * nbatch, SUBLANES)
                    n_re = a_re * h_re - a_im * h_im + xr_s[rs, sl]
                    n_im = a_re * h_im + a_im * h_re + xi_s[rs, sl]
                    xr_s[rs, sl] = n_re
                    xi_s[rs, sl] = n_im
                    h_re, h_im = n_re, n_im
                hr_ref[pl.ds(b0, SUBLANES), sl] = h_re
                hi_ref[pl.ds(b0, SUBLANES), sl] = h_im
                return carry

            lax.fori_loop(0, nbatch // SUBLANES, scan, 0)

        for blk in range(n_state // LANES):
            bs = slice(blk * LANES, (blk + 1) * LANES)
            hrt_ref[bs, :] = hr_ref[:, bs].T
            hit_ref[bs, :] = hi_ref[:, bs].T

        def ssm_out(r, carry):
            rs = pl.ds(r * ROW_TILE, ROW_TILE)
            ys = []
            for j in range(nj):
                h_cat = jnp.concatenate([xr_s[rs, j * gp:(j + 1) * gp], xi_s[rs, j * gp:(j + 1) * gp]],
                                        axis=1).astype(BF16)
                ys.append(jnp.dot(h_cat, cm_ref[j], preferred_element_type=F32))
            y = jnp.concatenate(ys, axis=1)
            ssm_ref[rs, :] = _ssm_readout(y, u_s[rs, :], d_ref, wglu_ref, bglu_ref).astype(BF16)
            return carry

        lax.fori_loop(0, n_tiles, ssm_out, 0)


def _mixer_sample(layer, x, ckt, cvt, h0rt, h0it, prev_k, prev_v, sinks, w_in, wglu,
                  g1, qg, kg, ar, ai, bx, cm, d, bglu):
    rows, D = x.shape
    _, nbatch, kw, past = ckt.shape
    aw = N_HEADS * HEAD_DIM
    sw = d.shape[-1]
    n_state = ar.shape[-1]
    gb = SUBLANES
    assert nbatch % LANES == 0 and past == LANES and kw == LANES
    cache = pl.BlockSpec((None, gb, kw, past), lambda i: (layer, i, 0, 0))
    h0 = pl.BlockSpec((None, n_state, nbatch), lambda i: (layer, 0, 0))
    prev = pl.BlockSpec((layer, gb, kw, past), lambda i: (0, i, 0, 0))
    out2 = lambda last: pl.BlockSpec((rows, last), lambda i: (0, 0))
    cat = pl.BlockSpec((layer + 1, gb, kw, past), lambda i: (0, i, 0, 0))
    state = pl.BlockSpec((n_state, nbatch), lambda i: (0, 0))
    hbm = pl.BlockSpec(memory_space=pl.ANY)
    stacked = (g1, qg, kg, ar, ai, bx, cm, d, bglu)
    prevs = (prev_k, prev_v) if layer else ()
    return pl.pallas_call(
        functools.partial(_mixer_sample_kernel, layer),
        grid=(nbatch // gb,),
        in_specs=[pl.BlockSpec(memory_space=pltpu.SMEM), pl.BlockSpec(x.shape, lambda i: (0, 0)), cache, cache, h0, h0]
                 + [prev] * len(prevs) + [hbm, hbm] + [_layer_spec(a, layer) for a in stacked],
        out_specs=[out2(aw), out2(sw), cat, cat, state, state],
        out_shape=[jax.ShapeDtypeStruct((rows, aw), BF16), jax.ShapeDtypeStruct((rows, sw), BF16),
                   jax.ShapeDtypeStruct((layer + 1, nbatch, kw, past), F32),
                   jax.ShapeDtypeStruct((layer + 1, nbatch, kw, past), F32),
                   jax.ShapeDtypeStruct((n_state, nbatch), F32), jax.ShapeDtypeStruct((n_state, nbatch), F32)],
        scratch_shapes=_mixer_weight_scratch(D, aw + 2 * kw + sw, sw)
                       + [pltpu.VMEM((2, rows, aw), F32), pltpu.VMEM((rows, aw), F32), pltpu.VMEM((rows, sw), F32),
                          pltpu.VMEM((rows, n_state), F32), pltpu.VMEM((rows, n_state), F32),
                          pltpu.VMEM((rows, kw), F32), pltpu.VMEM((rows, kw), F32)]
                       + [pltpu.VMEM((nbatch, n_state), F32)] * 4
                       + [pltpu.VMEM((SUBLANES * past, kw), F32)] * (4 * gb // SUBLANES),
        compiler_params=pltpu.CompilerParams(dimension_semantics=("arbitrary",), vmem_limit_bytes=VMEM_LIMIT),
        name="mixer_sample",
    )(sinks, x, ckt, cvt, h0rt, h0it, *prevs, w_in, wglu, *stacked)


def _mix_ffn_rows(x_ref, attn_ref, ssm_ref, o_ref, g1_ref, wg_ref, bg_ref, wao_ref, wso_ref, wout_ref,
                  g2_ref, wup_ref, wdn_ref):
    x = x_ref[...]
    D = x.shape[-1]
    xn = _rms(x, g1_ref[...]).astype(BF16)
    attn = attn_ref[...]
    ssm = ssm_ref[...]
    o_ref[...] = x
    cw = 2 * LANES
    for cc in range(D // cw):
        cs = slice(cc * cw, (cc + 1) * cw)
        gs = slice(D + cc * cw, D + (cc + 1) * cw)
        g_attn = jax.nn.sigmoid(jnp.dot(xn, wg_ref[:, cs], preferred_element_type=F32) + bg_ref[:, cs])
        g_ssm = jax.nn.sigmoid(jnp.dot(xn, wg_ref[:, gs], preferred_element_type=F32) + bg_ref[:, gs])
        mixed = (g_attn * jnp.dot(attn, wao_ref[:, cs], preferred_element_type=F32)
                 + g_ssm * jnp.dot(ssm, wso_ref[:, cs], preferred_element_type=F32))
        o_ref[...] += jnp.dot(mixed.astype(BF16), wout_ref[cs, :], preferred_element_type=F32)
    xn2 = _rms(o_ref[...], g2_ref[...]).astype(BF16)
    fw = 4 * LANES
    for fc in range(wup_ref.shape[-1] // fw):
        fs = slice(fc * fw, (fc + 1) * fw)
        hdn = jnp.maximum(jnp.dot(xn2, wup_ref[:, fs], preferred_element_type=F32), 0.0)
        o_ref[...] += jnp.dot((hdn * hdn).astype(BF16), wdn_ref[fs, :], preferred_element_type=F32)


def _mix_ffn_kernel(layer, n_prompt, xp_ref, ap_ref, sp_ref, xs_ref, as_ref, ss_ref,
                    w_in_hbm, wao_hbm, wso_hbm, wout_hbm, wup_hbm, wdn_hbm, g1_ref, bg_ref, g2_ref,
                    op_ref, os_ref, wg_ref, wao_ref, wso_ref, wout_ref, wup_ref, wdn_ref, stage, sem):
    i = pl.program_id(0)
    cr, cc = stage.shape[1:]

    @pl.when(i == 0)
    def _():
        chunks = []
        chunks += _weight_chunks(w_in_hbm.at[layer], wg_ref, wg_ref.shape[0], wg_ref.shape[1], cr, cc,
                                 src_col0=w_in_hbm.shape[-1] - wg_ref.shape[1])
        for hbm, ref in ((wao_hbm, wao_ref), (wso_hbm, wso_ref), (wout_hbm, wout_ref), (wup_hbm, wup_ref),
                         (wdn_hbm, wdn_ref)):
            chunks += _weight_chunks(hbm.at[layer], ref, ref.shape[0], ref.shape[1], cr, cc)
        _stage_weights(chunks, stage, sem)

    w_refs = (g1_ref, wg_ref, bg_ref, wao_ref, wso_ref, wout_ref, g2_ref, wup_ref, wdn_ref)

    @pl.when(i < n_prompt)
    def _():
        _mix_ffn_rows(xp_ref, ap_ref, sp_ref, op_ref, *w_refs)

    @pl.when(i >= n_prompt)
    def _():
        _mix_ffn_rows(xs_ref, as_ref, ss_ref, os_ref, *w_refs)


def _mix_ffn(layer, xp, attn_p, ssm_p, xs, attn_s, ssm_s, w_in, wao, wso, wout, wup, wdn, g1, bg, g2):
    rp, D = xp.shape
    rs = xs.shape[0]
    tp = min(512, rp)
    ts = min(512, rs)
    assert rp % tp == 0 and rs % ts == 0
    n_p, n_s = rp // tp, rs // ts
    prow = lambda last: pl.BlockSpec((tp, last), lambda i: (jnp.minimum(i, n_p - 1), 0))
    srow = lambda last: pl.BlockSpec((ts, last), lambda i: (jnp.maximum(i - n_p, 0), 0))
    hbm = pl.BlockSpec(memory_space=pl.ANY)
    resident = lambda a: pltpu.VMEM(a.shape[1:], BF16)
    return pl.pallas_call(
        functools.partial(_mix_ffn_kernel, layer, n_p),
        grid=(n_p + n_s,),
        in_specs=[prow(D), prow(attn_p.shape[-1]), prow(ssm_p.shape[-1]),
                  srow(D), srow(attn_s.shape[-1]), srow(ssm_s.shape[-1])]
                 + [hbm] * 6 + [_layer_spec(a, layer) for a in (g1, bg, g2)],
        out_specs=[prow(D), srow(D)],
        out_shape=[jax.ShapeDtypeStruct((rp, D), F32), jax.ShapeDtypeStruct((rs, D), F32)],
        scratch_shapes=[pltpu.VMEM((D, bg.shape[-1]), BF16), resident(wao), resident(wso), resident(wout),
                        resident(wup), resident(wdn),
                        pltpu.VMEM((WEIGHT_SLOTS, 2 * LANES, D), F32), pltpu.SemaphoreType.DMA((WEIGHT_SLOTS,))],
        compiler_params=pltpu.CompilerParams(dimension_semantics=("arbitrary",), vmem_limit_bytes=VMEM_LIMIT),
        name="mix_ffn",
    )(xp, attn_p, ssm_p, xs, attn_s, ssm_s, w_in, wao, wso, wout, wup, wdn, g1, bg, g2)


def kernel(x_prompt, x_sample, cache_k, cache_v, state_ssm_re, state_ssm_im, norm1_g, w_in, b_gate, q_norm_g,
           k_norm_g, attn_sinks, lam_re, lam_im, log_step, b_re, b_im, c_re, c_im, d_skip, w_glu, b_glu,
           w_attn_o, w_ssm_o, w_out, norm2_g, w_up, w_down):
    B, T, D = x_prompt.shape
    DB, DS, _ = x_sample.shape
    L = w_in.shape[0]
    aw = N_HEADS * HEAD_DIM
    kw = N_KV_HEADS * HEAD_DIM
    sw = d_skip.shape[-1]
    G, P = lam_re.shape[1:]
    past = cache_k.shape[2]
    assert past == WINDOW and T % WINDOW == 0 and (B * WINDOW) % ROW_TILE == 0 and (DB * DS) % ROW_TILE == 0

    ar, ai, bx, cm = _ssm_prep(lam_re, lam_im, log_step, b_re, b_im, c_re, c_im)
    rows = lambda a: a.reshape(L, 1, -1)
    g1 = rows(norm1_g)
    mixer_w = (w_in, w_glu, g1, rows(jnp.tile(q_norm_g, (1, N_HEADS))), rows(jnp.tile(k_norm_g, (1, N_KV_HEADS))),
               ar, ai, bx, cm, rows(d_skip), rows(b_glu))
    ffn_w = (w_in, w_attn_o, w_ssm_o, w_out, w_up, w_down, g1, rows(b_gate), rows(norm2_g))
    kv_t = lambda a: a.transpose(0, 1, 3, 4, 2).reshape(L, DB, kw, past)
    st_t = lambda a: a.transpose(0, 2, 3, 1).reshape(L, G * P, DB)
    ckt, cvt, h0rt, h0it = kv_t(cache_k), kv_t(cache_v), st_t(state_ssm_re), st_t(state_ssm_im)

    yp = x_prompt.reshape(B * T, D)
    ys = x_sample.transpose(1, 0, 2).reshape(DS * DB, D)
    outs = [[] for _ in range(6)]
    kcat = vcat = None
    for l in range(L):
        attn_p, ssm_p, kp, vp, hrp, hip = _mixer_prompt(l, yp.reshape(B, T, D), attn_sinks, *mixer_w)
        attn_s, ssm_s, kcat, vcat, hrs, his = _mixer_sample(l, ys, ckt, cvt, h0rt, h0it, kcat, vcat, attn_sinks,
                                                            *mixer_w)
        yp, ys = _mix_ffn(l, yp, attn_p.reshape(B * T, aw), ssm_p.reshape(B * T, sw), ys, attn_s, ssm_s, *ffn_w)
        for o, v in zip(outs, (kp, vp, hrp, hip, hrs, his)):
            o.append(v)

    kp, vp, hrp, hip, hrs, his = (jnp.stack(o) for o in outs)
    kv_back = lambda a: a.reshape(a.shape[:2] + (N_KV_HEADS, HEAD_DIM, a.shape[-1])).transpose(0, 1, 4, 2, 3)
    st_back = lambda a: a.reshape(L, G, P, DB).transpose(0, 3, 1, 2)
    y_sample = ys.reshape(DS, DB, D).transpose(1, 0, 2)
    return (yp.reshape(B, T, D), y_sample, kv_back(kp), kv_back(vp), hrp.reshape(L, B, G, P), hip.reshape(L, B, G, P),
            kv_back(kcat), kv_back(vcat), st_back(hrs), st_back(his))
```

```python
import functools
import math

import jax
import jax.numpy as jnp
from jax import lax
from jax.experimental import pallas as pl
from jax.experimental.pallas import tpu as pltpu

F32 = jnp.float32
BF16 = jnp.bfloat16

HEAD_DIM = 64
N_HEADS = 8
N_KV_HEADS = 2
Q_GROUP = N_HEADS // N_KV_HEADS
WINDOW = 128
GROUP_CH = 16
SSM_STATE = 64
EPS = 1e-6
NEG_INF = -1e30
LOG2E = math.log2(math.e)
LANES = 128
SUBLANES = 8
ROW_TILE = 256
U_PITCH = WINDOW + SUBLANES
VMEM_LIMIT = 60 * 1024 * 1024

WEIGHT_SLOTS = 4
SLOPES = tuple(2.0 ** (-8.0 * (h + 1) / N_HEADS) for h in range(N_HEADS))


def _rms(x, g):
    return x * lax.rsqrt(jnp.mean(x * x, axis=-1, keepdims=True) + EPS) * g


def _seg_rms(x, g):
    lower = lax.broadcasted_iota(jnp.int32, (x.shape[0], LANES), 1) < HEAD_DIM
    normed = []
    for p in range(x.shape[1] // LANES):
        xb = x[:, p * LANES:(p + 1) * LANES]
        sq = xb * xb
        lo = jnp.sum(jnp.where(lower, sq, 0.0), axis=-1, keepdims=True)
        hi = jnp.sum(jnp.where(lower, 0.0, sq), axis=-1, keepdims=True)
        ms = jnp.where(lower, lo, hi) * (1.0 / HEAD_DIM)
        normed.append(xb * lax.rsqrt(ms + EPS))
    return jnp.concatenate(normed, axis=1) * g


def _gelu_tanh(y):
    return 0.5 * y * (1.0 + jnp.tanh(math.sqrt(2.0 / math.pi) * (y + 0.044715 * (y * y * y))))


def _layer_spec(a, layer):
    n = a.ndim - 1
    return pl.BlockSpec((None,) + a.shape[1:], lambda *_: (layer,) + (0,) * n, pipeline_mode=pl.Buffered(1))


def _stage_weights(chunks, stage, sem):
    slots = stage.shape[0]

    def copy(k):
        src, _, cols = chunks[k]
        return pltpu.make_async_copy(src, stage.at[k % slots, :, pl.ds(0, cols)], sem.at[k % slots])

    for k in range(min(slots - 1, len(chunks))):
        copy(k).start()
    for k, (_, dst, cols) in enumerate(chunks):
        if k + slots - 1 < len(chunks):
            copy(k + slots - 1).start()
        copy(k).wait()
        dst[...] = stage[k % slots, :, :cols].astype(BF16)


def _weight_chunks(src, dst, rows, cols, chunk_rows, chunk_cols, src_col0=0):
    return [(src.at[pl.ds(r, chunk_rows), pl.ds(src_col0 + c, chunk_cols)],
             dst.at[pl.ds(r, chunk_rows), pl.ds(c, chunk_cols)], chunk_cols)
            for r in range(0, rows, chunk_rows) for c in range(0, cols, chunk_cols)]


def _ssm_prep_kernel(lr_ref, li_ref, ls_ref, btr_ref, bti_ref, ctr_ref, cti_ref,
                     ar_ref, ai_ref, bx_ref, cm_ref):
    lr = lr_ref[0]
    li = li_ref[0]
    step = jnp.exp(ls_ref[0])
    mag = jnp.exp(lr * step)
    ar = mag * jnp.cos(li * step)
    ai = mag * jnp.sin(li * step)
    den = lr * lr + li * li
    cr = ((ar - 1.0) * lr + ai * li) / den
    ci = (ai * lr - (ar - 1.0) * li) / den
    ar_ref[0] = ar
    ai_ref[0] = ai
    nj = btr_ref.shape[1]
    gp = SUBLANES * SSM_STATE
    rows_b = lax.broadcasted_iota(jnp.int32, (LANES, gp), 0) // GROUP_CH
    cols_b = lax.broadcasted_iota(jnp.int32, (LANES, gp), 1) // SSM_STATE
    mask_b = rows_b == cols_b
    rows_c = lax.broadcasted_iota(jnp.int32, (gp, LANES), 0) // SSM_STATE
    cols_c = lax.broadcasted_iota(jnp.int32, (gp, LANES), 1) // GROUP_CH
    mask_c = rows_c == cols_c
    spread = (lax.broadcasted_iota(jnp.int32, (GROUP_CH, LANES), 1) % GROUP_CH
              == lax.broadcasted_iota(jnp.int32, (GROUP_CH, LANES), 0)).astype(BF16)
    groups = LANES // GROUP_CH
    for j in range(nj):
        crj = cr[:, j * gp:(j + 1) * gp]
        cij = ci[:, j * gp:(j + 1) * gp]
        b_re = jnp.concatenate([btr_ref[0, j]] * groups, axis=0)
        b_im = jnp.concatenate([bti_ref[0, j]] * groups, axis=0)
        bb_re = jnp.where(mask_b, crj * b_re - cij * b_im, 0.0)
        bb_im = jnp.where(mask_b, crj * b_im + cij * b_re, 0.0)
        bx_ref[0, j, :, :gp] = bb_re.astype(BF16)
        bx_ref[0, j, :, gp:] = bb_im.astype(BF16)
        c_re = jnp.dot(ctr_ref[0, j].astype(BF16), spread, preferred_element_type=F32)
        c_im = jnp.dot(cti_ref[0, j].astype(BF16), spread, preferred_element_type=F32)
        cm_ref[0, j, :gp, :] = jnp.where(mask_c, c_re, 0.0).astype(BF16)
        cm_ref[0, j, gp:, :] = jnp.where(mask_c, -c_im, 0.0).astype(BF16)


def _ssm_prep(lam_re, lam_im, log_step, b_re, b_im, c_re, c_im):
    L, G, P = lam_re.shape
    GC = b_re.shape[-1]
    gpb = LANES // GC
    nj = G // gpb
    gp = gpb * P
    flat = lambda a: a.reshape(L, 1, G * P)
    ls = jnp.repeat(log_step, P, axis=-1).reshape(L, 1, G * P)

    def b_tiles(b):
        return b.reshape(L, nj, gpb, P, GC).transpose(0, 1, 4, 2, 3).reshape(L, nj, GC, gp)

    def c_tiles(c):
        return c.reshape(L, nj, gpb, GC, P).transpose(0, 1, 2, 4, 3).reshape(L, nj, gp, GC)

    vec = pl.BlockSpec((1, 1, G * P), lambda l: (l, 0, 0))
    bt = pl.BlockSpec((1, nj, GC, gp), lambda l: (l, 0, 0, 0))
    ct = pl.BlockSpec((1, nj, gp, GC), lambda l: (l, 0, 0, 0))
    return pl.pallas_call(
        _ssm_prep_kernel,
        grid=(L,),
        in_specs=[vec, vec, vec, bt, bt, ct, ct],
        out_specs=[vec, vec,
                   pl.BlockSpec((1, nj, LANES, 2 * gp), lambda l: (l, 0, 0, 0)),
                   pl.BlockSpec((1, nj, 2 * gp, LANES), lambda l: (l, 0, 0, 0))],
        out_shape=[jax.ShapeDtypeStruct((L, 1, G * P), F32), jax.ShapeDtypeStruct((L, 1, G * P), F32),
                   jax.ShapeDtypeStruct((L, nj, LANES, 2 * gp), BF16),
                   jax.ShapeDtypeStruct((L, nj, 2 * gp, LANES), BF16)],
        name="ssm_prep",
    )(flat(lam_re), flat(lam_im), ls, b_tiles(b_re), b_tiles(b_im), c_tiles(c_re), c_tiles(c_im))


def _project_rows(x, g1_ref, w_ref, qg_ref, kg_ref):
    aw = N_HEADS * HEAD_DIM
    kw = N_KV_HEADS * HEAD_DIM
    xn = _rms(x, g1_ref[...]).astype(BF16)
    z = jnp.dot(xn, w_ref[...], preferred_element_type=F32)
    q = _seg_rms(z[:, :aw], qg_ref[...]) * (HEAD_DIM ** -0.5 * LOG2E)
    k = _seg_rms(z[:, aw:aw + kw], kg_ref[...])
    v = z[:, aw + kw:aw + 2 * kw]
    u = z[:, aw + 2 * kw:]
    lower = (lax.broadcasted_iota(jnp.int32, q.shape, 1) % LANES) < HEAD_DIM
    q_lo = jnp.where(lower, q, 0.0)
    q_hi = jnp.where(lower, 0.0, q)
    return q_lo, q_hi, k, v, u


def _stage_mixer_weights(layer, w_in_hbm, wglu_hbm, w_ref, wglu_ref, stage, sem):
    _stage_weights(
        _weight_chunks(w_in_hbm.at[layer], w_ref, w_ref.shape[0], w_ref.shape[1], stage.shape[1], w_ref.shape[1])
        + _weight_chunks(wglu_hbm.at[layer], wglu_ref, wglu_ref.shape[0], wglu_ref.shape[1], stage.shape[1],
                         wglu_ref.shape[1]),
        stage, sem)


def _mixer_weight_scratch(D, in_cols, sw):
    return [pltpu.VMEM((D, in_cols), BF16), pltpu.VMEM((sw, sw), BF16),
            pltpu.VMEM((WEIGHT_SLOTS, LANES, in_cols), F32), pltpu.SemaphoreType.DMA((WEIGHT_SLOTS,))]


def _ssm_readout(y, u, d_ref, wglu_ref, bglu_ref):
    y = _gelu_tanh(y + d_ref[...] * u)
    gate = jnp.dot(y.astype(BF16), wglu_ref[...], preferred_element_type=F32) + bglu_ref[...]
    return y * jax.nn.sigmoid(gate)


def _softmax_pv(s, sink, v):
    m = jnp.maximum(jnp.max(s, axis=-1, keepdims=True), sink)
    e = jnp.exp2(s - m)
    den = jnp.sum(e, axis=-1, keepdims=True) + jnp.exp2(sink - m)
    return jnp.dot(e.astype(BF16), v, preferred_element_type=F32) * (1.0 / den)


def _mixer_prompt_kernel(layer, sink_ref, x_ref, w_in_hbm, wglu_hbm, g1_ref, qg_ref, kg_ref,
                         ar_ref, ai_ref, bx_ref, cm_ref, d_ref, bglu_ref,
                         attn_ref, ssm_ref, kout_ref, vout_ref, hr_ref, hi_ref,
                         w_ref, wglu_ref, stage, sem, q_s, k_s, v_s, u_s, y_s, *x_s):
    c = pl.program_id(0)
    nb, tt, _ = x_ref.shape
    rows = nb * tt
    n_tiles = rows // ROW_TILE
    bpt = ROW_TILE // tt
    n_state = ar_ref.shape[-1]
    nj = bx_ref.shape[0]
    gp = n_state // nj
    xr_s, xi_s = x_s[:nj], x_s[nj:]
    kw = kout_ref.shape[1]
    cur = c % 2

    @pl.when(c == 0)
    def _():
        _stage_mixer_weights(layer, w_in_hbm, wglu_hbm, w_ref, wglu_ref, stage, sem)
        k_s[:, 1] = jnp.zeros((2, nb, tt, kw), BF16)
        v_s[:, 1] = jnp.zeros((2, nb, tt, kw), BF16)
        hr_ref[...] = jnp.zeros(hr_ref.shape, F32)
        hi_ref[...] = jnp.zeros(hi_ref.shape, F32)

    def project(r, carry):
        x = x_ref[pl.ds(r * bpt, bpt)].reshape(ROW_TILE, x_ref.shape[-1])
        q_lo, q_hi, k, v, u = _project_rows(x, g1_ref, w_ref, qg_ref, kg_ref)
        q_s[0, pl.ds(r * bpt, bpt)] = q_lo.astype(BF16).reshape(bpt, tt, -1)
        q_s[1, pl.ds(r * bpt, bpt)] = q_hi.astype(BF16).reshape(bpt, tt, -1)
        k_s[0, cur, pl.ds(r * bpt, bpt)] = k.astype(BF16).reshape(bpt, tt, -1)
        v_s[0, cur, pl.ds(r * bpt, bpt)] = v.astype(BF16).reshape(bpt, tt, -1)
        k_s[1, cur, pl.ds(r * bpt, bpt)] = pltpu.roll(k, HEAD_DIM, 1).astype(BF16).reshape(bpt, tt, -1)
        v_s[1, cur, pl.ds(r * bpt, bpt)] = pltpu.roll(v, HEAD_DIM, 1).astype(BF16).reshape(bpt, tt, -1)
        for bb in range(bpt):
            kout_ref[r * bpt + bb] = k[bb * tt:(bb + 1) * tt, :].T
            vout_ref[r * bpt + bb] = v[bb * tt:(bb + 1) * tt, :].T
        for bb in range(bpt):
            row0 = pl.multiple_of((r * bpt + bb) * U_PITCH, SUBLANES)
            for j in range(nj):
                u_s[j, pl.ds(row0, tt), :] = u[bb * tt:(bb + 1) * tt, j * LANES:(j + 1) * LANES]
        return carry

    lax.fori_loop(0, n_tiles, project, 0, unroll=True)

    t_idx = lax.broadcasted_iota(jnp.int32, (tt, 2 * tt), 0)
    j_idx = lax.broadcasted_iota(jnp.int32, (tt, 2 * tt), 1)
    dist = t_idx - j_idx + tt
    valid = (dist >= 0) & (dist <= WINDOW) & (j_idx >= jnp.where(c > 0, 0, tt))
    key_pos = lax.broadcasted_iota(jnp.int32, (1, 2 * tt), 1).astype(F32)
    row_pos = (lax.broadcasted_iota(jnp.int32, (tt, 1), 0) + tt).astype(F32)
    lower = lax.broadcasted_iota(jnp.int32, (tt, LANES), 1) < HEAD_DIM

    def attend(b):
        for p in range(N_HEADS // 2):
            kv = (2 * p) // Q_GROUP
            outs = []
            for e in range(2):
                h = 2 * p + e
                swap = 0 if kv == e else 1
                qm = q_s[e, b, :, p * LANES:(p + 1) * LANES]
                keys = jnp.concatenate([k_s[swap, 1 - cur, b], k_s[swap, cur, b]], axis=0)
                s = lax.dot_general(qm, keys, (((1,), (1,)), ((), ())), preferred_element_type=F32)
                slope = SLOPES[h] * LOG2E
                s = jnp.where(valid, s + slope * key_pos, NEG_INF)
                sink = sink_ref[layer, h] * LOG2E + slope * row_pos
                vals = jnp.concatenate([v_s[swap, 1 - cur, b], v_s[swap, cur, b]], axis=0)
                outs.append(_softmax_pv(s, sink, vals))
            attn_ref[b, :, p * LANES:(p + 1) * LANES] = jnp.where(lower, outs[0], outs[1]).astype(BF16)

    def ssm_in_and_attend(r, carry):
        rs = pl.ds(r * ROW_TILE, ROW_TILE)
        t0 = r * (ROW_TILE // nb)
        for j in range(nj):
            up = jnp.concatenate([u_s[j, pl.ds(t0 + step, nb, stride=U_PITCH), :]
                                  for step in range(ROW_TILE // nb)], axis=0).astype(BF16)
            xj = jnp.dot(up, bx_ref[j], preferred_element_type=F32)
            xr_s[j][rs, :] = xj[:, :gp]
            xi_s[j][rs, :] = xj[:, gp:]
        for bb in range(bpt):
            attend(r * bpt + bb)
        return carry

    lax.fori_loop(0, n_tiles, ssm_in_and_attend, 0, unroll=2)

    def ssm_out_tile(j, r):
        rs = pl.ds(r * ROW_TILE, ROW_TILE)
        h_cat = jnp.concatenate([xr_s[j][rs, :], xi_s[j][rs, :]], axis=1).astype(BF16)
        y_s[j, rs, :] = jnp.dot(h_cat, cm_ref[j], preferred_element_type=F32)

    for lb in range(nj):
        sl = slice(lb * gp, (lb + 1) * gp)
        a_re = jnp.broadcast_to(ar_ref[:, sl], (nb, gp))
        a_im = jnp.broadcast_to(ai_ref[:, sl], (nb, gp))

        def scan_tile(r, h, lb=lb, a_re=a_re, a_im=a_im):
            if lb > 0:
                ssm_out_tile(lb - 1, r)
            h_re, h_im = h
            for step in range(ROW_TILE // nb):
                row = pl.ds(pl.multiple_of(r * ROW_TILE + step * nb, nb), nb)
                n_re = a_re * h_re - a_im * h_im + xr_s[lb][row, :]
                n_im = a_re * h_im + a_im * h_re + xi_s[lb][row, :]
                xr_s[lb][row, :] = n_re
                xi_s[lb][row, :] = n_im
                h_re, h_im = n_re, n_im
            return h_re, h_im

        h_re, h_im = lax.fori_loop(0, n_tiles, scan_tile, (hr_ref[:, sl], hi_ref[:, sl]), unroll=True)
        hr_ref[:, sl] = h_re
        hi_ref[:, sl] = h_im

    def ssm_out_last(r, carry):
        ssm_out_tile(nj - 1, r)
        return carry

    lax.fori_loop(0, n_tiles, ssm_out_last, 0, unroll=True)

    def readout(b, carry):
        y = jnp.concatenate([y_s[j, pl.ds(b, tt, stride=nb), :] for j in range(nj)], axis=1)
        row0 = pl.multiple_of(b * U_PITCH, SUBLANES)
        u = jnp.concatenate([u_s[j, pl.ds(row0, tt), :] for j in range(nj)], axis=1)
        ssm_ref[b] = _ssm_readout(y, u, d_ref, wglu_ref, bglu_ref).astype(BF16)
        return carry

    lax.fori_loop(0, nb, readout, 0, unroll=4)


def _mixer_prompt(layer, x, sinks, w_in, wglu, g1, qg, kg, ar, ai, bx, cm, d, bglu):
    nb, T, D = x.shape
    tt = WINDOW
    rows = nb * tt
    aw = N_HEADS * HEAD_DIM
    kw = N_KV_HEADS * HEAD_DIM
    sw = d.shape[-1]
    n_state = ar.shape[-1]
    nj = bx.shape[1]
    chunk = lambda last: pl.BlockSpec((nb, tt, last), lambda c: (0, c, 0))
    whole = pl.BlockSpec((nb, kw, tt), lambda c: (0, 0, 0))
    state = pl.BlockSpec((nb, n_state), lambda c: (0, 0))
    hbm = pl.BlockSpec(memory_space=pl.ANY)
    stacked = (g1, qg, kg, ar, ai, bx, cm, d, bglu)
    return pl.pallas_call(
        functools.partial(_mixer_prompt_kernel, layer),
        grid=(T // tt,),
        in_specs=[pl.BlockSpec(memory_space=pltpu.SMEM), chunk(D), hbm, hbm]
                 + [_layer_spec(a, layer) for a in stacked],
        out_specs=[chunk(aw), chunk(sw), whole, whole, state, state],
        out_shape=[jax.ShapeDtypeStruct((nb, T, aw), BF16), jax.ShapeDtypeStruct((nb, T, sw), BF16),
                   jax.ShapeDtypeStruct((nb, kw, tt), F32), jax.ShapeDtypeStruct((nb, kw, tt), F32),
                   jax.ShapeDtypeStruct((nb, n_state), F32), jax.ShapeDtypeStruct((nb, n_state), F32)],
        scratch_shapes=_mixer_weight_scratch(D, aw + 2 * kw + sw, sw)
                       + [pltpu.VMEM((2, nb, tt, aw), BF16),
                          pltpu.VMEM((2, 2, nb, tt, kw), BF16), pltpu.VMEM((2, 2, nb, tt, kw), BF16),
                          pltpu.VMEM((sw // LANES, nb * U_PITCH, LANES), F32),
                          pltpu.VMEM((nj, rows, LANES), F32)]
                       + [pltpu.VMEM((rows, n_state // nj), F32)] * (2 * nj),
        compiler_params=pltpu.CompilerParams(dimension_semantics=("arbitrary",), vmem_limit_bytes=VMEM_LIMIT),
        name="mixer_prompt",
    )(sinks, x, w_in, wglu, *stacked)


def _mixer_sample_kernel(layer, sink_ref, x_ref, ckt_ref, cvt_ref, h0rt_ref, h0it_ref, *refs):
    if layer:
        prevk_ref, prevv_ref, *refs = refs
    (w_in_hbm, wglu_hbm, g1_ref, qg_ref, kg_ref, ar_ref, ai_ref, bx_ref, cm_ref, d_ref, bglu_ref,
     attn_ref, ssm_ref, kcat_ref, vcat_ref, hrt_ref, hit_ref,
     w_ref, wglu_ref, stage, sem, q_s, o_s, u_s, xr_s, xi_s, knew_s, vnew_s,
     h0r_ref, h0i_ref, hr_ref, hi_ref, *group_s) = refs
    i = pl.program_id(0)
    rows = x_ref.shape[0]
    nbatch = h0rt_ref.shape[1]
    ts = rows // nbatch
    gb_step, kw, past = ckt_ref.shape
    gb = SUBLANES
    n_tiles = rows // ROW_TILE
    n_state = ar_ref.shape[-1]
    nj = bx_ref.shape[0]
    gp = n_state // nj

    @pl.when(i == 0)
    def _():
        _stage_mixer_weights(layer, w_in_hbm, wglu_hbm, w_ref, wglu_ref, stage, sem)

        def project(r, carry):
            rs = pl.ds(r * ROW_TILE, ROW_TILE)
            q_lo, q_hi, k, v, u = _project_rows(x_ref[rs, :], g1_ref, w_ref, qg_ref, kg_ref)
            q_s[0, rs, :] = q_lo
            q_s[1, rs, :] = q_hi
            knew_s[rs, :] = k
            vnew_s[rs, :] = v
            u_s[rs, :] = u
            return carry

        lax.fori_loop(0, n_tiles, project, 0)

    n_q = ts * Q_GROUP * gb
    n_keys = gb * past + ts * gb
    r_idx = lax.broadcasted_iota(jnp.int32, (n_q, n_keys), 0)
    c_idx = lax.broadcasted_iota(jnp.int32, (n_q, n_keys), 1)
    q_t = r_idx // (Q_GROUP * gb)
    q_b = r_idx % gb
    is_cache = c_idx < gb * past
    key_b = jnp.where(is_cache, c_idx // past, (c_idx - gb * past) % gb)
    dist = jnp.where(is_cache, past + q_t - c_idx % past, q_t - (c_idx - gb * past) // gb)
    valid = (key_b == q_b) & (dist >= 0) & (dist <= WINDOW)
    dist_f = dist.astype(F32)
    hh = (lax.broadcasted_iota(jnp.int32, (n_q, 1), 0) // gb) % Q_GROUP
    lower = lax.broadcasted_iota(jnp.int32, (gb, LANES), 1) < HEAD_DIM
    def attend_group(sg):
        ck_ref, cv_ref, kroll_s, vroll_s = group_s[4 * sg:4 * sg + 4]
        sbase = pl.multiple_of(i * gb_step + sg * gb, gb)
        for cat_ref, ct_ref, c_ref, roll_s, new_s in ((kcat_ref, ckt_ref, ck_ref, kroll_s, knew_s),
                                                      (vcat_ref, cvt_ref, cv_ref, vroll_s, vnew_s)):
            for b in range(gb):
                c_ref[b * past:(b + 1) * past, :] = ct_ref[sg * gb + b].T
            for b in range(gb):
                roll_s[b * past:(b + 1) * past - ts, :] = c_ref[b * past + ts:(b + 1) * past, :]
            for t in range(ts):
                roll_s[pl.ds(past - ts + t, gb, stride=past), :] = new_s[pl.ds(sbase + t * nbatch, gb), :]
            for b in range(gb):
                cat_ref[layer, sg * gb + b] = roll_s[b * past:(b + 1) * past, :].T

        k_all = jnp.concatenate(
            [ck_ref[...]] + [knew_s[pl.ds(sbase + t * nbatch, gb), :] for t in range(ts)], axis=0).astype(BF16)
        v_all = jnp.concatenate(
            [cv_ref[...]] + [vnew_s[pl.ds(sbase + t * nbatch, gb), :] for t in range(ts)], axis=0).astype(BF16)
        outs = []
        for kvh in range(N_KV_HEADS):
            slope = jnp.zeros((n_q, 1), F32)
            sink = jnp.zeros((n_q, 1), F32)
            for g in range(Q_GROUP):
                h = kvh * Q_GROUP + g
                slope = jnp.where(hh == g, SLOPES[h] * LOG2E, slope)
                sink = jnp.where(hh == g, sink_ref[layer, h] * LOG2E, sink)
            pieces = []
            for t in range(ts):
                for g in range(Q_GROUP):
                    h = kvh * Q_GROUP + g
                    piece = q_s[h % 2, pl.ds(sbase + t * nbatch, gb), (h // 2) * LANES:(h // 2 + 1) * LANES]
                    pieces.append(piece if h % 2 == kvh else pltpu.roll(piece, HEAD_DIM, 1))
            qm = jnp.concatenate(pieces, axis=0).astype(BF16)
            s = lax.dot_general(qm, k_all, (((1,), (1,)), ((), ())), preferred_element_type=F32)
            s = jnp.where(valid, s - slope * dist_f, NEG_INF)
            outs.append(_softmax_pv(s, sink, v_all))
        for t in range(ts):
            for p in range(N_HEADS // 2):
                kvh = (2 * p) // Q_GROUP
                r0 = (t * Q_GROUP + 2 * p - kvh * Q_GROUP) * gb
                lo = outs[kvh][r0:r0 + gb]
                hi = outs[kvh][r0 + gb:r0 + 2 * gb]
                if kvh == 1:
                    lo = pltpu.roll(lo, HEAD_DIM, 1)
                else:
                    hi = pltpu.roll(hi, HEAD_DIM, 1)
                o_s[pl.ds(sbase + t * nbatch, gb), p * LANES:(p + 1) * LANES] = jnp.where(lower, lo, hi)

    for sg in range(gb_step // gb):
        attend_group(sg)
    if layer:
        kcat_ref[:layer] = prevk_ref[...]
        vcat_ref[:layer] = prevv_ref[...]

    @pl.when(i == pl.num_programs(0) - 1)
    def _():
        attn_ref[...] = o_s[...].astype(BF16)

        def ssm_in(r, carry):
            rs = pl.ds(r * ROW_TILE, ROW_TILE)
            ub = u_s[rs, :].astype(BF16)
            for j in range(nj):
                xj = jnp.dot(ub[:, j * LANES:(j + 1) * LANES], bx_ref[j], preferred_element_type=F32)
                xr_s[rs, j * gp:(j + 1) * gp] = xj[:, :gp]
                xi_s[rs, j * gp:(j + 1) * gp] = xj[:, gp:]
            return carry

        lax.fori_loop(0, n_tiles, ssm_in, 0)

        for blk in range(n_state // LANES):
            bs = slice(blk * LANES, (blk + 1) * LANES)
            h0r_ref[:, bs] = h0rt_ref[bs, :].T
            h0i_ref[:, bs] = h0it_ref[bs, :].T

        lane_blk = 4 * LANES
        for lb in range(n_state // lane_blk):
            sl = slice(lb * lane_blk, (lb + 1) * lane_blk)
            a_re = jnp.broadcast_to(ar_ref[:, sl], (SUBLANES, lane_blk))
            a_im = jnp.broadcast_to(ai_ref[:, sl], (SUBLANES, lane_blk))

            def scan(bt, carry):
                b0 = pl.multiple_of(bt * SUBLANES, SUBLANES)
                h_re = h0r_ref[pl.ds(b0, SUBLANES), sl]
                h_im = h0i_ref[pl.ds(b0, SUBLANES), sl]
                for t in range(ts):
                    rs = pl.ds(b0 + t * nbatch, SUBLANES)
                    n_re = a_re * h_re - a_im * h_im + xr_s[rs, sl]
                    n_im = a_re * h_im + a_im * h_re + xi_s[rs, sl]
                    xr_s[rs, sl] = n_re
                    xi_s[rs, sl] = n_im
                    h_re, h_im = n_re, n_im
                hr_ref[pl.ds(b0, SUBLANES), sl] = h_re
                hi_ref[pl.ds(b0, SUBLANES), sl] = h_im
                return carry

            lax.fori_loop(0, nbatch // SUBLANES, scan, 0)

        for blk in range(n_state // LANES):
            bs = slice(blk * LANES, (blk + 1) * LANES)
            hrt_ref[bs, :] = hr_ref[:, bs].T
            hit_ref[bs, :] = hi_ref[:, bs].T

        def ssm_out(r, carry):
            rs = pl.ds(r * ROW_TILE, ROW_TILE)
            ys = []
            for j in range(nj):
                h_cat = jnp.concatenate([xr_s[rs, j * gp:(j + 1) * gp], xi_s[rs, j * gp:(j + 1) * gp]],
                                        axis=1).astype(BF16)
                ys.append(jnp.dot(h_cat, cm_ref[j], preferred_element_type=F32))
            y = jnp.concatenate(ys, axis=1)
            ssm_ref[rs, :] = _ssm_readout(y, u_s[rs, :], d_ref, wglu_ref, bglu_ref).astype(BF16)
            return carry

        lax.fori_loop(0, n_tiles, ssm_out, 0)


def _mixer_sample(layer, x, ckt, cvt, h0rt, h0it, prev_k, prev_v, sinks, w_in, wglu,
                  g1, qg, kg, ar, ai, bx, cm, d, bglu):
    rows, D = x.shape
    _, nbatch, kw, past = ckt.shape
    aw = N_HEADS * HEAD_DIM
    sw = d.shape[-1]
    n_state = ar.shape[-1]
    gb = SUBLANES
    assert nbatch % LANES == 0 and past == LANES and kw == LANES
    cache = pl.BlockSpec((None, gb, kw, past), lambda i: (layer, i, 0, 0))
    h0 = pl.BlockSpec((None, n_state, nbatch), lambda i: (layer, 0, 0))
    prev = pl.BlockSpec((layer, gb, kw, past), lambda i: (0, i, 0, 0))
    out2 = lambda last: pl.BlockSpec((rows, last), lambda i: (0, 0))
    cat = pl.BlockSpec((layer + 1, gb, kw, past), lambda i: (0, i, 0, 0))
    state = pl.BlockSpec((n_state, nbatch), lambda i: (0, 0))
    hbm = pl.BlockSpec(memory_space=pl.ANY)
    stacked = (g1, qg, kg, ar, ai, bx, cm, d, bglu)
    prevs = (prev_k, prev_v) if layer else ()
    return pl.pallas_call(
        functools.partial(_mixer_sample_kernel, layer),
        grid=(nbatch // gb,),
        in_specs=[pl.BlockSpec(memory_space=pltpu.SMEM), pl.BlockSpec(x.shape, lambda i: (0, 0)), cache, cache, h0, h0]
                 + [prev] * len(prevs) + [hbm, hbm] + [_layer_spec(a, layer) for a in stacked],
        out_specs=[out2(aw), out2(sw), cat, cat, state, state],
        out_shape=[jax.ShapeDtypeStruct((rows, aw), BF16), jax.ShapeDtypeStruct((rows, sw), BF16),
                   jax.ShapeDtypeStruct((layer + 1, nbatch, kw, past), F32),
                   jax.ShapeDtypeStruct((layer + 1, nbatch, kw, past), F32),
                   jax.ShapeDtypeStruct((n_state, nbatch), F32), jax.ShapeDtypeStruct((n_state, nbatch), F32)],
        scratch_shapes=_mixer_weight_scratch(D, aw + 2 * kw + sw, sw)
                       + [pltpu.VMEM((2, rows, aw), F32), pltpu.VMEM((rows, aw), F32), pltpu.VMEM((rows, sw), F32),
                          pltpu.VMEM((rows, n_state), F32), pltpu.VMEM((rows, n_state), F32),
                          pltpu.VMEM((rows, kw), F32), pltpu.VMEM((rows, kw), F32)]
                       + [pltpu.VMEM((nbatch, n_state), F32)] * 4
                       + [pltpu.VMEM((SUBLANES * past, kw), F32)] * (4 * gb // SUBLANES),
        compiler_params=pltpu.CompilerParams(dimension_semantics=("arbitrary",), vmem_limit_bytes=VMEM_LIMIT),
        name="mixer_sample",
    )(sinks, x, ckt, cvt, h0rt, h0it, *prevs, w_in, wglu, *stacked)


def _mix_ffn_rows(x_ref, attn_ref, ssm_ref, o_ref, g1_ref, wg_ref, bg_ref, wao_ref, wso_ref, wout_ref,
                  g2_ref, wup_ref, wdn_ref):
    x = x_ref[...]
    D = x.shape[-1]
    xn = _rms(x, g1_ref[...]).astype(BF16)
    attn = attn_ref[...]
    ssm = ssm_ref[...]
    o_ref[...] = x
    cw = 2 * LANES
    for cc in range(D // cw):
        cs = slice(cc * cw, (cc + 1) * cw)
        gs = slice(D + cc * cw, D + (cc + 1) * cw)
        g_attn = jax.nn.sigmoid(jnp.dot(xn, wg_ref[:, cs], preferred_element_type=F32) + bg_ref[:, cs])
        g_ssm = jax.nn.sigmoid(jnp.dot(xn, wg_ref[:, gs], preferred_element_type=F32) + bg_ref[:, gs])
        mixed = (g_attn * jnp.dot(attn, wao_ref[:, cs], preferred_element_type=F32)
                 + g_ssm * jnp.dot(ssm, wso_ref[:, cs], preferred_element_type=F32))
        o_ref[...] += jnp.dot(mixed.astype(BF16), wout_ref[cs, :], preferred_element_type=F32)
    xn2 = _rms(o_ref[...], g2_ref[...]).astype(BF16)
    fw = 4 * LANES
    for fc in range(wup_ref.shape[-1] // fw):
        fs = slice(fc * fw, (fc + 1) * fw)
        hdn = jnp.maximum(jnp.dot(xn2, wup_ref[:, fs], preferred_element_type=F32), 0.0)
        o_ref[...] += jnp.dot((hdn * hdn).astype(BF16), wdn_ref[fs, :], preferred_element_type=F32)


def _mix_ffn_kernel(layer, n_prompt, xp_ref, ap_ref, sp_ref, xs_ref, as_ref, ss_ref,
                    w_in_hbm, wao_hbm, wso_hbm, wout_hbm, wup_hbm, wdn_hbm, g1_ref, bg_ref, g2_ref,
                    op_ref, os_ref, wg_ref, wao_ref, wso_ref, wout_ref, wup_ref, wdn_ref, stage, sem):
    i = pl.program_id(0)
    cr, cc = stage.shape[1:]

    @pl.when(i == 0)
    def _():
        chunks = []
        chunks += _weight_chunks(w_in_hbm.at[layer], wg_ref, wg_ref.shape[0], wg_ref.shape[1], cr, cc,
                                 src_col0=w_in_hbm.shape[-1] - wg_ref.shape[1])
        for hbm, ref in ((wao_hbm, wao_ref), (wso_hbm, wso_ref), (wout_hbm, wout_ref), (wup_hbm, wup_ref),
                         (wdn_hbm, wdn_ref)):
            chunks += _weight_chunks(hbm.at[layer], ref, ref.shape[0], ref.shape[1], cr, cc)
        _stage_weights(chunks, stage, sem)

    w_refs = (g1_ref, wg_ref, bg_ref, wao_ref, wso_ref, wout_ref, g2_ref, wup_ref, wdn_ref)

    @pl.when(i < n_prompt)
    def _():
        _mix_ffn_rows(xp_ref, ap_ref, sp_ref, op_ref, *w_refs)

    @pl.when(i >= n_prompt)
    def _():
        _mix_ffn_rows(xs_ref, as_ref, ss_ref, os_ref, *w_refs)


def _mix_ffn(layer, xp, attn_p, ssm_p, xs, attn_s, ssm_s, w_in, wao, wso, wout, wup, wdn, g1, bg, g2):
    rp, D = xp.shape
    rs = xs.shape[0]
    tp = min(512, rp)
    ts = min(512, rs)
    assert rp % tp == 0 and rs % ts == 0
    n_p, n_s = rp // tp, rs // ts
    prow = lambda last: pl.BlockSpec((tp, last), lambda i: (jnp.minimum(i, n_p - 1), 0))
    srow = lambda last: pl.BlockSpec((ts, last), lambda i: (jnp.maximum(i - n_p, 0), 0))
    hbm = pl.BlockSpec(memory_space=pl.ANY)
    resident = lambda a: pltpu.VMEM(a.shape[1:], BF16)
    return pl.pallas_call(
        functools.partial(_mix_ffn_kernel, layer, n_p),
        grid=(n_p + n_s,),
        in_specs=[prow(D), prow(attn_p.shape[-1]), prow(ssm_p.shape[-1]),
                  srow(D), srow(attn_s.shape[-1]), srow(ssm_s.shape[-1])]
                 + [hbm] * 6 + [_layer_spec(a, layer) for a in (g1, bg, g2)],
        out_specs=[prow(D), srow(D)],
        out_shape=[jax.ShapeDtypeStruct((rp, D), F32), jax.ShapeDtypeStruct((rs, D), F32)],
        scratch_shapes=[pltpu.VMEM((D, bg.shape[-1]), BF16), resident(wao), resident(wso), resident(wout),
                        resident(wup), resident(wdn),
                        pltpu.VMEM((WEIGHT_SLOTS, 2 * LANES, D), F32), pltpu.SemaphoreType.DMA((WEIGHT_SLOTS,))],
        compiler_params=pltpu.CompilerParams(dimension_semantics=("arbitrary",), vmem_limit_bytes=VMEM_LIMIT),
        name="mix_ffn",
    )(xp, attn_p, ssm_p, xs, attn_s, ssm_s, w_in, wao, wso, wout, wup, wdn, g1, bg, g2)


def kernel(x_prompt, x_sample, cache_k, cache_v, state_ssm_re, state_ssm_im, norm1_g, w_in, b_gate, q_norm_g,
           k_norm_g, attn_sinks, lam_re, lam_im, log_step, b_re, b_im, c_re, c_im, d_skip, w_glu, b_glu,
           w_attn_o, w_ssm_o, w_out, norm2_g, w_up, w_down):
    B, T, D = x_prompt.shape
    DB, DS, _ = x_sample.shape
    L = w_in.shape[0]
    aw = N_HEADS * HEAD_DIM
    kw = N_KV_HEADS * HEAD_DIM
    sw = d_skip.shape[-1]
    G, P = lam_re.shape[1:]
    past = cache_k.shape[2]
    assert past == WINDOW and T % WINDOW == 0 and (B * WINDOW) % ROW_TILE == 0 and (DB * DS) % ROW_TILE == 0

    ar, ai, bx, cm = _ssm_prep(lam_re, lam_im, log_step, b_re, b_im, c_re, c_im)
    rows = lambda a: a.reshape(L, 1, -1)
    g1 = rows(norm1_g)
    mixer_w = (w_in, w_glu, g1, rows(jnp.tile(q_norm_g, (1, N_HEADS))), rows(jnp.tile(k_norm_g, (1, N_KV_HEADS))),
               ar, ai, bx, cm, rows(d_skip), rows(b_glu))
    ffn_w = (w_in, w_attn_o, w_ssm_o, w_out, w_up, w_down, g1, rows(b_gate), rows(norm2_g))
    kv_t = lambda a: a.transpose(0, 1, 3, 4, 2).reshape(L, DB, kw, past)
    st_t = lambda a: a.transpose(0, 2, 3, 1).reshape(L, G * P, DB)
    ckt, cvt, h0rt, h0it = kv_t(cache_k), kv_t(cache_v), st_t(state_ssm_re), st_t(state_ssm_im)

    yp = x_prompt.reshape(B * T, D)
    ys = x_sample.transpose(1, 0, 2).reshape(DS * DB, D)
    outs = [[] for _ in range(6)]
    kcat = vcat = None
    for l in range(L):
        attn_p, ssm_p, kp, vp, hrp, hip = _mixer_prompt(l, yp.reshape(B, T, D), attn_sinks, *mixer_w)
        attn_s, ssm_s, kcat, vcat, hrs, his = _mixer_sample(l, ys, ckt, cvt, h0rt, h0it, kcat, vcat, attn_sinks,
                                                            *mixer_w)
        yp, ys = _mix_ffn(l, yp, attn_p.reshape(B * T, aw), ssm_p.reshape(B * T, sw), ys, attn_s, ssm_s, *ffn_w)
        for o, v in zip(outs, (kp, vp, hrp, hip, hrs, his)):
            o.append(v)

    kp, vp, hrp, hip, hrs, his = (jnp.stack(o) for o in outs)
    kv_back = lambda a: a.reshape(a.shape[:2] + (N_KV_HEADS, HEAD_DIM, a.shape[-1])).transpose(0, 1, 4, 2, 3)
    st_back = lambda a: a.reshape(L, G, P, DB).transpose(0, 3, 1, 2)
    y_sample = ys.reshape(DS, DB, D).transpose(1, 0, 2)
    return (yp.reshape(B, T, D), y_sample, kv_back(kp), kv_back(vp), hrp.reshape(L, B, G, P), hip.reshape(L, B, G, P),
            kv_back(kcat), kv_back(vcat), st_back(hrs), st_back(his))
```

```python
import functools
import math

import jax
import jax.numpy as jnp
from jax import lax
from jax.experimental import pallas as pl
from jax.experimental.pallas import tpu as pltpu

F32 = jnp.float32
BF16 = jnp.bfloat16

HEAD_DIM = 64
N_HEADS = 8
N_KV_HEADS = 2
Q_GROUP = N_HEADS // N_KV_HEADS
WINDOW = 128
GROUP_CH = 16
SSM_STATE = 64
EPS = 1e-6
NEG_INF = -1e30
LOG2E = math.log2(math.e)
LANES = 128
SUBLANES = 8
ROW_TILE = 256
U_PITCH = WINDOW + SUBLANES
VMEM_LIMIT = 60 * 1024 * 1024

WEIGHT_SLOTS = 4
SLOPES = tuple(2.0 ** (-8.0 * (h + 1) / N_HEADS) for h in range(N_HEADS))


def _rms(x, g):
    return x * lax.rsqrt(jnp.mean(x * x, axis=-1, keepdims=True) + EPS) * g


def _seg_rms(x, g):
    lower = lax.broadcasted_iota(jnp.int32, (x.shape[0], LANES), 1) < HEAD_DIM
    normed = []
    for p in range(x.shape[1] // LANES):
        xb = x[:, p * LANES:(p + 1) * LANES]
        sq = xb * xb
        lo = jnp.sum(jnp.where(lower, sq, 0.0), axis=-1, keepdims=True)
        hi = jnp.sum(jnp.where(lower, 0.0, sq), axis=-1, keepdims=True)
        ms = jnp.where(lower, lo, hi) * (1.0 / HEAD_DIM)
        normed.append(xb * lax.rsqrt(ms + EPS))
    return jnp.concatenate(normed, axis=1) * g


def _gelu_tanh(y):
    return 0.5 * y * (1.0 + jnp.tanh(math.sqrt(2.0 / math.pi) * (y + 0.044715 * (y * y * y))))


def _layer_spec(a, layer):
    n = a.ndim - 1
    return pl.BlockSpec((None,) + a.shape[1:], lambda *_: (layer,) + (0,) * n, pipeline_mode=pl.Buffered(1))


def _stage_weights(chunks, stage, sem):
    slots = stage.shape[0]

    def copy(k):
        src, _, cols = chunks[k]
        return pltpu.make_async_copy(src, stage.at[k % slots, :, pl.ds(0, cols)], sem.at[k % slots])

    for k in range(min(slots - 1, len(chunks))):
        copy(k).start()
    for k, (_, dst, cols) in enumerate(chunks):
        if k + slots - 1 < len(chunks):
            copy(k + slots - 1).start()
        copy(k).wait()
        dst[...] = stage[k % slots, :, :cols].astype(BF16)


def _weight_chunks(src, dst, rows, cols, chunk_rows, chunk_cols, src_col0=0):
    return [(src.at[pl.ds(r, chunk_rows), pl.ds(src_col0 + c, chunk_cols)],
             dst.at[pl.ds(r, chunk_rows), pl.ds(c, chunk_cols)], chunk_cols)
            for r in range(0, rows, chunk_rows) for c in range(0, cols, chunk_cols)]


def _ssm_prep_kernel(lr_ref, li_ref, ls_ref, btr_ref, bti_ref, ctr_ref, cti_ref,
                     ar_ref, ai_ref, bx_ref, cm_ref):
    lr = lr_ref[0]
    li = li_ref[0]
    step = jnp.exp(ls_ref[0])
    mag = jnp.exp(lr * step)
    ar = mag * jnp.cos(li * step)
    ai = mag * jnp.sin(li * step)
    den = lr * lr + li * li
    cr = ((ar - 1.0) * lr + ai * li) / den
    ci = (ai * lr - (ar - 1.0) * li) / den
    ar_ref[0] = ar
    ai_ref[0] = ai
    nj = btr_ref.shape[1]
    gp = SUBLANES * SSM_STATE
    rows_b = lax.broadcasted_iota(jnp.int32, (LANES, gp), 0) // GROUP_CH
    cols_b = lax.broadcasted_iota(jnp.int32, (LANES, gp), 1) // SSM_STATE
    mask_b = rows_b == cols_b
    rows_c = lax.broadcasted_iota(jnp.int32, (gp, LANES), 0) // SSM_STATE
    cols_c = lax.broadcasted_iota(jnp.int32, (gp, LANES), 1) // GROUP_CH
    mask_c = rows_c == cols_c
    for j in range(nj):
        crj = cr[:, j * gp:(j + 1) * gp]
        cij = ci[:, j * gp:(j + 1) * gp]
        b_re = btr_ref[0, j]
        b_im = bti_ref[0, j]
        bb_re = jnp.where(mask_b, crj * b_re - cij * b_im, 0.0)
        bb_im = jnp.where(mask_b, crj * b_im + cij * b_re, 0.0)
        bx_ref[0, j, :, :gp] = bb_re.astype(BF16)
        bx_ref[0, j, :, gp:] = bb_im.astype(BF16)
        cm_ref[0, j, :gp, :] = jnp.where(mask_c, ctr_ref[0, j], 0.0).astype(BF16)
        cm_ref[0, j, gp:, :] = jnp.where(mask_c, -cti_ref[0, j], 0.0).astype(BF16)


def _ssm_prep(lam_re, lam_im, log_step, b_re, b_im, c_re, c_im):
    L, G, P = lam_re.shape
    GC = b_re.shape[-1]
    gpb = LANES // GC
    nj = G // gpb
    gp = gpb * P
    flat = lambda a: a.reshape(L, 1, G * P)
    ls = jnp.repeat(log_step, P, axis=-1).reshape(L, 1, G * P)

    def b_tiles(b):
        t = b.reshape(L, nj, gpb, P, GC).transpose(0, 1, 4, 2, 3).reshape(L, nj, 1, GC, gp)
        return jnp.broadcast_to(t, (L, nj, gpb, GC, gp)).reshape(L, nj, LANES, gp)

    def c_tiles(c):
        t = c.reshape(L, nj, gpb, GC, P).transpose(0, 1, 2, 4, 3).reshape(L, nj, gp, 1, GC)
        return jnp.broadcast_to(t, (L, nj, gp, gpb, GC)).reshape(L, nj, gp, LANES)

    vec = pl.BlockSpec((1, 1, G * P), lambda l: (l, 0, 0))
    bt = pl.BlockSpec((1, nj, LANES, gp), lambda l: (l, 0, 0, 0))
    ct = pl.BlockSpec((1, nj, gp, LANES), lambda l: (l, 0, 0, 0))
    return pl.pallas_call(
        _ssm_prep_kernel,
        grid=(L,),
        in_specs=[vec, vec, vec, bt, bt, ct, ct],
        out_specs=[vec, vec,
                   pl.BlockSpec((1, nj, LANES, 2 * gp), lambda l: (l, 0, 0, 0)),
                   pl.BlockSpec((1, nj, 2 * gp, LANES), lambda l: (l, 0, 0, 0))],
        out_shape=[jax.ShapeDtypeStruct((L, 1, G * P), F32), jax.ShapeDtypeStruct((L, 1, G * P), F32),
                   jax.ShapeDtypeStruct((L, nj, LANES, 2 * gp), BF16),
                   jax.ShapeDtypeStruct((L, nj, 2 * gp, LANES), BF16)],
        name="ssm_prep",
    )(flat(lam_re), flat(lam_im), ls, b_tiles(b_re), b_tiles(b_im), c_tiles(c_re), c_tiles(c_im))


def _project_rows(x, g1_ref, w_ref, qg_ref, kg_ref):
    aw = N_HEADS * HEAD_DIM
    kw = N_KV_HEADS * HEAD_DIM
    xn = _rms(x, g1_ref[...]).astype(BF16)
    z = jnp.dot(xn, w_ref[...], preferred_element_type=F32)
    q = _seg_rms(z[:, :aw], qg_ref[...]) * (HEAD_DIM ** -0.5 * LOG2E)
    k = _seg_rms(z[:, aw:aw + kw], kg_ref[...])
    v = z[:, aw + kw:aw + 2 * kw]
    u = z[:, aw + 2 * kw:]
    lower = (lax.broadcasted_iota(jnp.int32, q.shape, 1) % LANES) < HEAD_DIM
    q_lo = jnp.where(lower, q, 0.0)
    q_hi = jnp.where(lower, 0.0, q)
    return q_lo, q_hi, k, v, u


def _stage_mixer_weights(layer, w_in_hbm, wglu_hbm, w_ref, wglu_ref, stage, sem):
    _stage_weights(
        _weight_chunks(w_in_hbm.at[layer], w_ref, w_ref.shape[0], w_ref.shape[1], stage.shape[1], w_ref.shape[1])
        + _weight_chunks(wglu_hbm.at[layer], wglu_ref, wglu_ref.shape[0], wglu_ref.shape[1], stage.shape[1],
                         wglu_ref.shape[1]),
        stage, sem)


def _mixer_weight_scratch(D, in_cols, sw):
    return [pltpu.VMEM((D, in_cols), BF16), pltpu.VMEM((sw, sw), BF16),
            pltpu.VMEM((WEIGHT_SLOTS, LANES, in_cols), F32), pltpu.SemaphoreType.DMA((WEIGHT_SLOTS,))]


def _ssm_readout(y, u, d_ref, wglu_ref, bglu_ref):
    y = _gelu_tanh(y + d_ref[...] * u)
    gate = jnp.dot(y.astype(BF16), wglu_ref[...], preferred_element_type=F32) + bglu_ref[...]
    return y * jax.nn.sigmoid(gate)


def _softmax_pv(s, sink, v):
    m = jnp.maximum(jnp.max(s, axis=-1, keepdims=True), sink)
    e = jnp.exp2(s - m)
    den = jnp.sum(e, axis=-1, keepdims=True) + jnp.exp2(sink - m)
    return jnp.dot(e.astype(BF16), v, preferred_element_type=F32) * (1.0 / den)


def _mixer_prompt_kernel(layer, sink_ref, x_ref, w_in_hbm, wglu_hbm, g1_ref, qg_ref, kg_ref,
                         ar_ref, ai_ref, bx_ref, cm_ref, d_ref, bglu_ref,
                         attn_ref, ssm_ref, kout_ref, vout_ref, hr_ref, hi_ref,
                         w_ref, wglu_ref, stage, sem, q_s, k_s, v_s, u_s, y_s, *x_s):
    c = pl.program_id(0)
    nb, tt, _ = x_ref.shape
    rows = nb * tt
    n_tiles = rows // ROW_TILE
    bpt = ROW_TILE // tt
    n_state = ar_ref.shape[-1]
    nj = bx_ref.shape[0]
    gp = n_state // nj
    xr_s, xi_s = x_s[:nj], x_s[nj:]
    kw = kout_ref.shape[1]

    @pl.when(c == 0)
    def _():
        _stage_mixer_weights(layer, w_in_hbm, wglu_hbm, w_ref, wglu_ref, stage, sem)
        k_s[:, :, :tt, :] = jnp.zeros((2, nb, tt, kw), BF16)
        v_s[:, :, :tt, :] = jnp.zeros((2, nb, tt, kw), BF16)
        hr_ref[...] = jnp.zeros(hr_ref.shape, F32)
        hi_ref[...] = jnp.zeros(hi_ref.shape, F32)

    def project(r, carry):
        x = x_ref[pl.ds(r * bpt, bpt)].reshape(ROW_TILE, x_ref.shape[-1])
        q_lo, q_hi, k, v, u = _project_rows(x, g1_ref, w_ref, qg_ref, kg_ref)
        q_s[0, pl.ds(r * bpt, bpt)] = q_lo.astype(BF16).reshape(bpt, tt, -1)
        q_s[1, pl.ds(r * bpt, bpt)] = q_hi.astype(BF16).reshape(bpt, tt, -1)
        k_s[0, pl.ds(r * bpt, bpt), tt:, :] = k.astype(BF16).reshape(bpt, tt, -1)
        v_s[0, pl.ds(r * bpt, bpt), tt:, :] = v.astype(BF16).reshape(bpt, tt, -1)
        k_s[1, pl.ds(r * bpt, bpt), tt:, :] = pltpu.roll(k, HEAD_DIM, 1).astype(BF16).reshape(bpt, tt, -1)
        v_s[1, pl.ds(r * bpt, bpt), tt:, :] = pltpu.roll(v, HEAD_DIM, 1).astype(BF16).reshape(bpt, tt, -1)
        for bb in range(bpt):
            kout_ref[r * bpt + bb] = k[bb * tt:(bb + 1) * tt, :].T
            vout_ref[r * bpt + bb] = v[bb * tt:(bb + 1) * tt, :].T
        for bb in range(bpt):
            row0 = pl.multiple_of((r * bpt + bb) * U_PITCH, SUBLANES)
            for j in range(nj):
                u_s[j, pl.ds(row0, tt), :] = u[bb * tt:(bb + 1) * tt, j * LANES:(j + 1) * LANES]
        return carry

    lax.fori_loop(0, n_tiles, project, 0, unroll=True)

    t_idx = lax.broadcasted_iota(jnp.int32, (tt, 2 * tt), 0)
    j_idx = lax.broadcasted_iota(jnp.int32, (tt, 2 * tt), 1)
    dist = t_idx - j_idx + tt
    valid = (dist >= 0) & (dist <= WINDOW) & (j_idx >= jnp.where(c > 0, 0, tt))
    key_pos = lax.broadcasted_iota(jnp.int32, (1, 2 * tt), 1).astype(F32)
    row_pos = (lax.broadcasted_iota(jnp.int32, (tt, 1), 0) + tt).astype(F32)
    lower = lax.broadcasted_iota(jnp.int32, (tt, LANES), 1) < HEAD_DIM

    def attend(b):
        for p in range(N_HEADS // 2):
            kv = (2 * p) // Q_GROUP
            outs = []
            for e in range(2):
                h = 2 * p + e
                swap = 0 if kv == e else 1
                qm = q_s[e, b, :, p * LANES:(p + 1) * LANES]
                s = lax.dot_general(qm, k_s[swap, b], (((1,), (1,)), ((), ())), preferred_element_type=F32)
                slope = SLOPES[h] * LOG2E
                s = jnp.where(valid, s + slope * key_pos, NEG_INF)
                sink = sink_ref[layer, h] * LOG2E + slope * row_pos
                outs.append(_softmax_pv(s, sink, v_s[swap, b]))
            attn_ref[b, :, p * LANES:(p + 1) * LANES] = jnp.where(lower, outs[0], outs[1]).astype(BF16)

    def ssm_in_and_attend(r, carry):
        rs = pl.ds(r * ROW_TILE, ROW_TILE)
        t0 = r * (ROW_TILE // nb)
        for j in range(nj):
            up = jnp.concatenate([u_s[j, pl.ds(t0 + step, nb, stride=U_PITCH), :]
                                  for step in range(ROW_TILE // nb)], axis=0).astype(BF16)
            xj = jnp.dot(up, bx_ref[j], preferred_element_type=F32)
            xr_s[j][rs, :] = xj[:, :gp]
            xi_s[j][rs, :] = xj[:, gp:]
        for bb in range(bpt):
            attend(r * bpt + bb)
        return carry

    lax.fori_loop(0, n_tiles, ssm_in_and_attend, 0, unroll=2)

    k_s[:, :, :tt, :] = k_s[:, :, tt:, :]
    v_s[:, :, :tt, :] = v_s[:, :, tt:, :]

    def ssm_out_tile(j, r):
        rs = pl.ds(r * ROW_TILE, ROW_TILE)
        h_cat =jnp.concatenate([xr_s[j][rs, :], xi_s[j][rs, :]], axis=1).astype(BF16)
        y_s[j, rs, :] = jnp.dot(h_cat, cm_ref[j], preferred_element_type=F32)

    for lb in range(nj):
        sl = slice(lb * gp, (lb + 1) * gp)
        a_re = jnp.broadcast_to(ar_ref[:, sl], (nb, gp))
        a_im = jnp.broadcast_to(ai_ref[:, sl], (nb, gp))

        def scan_tile(r, h, lb=lb, a_re=a_re, a_im=a_im):
            if lb > 0:
                ssm_out_tile(lb - 1, r)
            h_re, h_im = h
            for step in range(ROW_TILE // nb):
                row = pl.ds(pl.multiple_of(r * ROW_TILE + step * nb, nb), nb)
                n_re = a_re * h_re - a_im * h_im + xr_s[lb][row, :]
                n_im = a_re * h_im + a_im * h_re + xi_s[lb][row, :]
                xr_s[lb][row, :] = n_re
                xi_s[lb][row, :] = n_im
                h_re, h_im = n_re, n_im
            return h_re, h_im

        h_re, h_im = lax.fori_loop(0, n_tiles, scan_tile, (hr_ref[:, sl], hi_ref[:, sl]), unroll=True)
        hr_ref[:, sl] = h_re
        hi_ref[:, sl] = h_im

    def ssm_out_last(r, carry):
        ssm_out_tile(nj - 1, r)
        return carry

    lax.fori_loop(0, n_tiles, ssm_out_last, 0, unroll=True)

    def readout(b, carry):
        y = jnp.concatenate([y_s[j, pl.ds(b, tt, stride=nb), :] for j in range(nj)], axis=1)
        row0 = pl.multiple_of(b * U_PITCH, SUBLANES)
        u = jnp.concatenate([u_s[j, pl.ds(row0, tt), :] for j in range(nj)], axis=1)
        ssm_ref[b] = _ssm_readout(y, u, d_ref, wglu_ref, bglu_ref).astype(BF16)
        return carry

    lax.fori_loop(0, nb, readout, 0, unroll=4)


def _mixer_prompt(layer, x, sinks, w_in, wglu, g1, qg, kg, ar, ai, bx, cm, d, bglu):
    nb, T, D = x.shape
    tt = WINDOW
    rows = nb * tt
    aw = N_HEADS * HEAD_DIM
    kw = N_KV_HEADS * HEAD_DIM
    sw = d.shape[-1]
    n_state = ar.shape[-1]
    nj = bx.shape[1]
    chunk = lambda last: pl.BlockSpec((nb, tt, last), lambda c: (0, c, 0))
    whole = pl.BlockSpec((nb, kw, tt), lambda c: (0, 0, 0))
    state = pl.BlockSpec((nb, n_state), lambda c: (0, 0))
    hbm = pl.BlockSpec(memory_space=pl.ANY)
    stacked = (g1, qg, kg, ar, ai, bx, cm, d, bglu)
    return pl.pallas_call(
        functools.partial(_mixer_prompt_kernel, layer),
        grid=(T // tt,),
        in_specs=[pl.BlockSpec(memory_space=pltpu.SMEM), chunk(D), hbm, hbm]
                 + [_layer_spec(a, layer) for a in stacked],
        out_specs=[chunk(aw), chunk(sw), whole, whole, state, state],
        out_shape=[jax.ShapeDtypeStruct((nb, T, aw), BF16), jax.ShapeDtypeStruct((nb, T, sw), BF16),
                   jax.ShapeDtypeStruct((nb, kw, tt), F32), jax.ShapeDtypeStruct((nb, kw, tt), F32),
                   jax.ShapeDtypeStruct((nb, n_state), F32), jax.ShapeDtypeStruct((nb, n_state), F32)],
        scratch_shapes=_mixer_weight_scratch(D, aw + 2 * kw + sw, sw)
                       + [pltpu.VMEM((2, nb, tt, aw), BF16),
                          pltpu.VMEM((2, nb, 2 * tt, kw), BF16), pltpu.VMEM((2, nb, 2 * tt, kw), BF16),
                          pltpu.VMEM((sw // LANES, nb * U_PITCH, LANES), F32),
                          pltpu.VMEM((nj, rows, LANES), F32)]
                       + [pltpu.VMEM((rows, n_state // nj), F32)] * (2 * nj),
        compiler_params=pltpu.CompilerParams(dimension_semantics=("arbitrary",), vmem_limit_bytes=VMEM_LIMIT),
        name="mixer_prompt",
    )(sinks, x, w_in, wglu, *stacked)


def _mixer_sample_kernel(layer, sink_ref, x_ref, ckt_ref, cvt_ref, h0rt_ref, h0it_ref, *refs):
    if layer:
        prevk_ref, prevv_ref, *refs = refs
    (w_in_hbm, wglu_hbm, g1_ref, qg_ref, kg_ref, ar_ref, ai_ref, bx_ref, cm_ref, d_ref, bglu_ref,
     attn_ref, ssm_ref, kcat_ref, vcat_ref, hrt_ref, hit_ref,
     w_ref, wglu_ref, stage, sem, q_s, o_s, u_s, xr_s, xi_s, knew_s, vnew_s,
     h0r_ref, h0i_ref, hr_ref, hi_ref, *group_s) = refs
    i = pl.program_id(0)
    rows = x_ref.shape[0]
    nbatch = h0rt_ref.shape[1]
    ts = rows // nbatch
    gb_step, kw, past = ckt_ref.shape
    gb = SUBLANES
    n_tiles = rows // ROW_TILE
    n_state = ar_ref.shape[-1]
    nj = bx_ref.shape[0]
    gp = n_state // nj

    @pl.when(i == 0)
    def _():
        _stage_mixer_weights(layer, w_in_hbm, wglu_hbm, w_ref, wglu_ref, stage, sem)

        def project(r, carry):
            rs = pl.ds(r * ROW_TILE, ROW_TILE)
            q_lo, q_hi, k, v, u = _project_rows(x_ref[rs, :], g1_ref, w_ref, qg_ref, kg_ref)
            q_s[0, rs, :] = q_lo
            q_s[1, rs, :] = q_hi
            knew_s[rs, :] = k
            vnew_s[rs, :] = v
            u_s[rs, :] = u
            return carry

        lax.fori_loop(0, n_tiles, project, 0)

    n_q = ts * Q_GROUP * gb
    n_keys = gb * past + ts * gb
    r_idx = lax.broadcasted_iota(jnp.int32, (n_q, n_keys), 0)
    c_idx = lax.broadcasted_iota(jnp.int32, (n_q, n_keys), 1)
    q_t = r_idx // (Q_GROUP * gb)
    q_b = r_idx % gb
    is_cache = c_idx < gb * past
    key_b = jnp.where(is_cache, c_idx // past, (c_idx - gb * past) % gb)
    dist = jnp.where(is_cache, past + q_t - c_idx % past, q_t - (c_idx - gb * past) // gb)
    valid = (key_b == q_b) & (dist >= 0) & (dist <= WINDOW)
    dist_f = dist.astype(F32)
    hh = (lax.broadcasted_iota(jnp.int32, (n_q, 1), 0) // gb) % Q_GROUP
    lower = lax.broadcasted_iota(jnp.int32, (gb, LANES), 1) < HEAD_DIM
    def attend_group(sg):
        ck_ref, cv_ref, kroll_s, vroll_s = group_s[4 * sg:4 * sg + 4]
        sbase = pl.multiple_of(i * gb_step + sg * gb, gb)
        for cat_ref, ct_ref, c_ref, roll_s, new_s in ((kcat_ref, ckt_ref, ck_ref, kroll_s, knew_s),
                                                      (vcat_ref, cvt_ref, cv_ref, vroll_s, vnew_s)):
            for b in range(gb):
                c_ref[b * past:(b + 1) * past, :] = ct_ref[sg * gb + b].T
            for b in range(gb):
                roll_s[b * past:(b + 1) * past - ts, :] = c_ref[b * past + ts:(b + 1) * past, :]
            for t in range(ts):
                roll_s[pl.ds(past - ts + t, gb, stride=past), :] = new_s[pl.ds(sbase + t * nbatch, gb), :]
            for b in range(gb):
                cat_ref[layer, sg * gb + b] = roll_s[b * past:(b + 1) * past, :].T

        k_all = jnp.concatenate(
            [ck_ref[...]] + [knew_s[pl.ds(sbase + t * nbatch, gb), :] for t in range(ts)], axis=0).astype(BF16)
        v_all = jnp.concatenate(
            [cv_ref[...]] + [vnew_s[pl.ds(sbase + t * nbatch, gb), :] for t in range(ts)], axis=0).astype(BF16)
        outs = []
        for kvh in range(N_KV_HEADS):
            slope = jnp.zeros((n_q, 1), F32)
            sink = jnp.zeros((n_q, 1), F32)
            for g in range(Q_GROUP):
                h = kvh * Q_GROUP + g
                slope = jnp.where(hh == g, SLOPES[h] * LOG2E, slope)
                sink = jnp.where(hh == g, sink_ref[layer, h] * LOG2E, sink)
            pieces = []
            for t in range(ts):
                for g in range(Q_GROUP):
                    h = kvh * Q_GROUP + g
                    piece = q_s[h % 2, pl.ds(sbase + t * nbatch, gb), (h // 2) * LANES:(h // 2 + 1) * LANES]
                    pieces.append(piece if h % 2 == kvh else pltpu.roll(piece, HEAD_DIM, 1))
            qm = jnp.concatenate(pieces, axis=0).astype(BF16)
            s = lax.dot_general(qm, k_all, (((1,), (1,)), ((), ())), preferred_element_type=F32)
            s = jnp.where(valid, s - slope * dist_f, NEG_INF)
            outs.append(_softmax_pv(s, sink, v_all))
        for t in range(ts):
            for p in range(N_HEADS // 2):
                kvh = (2 * p) // Q_GROUP
                r0 = (t * Q_GROUP + 2 * p - kvh * Q_GROUP) * gb
                lo = outs[kvh][r0:r0 + gb]
                hi = outs[kvh][r0 + gb:r0 + 2 * gb]
                if kvh == 1:
                    lo = pltpu.roll(lo, HEAD_DIM, 1)
                else:
                    hi = pltpu.roll(hi, HEAD_DIM, 1)
                o_s[pl.ds(sbase + t * nbatch, gb), p * LANES:(p + 1) * LANES] = jnp.where(lower, lo, hi)

    for sg in range(gb_step // gb):
        attend_group(sg)
    if layer:
        kcat_ref[:layer] = prevk_ref[...]
        vcat_ref[:layer] = prevv_ref[...]

    @pl.when(i == pl.num_programs(0) - 1)
    def _():
        attn_ref[...] = o_s[...].astype(BF16)

        def ssm_in(r, carry):
            rs = pl.ds(r * ROW_TILE, ROW_TILE)
            ub = u_s[rs, :].astype(BF16)
            for j in range(nj):
                xj = jnp.dot(ub[:, j * LANES:(j + 1) * LANES], bx_ref[j], preferred_element_type=F32)
                xr_s[rs, j * gp:(j + 1) * gp] = xj[:, :gp]
                xi_s[rs, j * gp:(j + 1) * gp] = xj[:, gp:]
            return carry

        lax.fori_loop(0, n_tiles, ssm_in, 0)

        for blk in range(n_state // LANES):
            bs = slice(blk * LANES, (blk + 1) * LANES)
            h0r_ref[:, bs] = h0rt_ref[bs, :].T
            h0i_ref[:, bs] = h0it_ref[bs, :].T

        lane_blk = 4 * LANES
        for lb in range(n_state // lane_blk):
            sl = slice(lb * lane_blk, (lb + 1) * lane_blk)
            a_re = jnp.broadcast_to(ar_ref[:, sl], (SUBLANES, lane_blk))
            a_im = jnp.broadcast_to(ai_ref[:, sl], (SUBLANES, lane_blk))

            def scan(bt, carry):
                b0 = pl.multiple_of(bt * SUBLANES, SUBLANES)
                h_re = h0r_ref[pl.ds(b0, SUBLANES), sl]
                h_im = h0i_ref[pl.ds(b0, SUBLANES), sl]
                for t in range(ts):
                    rs = pl.ds(b0 + t * nbatch, SUBLANES)
                    n_re = a_re * h_re - a_im * h_im + xr_s[rs, sl]
                    n_im = a_re * h_im + a_im * h_re + xi_s[rs, sl]
                    xr_s[rs, sl] = n_re
                    xi_s[rs, sl] = n_im
                    h_re, h_im = n_re, n_im
                hr_ref[pl.ds(b0, SUBLANES), sl] = h_re
                hi_ref[pl.ds(b0, SUBLANES), sl] = h_im
                return carry

            lax.fori_loop(0, nbatch // SUBLANES, scan, 0)

        for blk in range(n_state // LANES):
            bs = slice(blk * LANES, (blk + 1) * LANES)
            hrt_ref[bs, :] = hr_ref[:, bs].T
            hit_ref[bs, :] = hi_ref[:, bs].T

        def ssm_out(r, carry):
            rs = pl.ds(r * ROW_TILE, ROW_TILE)
            ys = []
            for j in range(nj):
                h_cat = jnp.concatenate([xr_s[rs, j * gp:(j + 1) * gp], xi_s[rs, j * gp:(j + 1) * gp]],
                                        axis=1).astype(BF16)
                ys.append(jnp.dot(h_cat, cm_ref[j], preferred_element_type=F32))
            y = jnp.concatenate(ys, axis=1)
            ssm_ref[rs, :] = _ssm_readout(y, u_s[rs, :], d_ref, wglu_ref, bglu_ref).astype(BF16)
            return carry

        lax.fori_loop(0, n_tiles, ssm_out, 0)


def _mixer_sample(layer, x, ckt, cvt, h0rt, h0it, prev_k, prev_v, sinks, w_in, wglu,
                  g1, qg, kg, ar, ai, bx, cm, d, bglu):
    rows, D = x.shape
    _, nbatch, kw, past = ckt.shape
    aw = N_HEADS * HEAD_DIM
    sw = d.shape[-1]
    n_state = ar.shape[-1]
    gb = SUBLANES
    assert nbatch % LANES == 0 and past == LANES and kw == LANES
    cache = pl.BlockSpec((None, gb, kw, past), lambda i: (layer, i, 0, 0))
    h0 = pl.BlockSpec((None, n_state, nbatch), lambda i: (layer, 0, 0))
    prev = pl.BlockSpec((layer, gb, kw, past), lambda i: (0, i, 0, 0))
    out2 = lambda last: pl.BlockSpec((rows, last), lambda i: (0, 0))
    cat = pl.BlockSpec((layer + 1, gb, kw, past), lambda i: (0, i, 0, 0))
    state = pl.BlockSpec((n_state, nbatch), lambda i: (0, 0))
    hbm = pl.BlockSpec(memory_space=pl.ANY)
    stacked = (g1, qg, kg, ar, ai, bx, cm, d, bglu)
    prevs = (prev_k, prev_v) if layer else ()
    return pl.pallas_call(
        functools.partial(_mixer_sample_kernel, layer),
        grid=(nbatch // gb,),
        in_specs=[pl.BlockSpec(memory_space=pltpu.SMEM), pl.BlockSpec(x.shape, lambda i: (0, 0)), cache, cache, h0, h0]
                 + [prev] * len(prevs) + [hbm, hbm] + [_layer_spec(a, layer) for a in stacked],
        out_specs=[out2(aw), out2(sw), cat, cat, state, state],
        out_shape=[jax.ShapeDtypeStruct((rows, aw), BF16), jax.ShapeDtypeStruct((rows, sw), BF16),
                   jax.ShapeDtypeStruct((layer + 1, nbatch, kw, past), F32),
                   jax.ShapeDtypeStruct((layer + 1, nbatch, kw, past), F32),
                   jax.ShapeDtypeStruct((n_state, nbatch), F32), jax.ShapeDtypeStruct((n_state, nbatch), F32)],
        scratch_shapes=_mixer_weight_scratch(D, aw + 2 * kw + sw, sw)
                       + [pltpu.VMEM((2, rows, aw), F32), pltpu.VMEM((rows, aw), F32), pltpu.VMEM((rows, sw), F32),
                          pltpu.VMEM((rows, n_state), F32), pltpu.VMEM((rows, n_state), F32),
                          pltpu.VMEM((rows, kw), F32), pltpu.VMEM((rows, kw), F32)]
                       + [pltpu.VMEM((nbatch, n_state), F32)] * 4
                       + [pltpu.VMEM((SUBLANES * past, kw), F32)] * (4 * gb // SUBLANES),
        compiler_params=pltpu.CompilerParams(dimension_semantics=("arbitrary",), vmem_limit_bytes=VMEM_LIMIT),
        name="mixer_sample",
    )(sinks, x, ckt, cvt, h0rt, h0it, *prevs, w_in, wglu, *stacked)


def _mix_ffn_rows(x_ref, attn_ref, ssm_ref, o_ref, g1_ref, wg_ref, bg_ref, wao_ref, wso_ref, wout_ref,
                  g2_ref, wup_ref, wdn_ref):
    x = x_ref[...]
    D = x.shape[-1]
    r1 = lax.rsqrt(jnp.mean(x * x, axis=-1, keepdims=True) + EPS)
    xg = (x * g1_ref[...]).astype(BF16)
    attn = attn_ref[...]
    ssm = ssm_ref[...]
    o_ref[...] = x
    cw = 2 * LANES
    for cc in range(D // cw):
        cs = slice(cc * cw, (cc + 1) * cw)
        gs = slice(D + cc * cw, D + (cc + 1) * cw)
        g_attn = jax.nn.sigmoid(r1 * jnp.dot(xg, wg_ref[:, cs], preferred_element_type=F32) + bg_ref[:, cs])
        g_ssm = jax.nn.sigmoid(r1 * jnp.dot(xg, wg_ref[:, gs], preferred_element_type=F32) + bg_ref[:, gs])
        mixed = (g_attn * jnp.dot(attn, wao_ref[:, cs], preferred_element_type=F32)
                 + g_ssm * jnp.dot(ssm, wso_ref[:, cs], preferred_element_type=F32))
        o_ref[...] += jnp.dot(mixed.astype(BF16), wout_ref[cs, :], preferred_element_type=F32)
    x1 = o_ref[...]
    r2 = lax.rsqrt(jnp.mean(x1 * x1, axis=-1, keepdims=True) + EPS)
    xg2 = (x1 * g2_ref[...]).astype(BF16)
    r2_sq = r2 * r2
    fw = 4 * LANES
    for fc in range(wup_ref.shape[-1] // fw):
        fs = slice(fc * fw, (fc + 1) * fw)
        hdn = jnp.maximum(jnp.dot(xg2, wup_ref[:, fs], preferred_element_type=F32), 0.0)
        o_ref[...] += jnp.dot((r2_sq * (hdn * hdn)).astype(BF16), wdn_ref[fs, :], preferred_element_type=F32)


def _mix_ffn_kernel(layer, n_prompt, xp_ref, ap_ref, sp_ref, xs_ref, as_ref, ss_ref,
                    w_in_hbm, wao_hbm, wso_hbm, wout_hbm, wup_hbm, wdn_hbm, g1_ref, bg_ref, g2_ref,
                    op_ref, os_ref, wg_ref, wao_ref, wso_ref, wout_ref, wup_ref, wdn_ref, stage, sem):
    i = pl.program_id(0)
    cr, cc = stage.shape[1:]

    @pl.when(i == 0)
    def _():
        chunks = []
        chunks += _weight_chunks(w_in_hbm.at[layer], wg_ref, wg_ref.shape[0], wg_ref.shape[1], cr, cc,
                                 src_col0=w_in_hbm.shape[-1] - wg_ref.shape[1])
        for hbm, ref in ((wao_hbm, wao_ref), (wso_hbm, wso_ref), (wout_hbm, wout_ref), (wup_hbm, wup_ref),
                         (wdn_hbm, wdn_ref)):
            chunks += _weight_chunks(hbm.at[layer], ref, ref.shape[0], ref.shape[1], cr, cc)
        _stage_weights(chunks, stage, sem)

    w_refs = (g1_ref, wg_ref, bg_ref, wao_ref, wso_ref, wout_ref, g2_ref, wup_ref, wdn_ref)

    @pl.when(i < n_prompt)
    def _():
        _mix_ffn_rows(xp_ref, ap_ref, sp_ref, op_ref, *w_refs)

    @pl.when(i >= n_prompt)
    def _():
        _mix_ffn_rows(xs_ref, as_ref, ss_ref, os_ref, *w_refs)


def _mix_ffn(layer, xp, attn_p, ssm_p, xs, attn_s, ssm_s, w_in, wao, wso, wout, wup, wdn, g1, bg, g2):
    rp, D = xp.shape
    rs = xs.shape[0]
    tp = min(512, rp)
    ts = min(512, rs)
    assert rp % tp == 0 and rs % ts == 0
    n_p, n_s = rp // tp, rs // ts
    prow = lambda last: pl.BlockSpec((tp, last), lambda i: (jnp.minimum(i, n_p - 1), 0))
    srow = lambda last: pl.BlockSpec((ts, last), lambda i: (jnp.maximum(i - n_p, 0), 0))
    hbm = pl.BlockSpec(memory_space=pl.ANY)
    resident = lambda a: pltpu.VMEM(a.shape[1:], BF16)
    return pl.pallas_call(
        functools.partial(_mix_ffn_kernel, layer, n_p),
        grid=(n_p + n_s,),
        in_specs=[prow(D), prow(attn_p.shape[-1]), prow(ssm_p.shape[-1]),
                  srow(D), srow(attn_s.shape[-1]), srow(ssm_s.shape[-1])]
                 + [hbm] * 6 + [_layer_spec(a, layer) for a in (g1, bg, g2)],
        out_specs=[prow(D), srow(D)],
        out_shape=[jax.ShapeDtypeStruct((rp, D), F32), jax.ShapeDtypeStruct((rs, D), F32)],
        scratch_shapes=[pltpu.VMEM((D, bg.shape[-1]), BF16), resident(wao), resident(wso), resident(wout),
                        resident(wup), resident(wdn),
                        pltpu.VMEM((WEIGHT_SLOTS, 2 * LANES, D), F32), pltpu.SemaphoreType.DMA((WEIGHT_SLOTS,))],
        compiler_params=pltpu.CompilerParams(dimension_semantics=("arbitrary",), vmem_limit_bytes=VMEM_LIMIT),
        name="mix_ffn",
    )(xp, attn_p, ssm_p, xs, attn_s, ssm_s, w_in, wao, wso, wout, wup, wdn, g1, bg, g2)


def kernel(x_prompt, x_sample, cache_k, cache_v, state_ssm_re, state_ssm_im, norm1_g, w_in, b_gate, q_norm_g,
           k_norm_g, attn_sinks, lam_re, lam_im, log_step, b_re, b_im, c_re, c_im, d_skip, w_glu, b_glu,
           w_attn_o, w_ssm_o, w_out, norm2_g, w_up, w_down):
    B, T, D = x_prompt.shape
    DB, DS, _ = x_sample.shape
    L = w_in.shape[0]
    aw = N_HEADS * HEAD_DIM
    kw = N_KV_HEADS * HEAD_DIM
    sw = d_skip.shape[-1]
    G, P = lam_re.shape[1:]
    past = cache_k.shape[2]
    assert past == WINDOW and T % WINDOW == 0 and (B * WINDOW) % ROW_TILE == 0 and (DB * DS) % ROW_TILE == 0

    ar, ai, bx, cm = _ssm_prep(lam_re, lam_im, log_step, b_re, b_im, c_re, c_im)
    rows = lambda a: a.reshape(L, 1, -1)
    g1 = rows(norm1_g)
    mixer_w = (w_in, w_glu, g1, rows(jnp.tile(q_norm_g, (1, N_HEADS))), rows(jnp.tile(k_norm_g, (1, N_KV_HEADS))),
               ar, ai, bx, cm, rows(d_skip), rows(b_glu))
    ffn_w = (w_in, w_attn_o, w_ssm_o, w_out, w_up, w_down, g1, rows(b_gate), rows(norm2_g))
    kv_t = lambda a: a.transpose(0, 1, 3, 4, 2).reshape(L, DB, kw, past)
    st_t = lambda a: a.transpose(0, 2, 3, 1).reshape(L, G * P, DB)
    ckt, cvt, h0rt, h0it = kv_t(cache_k), kv_t(cache_v), st_t(state_ssm_re), st_t(state_ssm_im)

    yp = x_prompt.reshape(B * T, D)
    ys = x_sample.transpose(1, 0, 2).reshape(DS * DB, D)
    outs = [[] for _ in range(6)]
    kcat = vcat = None
    for l in range(L):
        attn_p, ssm_p, kp, vp, hrp, hip = _mixer_prompt(l, yp.reshape(B, T, D), attn_sinks, *mixer_w)
        attn_s, ssm_s, kcat, vcat, hrs, his = _mixer_sample(l, ys, ckt, cvt, h0rt, h0it, kcat, vcat, attn_sinks,
                                                            *mixer_w)
        yp, ys = _mix_ffn(l, yp, attn_p.reshape(B * T, aw), ssm_p.reshape(B * T, sw), ys, attn_s, ssm_s, *ffn_w)
        for o, v in zip(outs, (kp, vp, hrp, hip, hrs, his)):
            o.append(v)

    kp, vp, hrp, hip, hrs, his = (jnp.stack(o) for o in outs)
    kv_back = lambda a: a.reshape(a.shape[:2] + (N_KV_HEADS, HEAD_DIM, a.shape[-1])).transpose(0, 1, 4, 2, 3)
    st_back = lambda a: a.reshape(L, G, P, DB).transpose(0, 3, 1, 2)
    y_sample = ys.reshape(DS, DB, D).transpose(1, 0, 2)
    return (yp.reshape(B, T, D), y_sample, kv_back(kp), kv_back(vp), hrp.reshape(L, B, G, P), hip.reshape(L, B, G, P),
            kv_back(kcat), kv_back(vcat), st_back(hrs), st_back(his))
```

```python
import functools
import math

import jax
import jax.numpy as jnp
from jax import lax
from jax.experimental import pallas as pl
from jax.experimental.pallas import tpu as pltpu

F32 = jnp.float32
BF16 = jnp.bfloat16

HEAD_DIM = 64
N_HEADS = 8
N_KV_HEADS = 2
Q_GROUP = N_HEADS // N_KV_HEADS
WINDOW = 128
GROUP_CH = 16
SSM_STATE = 64
EPS = 1e-6
NEG_INF = -1e30
LOG2E = math.log2(math.e)
LANES = 128
SUBLANES = 8
ROW_TILE = 256
U_PITCH = WINDOW + SUBLANES
VMEM_LIMIT = 60 * 1024 * 1024

WEIGHT_SLOTS = 4
SLOPES = tuple(2.0 ** (-8.0 * (h + 1) / N_HEADS) for h in range(N_HEADS))


def _rms(x, g):
    return x * lax.rsqrt(jnp.mean(x * x, axis=-1, keepdims=True) + EPS) * g


def _seg_rms(x, g):
    lower = lax.broadcasted_iota(jnp.int32, (x.shape[0], LANES), 1) < HEAD_DIM
    normed = []
    for p in range(x.shape[1] // LANES):
        xb = x[:, p * LANES:(p + 1) * LANES]
        sq = xb * xb
        lo = jnp.sum(jnp.where(lower, sq, 0.0), axis=-1, keepdims=True)
        hi = jnp.sum(jnp.where(lower, 0.0, sq), axis=-1, keepdims=True)
        ms = jnp.where(lower, lo, hi) * (1.0 / HEAD_DIM)
        normed.append(xb * lax.rsqrt(ms + EPS))
    return jnp.concatenate(normed, axis=1) * g


def _gelu_tanh(y):
    return 0.5 * y * (1.0 + jnp.tanh(math.sqrt(2.0 / math.pi) * (y + 0.044715 * (y * y * y))))


def _layer_spec(a, layer):
    n = a.ndim - 1
    return pl.BlockSpec((None,) + a.shape[1:], lambda *_: (layer,) + (0,) * n, pipeline_mode=pl.Buffered(1))


def _stage_weights(chunks, stage, sem):
    slots = stage.shape[0]

    def copy(k):
        src, _, cols = chunks[k]
        return pltpu.make_async_copy(src, stage.at[k % slots, :, pl.ds(0, cols)], sem.at[k % slots])

    for k in range(min(slots - 1, len(chunks))):
        copy(k).start()
    for k, (_, dst, cols) in enumerate(chunks):
        if k + slots - 1 < len(chunks):
            copy(k + slots - 1).start()
        copy(k).wait()
        dst[...] = stage[k % slots, :, :cols].astype(BF16)


def _weight_chunks(src, dst, rows, cols, chunk_rows, chunk_cols, src_col0=0):
    return [(src.at[pl.ds(r, chunk_rows), pl.ds(src_col0 + c, chunk_cols)],
             dst.at[pl.ds(r, chunk_rows), pl.ds(c, chunk_cols)], chunk_cols)
            for r in range(0, rows, chunk_rows) for c in range(0, cols, chunk_cols)]


def _ssm_prep_kernel(lr_ref, li_ref, ls_ref, btr_ref, bti_ref, ctr_ref, cti_ref,
                     ar_ref, ai_ref, bx_ref, cm_ref):
    lr = lr_ref[0]
    li = li_ref[0]
    step = jnp.exp(ls_ref[0])
    mag = jnp.exp(lr * step)
    ar = mag * jnp.cos(li * step)
    ai = mag * jnp.sin(li * step)
    den = lr * lr + li * li
    cr = ((ar - 1.0) * lr + ai * li) / den
    ci = (ai * lr - (ar - 1.0) * li) / den
    ar_ref[0] = ar
    ai_ref[0] = ai
    nj = btr_ref.shape[1]
    gp = SUBLANES * SSM_STATE
    rows_b = lax.broadcasted_iota(jnp.int32, (LANES, gp), 0) // GROUP_CH
    cols_b = lax.broadcasted_iota(jnp.int32, (LANES, gp), 1) // SSM_STATE
    mask_b = rows_b == cols_b
    rows_c = lax.broadcasted_iota(jnp.int32, (gp, LANES), 0) // SSM_STATE
    cols_c = lax.broadcasted_iota(jnp.int32, (gp, LANES), 1) // GROUP_CH
    mask_c = rows_c == cols_c
    spread = (lax.broadcasted_iota(jnp.int32, (GROUP_CH, LANES), 1) % GROUP_CH
              == lax.broadcasted_iota(jnp.int32, (GROUP_CH, LANES), 0)).astype(BF16)
    groups = LANES // GROUP_CH
    for j in range(nj):
        crj = cr[:, j * gp:(j + 1) * gp]
        cij = ci[:, j * gp:(j + 1) * gp]
        b_re = jnp.concatenate([btr_ref[0, j]] * groups, axis=0)
        b_im = jnp.concatenate([bti_ref[0, j]] * groups, axis=0)
        bb_re = jnp.where(mask_b, crj * b_re - cij * b_im, 0.0)
        bb_im = jnp.where(mask_b, crj * b_im + cij * b_re, 0.0)
        bx_ref[0, j, :, :gp] = bb_re.astype(BF16)
        bx_ref[0, j, :, gp:] = bb_im.astype(BF16)
        c_re = jnp.dot(ctr_ref[0, j].astype(BF16), spread, preferred_element_type=F32)
        c_im = jnp.dot(cti_ref[0, j].astype(BF16), spread, preferred_element_type=F32)
        cm_ref[0, j, :gp, :] = jnp.where(mask_c, c_re, 0.0).astype(BF16)
        cm_ref[0, j, gp:, :] = jnp.where(mask_c, -c_im, 0.0).astype(BF16)


def _ssm_prep(lam_re, lam_im, log_step, b_re, b_im, c_re, c_im):
    L, G, P = lam_re.shape
    GC = b_re.shape[-1]
    gpb = LANES // GC
    nj = G // gpb
    gp = gpb * P
    flat = lambda a: a.reshape(L, 1, G * P)
    ls = jnp.repeat(log_step, P, axis=-1).reshape(L, 1, G * P)

    def b_tiles(b):
        return b.reshape(L, nj, gpb, P, GC).transpose(0, 1, 4, 2, 3).reshape(L, nj, GC, gp)

    def c_tiles(c):
        return c.reshape(L, nj, gpb, GC, P).transpose(0, 1, 2, 4, 3).reshape(L, nj, gp, GC)

    vec = pl.BlockSpec((1, 1, G * P), lambda l: (l, 0, 0))
    bt = pl.BlockSpec((1, nj, GC, gp), lambda l: (l, 0, 0, 0))
    ct = pl.BlockSpec((1, nj, gp, GC), lambda l: (l, 0, 0, 0))
    return pl.pallas_call(
        _ssm_prep_kernel,
        grid=(L,),
        in_specs=[vec, vec, vec, bt, bt, ct, ct],
        out_specs=[vec, vec,
                   pl.BlockSpec((1, nj, LANES, 2 * gp), lambda l: (l, 0, 0, 0)),
                   pl.BlockSpec((1, nj, 2 * gp, LANES), lambda l: (l, 0, 0, 0))],
        out_shape=[jax.ShapeDtypeStruct((L, 1, G * P), F32), jax.ShapeDtypeStruct((L, 1, G * P), F32),
                   jax.ShapeDtypeStruct((L, nj, LANES, 2 * gp), BF16),
                   jax.ShapeDtypeStruct((L, nj, 2 * gp, LANES), BF16)],
        name="ssm_prep",
    )(flat(lam_re), flat(lam_im), ls, b_tiles(b_re), b_tiles(b_im), c_tiles(c_re), c_tiles(c_im))


def _project_rows(x, g1_ref, w_ref, qg_ref, kg_ref):
    aw = N_HEADS * HEAD_DIM
    kw = N_KV_HEADS * HEAD_DIM
    xn = _rms(x, g1_ref[...]).astype(BF16)
    z = jnp.dot(xn, w_ref[...], preferred_element_type=F32)
    q = _seg_rms(z[:, :aw], qg_ref[...]) * (HEAD_DIM ** -0.5 * LOG2E)
    k = _seg_rms(z[:, aw:aw + kw], kg_ref[...])
    v = z[:, aw + kw:aw + 2 * kw]
    u = z[:, aw + 2 * kw:]
    lower = (lax.broadcasted_iota(jnp.int32, q.shape, 1) % LANES) < HEAD_DIM
    q_lo = jnp.where(lower, q, 0.0)
    q_hi = jnp.where(lower, 0.0, q)
    return q_lo, q_hi, k, v, u


def _stage_mixer_weights(layer, w_in_hbm, wglu_hbm, w_ref, wglu_ref, stage, sem):
    _stage_weights(
        _weight_chunks(w_in_hbm.at[layer], w_ref, w_ref.shape[0], w_ref.shape[1], stage.shape[1], w_ref.shape[1])
        + _weight_chunks(wglu_hbm.at[layer], wglu_ref, wglu_ref.shape[0], wglu_ref.shape[1], stage.shape[1],
                         wglu_ref.shape[1]),
        stage, sem)


def _mixer_weight_scratch(D, in_cols, sw):
    return [pltpu.VMEM((D, in_cols), BF16), pltpu.VMEM((sw, sw), BF16),
            pltpu.VMEM((WEIGHT_SLOTS, LANES, in_cols), F32), pltpu.SemaphoreType.DMA((WEIGHT_SLOTS,))]


def _ssm_readout(y, u, d_ref, wglu_ref, bglu_ref):
    y = _gelu_tanh(y + d_ref[...] * u)
    gate = jnp.dot(y.astype(BF16), wglu_ref[...], preferred_element_type=F32) + bglu_ref[...]
    return y * jax.nn.sigmoid(gate)


def _softmax_pv(s, sink, v):
    m = jnp.maximum(jnp.max(s, axis=-1, keepdims=True), sink)
    e = jnp.exp2(s - m)
    den = jnp.sum(e, axis=-1, keepdims=True) + jnp.exp2(sink - m)
    return jnp.dot(e.astype(BF16), v, preferred_element_type=F32) * (1.0 / den)


def _mixer_prompt_kernel(layer, sink_ref, x_ref, w_in_hbm, wglu_hbm, g1_ref, qg_ref, kg_ref,
                         ar_ref, ai_ref, bx_ref, cm_ref, d_ref, bglu_ref,
                         attn_ref, ssm_ref, kout_ref, vout_ref, hr_ref, hi_ref,
                         w_ref, wglu_ref, stage, sem, q_s, k_s, v_s, u_s, y_s, *x_s):
    c = pl.program_id(0)
    nb, tt, _ = x_ref.shape
    rows = nb * tt
    n_tiles = rows // ROW_TILE
    bpt = ROW_TILE // tt
    n_state = ar_ref.shape[-1]
    nj = bx_ref.shape[0]
    gp = n_state // nj
    xr_s, xi_s = x_s[:nj], x_s[nj:]
    kw = kout_ref.shape[1]
    cur = c % 2

    @pl.when(c == 0)
    def _():
        _stage_mixer_weights(layer, w_in_hbm, wglu_hbm, w_ref, wglu_ref, stage, sem)
        k_s[:, 1] = jnp.zeros((2, nb, tt, kw), BF16)
        v_s[:, 1] = jnp.zeros((2, nb, tt, kw), BF16)
        hr_ref[...] = jnp.zeros(hr_ref.shape, F32)
        hi_ref[...] = jnp.zeros(hi_ref.shape, F32)

    def project(r, carry):
        x = x_ref[pl.ds(r * bpt, bpt)].reshape(ROW_TILE, x_ref.shape[-1])
        q_lo, q_hi, k, v, u = _project_rows(x, g1_ref, w_ref, qg_ref, kg_ref)
        q_s[0, pl.ds(r * bpt, bpt)] = q_lo.astype(BF16).reshape(bpt, tt, -1)
        q_s[1, pl.ds(r * bpt, bpt)] = q_hi.astype(BF16).reshape(bpt, tt, -1)
        k_s[0, cur, pl.ds(r * bpt, bpt)] = k.astype(BF16).reshape(bpt, tt, -1)
        v_s[0, cur, pl.ds(r * bpt, bpt)] = v.astype(BF16).reshape(bpt, tt, -1)
        k_s[1, cur, pl.ds(r * bpt, bpt)] = pltpu.roll(k, HEAD_DIM, 1).astype(BF16).reshape(bpt, tt, -1)
        v_s[1, cur, pl.ds(r * bpt, bpt)] = pltpu.roll(v, HEAD_DIM, 1).astype(BF16).reshape(bpt, tt, -1)
        for bb in range(bpt):
            kout_ref[r * bpt + bb] = k[bb * tt:(bb + 1) * tt, :].T
            vout_ref[r * bpt + bb] = v[bb * tt:(bb + 1) * tt, :].T
        for bb in range(bpt):
            row0 = pl.multiple_of((r * bpt + bb) * U_PITCH, SUBLANES)
            for j in range(nj):
                u_s[j, pl.ds(row0, tt), :] = u[bb * tt:(bb + 1) * tt, j * LANES:(j + 1) * LANES]
        return carry

    lax.fori_loop(0, n_tiles, project, 0, unroll=True)

    t_idx = lax.broadcasted_iota(jnp.int32, (tt, 2 * tt), 0)
    j_idx = lax.broadcasted_iota(jnp.int32, (tt, 2 * tt), 1)
    dist = t_idx - j_idx + tt
    valid = (dist >= 0) & (dist <= WINDOW) & (j_idx >= jnp.where(c > 0, 0, tt))
    key_pos = lax.broadcasted_iota(jnp.int32, (1, 2 * tt), 1).astype(F32)
    row_pos = (lax.broadcasted_iota(jnp.int32, (tt, 1), 0) + tt).astype(F32)
    lower = lax.broadcasted_iota(jnp.int32, (tt, LANES), 1) < HEAD_DIM

    def attend(b):
        for p in range(N_HEADS // 2):
            kv = (2 * p) // Q_GROUP
            outs = []
            for e in range(2):
                h = 2 * p + e
                swap = 0 if kv == e else 1
                qm = q_s[e, b, :, p * LANES:(p + 1) * LANES]
                keys = jnp.concatenate([k_s[swap, 1 - cur, b], k_s[swap, cur, b]], axis=0)
                s = lax.dot_general(qm, keys, (((1,), (1,)), ((), ())), preferred_element_type=F32)
                slope = SLOPES[h] * LOG2E
                s = jnp.where(valid, s + slope * key_pos, NEG_INF)
                sink = sink_ref[layer, h] * LOG2E + slope * row_pos
                vals = jnp.concatenate([v_s[swap, 1 - cur, b], v_s[swap, cur, b]], axis=0)
                outs.append(_softmax_pv(s, sink, vals))
            attn_ref[b, :, p * LANES:(p + 1) * LANES] = jnp.where(lower, outs[0], outs[1]).astype(BF16)

    def ssm_in_and_attend(r, carry):
        rs = pl.ds(r * ROW_TILE, ROW_TILE)
        t0 = r * (ROW_TILE // nb)
        for j in range(nj):
            up = jnp.concatenate([u_s[j, pl.ds(t0 + step, nb, stride=U_PITCH), :]
                                  for step in range(ROW_TILE // nb)], axis=0).astype(BF16)
            xj = jnp.dot(up, bx_ref[j], preferred_element_type=F32)
            xr_s[j][rs, :] = xj[:, :gp]
            xi_s[j][rs, :] = xj[:, gp:]
        for bb in range(bpt):
            attend(r * bpt + bb)
        return carry

    lax.fori_loop(0, n_tiles, ssm_in_and_attend, 0, unroll=2)

    def ssm_out_tile(j, r):
        rs = pl.ds(r * ROW_TILE, ROW_TILE)
        h_cat = jnp.concatenate([xr_s[j][rs, :], xi_s[j][rs, :]], axis=1).astype(BF16)
        y_s[j, rs, :] = jnp.dot(h_cat, cm_ref[j], preferred_element_type=F32)

    for lb in range(nj):
        sl = slice(lb * gp, (lb + 1) * gp)
        a_re = jnp.broadcast_to(ar_ref[:, sl], (nb, gp))
        a_im = jnp.broadcast_to(ai_ref[:, sl], (nb, gp))

        def scan_tile(r, h, lb=lb, a_re=a_re, a_im=a_im):
            if lb > 0:
                ssm_out_tile(lb - 1, r)
            h_re, h_im = h
            for step in range(ROW_TILE // nb):
                row = pl.ds(pl.multiple_of(r * ROW_TILE + step * nb, nb), nb)
                n_re = a_re * h_re - a_im * h_im + xr_s[lb][row, :]
                n_im = a_re * h_im + a_im * h_re + xi_s[lb][row, :]
                xr_s[lb][row, :] = n_re
                xi_s[lb][row, :] = n_im
                h_re, h_im = n_re, n_im
            return h_re, h_im

        h_re, h_im = lax.fori_loop(0, n_tiles, scan_tile, (hr_ref[:, sl], hi_ref[:, sl]), unroll=True)
        hr_ref[:, sl] = h_re
        hi_ref[:, sl] = h_im

    def ssm_out_last(r, carry):
        ssm_out_tile(nj - 1, r)
        return carry

    lax.fori_loop(0, n_tiles, ssm_out_last, 0, unroll=True)

    def readout(b, carry):
        y = jnp.concatenate([y_s[j, pl.ds(b, tt, stride=nb), :] for j in range(nj)], axis=1)
        row0 = pl.multiple_of(b * U_PITCH, SUBLANES)
        u = jnp.concatenate([u_s[j, pl.ds(row0, tt), :] for j in range(nj)], axis=1)
        ssm_ref[b] = _ssm_readout(y, u, d_ref, wglu_ref, bglu_ref).astype(BF16)
        return carry

    lax.fori_loop(0, nb, readout, 0, unroll=4)


def _mixer_prompt(layer, x, sinks, w_in, wglu, g1, qg, kg, ar, ai, bx, cm, d, bglu):
    nb, T, D = x.shape
    tt = WINDOW
    rows = nb * tt
    aw = N_HEADS * HEAD_DIM
    kw = N_KV_HEADS * HEAD_DIM
    sw = d.shape[-1]
    n_state = ar.shape[-1]
    nj = bx.shape[1]
    chunk = lambda last: pl.BlockSpec((nb, tt, last), lambda c: (0, c, 0))
    whole = pl.BlockSpec((nb, kw, tt), lambda c: (0, 0, 0))
    state = pl.BlockSpec((nb, n_state), lambda c: (0, 0))
    hbm = pl.BlockSpec(memory_space=pl.ANY)
    stacked = (g1, qg, kg, ar, ai, bx, cm, d, bglu)
    return pl.pallas_call(
        functools.partial(_mixer_prompt_kernel, layer),
        grid=(T // tt,),
        in_specs=[pl.BlockSpec(memory_space=pltpu.SMEM), chunk(D), hbm, hbm]
                 + [_layer_spec(a, layer) for a in stacked],
        out_specs=[chunk(aw), chunk(sw), whole, whole, state, state],
        out_shape=[jax.ShapeDtypeStruct((nb, T, aw), BF16), jax.ShapeDtypeStruct((nb, T, sw), BF16),
                   jax.ShapeDtypeStruct((nb, kw, tt), F32), jax.ShapeDtypeStruct((nb, kw, tt), F32),
                   jax.ShapeDtypeStruct((nb, n_state), F32), jax.ShapeDtypeStruct((nb, n_state), F32)],
        scratch_shapes=_mixer_weight_scratch(D, aw + 2 * kw + sw, sw)
                       + [pltpu.VMEM((2, nb, tt, aw), BF16),
                          pltpu.VMEM((2, 2, nb, tt, kw), BF16), pltpu.VMEM((2, 2, nb, tt, kw), BF16),
                          pltpu.VMEM((sw // LANES, nb * U_PITCH, LANES), F32),
                          pltpu.VMEM((nj, rows, LANES), F32)]
                       + [pltpu.VMEM((rows, n_state // nj), F32)] * (2 * nj),
        compiler_params=pltpu.CompilerParams(dimension_semantics=("arbitrary",), vmem_limit_bytes=VMEM_LIMIT),
        name="mixer_prompt",
    )(sinks, x, w_in, wglu, *stacked)


def _mixer_sample_kernel(layer, sink_ref, x_ref, ckt_ref, cvt_ref, h0rt_ref, h0it_ref, *refs):
    if layer:
        prevk_ref, prevv_ref, *refs = refs
    (w_in_hbm, wglu_hbm, g1_ref, qg_ref, kg_ref, ar_ref, ai_ref, bx_ref, cm_ref, d_ref, bglu_ref,
     attn_ref, ssm_ref, kcat_ref, vcat_ref, hrt_ref, hit_ref,
     w_ref, wglu_ref, stage, sem, q_s, o_s, u_s, xr_s, xi_s, knew_s, vnew_s,
     h0r_ref, h0i_ref, hr_ref, hi_ref, *group_s) = refs
    i = pl.program_id(0)
    rows = x_ref.shape[0]
    nbatch = h0rt_ref.shape[1]
    ts = rows // nbatch
    gb_step, kw, past = ckt_ref.shape
    gb = SUBLANES
    n_tiles = rows // ROW_TILE
    n_state = ar_ref.shape[-1]
    nj = bx_ref.shape[0]
    gp = n_state // nj

    @pl.when(i == 0)
    def _():
        _stage_mixer_weights(layer, w_in_hbm, wglu_hbm, w_ref, wglu_ref, stage, sem)

        def project(r, carry):
            rs = pl.ds(r * ROW_TILE, ROW_TILE)
            q_lo, q_hi, k, v, u = _project_rows(x_ref[rs, :], g1_ref, w_ref, qg_ref, kg_ref)
            q_s[0, rs, :] = q_lo
            q_s[1, rs, :] = q_hi
            knew_s[rs, :] = k
            vnew_s[rs, :] = v
            u_s[rs, :] = u
            return carry

        lax.fori_loop(0, n_tiles, project, 0)

    n_q = ts * Q_GROUP * gb
    n_keys = gb * past + ts * gb
    r_idx = lax.broadcasted_iota(jnp.int32, (n_q, n_keys), 0)
    c_idx = lax.broadcasted_iota(jnp.int32, (n_q, n_keys), 1)
    q_t = r_idx // (Q_GROUP * gb)
    q_b = r_idx % gb
    is_cache = c_idx < gb * past
    key_b = jnp.where(is_cache, c_idx // past, (c_idx - gb * past) % gb)
    dist = jnp.where(is_cache, past + q_t - c_idx % past, q_t - (c_idx - gb * past) // gb)
    valid = (key_b == q_b) & (dist >= 0) & (dist <= WINDOW)
    dist_f = dist.astype(F32)
    hh = (lax.broadcasted_iota(jnp.int32, (n_q, 1), 0) // gb) % Q_GROUP
    lower = lax.broadcasted_iota(jnp.int32, (gb, LANES), 1) < HEAD_DIM
    def attend_group(sg):
        ck_ref, cv_ref, kroll_s, vroll_s = group_s[4 * sg:4 * sg + 4]
        sbase = pl.multiple_of(i * gb_step + sg * gb, gb)
        for cat_ref, ct_ref, c_ref, roll_s, new_s in ((kcat_ref, ckt_ref, ck_ref, kroll_s, knew_s),
                                                      (vcat_ref, cvt_ref, cv_ref, vroll_s, vnew_s)):
            for b in range(gb):
                c_ref[b * past:(b + 1) * past, :] = ct_ref[sg * gb + b].T
            for b in range(gb):
                roll_s[b * past:(b + 1) * past - ts, :] = c_ref[b * past + ts:(b + 1) * past, :]
            for t in range(ts):
                roll_s[pl.ds(past - ts + t, gb, stride=past), :] = new_s[pl.ds(sbase + t * nbatch, gb), :]
            for b in range(gb):
                cat_ref[layer, sg * gb + b] = roll_s[b * past:(b + 1) * past, :].T

        k_all = jnp.concatenate(
            [ck_ref[...]] + [knew_s[pl.ds(sbase + t * nbatch, gb), :] for t in range(ts)], axis=0).astype(BF16)
        v_all = jnp.concatenate(
            [cv_ref[...]] + [vnew_s[pl.ds(sbase + t * nbatch, gb), :] for t in range(ts)], axis=0).astype(BF16)
        outs = []
        for kvh in range(N_KV_HEADS):
            slope = jnp.zeros((n_q, 1), F32)
            sink = jnp.zeros((n_q, 1), F32)
            for g in range(Q_GROUP):
                h = kvh * Q_GROUP + g
                slope = jnp.where(hh == g, SLOPES[h] * LOG2E, slope)
                sink = jnp.where(hh == g, sink_ref[layer, h] * LOG2E, sink)
            pieces = []
            for t in range(ts):
                for g in range(Q_GROUP):
                    h = kvh * Q_GROUP + g
                    piece = q_s[h % 2, pl.ds(sbase + t * nbatch, gb), (h // 2) * LANES:(h // 2 + 1) * LANES]
                    pieces.append(piece if h % 2 == kvh else pltpu.roll(piece, HEAD_DIM, 1))
            qm = jnp.concatenate(pieces, axis=0).astype(BF16)
            s = lax.dot_general(qm, k_all, (((1,), (1,)), ((), ())), preferred_element_type=F32)
            s = jnp.where(valid, s - slope * dist_f, NEG_INF)
            outs.append(_softmax_pv(s, sink, v_all))
        for t in range(ts):
            for p in range(N_HEADS // 2):
                kvh = (2 * p) // Q_GROUP
                r0 = (t * Q_GROUP + 2 * p - kvh * Q_GROUP) * gb
                lo = outs[kvh][r0:r0 + gb]
                hi = outs[kvh][r0 + gb:r0 + 2 * gb]
                if kvh == 1:
                    lo = pltpu.roll(lo, HEAD_DIM, 1)
                else:
                    hi = pltpu.roll(hi, HEAD_DIM, 1)
                o_s[pl.ds(sbase + t * nbatch, gb), p * LANES:(p + 1) * LANES] = jnp.where(lower, lo, hi)

    for sg in range(gb_step // gb):
        attend_group(sg)
    if layer:
        kcat_ref[:layer] = prevk_ref[...]
        vcat_ref[:layer] = prevv_ref[...]

    @pl.when(i == pl.num_programs(0) - 1)
    def _():
        attn_ref[...] = o_s[...].astype(BF16)

        def ssm_in(r, carry):
            rs = pl.ds(r * ROW_TILE, ROW_TILE)
            ub = u_s[rs, :].astype(BF16)
            for j in range(nj):
                xj = jnp.dot(ub[:, j * LANES:(j + 1) * LANES], bx_ref[j], preferred_element_type=F32)
                xr_s[rs, j * gp:(j + 1) * gp] = xj[:, :gp]
                xi_s[rs, j * gp:(j + 1) * gp] = xj[:, gp:]
            return carry

        lax.fori_loop(0, n_tiles, ssm_in, 0)

        for blk in range(n_state // LANES):
            bs = slice(blk * LANES, (blk + 1) * LANES)
            h0r_ref[:, bs] = h0rt_ref[bs, :].T
            h0i_ref[:, bs] = h0it_ref[bs, :].T

        lane_blk = 4 * LANES
        for lb in range(n_state // lane_blk):
            sl = slice(lb * lane_blk, (lb + 1) * lane_blk)
            a_re = jnp.broadcast_to(ar_ref[:, sl], (SUBLANES, lane_blk))
            a_im = jnp.broadcast_to(ai_ref[:, sl], (SUBLANES, lane_blk))

            def scan(bt, carry):
                b0 = pl.multiple_of(bt * SUBLANES, SUBLANES)
                h_re = h0r_ref[pl.ds(b0, SUBLANES), sl]
                h_im = h0i_ref[pl.ds(b0, SUBLANES), sl]
                for t in range(ts):
                    rs = pl.ds(b0 + t * nbatch, SUBLANES)
                    n_re = a_re * h_re - a_im * h_im + xr_s[rs, sl]
                    n_im = a_re * h_im + a_im * h_re + xi_s[rs, sl]
                    xr_s[rs, sl] = n_re
                    xi_s[rs, sl] = n_im
                    h_re, h_im = n_re, n_im
                hr_ref[pl.ds(b0, SUBLANES), sl] = h_re
                hi_ref[pl.ds(b0, SUBLANES), sl] = h_im
                return carry

            lax.fori_loop(0, nbatch // SUBLANES, scan, 0)

        for blk in range(n_state // LANES):
            bs = slice(blk * LANES, (blk + 1) * LANES)
            hrt_ref[bs, :] = hr_ref[:, bs].T
            hit_ref[bs, :] = hi_ref[:, bs].T

        def ssm_out(r, carry):
            rs = pl.ds(r * ROW_TILE, ROW_TILE)
            ys = []
            for j in range(nj):
                h_cat = jnp.concatenate([xr_s[rs, j * gp:(j + 1) * gp], xi_s[rs, j * gp:(j + 1) * gp]],
                                        axis=1).astype(BF16)
                ys.append(jnp.dot(h_cat, cm_ref[j], preferred_element_type=F32))
            y = jnp.concatenate(ys, axis=1)
            ssm_ref[rs, :] = _ssm_readout(y, u_s[rs, :], d_ref, wglu_ref, bglu_ref).astype(BF16)
            return carry

        lax.fori_loop(0, n_tiles, ssm_out, 0)


def _mixer_sample(layer, x, ckt, cvt, h0rt, h0it, prev_k, prev_v, sinks, w_in, wglu,
                  g1, qg, kg, ar, ai, bx, cm, d, bglu):
    rows, D = x.shape
    _, nbatch, kw, past = ckt.shape
    aw = N_HEADS * HEAD_DIM
    sw = d.shape[-1]
    n_state = ar.shape[-1]
    gb = SUBLANES
    assert nbatch % LANES == 0 and past == LANES and kw == LANES
    cache = pl.BlockSpec((None, gb, kw, past), lambda i: (layer, i, 0, 0))
    h0 = pl.BlockSpec((None, n_state, nbatch), lambda i: (layer, 0, 0))
    prev = pl.BlockSpec((layer, gb, kw, past), lambda i: (0, i, 0, 0))
    out2 = lambda last: pl.BlockSpec((rows, last), lambda i: (0, 0))
    cat = pl.BlockSpec((layer + 1, gb, kw, past), lambda i: (0, i, 0, 0))
    state = pl.BlockSpec((n_state, nbatch), lambda i: (0, 0))
    hbm = pl.BlockSpec(memory_space=pl.ANY)
    stacked = (g1, qg, kg, ar, ai, bx, cm, d, bglu)
    prevs = (prev_k, prev_v) if layer else ()
    return pl.pallas_call(
        functools.partial(_mixer_sample_kernel, layer),
        grid=(nbatch // gb,),
        in_specs=[pl.BlockSpec(memory_space=pltpu.SMEM), pl.BlockSpec(x.shape, lambda i: (0, 0)), cache, cache, h0, h0]
                 + [prev] * len(prevs) + [hbm, hbm] + [_layer_spec(a, layer) for a in stacked],
        out_specs=[out2(aw), out2(sw), cat, cat, state, state],
        out_shape=[jax.ShapeDtypeStruct((rows, aw), BF16), jax.ShapeDtypeStruct((rows, sw), BF16),
                   jax.ShapeDtypeStruct((layer + 1, nbatch, kw, past), F32),
                   jax.ShapeDtypeStruct((layer + 1, nbatch, kw, past), F32),
                   jax.ShapeDtypeStruct((n_state, nbatch), F32), jax.ShapeDtypeStruct((n_state, nbatch), F32)],
        scratch_shapes=_mixer_weight_scratch(D, aw + 2 * kw + sw, sw)
                       + [pltpu.VMEM((2, rows, aw), F32), pltpu.VMEM((rows, aw), F32), pltpu.VMEM((rows, sw), F32),
                          pltpu.VMEM((rows, n_state), F32), pltpu.VMEM((rows, n_state), F32),
                          pltpu.VMEM((rows, kw), F32), pltpu.VMEM((rows, kw), F32)]
                       + [pltpu.VMEM((nbatch, n_state), F32)] * 4
                       + [pltpu.VMEM((SUBLANES * past, kw), F32)] * (4 * gb // SUBLANES),
        compiler_params=pltpu.CompilerParams(dimension_semantics=("arbitrary",), vmem_limit_bytes=VMEM_LIMIT),
        name="mixer_sample",
    )(sinks, x, ckt, cvt, h0rt, h0it, *prevs, w_in, wglu, *stacked)


def _mix_ffn_rows(x_ref, attn_ref, ssm_ref, o_ref, g1_ref, wg_ref, bg_ref, wao_ref, wso_ref, wout_ref,
                  g2_ref, wup_ref, wdn_ref):
    x = x_ref[...]
    D = x.shape[-1]
    r1 = lax.rsqrt(jnp.mean(x * x, axis=-1, keepdims=True) + EPS)
    xg = (x * g1_ref[...]).astype(BF16)
    attn = attn_ref[...]
    ssm = ssm_ref[...]
    o_ref[...] = x
    cw = 2 * LANES
    for cc in range(D // cw):
        cs = slice(cc * cw, (cc + 1) * cw)
        gs = slice(D + cc * cw, D + (cc + 1) * cw)
        g_attn = jax.nn.sigmoid(r1 * jnp.dot(xg, wg_ref[:, cs], preferred_element_type=F32) + bg_ref[:, cs])
        g_ssm = jax.nn.sigmoid(r1 * jnp.dot(xg, wg_ref[:, gs], preferred_element_type=F32) + bg_ref[:, gs])
        mixed = (g_attn * jnp.dot(attn, wao_ref[:, cs], preferred_element_type=F32)
                 + g_ssm * jnp.dot(ssm, wso_ref[:, cs], preferred_element_type=F32))
        o_ref[...] += jnp.dot(mixed.astype(BF16), wout_ref[cs, :], preferred_element_type=F32)
    x1 = o_ref[...]
    r2 = lax.rsqrt(jnp.mean(x1 * x1, axis=-1, keepdims=True) + EPS)
    xg2 = (x1 * g2_ref[...]).astype(BF16)
    r2_sq = r2 * r2
    fw = 4 * LANES
    for fc in range(wup_ref.shape[-1] // fw):
        fs = slice(fc * fw, (fc + 1) * fw)
        hdn = jnp.maximum(jnp.dot(xg2, wup_ref[:, fs], preferred_element_type=F32), 0.0)
        o_ref[...] += jnp.dot((r2_sq * (hdn * hdn)).astype(BF16), wdn_ref[fs, :], preferred_element_type=F32)


def _mix_ffn_kernel(layer, n_prompt, xp_ref, ap_ref, sp_ref, xs_ref, as_ref, ss_ref,
                    w_in_hbm, wao_hbm, wso_hbm, wout_hbm, wup_hbm, wdn_hbm, g1_ref, bg_ref, g2_ref,
                    op_ref, os_ref, wg_ref, wao_ref, wso_ref, wout_ref, wup_ref, wdn_ref, stage, sem):
    i = pl.program_id(0)
    cr, cc = stage.shape[1:]

    @pl.when(i == 0)
    def _():
        chunks = []
        chunks += _weight_chunks(w_in_hbm.at[layer], wg_ref, wg_ref.shape[0], wg_ref.shape[1], cr, cc,
                                 src_col0=w_in_hbm.shape[-1] - wg_ref.shape[1])
        for hbm, ref in ((wao_hbm, wao_ref), (wso_hbm, wso_ref), (wout_hbm, wout_ref), (wup_hbm, wup_ref),
                         (wdn_hbm, wdn_ref)):
            chunks += _weight_chunks(hbm.at[layer], ref, ref.shape[0], ref.shape[1], cr, cc)
        _stage_weights(chunks, stage, sem)

    w_refs = (g1_ref, wg_ref, bg_ref, wao_ref, wso_ref, wout_ref, g2_ref, wup_ref, wdn_ref)

    @pl.when(i < n_prompt)
    def _():
        _mix_ffn_rows(xp_ref, ap_ref, sp_ref, op_ref, *w_refs)

    @pl.when(i >= n_prompt)
    def _():
        _mix_ffn_rows(xs_ref, as_ref, ss_ref, os_ref, *w_refs)


def _mix_ffn(layer, xp, attn_p, ssm_p, xs, attn_s, ssm_s, w_in, wao, wso, wout, wup, wdn, g1, bg, g2):
    rp, D = xp.shape
    rs = xs.shape[0]
    tp = min(512, rp)
    ts = min(512, rs)
    assert rp % tp == 0 and rs % ts == 0
    n_p, n_s = rp // tp, rs // ts
    prow = lambda last: pl.BlockSpec((tp, last), lambda i: (jnp.minimum(i, n_p - 1), 0))
    srow = lambda last: pl.BlockSpec((ts, last), lambda i: (jnp.maximum(i - n_p, 0), 0))
    hbm = pl.BlockSpec(memory_space=pl.ANY)
    resident = lambda a: pltpu.VMEM(a.shape[1:], BF16)
    return pl.pallas_call(
        functools.partial(_mix_ffn_kernel, layer, n_p),
        grid=(n_p + n_s,),
        in_specs=[prow(D), prow(attn_p.shape[-1]), prow(ssm_p.shape[-1]),
                  srow(D), srow(attn_s.shape[-1]), srow(ssm_s.shape[-1])]
                 + [hbm] * 6 + [_layer_spec(a, layer) for a in (g1, bg, g2)],
        out_specs=[prow(D), srow(D)],
        out_shape=[jax.ShapeDtypeStruct((rp, D), F32), jax.ShapeDtypeStruct((rs, D), F32)],
        scratch_shapes=[pltpu.VMEM((D, bg.shape[-1]), BF16), resident(wao), resident(wso), resident(wout),
                        resident(wup), resident(wdn),
                        pltpu.VMEM((WEIGHT_SLOTS, 2 * LANES, D), F32), pltpu.SemaphoreType.DMA((WEIGHT_SLOTS,))],
        compiler_params=pltpu.CompilerParams(dimension_semantics=("arbitrary",), vmem_limit_bytes=VMEM_LIMIT),
        name="mix_ffn",
    )(xp, attn_p, ssm_p, xs, attn_s, ssm_s, w_in, wao, wso, wout, wup, wdn, g1, bg, g2)


def kernel(x_prompt, x_sample, cache_k, cache_v, state_ssm_re, state_ssm_im, norm1_g, w_in, b_gate, q_norm_g,
           k_norm_g, attn_sinks, lam_re, lam_im, log_step, b_re, b_im, c_re, c_im, d_skip, w_glu, b_glu,
           w_attn_o, w_ssm_o, w_out, norm2_g, w_up, w_down):
    B, T, D = x_prompt.shape
    DB, DS, _ = x_sample.shape
    L = w_in.shape[0]
    aw = N_HEADS * HEAD_DIM
    kw = N_KV_HEADS * HEAD_DIM
    sw = d_skip.shape[-1]
    G, P = lam_re.shape[1:]
    past = cache_k.shape[2]
    assert past == WINDOW and T % WINDOW == 0 and (B * WINDOW) % ROW_TILE == 0 and (DB * DS) % ROW_TILE == 0

    ar, ai, bx, cm = _ssm_prep(lam_re, lam_im, log_step, b_re, b_im, c_re, c_im)
    rows = lambda a: a.reshape(L, 1, -1)
    g1 = rows(norm1_g)
    mixer_w = (w_in, w_glu, g1, rows(jnp.tile(q_norm_g, (1, N_HEADS))), rows(jnp.tile(k_norm_g, (1, N_KV_HEADS))),
               ar, ai, bx, cm, rows(d_skip), rows(b_glu))
    ffn_w = (w_in, w_attn_o, w_ssm_o, w_out, w_up, w_down, g1, rows(b_gate), rows(norm2_g))
    kv_t = lambda a: a.transpose(0, 1, 3, 4, 2).reshape(L, DB, kw, past)
    st_t = lambda a: a.transpose(0, 2, 3, 1).reshape(L, G * P, DB)
    ckt, cvt, h0rt, h0it = kv_t(cache_k), kv_t(cache_v), st_t(state_ssm_re), st_t(state_ssm_im)

    yp = x_prompt.reshape(B * T, D)
    ys = x_sample.transpose(1, 0, 2).reshape(DS * DB, D)
    outs = [[] for _ in range(6)]
    kcat = vcat = None
    for l in range(L):
        attn_p, ssm_p, kp, vp, hrp, hip = _mixer_prompt(l, yp.reshape(B, T, D), attn_sinks, *mixer_w)
        attn_s, ssm_s, kcat, vcat, hrs, his = _mixer_sample(l, ys, ckt, cvt, h0rt, h0it, kcat, vcat, attn_sinks,
                                                            *mixer_w)
        yp, ys = _mix_ffn(l, yp, attn_p.reshape(B * T, aw), ssm_p.reshape(B * T, sw), ys, attn_s, ssm_s, *ffn_w)
        for o, v in zip(outs, (kp, vp, hrp, hip, hrs, his)):
            o.append(v)

    kp, vp, hrp, hip, hrs, his = (jnp.stack(o) for o in outs)
    kv_back = lambda a: a.reshape(a.shape[:2] + (N_KV_HEADS, HEAD_DIM, a.shape[-1])).transpose(0, 1, 4, 2, 3)
    st_back = lambda a: a.reshape(L, G, P, DB).transpose(0, 3, 1, 2)
    y_sample = ys.reshape(DS, DB, D).transpose(1, 0, 2)
    return (yp.reshape(B, T, D), y_sample, kv_back(kp), kv_back(vp), hrp.reshape(L, B, G, P), hip.reshape(L, B, G, P),
            kv_back(kcat), kv_back(vcat), st_back(hrs), st_back(his))
```

```python
import functools
import math

import jax
import jax.numpy as jnp
from jax import lax
from jax.experimental import pallas as pl
from jax.experimental.pallas import tpu as pltpu

F32 = jnp.float32
BF16 = jnp.bfloat16

HEAD_DIM = 64
N_HEADS = 8
N_KV_HEADS = 2
Q_GROUP = N_HEADS // N_KV_HEADS
WINDOW = 128
GROUP_CH = 16
SSM_STATE = 64
EPS = 1e-6
NEG_INF = -1e30
LOG2E = math.log2(math.e)
LANES = 128
SUBLANES = 8
ROW_TILE = 256
U_PITCH = WINDOW + SUBLANES
VMEM_LIMIT = 60 * 1024 * 1024

WEIGHT_SLOTS = 4
SLOPES = tuple(2.0 ** (-8.0 * (h + 1) / N_HEADS) for h in range(N_HEADS))


def _rms(x, g):
    return x * lax.rsqrt(jnp.mean(x * x, axis=-1, keepdims=True) + EPS) * g


def _seg_rms(x, g):
    lower = lax.broadcasted_iota(jnp.int32, (x.shape[0], LANES), 1) < HEAD_DIM
    normed = []
    for p in range(x.shape[1] // LANES):
        xb = x[:, p * LANES:(p + 1) * LANES]
        sq = xb * xb
        lo = jnp.sum(jnp.where(lower, sq, 0.0), axis=-1, keepdims=True)
        hi = jnp.sum(jnp.where(lower, 0.0, sq), axis=-1, keepdims=True)
        ms = jnp.where(lower, lo, hi) * (1.0 / HEAD_DIM)
        normed.append(xb * lax.rsqrt(ms + EPS))
    return jnp.concatenate(normed, axis=1) * g


def _gelu_tanh(y):
    return 0.5 * y * (1.0 + jnp.tanh(math.sqrt(2.0 / math.pi) * (y + 0.044715 * (y * y * y))))


def _layer_spec(a, layer):
    n = a.ndim - 1
    return pl.BlockSpec((None,) + a.shape[1:], lambda *_: (layer,) + (0,) * n, pipeline_mode=pl.Buffered(1))


def _stage_weights(chunks, stage, sem):
    slots = stage.shape[0]

    def copy(k):
        src, _, cols = chunks[k]
        return pltpu.make_async_copy(src, stage.at[k % slots, :, pl.ds(0, cols)], sem.at[k % slots])

    for k in range(min(slots - 1, len(chunks))):
        copy(k).start()
    for k, (_, dst, cols) in enumerate(chunks):
        if k + slots - 1 < len(chunks):
            copy(k + slots - 1).start()
        copy(k).wait()
        dst[...] = stage[k % slots, :, :cols].astype(BF16)


def _weight_chunks(src, dst, rows, cols, chunk_rows, chunk_cols, src_col0=0):
    return [(src.at[pl.ds(r, chunk_rows), pl.ds(src_col0 + c, chunk_cols)],
             dst.at[pl.ds(r, chunk_rows), pl.ds(c, chunk_cols)], chunk_cols)
            for r in range(0, rows, chunk_rows) for c in range(0, cols, chunk_cols)]


def _ssm_prep_kernel(lr_ref, li_ref, ls_ref, btr_ref, bti_ref, ctr_ref, cti_ref,
                     ar_ref, ai_ref, bx_ref, cm_ref):
    lr = lr_ref[0]
    li = li_ref[0]
    step = jnp.exp(ls_ref[0])
    mag = jnp.exp(lr * step)
    ar = mag * jnp.cos(li * step)
    ai = mag * jnp.sin(li * step)
    den = lr * lr + li * li
    cr = ((ar - 1.0) * lr + ai * li) / den
    ci = (ai * lr - (ar - 1.0) * li) / den
    ar_ref[0] = ar
    ai_ref[0] = ai
    nj = btr_ref.shape[1]
    gp = SUBLANES * SSM_STATE
    rows_b = lax.broadcasted_iota(jnp.int32, (LANES, gp), 0) // GROUP_CH
    cols_b = lax.broadcasted_iota(jnp.int32, (LANES, gp), 1) // SSM_STATE
    mask_b = rows_b == cols_b
    rows_c = lax.broadcasted_iota(jnp.int32, (gp, LANES), 0) // SSM_STATE
    cols_c = lax.broadcasted_iota(jnp.int32, (gp, LANES), 1) // GROUP_CH
    mask_c = rows_c == cols_c
    for j in range(nj):
        crj = cr[:, j * gp:(j + 1) * gp]
        cij = ci[:, j * gp:(j + 1) * gp]
        b_re = btr_ref[0, j]
        b_im = bti_ref[0, j]
        bb_re = jnp.where(mask_b, crj * b_re - cij * b_im, 0.0)
        bb_im = jnp.where(mask_b, crj * b_im + cij * b_re, 0.0)
        bx_ref[0, j, :, :gp] = bb_re.astype(BF16)
        bx_ref[0, j, :, gp:] = bb_im.astype(BF16)
        cm_ref[0, j, :gp, :] = jnp.where(mask_c, ctr_ref[0, j], 0.0).astype(BF16)
        cm_ref[0, j, gp:, :] = jnp.where(mask_c, -cti_ref[0, j], 0.0).astype(BF16)


def _ssm_prep(lam_re, lam_im, log_step, b_re, b_im, c_re, c_im):
    L, G, P = lam_re.shape
    GC = b_re.shape[-1]
    gpb = LANES // GC
    nj = G // gpb
    gp = gpb * P
    flat = lambda a: a.reshape(L, 1, G * P)
    ls = jnp.repeat(log_step, P, axis=-1).reshape(L, 1, G * P)

    def b_tiles(b):
        t = b.reshape(L, nj, gpb, P, GC).transpose(0, 1, 4, 2, 3).reshape(L, nj, 1, GC, gp)
        return jnp.broadcast_to(t, (L, nj, gpb, GC, gp)).reshape(L, nj, LANES, gp)

    def c_tiles(c):
        t = c.reshape(L, nj, gpb, GC, P).transpose(0, 1, 2, 4, 3).reshape(L, nj, gp, 1, GC)
        return jnp.broadcast_to(t, (L, nj, gp, gpb, GC)).reshape(L, nj, gp, LANES)

    vec = pl.BlockSpec((1, 1, G * P), lambda l: (l, 0, 0))
    bt = pl.BlockSpec((1, nj, LANES, gp), lambda l: (l, 0, 0, 0))
    ct = pl.BlockSpec((1, nj, gp, LANES), lambda l: (l, 0, 0, 0))
    return pl.pallas_call(
        _ssm_prep_kernel,
        grid=(L,),
        in_specs=[vec, vec, vec, bt, bt, ct, ct],
        out_specs=[vec, vec,
                   pl.BlockSpec((1, nj, LANES, 2 * gp), lambda l: (l, 0, 0, 0)),
                   pl.BlockSpec((1, nj, 2 * gp, LANES), lambda l: (l, 0, 0, 0))],
        out_shape=[jax.ShapeDtypeStruct((L, 1, G * P), F32), jax.ShapeDtypeStruct((L, 1, G * P), F32),
                   jax.ShapeDtypeStruct((L, nj, LANES, 2 * gp), BF16),
                   jax.ShapeDtypeStruct((L, nj, 2 * gp, LANES), BF16)],
        name="ssm_prep",
    )(flat(lam_re), flat(lam_im), ls, b_tiles(b_re), b_tiles(b_im), c_tiles(c_re), c_tiles(c_im))


def _project_rows(x, g1_ref, w_ref, qg_ref, kg_ref):
    aw = N_HEADS * HEAD_DIM
    kw = N_KV_HEADS * HEAD_DIM
    xn = _rms(x, g1_ref[...]).astype(BF16)
    z = jnp.dot(xn, w_ref[...], preferred_element_type=F32)
    q = _seg_rms(z[:, :aw], qg_ref[...]) * (HEAD_DIM ** -0.5 * LOG2E)
    k = _seg_rms(z[:, aw:aw + kw], kg_ref[...])
    v = z[:, aw + kw:aw + 2 * kw]
    u = z[:, aw + 2 * kw:]
    lower = (lax.broadcasted_iota(jnp.int32, q.shape, 1) % LANES) < HEAD_DIM
    q_lo = jnp.where(lower, q, 0.0)
    q_hi = jnp.where(lower, 0.0, q)
    return q_lo, q_hi, k, v, u


def _stage_mixer_weights(layer, w_in_hbm, wglu_hbm, w_ref, wglu_ref, stage, sem):
    _stage_weights(
        _weight_chunks(w_in_hbm.at[layer], w_ref, w_ref.shape[0], w_ref.shape[1], stage.shape[1], w_ref.shape[1])
        + _weight_chunks(wglu_hbm.at[layer], wglu_ref, wglu_ref.shape[0], wglu_ref.shape[1], stage.shape[1],
                         wglu_ref.shape[1]),
        stage, sem)


def _mixer_weight_scratch(D, in_cols, sw):
    return [pltpu.VMEM((D, in_cols), BF16), pltpu.VMEM((sw, sw), BF16),
            pltpu.VMEM((WEIGHT_SLOTS, LANES, in_cols), F32), pltpu.SemaphoreType.DMA((WEIGHT_SLOTS,))]


def _ssm_readout(y, u, d_ref, wglu_ref, bglu_ref):
    y = _gelu_tanh(y + d_ref[...] * u)
    gate = jnp.dot(y.astype(BF16), wglu_ref[...], preferred_element_type=F32) + bglu_ref[...]
    return y * jax.nn.sigmoid(gate)


def _softmax_pv(s, sink, v):
    m = jnp.maximum(jnp.max(s, axis=-1, keepdims=True), sink)
    e = jnp.exp2(s - m)
    den = jnp.sum(e, axis=-1, keepdims=True) + jnp.exp2(sink - m)
    return jnp.dot(e.astype(BF16), v, preferred_element_type=F32) * (1.0 / den)


def _mixer_prompt_kernel(layer, sink_ref, x_ref, w_in_hbm, wglu_hbm, g1_ref, qg_ref, kg_ref,
                         ar_ref, ai_ref, bx_ref, cm_ref, d_ref, bglu_ref,
                         attn_ref, ssm_ref, kout_ref, vout_ref, hr_ref, hi_ref,
                         w_ref, wglu_ref, stage, sem, q_s, k_s, v_s, u_s, y_s, *x_s):
    c = pl.program_id(0)
    nb, tt, _ = x_ref.shape
    rows = nb * tt
    n_tiles = rows // ROW_TILE
    bpt = ROW_TILE // tt
    n_state = ar_ref.shape[-1]
    nj = bx_ref.shape[0]
    gp = n_state // nj
    xr_s, xi_s = x_s[:nj], x_s[nj:]
    kw = kout_ref.shape[1]

    @pl.when(c == 0)
    def _():
        _stage_mixer_weights(layer, w_in_hbm, wglu_hbm, w_ref, wglu_ref, stage, sem)
        k_s[:, :, :tt, :] = jnp.zeros((2, nb, tt, kw), BF16)
        v_s[:, :, :tt, :] = jnp.zeros((2, nb, tt, kw), BF16)
        hr_ref[...] = jnp.zeros(hr_ref.shape, F32)
        hi_ref[...] = jnp.zeros(hi_ref.shape, F32)

    def project(r, carry):
        x = x_ref[pl.ds(r * bpt, bpt)].reshape(ROW_TILE, x_ref.shape[-1])
        q_lo, q_hi, k, v, u = _project_rows(x, g1_ref, w_ref, qg_ref, kg_ref)
        q_s[0, pl.ds(r * bpt, bpt)] = q_lo.astype(BF16).reshape(bpt, tt, -1)
        q_s[1, pl.ds(r * bpt, bpt)] = q_hi.astype(BF16).reshape(bpt, tt, -1)
        k_s[0, pl.ds(r * bpt, bpt), tt:, :] = k.astype(BF16).reshape(bpt, tt, -1)
        v_s[0, pl.ds(r * bpt, bpt), tt:, :] = v.astype(BF16).reshape(bpt, tt, -1)
        k_s[1, pl.ds(r * bpt, bpt), tt:, :] = pltpu.roll(k, HEAD_DIM, 1).astype(BF16).reshape(bpt, tt, -1)
        v_s[1, pl.ds(r * bpt, bpt), tt:, :] = pltpu.roll(v, HEAD_DIM, 1).astype(BF16).reshape(bpt, tt, -1)
        for bb in range(bpt):
            kout_ref[r * bpt + bb] = k[bb * tt:(bb + 1) * tt, :].T
            vout_ref[r * bpt + bb] = v[bb * tt:(bb + 1) * tt, :].T
        for bb in range(bpt):
            row0 = pl.multiple_of((r * bpt + bb) * U_PITCH, SUBLANES)
            for j in range(nj):
                u_s[j, pl.ds(row0, tt), :] = u[bb * tt:(bb + 1) * tt, j * LANES:(j + 1) * LANES]
        return carry

    lax.fori_loop(0, n_tiles, project, 0, unroll=True)

    t_idx = lax.broadcasted_iota(jnp.int32, (tt, 2 * tt), 0)
    j_idx = lax.broadcasted_iota(jnp.int32, (tt, 2 * tt), 1)
    dist = t_idx - j_idx + tt
    valid = (dist >= 0) & (dist <= WINDOW) & (j_idx >= jnp.where(c > 0, 0, tt))
    key_pos = lax.broadcasted_iota(jnp.int32, (1, 2 * tt), 1).astype(F32)
    row_pos = (lax.broadcasted_iota(jnp.int32, (tt, 1), 0) + tt).astype(F32)
    lower = lax.broadcasted_iota(jnp.int32, (tt, LANES), 1) < HEAD_DIM

    def attend(b):
        for p in range(N_HEADS // 2):
            kv = (2 * p) // Q_GROUP
            outs = []
            for e in range(2):
                h = 2 * p + e
                swap = 0 if kv == e else 1
                qm = q_s[e, b, :, p * LANES:(p + 1) * LANES]
                s = lax.dot_general(qm, k_s[swap, b], (((1,), (1,)), ((), ())), preferred_element_type=F32)
                slope = SLOPES[h] * LOG2E
                s = jnp.where(valid, s + slope * key_pos, NEG_INF)
                sink = sink_ref[layer, h] * LOG2E + slope * row_pos
                outs.append(_softmax_pv(s, sink, v_s[swap, b]))
            attn_ref[b, :, p * LANES:(p + 1) * LANES] = jnp.where(lower, outs[0], outs[1]).astype(BF16)

    def ssm_in_and_attend(r, carry):
        rs = pl.ds(r * ROW_TILE, ROW_TILE)
        t0 = r * (ROW_TILE // nb)
        for j in range(nj):
            up = jnp.concatenate([u_s[j, pl.ds(t0 + step, nb, stride=U_PITCH), :]
                                  for step in range(ROW_TILE // nb)], axis=0).astype(BF16)
            xj = jnp.dot(up, bx_ref[j], preferred_element_type=F32)
            xr_s[j][rs, :] = xj[:, :gp]
            xi_s[j][rs, :] = xj[:, gp:]
        for bb in range(bpt):
            attend(r * bpt + bb)
        return carry

    lax.fori_loop(0, n_tiles, ssm_in_and_attend, 0, unroll=2)

    k_s[:, :, :tt, :] = k_s[:, :, tt:, :]
    v_s[:, :, :tt, :] = v_s[:, :, tt:, :]

    def ssm_out_tile(j, r):
        rs = pl.ds(r * ROW_TILE, ROW_TILE)
        h_cat =jnp.concatenate([xr_s[j][rs, :], xi_s[j][rs, :]], axis=1).astype(BF16)
        y_s[j, rs, :] = jnp.dot(h_cat, cm_ref[j], preferred_element_type=F32)

    for lb in range(nj):
        sl = slice(lb * gp, (lb + 1) * gp)
        a_re = jnp.broadcast_to(ar_ref[:, sl], (nb, gp))
        a_im = jnp.broadcast_to(ai_ref[:, sl], (nb, gp))

        def scan_tile(r, h, lb=lb, a_re=a_re, a_im=a_im):
            if lb > 0:
                ssm_out_tile(lb - 1, r)
            h_re, h_im = h
            for step in range(ROW_TILE // nb):
                row = pl.ds(pl.multiple_of(r * ROW_TILE + step * nb, nb), nb)
                n_re = a_re * h_re - a_im * h_im + xr_s[lb][row, :]
                n_im = a_re * h_im + a_im * h_re + xi_s[lb][row, :]
                xr_s[lb][row, :] = n_re
                xi_s[lb][row, :] = n_im
                h_re, h_im = n_re, n_im
            return h_re, h_im

        h_re, h_im = lax.fori_loop(0, n_tiles, scan_tile, (hr_ref[:, sl], hi_ref[:, sl]), unroll=True)
        hr_ref[:, sl] = h_re
        hi_ref[:, sl] = h_im

    def ssm_out_last(r, carry):
        ssm_out_tile(nj - 1, r)
        return carry

    lax.fori_loop(0, n_tiles, ssm_out_last, 0, unroll=True)

    def readout(b, carry):
        y = jnp.concatenate([y_s[j, pl.ds(b, tt, stride=nb), :] for j in range(nj)], axis=1)
        row0 = pl.multiple_of(b * U_PITCH, SUBLANES)
        u = jnp.concatenate([u_s[j, pl.ds(row0, tt), :] for j in range(nj)], axis=1)
        ssm_ref[b] = _ssm_readout(y, u, d_ref, wglu_ref, bglu_ref).astype(BF16)
        return carry

    lax.fori_loop(0, nb, readout, 0, unroll=4)


def _mixer_prompt(layer, x, sinks, w_in, wglu, g1, qg, kg, ar, ai, bx, cm, d, bglu):
    nb, T, D = x.shape
    tt = WINDOW
    rows = nb * tt
    aw = N_HEADS * HEAD_DIM
    kw = N_KV_HEADS * HEAD_DIM
    sw = d.shape[-1]
    n_state = ar.shape[-1]
    nj = bx.shape[1]
    chunk = lambda last: pl.BlockSpec((nb, tt, last), lambda c: (0, c, 0))
    whole = pl.BlockSpec((nb, kw, tt), lambda c: (0, 0, 0))
    state = pl.BlockSpec((nb, n_state), lambda c: (0, 0))
    hbm = pl.BlockSpec(memory_space=pl.ANY)
    stacked = (g1, qg, kg, ar, ai, bx, cm, d, bglu)
    return pl.pallas_call(
        functools.partial(_mixer_prompt_kernel, layer),
        grid=(T // tt,),
        in_specs=[pl.BlockSpec(memory_space=pltpu.SMEM), chunk(D), hbm, hbm]
                 + [_layer_spec(a, layer) for a in stacked],
        out_specs=[chunk(aw), chunk(sw), whole, whole, state, state],
        out_shape=[jax.ShapeDtypeStruct((nb, T, aw), BF16), jax.ShapeDtypeStruct((nb, T, sw), BF16),
                   jax.ShapeDtypeStruct((nb, kw, tt), F32), jax.ShapeDtypeStruct((nb, kw, tt), F32),
                   jax.ShapeDtypeStruct((nb, n_state), F32), jax.ShapeDtypeStruct((nb, n_state), F32)],
        scratch_shapes=_mixer_weight_scratch(D, aw + 2 * kw + sw, sw)
                       + [pltpu.VMEM((2, nb, tt, aw), BF16),
                          pltpu.VMEM((2, nb, 2 * tt, kw), BF16), pltpu.VMEM((2, nb, 2 * tt, kw), BF16),
                          pltpu.VMEM((sw // LANES, nb * U_PITCH, LANES), F32),
                          pltpu.VMEM((nj, rows, LANES), F32)]
                       + [pltpu.VMEM((rows, n_state // nj), F32)] * (2 * nj),
        compiler_params=pltpu.CompilerParams(dimension_semantics=("arbitrary",), vmem_limit_bytes=VMEM_LIMIT),
        name="mixer_prompt",
    )(sinks, x, w_in, wglu, *stacked)


def _mixer_sample_kernel(layer, sink_ref, x_ref, ckt_ref, cvt_ref, h0rt_ref, h0it_ref, *refs):
    if layer:
        prevk_ref, prevv_ref, *refs = refs
    (w_in_hbm, wglu_hbm, g1_ref, qg_ref, kg_ref, ar_ref, ai_ref, bx_ref, cm_ref, d_ref, bglu_ref,
     attn_ref, ssm_ref, kcat_ref, vcat_ref, hrt_ref, hit_ref,
     w_ref, wglu_ref, stage, sem, q_s, o_s, u_s, xr_s, xi_s, knew_s, vnew_s,
     h0r_ref, h0i_ref, hr_ref, hi_ref, *group_s) = refs
    i = pl.program_id(0)
    rows = x_ref.shape[0]
    nbatch = h0rt_ref.shape[1]
    ts = rows // nbatch
    gb_step, kw, past = ckt_ref.shape
    gb = SUBLANES
    n_tiles = rows // ROW_TILE
    n_state = ar_ref.shape[-1]
    nj = bx_ref.shape[0]
    gp = n_state // nj

    @pl.when(i == 0)
    def _():
        _stage_mixer_weights(layer, w_in_hbm, wglu_hbm, w_ref, wglu_ref, stage, sem)

        def project(r, carry):
            rs = pl.ds(r * ROW_TILE, ROW_TILE)
            q_lo, q_hi, k, v, u = _project_rows(x_ref[rs, :], g1_ref, w_ref, qg_ref, kg_ref)
            q_s[0, rs, :] = q_lo
            q_s[1, rs, :] = q_hi
            knew_s[rs, :] = k
            vnew_s[rs, :] = v
            u_s[rs, :] = u
            return carry

        lax.fori_loop(0, n_tiles, project, 0)

    n_q = ts * Q_GROUP * gb
    n_keys = gb * past + ts * gb
    r_idx = lax.broadcasted_iota(jnp.int32, (n_q, n_keys), 0)
    c_idx = lax.broadcasted_iota(jnp.int32, (n_q, n_keys), 1)
    q_t = r_idx // (Q_GROUP * gb)
    q_b = r_idx % gb
    is_cache = c_idx < gb * past
    key_b = jnp.where(is_cache, c_idx // past, (c_idx - gb * past) % gb)
    dist = jnp.where(is_cache, past + q_t - c_idx % past, q_t - (c_idx - gb * past) // gb)
    valid = (key_b == q_b) & (dist >= 0) & (dist <= WINDOW)
    dist_f = dist.astype(F32)
    hh = (lax.broadcasted_iota(jnp.int32, (n_q, 1), 0) // gb) % Q_GROUP
    lower = lax.broadcasted_iota(jnp.int32, (gb, LANES), 1) < HEAD_DIM
    def attend_group(sg):
        ck_ref, cv_ref, kroll_s, vroll_s = group_s[4 * sg:4 * sg + 4]
        sbase = pl.multiple_of(i * gb_step + sg * gb, gb)
        for cat_ref, ct_ref, c_ref, roll_s, new_s in ((kcat_ref, ckt_ref, ck_ref, kroll_s, knew_s),
                                                      (vcat_ref, cvt_ref, cv_ref, vroll_s, vnew_s)):
            for b in range(gb):
                c_ref[b * past:(b + 1) * past, :] = ct_ref[sg * gb + b].T
            for b in range(gb):
                roll_s[b * past:(b + 1) * past - ts, :] = c_ref[b * past + ts:(b + 1) * past, :]
            for t in range(ts):
                roll_s[pl.ds(past - ts + t, gb, stride=past), :] = new_s[pl.ds(sbase + t * nbatch, gb), :]
            for b in range(gb):
                cat_ref[layer, sg * gb + b] = roll_s[b * past:(b + 1) * past, :].T

        k_all = jnp.concatenate(
            [ck_ref[...]] + [knew_s[pl.ds(sbase + t * nbatch, gb), :] for t in range(ts)], axis=0).astype(BF16)
        v_all = jnp.concatenate(
            [cv_ref[...]] + [vnew_s[pl.ds(sbase + t * nbatch, gb), :] for t in range(ts)], axis=0).astype(BF16)
        outs = []
        for kvh in range(N_KV_HEADS):
            slope = jnp.zeros((n_q, 1), F32)
            sink = jnp.zeros((n_q, 1), F32)
            for g in range(Q_GROUP):
                h = kvh * Q_GROUP + g
                slope = jnp.where(hh == g, SLOPES[h] * LOG2E, slope)
                sink = jnp.where(hh == g, sink_ref[layer, h] * LOG2E, sink)
            pieces = []
            for t in range(ts):
                for g in range(Q_GROUP):
                    h = kvh * Q_GROUP + g
                    piece = q_s[h % 2, pl.ds(sbase + t * nbatch, gb), (h // 2) * LANES:(h // 2 + 1) * LANES]
                    pieces.append(piece if h % 2 == kvh else pltpu.roll(piece, HEAD_DIM, 1))
            qm = jnp.concatenate(pieces, axis=0).astype(BF16)
            s = lax.dot_general(qm, k_all, (((1,), (1,)), ((), ())), preferred_element_type=F32)
            s = jnp.where(valid, s - slope * dist_f, NEG_INF)
            outs.append(_softmax_pv(s, sink, v_all))
        for t in range(ts):
            for p in range(N_HEADS // 2):
                kvh = (2 * p) // Q_GROUP
                r0 = (t * Q_GROUP + 2 * p - kvh * Q_GROUP) * gb
                lo = outs[kvh][r0:r0 + gb]
                hi = outs[kvh][r0 + gb:r0 + 2 * gb]
                if kvh == 1:
                    lo = pltpu.roll(lo, HEAD_DIM, 1)
                else:
                    hi = pltpu.roll(hi, HEAD_DIM, 1)
                o_s[pl.ds(sbase + t * nbatch, gb), p * LANES:(p + 1) * LANES] = jnp.where(lower, lo, hi)

    for sg in range(gb_step // gb):
        attend_group(sg)
    if layer:
        kcat_ref[:layer] = prevk_ref[...]
        vcat_ref[:layer] = prevv_ref[...]

    @pl.when(i == pl.num_programs(0) - 1)
    def _():
        attn_ref[...] = o_s[...].astype(BF16)

        def ssm_in(r, carry):
            rs = pl.ds(r * ROW_TILE, ROW_TILE)
            ub = u_s[rs, :].astype(BF16)
            for j in range(nj):
                xj = jnp.dot(ub[:, j * LANES:(j + 1) * LANES], bx_ref[j], preferred_element_type=F32)
                xr_s[rs, j * gp:(j + 1) * gp] = xj[:, :gp]
                xi_s[rs, j * gp:(j + 1) * gp] = xj[:, gp:]
            return carry

        lax.fori_loop(0, n_tiles, ssm_in, 0)

        for blk in range(n_state // LANES):
            bs = slice(blk * LANES, (blk + 1) * LANES)
            h0r_ref[:, bs] = h0rt_ref[bs, :].T
            h0i_ref[:, bs] = h0it_ref[bs, :].T

        lane_blk = 4 * LANES
        for lb in range(n_state // lane_blk):
            sl = slice(lb * lane_blk, (lb + 1) * lane_blk)
            a_re = jnp.broadcast_to(ar_ref[:, sl], (SUBLANES, lane_blk))
            a_im = jnp.broadcast_to(ai_ref[:, sl], (SUBLANES, lane_blk))

            def scan(bt, carry):
                b0 = pl.multiple_of(bt * SUBLANES, SUBLANES)
                h_re = h0r_ref[pl.ds(b0, SUBLANES), sl]
                h_im = h0i_ref[pl.ds(b0, SUBLANES), sl]
                for t in range(ts):
                    rs = pl.ds(b0 + t * nbatch, SUBLANES)
                    n_re = a_re * h_re - a_im * h_im + xr_s[rs, sl]
                    n_im = a_re * h_im + a_im * h_re + xi_s[rs, sl]
                    xr_s[rs, sl] = n_re
                    xi_s[rs, sl] = n_im
                    h_re, h_im = n_re, n_im
                hr_ref[pl.ds(b0, SUBLANES), sl] = h_re
                hi_ref[pl.ds(b0, SUBLANES), sl] = h_im
                return carry

            lax.fori_loop(0, nbatch // SUBLANES, scan, 0)

        for blk in range(n_state // LANES):
            bs = slice(blk * LANES, (blk + 1) * LANES)
            hrt_ref[bs, :] = hr_ref[:, bs].T
            hit_ref[bs, :] = hi_ref[:, bs].T

        def ssm_out(r, carry):
            rs = pl.ds(r * ROW_TILE, ROW_TILE)
            ys = []
            for j in range(nj):
                h_cat = jnp.concatenate([xr_s[rs, j * gp:(j + 1) * gp], xi_s[rs, j * gp:(j + 1) * gp]],
                                        axis=1).astype(BF16)
                ys.append(jnp.dot(h_cat, cm_ref[j], preferred_element_type=F32))
            y = jnp.concatenate(ys, axis=1)
            ssm_ref[rs, :] = _ssm_readout(y, u_s[rs, :], d_ref, wglu_ref, bglu_ref).astype(BF16)
            return carry

        lax.fori_loop(0, n_tiles, ssm_out, 0)


def _mixer_sample(layer, x, ckt, cvt, h0rt, h0it, prev_k, prev_v, sinks, w_in, wglu,
                  g1, qg, kg, ar, ai, bx, cm, d, bglu):
    rows, D = x.shape
    _, nbatch, kw, past = ckt.shape
    aw = N_HEADS * HEAD_DIM
    sw = d.shape[-1]
    n_state = ar.shape[-1]
    gb = SUBLANES
    assert nbatch % LANES == 0 and past == LANES and kw == LANES
    cache = pl.BlockSpec((None, gb, kw, past), lambda i: (layer, i, 0, 0))
    h0 = pl.BlockSpec((None, n_state, nbatch), lambda i: (layer, 0, 0))
    prev = pl.BlockSpec((layer, gb, kw, past), lambda i: (0, i, 0, 0))
    out2 = lambda last: pl.BlockSpec((rows, last), lambda i: (0, 0))
    cat = pl.BlockSpec((layer + 1, gb, kw, past), lambda i: (0, i, 0, 0))
    state = pl.BlockSpec((n_state, nbatch), lambda i: (0, 0))
    hbm = pl.BlockSpec(memory_space=pl.ANY)
    stacked = (g1, qg, kg, ar, ai, bx, cm, d, bglu)
    prevs = (prev_k, prev_v) if layer else ()
    return pl.pallas_call(
        functools.partial(_mixer_sample_kernel, layer),
        grid=(nbatch // gb,),
        in_specs=[pl.BlockSpec(memory_space=pltpu.SMEM), pl.BlockSpec(x.shape, lambda i: (0, 0)), cache, cache, h0, h0]
                 + [prev] * len(prevs) + [hbm, hbm] + [_layer_spec(a, layer) for a in stacked],
        out_specs=[out2(aw), out2(sw), cat, cat, state, state],
        out_shape=[jax.ShapeDtypeStruct((rows, aw), BF16), jax.ShapeDtypeStruct((rows, sw), BF16),
                   jax.ShapeDtypeStruct((layer + 1, nbatch, kw, past), F32),
                   jax.ShapeDtypeStruct((layer + 1, nbatch, kw, past), F32),
                   jax.ShapeDtypeStruct((n_state, nbatch), F32), jax.ShapeDtypeStruct((n_state, nbatch), F32)],
        scratch_shapes=_mixer_weight_scratch(D, aw + 2 * kw + sw, sw)
                       + [pltpu.VMEM((2, rows, aw), F32), pltpu.VMEM((rows, aw), F32), pltpu.VMEM((rows, sw), F32),
                          pltpu.VMEM((rows, n_state), F32), pltpu.VMEM((rows, n_state), F32),
                          pltpu.VMEM((rows, kw), F32), pltpu.VMEM((rows, kw), F32)]
                       + [pltpu.VMEM((nbatch, n_state), F32)] * 4
                       + [pltpu.VMEM((SUBLANES * past, kw), F32)] * (4 * gb // SUBLANES),
        compiler_params=pltpu.CompilerParams(dimension_semantics=("arbitrary",), vmem_limit_bytes=VMEM_LIMIT),
        name="mixer_sample",
    )(sinks, x, ckt, cvt, h0rt, h0it, *prevs, w_in, wglu, *stacked)


def _mix_ffn_rows(x_ref, attn_ref, ssm_ref, o_ref, g1_ref, wg_ref, bg_ref, wao_ref, wso_ref, wout_ref,
                  g2_ref, wup_ref, wdn_ref):
    x = x_ref[...]
    D = x.shape[-1]
    r1 = lax.rsqrt(jnp.mean(x * x, axis=-1, keepdims=True) + EPS)
    xg = (x * g1_ref[...]).astype(BF16)
    attn = attn_ref[...]
    ssm = ssm_ref[...]
    cw = 2 * LANES
    mixed = []
    for cc in range(D // cw):
        cs = slice(cc * cw, (cc + 1) * cw)
        gs = slice(D + cc * cw, D + (cc + 1) * cw)
        g_attn = jax.nn.sigmoid(r1 * jnp.dot(xg, wg_ref[:, cs], preferred_element_type=F32) + bg_ref[:, cs])
        g_ssm = jax.nn.sigmoid(r1 * jnp.dot(xg, wg_ref[:, gs], preferred_element_type=F32) + bg_ref[:, gs])
        mixed.append((g_attn * jnp.dot(attn, wao_ref[:, cs], preferred_element_type=F32)
                      + g_ssm * jnp.dot(ssm, wso_ref[:, cs], preferred_element_type=F32)).astype(BF16))
    mixed = jnp.concatenate(mixed, axis=1)
    for cc in range(D // cw):
        cs = slice(cc * cw, (cc + 1) * cw)
        o_ref[:, cs] = x[:, cs] + jnp.dot(mixed, wout_ref[:, cs], preferred_element_type=F32)
    x1 = o_ref[...]
    r2 = lax.rsqrt(jnp.mean(x1 * x1, axis=-1, keepdims=True) + EPS)
    xg2 = (x1 * g2_ref[...]).astype(BF16)
    r2_sq = r2 * r2
    fw = 4 * LANES
    for fc in range(wup_ref.shape[-1] // fw):
        fs = slice(fc * fw, (fc + 1) * fw)
        hdn = jnp.maximum(jnp.dot(xg2, wup_ref[:, fs], preferred_element_type=F32), 0.0)
        o_ref[...] += jnp.dot((r2_sq * (hdn * hdn)).astype(BF16), wdn_ref[fs, :], preferred_element_type=F32)


def _mix_ffn_kernel(layer, n_prompt, xp_ref, ap_ref, sp_ref, xs_ref, as_ref, ss_ref,
                    w_in_hbm, wao_hbm, wso_hbm, wout_hbm, wup_hbm, wdn_hbm, g1_ref, bg_ref, g2_ref,
                    op_ref, os_ref, wg_ref, wao_ref, wso_ref, wout_ref, wup_ref, wdn_ref, stage, sem):
    i = pl.program_id(0)
    cr, cc = stage.shape[1:]

    @pl.when(i == 0)
    def _():
        chunks = []
        chunks += _weight_chunks(w_in_hbm.at[layer], wg_ref, wg_ref.shape[0], wg_ref.shape[1], cr, cc,
                                 src_col0=w_in_hbm.shape[-1] - wg_ref.shape[1])
        for hbm, ref in ((wao_hbm, wao_ref), (wso_hbm, wso_ref), (wout_hbm, wout_ref), (wup_hbm, wup_ref),
                         (wdn_hbm, wdn_ref)):
            chunks += _weight_chunks(hbm.at[layer], ref, ref.shape[0], ref.shape[1], cr, cc)
        _stage_weights(chunks, stage, sem)

    w_refs = (g1_ref, wg_ref, bg_ref, wao_ref, wso_ref, wout_ref, g2_ref, wup_ref, wdn_ref)

    @pl.when(i < n_prompt)
    def _():
        _mix_ffn_rows(xp_ref, ap_ref, sp_ref, op_ref, *w_refs)

    @pl.when(i >= n_prompt)
    def _():
        _mix_ffn_rows(xs_ref, as_ref, ss_ref, os_ref, *w_refs)


def _mix_ffn(layer, xp, attn_p, ssm_p, xs, attn_s, ssm_s, w_in, wao, wso, wout, wup, wdn, g1, bg, g2):
    rp, D = xp.shape
    rs = xs.shape[0]
    tp = min(512, rp)
    ts = min(512, rs)
    assert rp % tp == 0 and rs % ts == 0
    n_p, n_s = rp // tp, rs // ts
    prow = lambda last: pl.BlockSpec((tp, last), lambda i: (jnp.minimum(i, n_p - 1), 0))
    srow = lambda last: pl.BlockSpec((ts, last), lambda i: (jnp.maximum(i - n_p, 0), 0))
    hbm = pl.BlockSpec(memory_space=pl.ANY)
    resident = lambda a: pltpu.VMEM(a.shape[1:], BF16)
    return pl.pallas_call(
        functools.partial(_mix_ffn_kernel, layer, n_p),
        grid=(n_p + n_s,),
        in_specs=[prow(D), prow(attn_p.shape[-1]), prow(ssm_p.shape[-1]),
                  srow(D), srow(attn_s.shape[-1]), srow(ssm_s.shape[-1])]
                 + [hbm] * 6 + [_layer_spec(a, layer) for a in (g1, bg, g2)],
        out_specs=[prow(D), srow(D)],
        out_shape=[jax.ShapeDtypeStruct((rp, D), F32), jax.ShapeDtypeStruct((rs, D), F32)],
        scratch_shapes=[pltpu.VMEM((D, bg.shape[-1]), BF16), resident(wao), resident(wso), resident(wout),
                        resident(wup), resident(wdn),
                        pltpu.VMEM((WEIGHT_SLOTS, 2 * LANES, D), F32), pltpu.SemaphoreType.DMA((WEIGHT_SLOTS,))],
        compiler_params=pltpu.CompilerParams(dimension_semantics=("arbitrary",), vmem_limit_bytes=VMEM_LIMIT),
        name="mix_ffn",
    )(xp, attn_p, ssm_p, xs, attn_s, ssm_s, w_in, wao, wso, wout, wup, wdn, g1, bg, g2)


def kernel(x_prompt, x_sample, cache_k, cache_v, state_ssm_re, state_ssm_im, norm1_g, w_in, b_gate, q_norm_g,
           k_norm_g, attn_sinks, lam_re, lam_im, log_step, b_re, b_im, c_re, c_im, d_skip, w_glu, b_glu,
           w_attn_o, w_ssm_o, w_out, norm2_g, w_up, w_down):
    B, T, D = x_prompt.shape
    DB, DS, _ = x_sample.shape
    L = w_in.shape[0]
    aw = N_HEADS * HEAD_DIM
    kw = N_KV_HEADS * HEAD_DIM
    sw = d_skip.shape[-1]
    G, P = lam_re.shape[1:]
    past = cache_k.shape[2]
    assert past == WINDOW and T % WINDOW == 0 and (B * WINDOW) % ROW_TILE == 0 and (DB * DS) % ROW_TILE == 0

    ar, ai, bx, cm = _ssm_prep(lam_re, lam_im, log_step, b_re, b_im, c_re, c_im)
    rows = lambda a: a.reshape(L, 1, -1)
    g1 = rows(norm1_g)
    mixer_w = (w_in, w_glu, g1, rows(jnp.tile(q_norm_g, (1, N_HEADS))), rows(jnp.tile(k_norm_g, (1, N_KV_HEADS))),
               ar, ai, bx, cm, rows(d_skip), rows(b_glu))
    ffn_w = (w_in, w_attn_o, w_ssm_o, w_out, w_up, w_down, g1, rows(b_gate), rows(norm2_g))
    kv_t = lambda a: a.transpose(0, 1, 3, 4, 2).reshape(L, DB, kw, past)
    st_t = lambda a: a.transpose(0, 2, 3, 1).reshape(L, G * P, DB)
    ckt, cvt, h0rt, h0it = kv_t(cache_k), kv_t(cache_v), st_t(state_ssm_re), st_t(state_ssm_im)

    yp = x_prompt.reshape(B * T, D)
    ys = x_sample.transpose(1, 0, 2).reshape(DS * DB, D)
    outs = [[] for _ in range(6)]
    kcat = vcat = None
    for l in range(L):
        attn_p, ssm_p, kp, vp, hrp, hip = _mixer_prompt(l, yp.reshape(B, T, D), attn_sinks, *mixer_w)
        attn_s, ssm_s, kcat, vcat, hrs, his = _mixer_sample(l, ys, ckt, cvt, h0rt, h0it, kcat, vcat, attn_sinks,
                                                            *mixer_w)
        yp, ys = _mix_ffn(l, yp, attn_p.reshape(B * T, aw), ssm_p.reshape(B * T, sw), ys, attn_s, ssm_s, *ffn_w)
        for o, v in zip(outs, (kp, vp, hrp, hip, hrs, his)):
            o.append(v)

    kp, vp, hrp, hip, hrs, his = (jnp.stack(o) for o in outs)
    kv_back = lambda a: a.reshape(a.shape[:2] + (N_KV_HEADS, HEAD_DIM, a.shape[-1])).transpose(0, 1, 4, 2, 3)
    st_back = lambda a: a.reshape(L, G, P, DB).transpose(0, 3, 1, 2)
    y_sample = ys.reshape(DS, DB, D).transpose(1, 0, 2)
    return (yp.reshape(B, T, D), y_sample, kv_back(kp), kv_back(vp), hrp.reshape(L, B, G, P), hip.reshape(L, B, G, P),
            kv_back(kcat), kv_back(vcat), st_back(hrs), st_back(his))
```

```python
import functools
import math

import jax
import jax.numpy as jnp
from jax import lax
from jax.experimental import pallas as pl
from jax.experimental.pallas import tpu as pltpu

F32 = jnp.float32
BF16 = jnp.bfloat16

HEAD_DIM = 64
N_HEADS = 8
N_KV_HEADS = 2
Q_GROUP = N_HEADS // N_KV_HEADS
WINDOW = 128
GROUP_CH = 16
SSM_STATE = 64
EPS = 1e-6
NEG_INF = -1e30
LOG2E = math.log2(math.e)
LANES = 128
SUBLANES = 8
ROW_TILE = 256
U_PITCH = WINDOW + SUBLANES
VMEM_LIMIT = 60 * 1024 * 1024

WEIGHT_SLOTS = 4
SLOPES = tuple(2.0 ** (-8.0 * (h + 1) / N_HEADS) for h in range(N_HEADS))


def _rms(x, g):
    return x * lax.rsqrt(jnp.mean(x * x, axis=-1, keepdims=True) + EPS) * g


def _seg_rms(x, g):
    lower = lax.broadcasted_iota(jnp.int32, (x.shape[0], LANES), 1) < HEAD_DIM
    normed = []
    for p in range(x.shape[1] // LANES):
        xb = x[:, p * LANES:(p + 1) * LANES]
        sq = xb * xb
        lo = jnp.sum(jnp.where(lower, sq, 0.0), axis=-1, keepdims=True)
        hi = jnp.sum(jnp.where(lower, 0.0, sq), axis=-1, keepdims=True)
        ms = jnp.where(lower, lo, hi) * (1.0 / HEAD_DIM)
        normed.append(xb * lax.rsqrt(ms + EPS))
    return jnp.concatenate(normed, axis=1) * g


def _gelu_tanh(y):
    return 0.5 * y * (1.0 + jnp.tanh(math.sqrt(2.0 / math.pi) * (y + 0.044715 * (y * y * y))))


def _layer_spec(a, layer):
    n = a.ndim - 1
    return pl.BlockSpec((None,) + a.shape[1:], lambda *_: (layer,) + (0,) * n, pipeline_mode=pl.Buffered(1))


def _stage_weights(chunks, stage, sem):
    slots = stage.shape[0]

    def copy(k):
        src, _, cols = chunks[k]
        return pltpu.make_async_copy(src, stage.at[k % slots, :, pl.ds(0, cols)], sem.at[k % slots])

    for k in range(min(slots - 1, len(chunks))):
        copy(k).start(priority=k % 2)
    for k, (_, dst, cols) in enumerate(chunks):
        if k + slots - 1 < len(chunks):
            copy(k + slots - 1).start(priority=(k + slots - 1) % 2)
        copy(k).wait()
        dst[...] = stage[k % slots, :, :cols].astype(BF16)


def _weight_chunks(src, dst, rows, cols, chunk_rows, chunk_cols, src_col0=0):
    return [(src.at[pl.ds(r, chunk_rows), pl.ds(src_col0 + c, chunk_cols)],
             dst.at[pl.ds(r, chunk_rows), pl.ds(c, chunk_cols)], chunk_cols)
            for r in range(0, rows, chunk_rows) for c in range(0, cols, chunk_cols)]


def _ssm_prep_kernel(lr_ref, li_ref, ls_ref, btr_ref, bti_ref, ctr_ref, cti_ref,
                     ar_ref, ai_ref, bx_ref, cm_ref):
    lr = lr_ref[0]
    li = li_ref[0]
    step = jnp.exp(ls_ref[0])
    mag = jnp.exp(lr * step)
    ar = mag * jnp.cos(li * step)
    ai = mag * jnp.sin(li * step)
    den = lr * lr + li * li
    cr = ((ar - 1.0) * lr + ai * li) / den
    ci = (ai * lr - (ar - 1.0) * li) / den
    ar_ref[0] = ar
    ai_ref[0] = ai
    nj = btr_ref.shape[1]
    gp = SUBLANES * SSM_STATE
    rows_b = lax.broadcasted_iota(jnp.int32, (LANES, gp), 0) // GROUP_CH
    cols_b = lax.broadcasted_iota(jnp.int32, (LANES, gp), 1) // SSM_STATE
    mask_b = rows_b == cols_b
    rows_c = lax.broadcasted_iota(jnp.int32, (gp, LANES), 0) // SSM_STATE
    cols_c = lax.broadcasted_iota(jnp.int32, (gp, LANES), 1) // GROUP_CH
    mask_c = rows_c == cols_c
    for j in range(nj):
        crj = cr[:, j * gp:(j + 1) * gp]
        cij = ci[:, j * gp:(j + 1) * gp]
        b_re = btr_ref[0, j]
        b_im = bti_ref[0, j]
        bb_re = jnp.where(mask_b, crj * b_re - cij * b_im, 0.0)
        bb_im = jnp.where(mask_b, crj * b_im + cij * b_re, 0.0)
        bx_ref[0, j, :, :gp] = bb_re.astype(BF16)
        bx_ref[0, j, :, gp:] = bb_im.astype(BF16)
        cm_ref[0, j, :gp, :] = jnp.where(mask_c, ctr_ref[0, j], 0.0).astype(BF16)
        cm_ref[0, j, gp:, :] = jnp.where(mask_c, -cti_ref[0, j], 0.0).astype(BF16)


def _ssm_prep(lam_re, lam_im, log_step, b_re, b_im, c_re, c_im):
    L, G, P = lam_re.shape
    GC = b_re.shape[-1]
    gpb = LANES // GC
    nj = G // gpb
    gp = gpb * P
    flat = lambda a: a.reshape(L, 1, G * P)
    ls = jnp.repeat(log_step, P, axis=-1).reshape(L, 1, G * P)

    def b_tiles(b):
        t = b.reshape(L, nj, gpb, P, GC).transpose(0, 1, 4, 2, 3).reshape(L, nj, 1, GC, gp)
        return jnp.broadcast_to(t, (L, nj, gpb, GC, gp)).reshape(L, nj, LANES, gp)

    def c_tiles(c):
        t = c.reshape(L, nj, gpb, GC, P).transpose(0, 1, 2, 4, 3).reshape(L, nj, gp, 1, GC)
        return jnp.broadcast_to(t, (L, nj, gp, gpb, GC)).reshape(L, nj, gp, LANES)

    vec = pl.BlockSpec((1, 1, G * P), lambda l: (l, 0, 0))
    bt = pl.BlockSpec((1, nj, LANES, gp), lambda l: (l, 0, 0, 0))
    ct = pl.BlockSpec((1, nj, gp, LANES), lambda l: (l, 0, 0, 0))
    return pl.pallas_call(
        _ssm_prep_kernel,
        grid=(L,),
        in_specs=[vec, vec, vec, bt, bt, ct, ct],
        out_specs=[vec, vec,
                   pl.BlockSpec((1, nj, LANES, 2 * gp), lambda l: (l, 0, 0, 0)),
                   pl.BlockSpec((1, nj, 2 * gp, LANES), lambda l: (l, 0, 0, 0))],
        out_shape=[jax.ShapeDtypeStruct((L, 1, G * P), F32), jax.ShapeDtypeStruct((L, 1, G * P), F32),
                   jax.ShapeDtypeStruct((L, nj, LANES, 2 * gp), BF16),
                   jax.ShapeDtypeStruct((L, nj, 2 * gp, LANES), BF16)],
        name="ssm_prep",
    )(flat(lam_re), flat(lam_im), ls, b_tiles(b_re), b_tiles(b_im), c_tiles(c_re), c_tiles(c_im))


def _project_rows(x, g1_ref, w_ref, qg_ref, kg_ref):
    aw = N_HEADS * HEAD_DIM
    kw = N_KV_HEADS * HEAD_DIM
    xn = _rms(x, g1_ref[...]).astype(BF16)
    z = jnp.dot(xn, w_ref[...], preferred_element_type=F32)
    q = _seg_rms(z[:, :aw], qg_ref[...]) * (HEAD_DIM ** -0.5 * LOG2E)
    k = _seg_rms(z[:, aw:aw + kw], kg_ref[...])
    v = z[:, aw + kw:aw + 2 * kw]
    u = z[:, aw + 2 * kw:]
    lower = (lax.broadcasted_iota(jnp.int32, q.shape, 1) % LANES) < HEAD_DIM
    q_lo = jnp.where(lower, q, 0.0)
    q_hi = jnp.where(lower, 0.0, q)
    return q_lo, q_hi, k, v, u


def _stage_mixer_weights(layer, w_in_hbm, wglu_hbm, w_ref, wglu_ref, stage, sem):
    _stage_weights(
        _weight_chunks(w_in_hbm.at[layer], w_ref, w_ref.shape[0], w_ref.shape[1], stage.shape[1], w_ref.shape[1])
        + _weight_chunks(wglu_hbm.at[layer], wglu_ref, wglu_ref.shape[0], wglu_ref.shape[1], stage.shape[1],
                         wglu_ref.shape[1]),
        stage, sem)


def _mixer_weight_scratch(D, in_cols, sw):
    return [pltpu.VMEM((D, in_cols), BF16), pltpu.VMEM((sw, sw), BF16),
            pltpu.VMEM((WEIGHT_SLOTS, LANES, in_cols), F32), pltpu.SemaphoreType.DMA((WEIGHT_SLOTS,))]


def _ssm_readout(y, u, d_ref, wglu_ref, bglu_ref):
    y = _gelu_tanh(y + d_ref[...] * u)
    gate = jnp.dot(y.astype(BF16), wglu_ref[...], preferred_element_type=F32) + bglu_ref[...]
    return y * jax.nn.sigmoid(gate)


def _softmax_pv(s, sink, v):
    m = jnp.maximum(jnp.max(s, axis=-1, keepdims=True), sink)
    e = jnp.exp2(s - m)
    den = jnp.sum(e, axis=-1, keepdims=True) + jnp.exp2(sink - m)
    return jnp.dot(e.astype(BF16), v, preferred_element_type=F32) * (1.0 / den)


def _mixer_prompt_kernel(layer, sink_ref, x_ref, w_in_hbm, wglu_hbm, g1_ref, qg_ref, kg_ref,
                         ar_ref, ai_ref, bx_ref, cm_ref, d_ref, bglu_ref,
                         attn_ref, ssm_ref, kout_ref, vout_ref, hr_ref, hi_ref,
                         w_ref, wglu_ref, stage, sem, q_s, k_s, v_s, u_s, y_s, *x_s):
    c = pl.program_id(0)
    nb, tt, _ = x_ref.shape
    rows = nb * tt
    n_tiles = rows // ROW_TILE
    bpt = ROW_TILE // tt
    n_state = ar_ref.shape[-1]
    nj = bx_ref.shape[0]
    gp = n_state // nj
    xr_s, xi_s = x_s[:nj], x_s[nj:]
    kw = kout_ref.shape[1]

    @pl.when(c == 0)
    def _():
        _stage_mixer_weights(layer, w_in_hbm, wglu_hbm, w_ref, wglu_ref, stage, sem)
        k_s[:, :, :tt, :] = jnp.zeros((2, nb, tt, kw), BF16)
        v_s[:, :, :tt, :] = jnp.zeros((2, nb, tt, kw), BF16)
        hr_ref[...] = jnp.zeros(hr_ref.shape, F32)
        hi_ref[...] = jnp.zeros(hi_ref.shape, F32)

    def project(r, carry):
        x = x_ref[pl.ds(r * bpt, bpt)].reshape(ROW_TILE, x_ref.shape[-1])
        q_lo, q_hi, k, v, u = _project_rows(x, g1_ref, w_ref, qg_ref, kg_ref)
        q_s[0, pl.ds(r * bpt, bpt)] = q_lo.astype(BF16).reshape(bpt, tt, -1)
        q_s[1, pl.ds(r * bpt, bpt)] = q_hi.astype(BF16).reshape(bpt, tt, -1)
        k_s[0, pl.ds(r * bpt, bpt), tt:, :] = k.astype(BF16).reshape(bpt, tt, -1)
        v_s[0, pl.ds(r * bpt, bpt), tt:, :] = v.astype(BF16).reshape(bpt, tt, -1)
        k_s[1, pl.ds(r * bpt, bpt), tt:, :] = pltpu.roll(k, HEAD_DIM, 1).astype(BF16).reshape(bpt, tt, -1)
        v_s[1, pl.ds(r * bpt, bpt), tt:, :] = pltpu.roll(v, HEAD_DIM, 1).astype(BF16).reshape(bpt, tt, -1)
        for bb in range(bpt):
            kout_ref[r * bpt + bb] = k[bb * tt:(bb + 1) * tt, :].T
            vout_ref[r * bpt + bb] = v[bb * tt:(bb + 1) * tt, :].T
        for bb in range(bpt):
            row0 = pl.multiple_of((r * bpt + bb) * U_PITCH, SUBLANES)
            for j in range(nj):
                u_s[j, pl.ds(row0, tt), :] = u[bb * tt:(bb + 1) * tt, j * LANES:(j + 1) * LANES]
        return carry

    lax.fori_loop(0, n_tiles, project, 0, unroll=True)

    t_idx = lax.broadcasted_iota(jnp.int32, (tt, 2 * tt), 0)
    j_idx = lax.broadcasted_iota(jnp.int32, (tt, 2 * tt), 1)
    dist = t_idx - j_idx + tt
    valid = (dist >= 0) & (dist <= WINDOW) & (j_idx >= jnp.where(c > 0, 0, tt))
    key_pos = lax.broadcasted_iota(jnp.int32, (1, 2 * tt), 1).astype(F32)
    row_pos = (lax.broadcasted_iota(jnp.int32, (tt, 1), 0) + tt).astype(F32)
    lower = lax.broadcasted_iota(jnp.int32, (tt, LANES), 1) < HEAD_DIM

    def attend(b):
        for p in range(N_HEADS // 2):
            kv = (2 * p) // Q_GROUP
            outs = []
            for e in range(2):
                h = 2 * p + e
                swap = 0 if kv == e else 1
                qm = q_s[e, b, :, p * LANES:(p + 1) * LANES]
                s = lax.dot_general(qm, k_s[swap, b], (((1,), (1,)), ((), ())), preferred_element_type=F32)
                slope = SLOPES[h] * LOG2E
                s = jnp.where(valid, s + slope * key_pos, NEG_INF)
                sink = sink_ref[layer, h] * LOG2E + slope * row_pos
                outs.append(_softmax_pv(s, sink, v_s[swap, b]))
            attn_ref[b, :, p * LANES:(p + 1) * LANES] = jnp.where(lower, outs[0], outs[1]).astype(BF16)

    def ssm_in_and_attend(r, carry):
        rs = pl.ds(r * ROW_TILE, ROW_TILE)
        t0 = r * (ROW_TILE // nb)
        for j in range(nj):
            up = jnp.concatenate([u_s[j, pl.ds(t0 + step, nb, stride=U_PITCH), :]
                                  for step in range(ROW_TILE // nb)], axis=0).astype(BF16)
            xj = jnp.dot(up, bx_ref[j], preferred_element_type=F32)
            xr_s[j][rs, :] = xj[:, :gp]
            xi_s[j][rs, :] = xj[:, gp:]
        for bb in range(bpt):
            attend(r * bpt + bb)
        return carry

    lax.fori_loop(0, n_tiles, ssm_in_and_attend, 0, unroll=2)

    k_s[:, :, :tt, :] = k_s[:, :, tt:, :]
    v_s[:, :, :tt, :] = v_s[:, :, tt:, :]

    def ssm_out_tile(j, r):
        rs = pl.ds(r * ROW_TILE, ROW_TILE)
        h_cat =jnp.concatenate([xr_s[j][rs, :], xi_s[j][rs, :]], axis=1).astype(BF16)
        y_s[j, rs, :] = jnp.dot(h_cat, cm_ref[j], preferred_element_type=F32)

    for lb in range(nj):
        sl = slice(lb * gp, (lb + 1) * gp)
        a_re = jnp.broadcast_to(ar_ref[:, sl], (nb, gp))
        a_im = jnp.broadcast_to(ai_ref[:, sl], (nb, gp))

        def scan_tile(r, h, lb=lb, a_re=a_re, a_im=a_im):
            if lb > 0:
                ssm_out_tile(lb - 1, r)
            h_re, h_im = h
            for step in range(ROW_TILE // nb):
                row = pl.ds(pl.multiple_of(r * ROW_TILE + step * nb, nb), nb)
                n_re = a_re * h_re - a_im * h_im + xr_s[lb][row, :]
                n_im = a_re * h_im + a_im * h_re + xi_s[lb][row, :]
                xr_s[lb][row, :] = n_re
                xi_s[lb][row, :] = n_im
                h_re, h_im = n_re, n_im
            return h_re, h_im

        h_re, h_im = lax.fori_loop(0, n_tiles, scan_tile, (hr_ref[:, sl], hi_ref[:, sl]), unroll=True)
        hr_ref[:, sl] = h_re
        hi_ref[:, sl] = h_im

    def ssm_out_last(r, carry):
        ssm_out_tile(nj - 1, r)
        return carry

    lax.fori_loop(0, n_tiles, ssm_out_last, 0, unroll=True)

    def readout(b, carry):
        y = jnp.concatenate([y_s[j, pl.ds(b, tt, stride=nb), :] for j in range(nj)], axis=1)
        row0 = pl.multiple_of(b * U_PITCH, SUBLANES)
        u = jnp.concatenate([u_s[j, pl.ds(row0, tt), :] for j in range(nj)], axis=1)
        ssm_ref[b] = _ssm_readout(y, u, d_ref, wglu_ref, bglu_ref).astype(BF16)
        return carry

    lax.fori_loop(0, nb, readout, 0, unroll=4)


def _mixer_prompt(layer, x, sinks, w_in, wglu, g1, qg, kg, ar, ai, bx, cm, d, bglu):
    nb, T, D = x.shape
    tt = WINDOW
    rows = nb * tt
    aw = N_HEADS * HEAD_DIM
    kw = N_KV_HEADS * HEAD_DIM
    sw = d.shape[-1]
    n_state = ar.shape[-1]
    nj = bx.shape[1]
    chunk = lambda last: pl.BlockSpec((nb, tt, last), lambda c: (0, c, 0))
    whole = pl.BlockSpec((nb, kw, tt), lambda c: (0, 0, 0))
    state = pl.BlockSpec((nb, n_state), lambda c: (0, 0))
    hbm = pl.BlockSpec(memory_space=pl.ANY)
    stacked = (g1, qg, kg, ar, ai, bx, cm, d, bglu)
    return pl.pallas_call(
        functools.partial(_mixer_prompt_kernel, layer),
        grid=(T // tt,),
        in_specs=[pl.BlockSpec(memory_space=pltpu.SMEM), chunk(D), hbm, hbm]
                 + [_layer_spec(a, layer) for a in stacked],
        out_specs=[chunk(aw), chunk(sw), whole, whole, state, state],
        out_shape=[jax.ShapeDtypeStruct((nb, T, aw), BF16), jax.ShapeDtypeStruct((nb, T, sw), BF16),
                   jax.ShapeDtypeStruct((nb, kw, tt), F32), jax.ShapeDtypeStruct((nb, kw, tt), F32),
                   jax.ShapeDtypeStruct((nb, n_state), F32), jax.ShapeDtypeStruct((nb, n_state), F32)],
        scratch_shapes=_mixer_weight_scratch(D, aw + 2 * kw + sw, sw)
                       + [pltpu.VMEM((2, nb, tt, aw), BF16),
                          pltpu.VMEM((2, nb, 2 * tt, kw), BF16), pltpu.VMEM((2, nb, 2 * tt, kw), BF16),
                          pltpu.VMEM((sw // LANES, nb * U_PITCH, LANES), F32),
                          pltpu.VMEM((nj, rows, LANES), F32)]
                       + [pltpu.VMEM((rows, n_state // nj), F32)] * (2 * nj),
        compiler_params=pltpu.CompilerParams(dimension_semantics=("arbitrary",), vmem_limit_bytes=VMEM_LIMIT),
        name="mixer_prompt",
    )(sinks, x, w_in, wglu, *stacked)


def _mixer_sample_kernel(layer, sink_ref, x_ref, ckt_ref, cvt_ref, h0rt_ref, h0it_ref, *refs):
    if layer:
        prevk_ref, prevv_ref, *refs = refs
    (w_in_hbm, wglu_hbm, g1_ref, qg_ref, kg_ref, ar_ref, ai_ref, bx_ref, cm_ref, d_ref, bglu_ref,
     attn_ref, ssm_ref, kcat_ref, vcat_ref, hrt_ref, hit_ref,
     w_ref, wglu_ref, stage, sem, q_s, o_s, u_s, xr_s, xi_s, knew_s, vnew_s,
     h0r_ref, h0i_ref, hr_ref, hi_ref, *group_s) = refs
    i = pl.program_id(0)
    rows = x_ref.shape[0]
    nbatch = h0rt_ref.shape[1]
    ts = rows // nbatch
    gb_step, kw, past = ckt_ref.shape
    gb = SUBLANES
    n_tiles = rows // ROW_TILE
    n_state = ar_ref.shape[-1]
    nj = bx_ref.shape[0]
    gp = n_state // nj

    @pl.when(i == 0)
    def _():
        _stage_mixer_weights(layer, w_in_hbm, wglu_hbm, w_ref, wglu_ref, stage, sem)

        def project(r, carry):
            rs = pl.ds(r * ROW_TILE, ROW_TILE)
            q_lo, q_hi, k, v, u = _project_rows(x_ref[rs, :], g1_ref, w_ref, qg_ref, kg_ref)
            q_s[0, rs, :] = q_lo
            q_s[1, rs, :] = q_hi
            knew_s[rs, :] = k
            vnew_s[rs, :] = v
            u_s[rs, :] = u
            return carry

        lax.fori_loop(0, n_tiles, project, 0)

    n_q = ts * Q_GROUP * gb
    n_keys = gb * past + ts * gb
    r_idx = lax.broadcasted_iota(jnp.int32, (n_q, n_keys), 0)
    c_idx = lax.broadcasted_iota(jnp.int32, (n_q, n_keys), 1)
    q_t = r_idx // (Q_GROUP * gb)
    q_b = r_idx % gb
    is_cache = c_idx < gb * past
    key_b = jnp.where(is_cache, c_idx // past, (c_idx - gb * past) % gb)
    dist = jnp.where(is_cache, past + q_t - c_idx % past, q_t - (c_idx - gb * past) // gb)
    valid = (key_b == q_b) & (dist >= 0) & (dist <= WINDOW)
    dist_f = dist.astype(F32)
    hh = (lax.broadcasted_iota(jnp.int32, (n_q, 1), 0) // gb) % Q_GROUP
    lower = lax.broadcasted_iota(jnp.int32, (gb, LANES), 1) < HEAD_DIM
    def attend_group(sg):
        ck_ref, cv_ref, kroll_s, vroll_s = group_s[4 * sg:4 * sg + 4]
        sbase = pl.multiple_of(i * gb_step + sg * gb, gb)
        for cat_ref, ct_ref, c_ref, roll_s, new_s in ((kcat_ref, ckt_ref, ck_ref, kroll_s, knew_s),
                                                      (vcat_ref, cvt_ref, cv_ref, vroll_s, vnew_s)):
            for b in range(gb):
                c_ref[b * past:(b + 1) * past, :] = ct_ref[sg * gb + b].T
            for b in range(gb):
                roll_s[b * past:(b + 1) * past - ts, :] = c_ref[b * past + ts:(b + 1) * past, :]
            for t in range(ts):
                roll_s[pl.ds(past - ts + t, gb, stride=past), :] = new_s[pl.ds(sbase + t * nbatch, gb), :]
            for b in range(gb):
                cat_ref[layer, sg * gb + b] = roll_s[b * past:(b + 1) * past, :].T

        k_all = jnp.concatenate(
            [ck_ref[...]] + [knew_s[pl.ds(sbase + t * nbatch, gb), :] for t in range(ts)], axis=0).astype(BF16)
        v_all = jnp.concatenate(
            [cv_ref[...]] + [vnew_s[pl.ds(sbase + t * nbatch, gb), :] for t in range(ts)], axis=0).astype(BF16)
        outs = []
        for kvh in range(N_KV_HEADS):
            slope = jnp.zeros((n_q, 1), F32)
            sink = jnp.zeros((n_q, 1), F32)
            for g in range(Q_GROUP):
                h = kvh * Q_GROUP + g
                slope = jnp.where(hh == g, SLOPES[h] * LOG2E, slope)
                sink = jnp.where(hh == g, sink_ref[layer, h] * LOG2E, sink)
            pieces = []
            for t in range(ts):
                for g in range(Q_GROUP):
                    h = kvh * Q_GROUP + g
                    piece = q_s[h % 2, pl.ds(sbase + t * nbatch, gb), (h // 2) * LANES:(h // 2 + 1) * LANES]
                    pieces.append(piece if h % 2 == kvh else pltpu.roll(piece, HEAD_DIM, 1))
            qm = jnp.concatenate(pieces, axis=0).astype(BF16)
            s = lax.dot_general(qm, k_all, (((1,), (1,)), ((), ())), preferred_element_type=F32)
            s = jnp.where(valid, s - slope * dist_f, NEG_INF)
            outs.append(_softmax_pv(s, sink, v_all))
        for t in range(ts):
            for p in range(N_HEADS // 2):
                kvh = (2 * p) // Q_GROUP
                r0 = (t * Q_GROUP + 2 * p - kvh * Q_GROUP) * gb
                lo = outs[kvh][r0:r0 + gb]
                hi = outs[kvh][r0 + gb:r0 + 2 * gb]
                if kvh == 1:
                    lo = pltpu.roll(lo, HEAD_DIM, 1)
                else:
                    hi = pltpu.roll(hi, HEAD_DIM, 1)
                o_s[pl.ds(sbase + t * nbatch, gb), p * LANES:(p + 1) * LANES] = jnp.where(lower, lo, hi)

    for sg in range(gb_step // gb):
        attend_group(sg)
    if layer:
        kcat_ref[:layer] = prevk_ref[...]
        vcat_ref[:layer] = prevv_ref[...]

    @pl.when(i == pl.num_programs(0) - 1)
    def _():
        attn_ref[...] = o_s[...].astype(BF16)

        def ssm_in(r, carry):
            rs = pl.ds(r * ROW_TILE, ROW_TILE)
            ub = u_s[rs, :].astype(BF16)
            for j in range(nj):
                xj = jnp.dot(ub[:, j * LANES:(j + 1) * LANES], bx_ref[j], preferred_element_type=F32)
                xr_s[rs, j * gp:(j + 1) * gp] = xj[:, :gp]
                xi_s[rs, j * gp:(j + 1) * gp] = xj[:, gp:]
            return carry

        lax.fori_loop(0, n_tiles, ssm_in, 0)

        for blk in range(n_state // LANES):
            bs = slice(blk * LANES, (blk + 1) * LANES)
            h0r_ref[:, bs] = h0rt_ref[bs, :].T
            h0i_ref[:, bs] = h0it_ref[bs, :].T

        lane_blk = 4 * LANES
        for lb in range(n_state // lane_blk):
            sl = slice(lb * lane_blk, (lb + 1) * lane_blk)
            a_re = jnp.broadcast_to(ar_ref[:, sl], (SUBLANES, lane_blk))
            a_im = jnp.broadcast_to(ai_ref[:, sl], (SUBLANES, lane_blk))

            def scan(bt, carry):
                b0 = pl.multiple_of(bt * SUBLANES, SUBLANES)
                h_re = h0r_ref[pl.ds(b0, SUBLANES), sl]
                h_im = h0i_ref[pl.ds(b0, SUBLANES), sl]
                for t in range(ts):
                    rs = pl.ds(b0 + t * nbatch, SUBLANES)
                    n_re = a_re * h_re - a_im * h_im + xr_s[rs, sl]
                    n_im = a_re * h_im + a_im * h_re + xi_s[rs, sl]
                    xr_s[rs, sl] = n_re
                    xi_s[rs, sl] = n_im
                    h_re, h_im = n_re, n_im
                hr_ref[pl.ds(b0, SUBLANES), sl] = h_re
                hi_ref[pl.ds(b0, SUBLANES), sl] = h_im
                return carry

            lax.fori_loop(0, nbatch // SUBLANES, scan, 0)

        for blk in range(n_state // LANES):
            bs = slice(blk * LANES, (blk + 1) * LANES)
            hrt_ref[bs, :] = hr_ref[:, bs].T
            hit_ref[bs, :] = hi_ref[:, bs].T

        def ssm_out(r, carry):
            rs = pl.ds(r * ROW_TILE, ROW_TILE)
            ys = []
            for j in range(nj):
                h_cat = jnp.concatenate([xr_s[rs, j * gp:(j + 1) * gp], xi_s[rs, j * gp:(j + 1) * gp]],
                                        axis=1).astype(BF16)
                ys.append(jnp.dot(h_cat, cm_ref[j], preferred_element_type=F32))
            y = jnp.concatenate(ys, axis=1)
            ssm_ref[rs, :] = _ssm_readout(y, u_s[rs, :], d_ref, wglu_ref, bglu_ref).astype(BF16)
            return carry

        lax.fori_loop(0, n_tiles, ssm_out, 0)


def _mixer_sample(layer, x, ckt, cvt, h0rt, h0it, prev_k, prev_v, sinks, w_in, wglu,
                  g1, qg, kg, ar, ai, bx, cm, d, bglu):
    rows, D = x.shape
    _, nbatch, kw, past = ckt.shape
    aw = N_HEADS * HEAD_DIM
    sw = d.shape[-1]
    n_state = ar.shape[-1]
    gb = SUBLANES
    assert nbatch % LANES == 0 and past == LANES and kw == LANES
    cache = pl.BlockSpec((None, gb, kw, past), lambda i: (layer, i, 0, 0))
    h0 = pl.BlockSpec((None, n_state, nbatch), lambda i: (layer, 0, 0))
    prev = pl.BlockSpec((layer, gb, kw, past), lambda i: (0, i, 0, 0))
    out2 = lambda last: pl.BlockSpec((rows, last), lambda i: (0, 0))
    cat = pl.BlockSpec((layer + 1, gb, kw, past), lambda i: (0, i, 0, 0))
    state = pl.BlockSpec((n_state, nbatch), lambda i: (0, 0))
    hbm = pl.BlockSpec(memory_space=pl.ANY)
    stacked = (g1, qg, kg, ar, ai, bx, cm, d, bglu)
    prevs = (prev_k, prev_v) if layer else ()
    return pl.pallas_call(
        functools.partial(_mixer_sample_kernel, layer),
        grid=(nbatch // gb,),
        in_specs=[pl.BlockSpec(memory_space=pltpu.SMEM), pl.BlockSpec(x.shape, lambda i: (0, 0)), cache, cache, h0, h0]
                 + [prev] * len(prevs) + [hbm, hbm] + [_layer_spec(a, layer) for a in stacked],
        out_specs=[out2(aw), out2(sw), cat, cat, state, state],
        out_shape=[jax.ShapeDtypeStruct((rows, aw), BF16), jax.ShapeDtypeStruct((rows, sw), BF16),
                   jax.ShapeDtypeStruct((layer + 1, nbatch, kw, past), F32),
                   jax.ShapeDtypeStruct((layer + 1, nbatch, kw, past), F32),
                   jax.ShapeDtypeStruct((n_state, nbatch), F32), jax.ShapeDtypeStruct((n_state, nbatch), F32)],
        scratch_shapes=_mixer_weight_scratch(D, aw + 2 * kw + sw, sw)
                       + [pltpu.VMEM((2, rows, aw), F32), pltpu.VMEM((rows, aw), F32), pltpu.VMEM((rows, sw), F32),
                          pltpu.VMEM((rows, n_state), F32), pltpu.VMEM((rows, n_state), F32),
                          pltpu.VMEM((rows, kw), F32), pltpu.VMEM((rows, kw), F32)]
                       + [pltpu.VMEM((nbatch, n_state), F32)] * 4
                       + [pltpu.VMEM((SUBLANES * past, kw), F32)] * (4 * gb // SUBLANES),
        compiler_params=pltpu.CompilerParams(dimension_semantics=("arbitrary",), vmem_limit_bytes=VMEM_LIMIT),
        name="mixer_sample",
    )(sinks, x, ckt, cvt, h0rt, h0it, *prevs, w_in, wglu, *stacked)


def _mix_ffn_rows(x_ref, attn_ref, ssm_ref, o_ref, g1_ref, wg_ref, bg_ref, wao_ref, wso_ref, wout_ref,
                  g2_ref, wup_ref, wdn_ref):
    x = x_ref[...]
    D = x.shape[-1]
    r1 = lax.rsqrt(jnp.mean(x * x, axis=-1, keepdims=True) + EPS)
    xg = (x * g1_ref[...]).astype(BF16)
    attn = attn_ref[...]
    ssm = ssm_ref[...]
    cw = 2 * LANES
    mixed = []
    for cc in range(D // cw):
        cs = slice(cc * cw, (cc + 1) * cw)
        gs = slice(D + cc * cw, D + (cc + 1) * cw)
        g_attn = jax.nn.sigmoid(r1 * jnp.dot(xg, wg_ref[:, cs], preferred_element_type=F32) + bg_ref[:, cs])
        g_ssm = jax.nn.sigmoid(r1 * jnp.dot(xg, wg_ref[:, gs], preferred_element_type=F32) + bg_ref[:, gs])
        mixed.append((g_attn * jnp.dot(attn, wao_ref[:, cs], preferred_element_type=F32)
                      + g_ssm * jnp.dot(ssm, wso_ref[:, cs], preferred_element_type=F32)).astype(BF16))
    mixed = jnp.concatenate(mixed, axis=1)
    for cc in range(D // cw):
        cs = slice(cc * cw, (cc + 1) * cw)
        o_ref[:, cs] = x[:, cs] + jnp.dot(mixed, wout_ref[:, cs], preferred_element_type=F32)
    x1 = o_ref[...]
    r2 = lax.rsqrt(jnp.mean(x1 * x1, axis=-1, keepdims=True) + EPS)
    xg2 = (x1 * g2_ref[...]).astype(BF16)
    r2_sq = r2 * r2
    fw = 4 * LANES
    for fc in range(wup_ref.shape[-1] // fw):
        fs = slice(fc * fw, (fc + 1) * fw)
        hdn = jnp.maximum(jnp.dot(xg2, wup_ref[:, fs], preferred_element_type=F32), 0.0)
        o_ref[...] += jnp.dot((r2_sq * (hdn * hdn)).astype(BF16), wdn_ref[fs, :], preferred_element_type=F32)


def _mix_ffn_kernel(layer, n_prompt, xp_ref, ap_ref, sp_ref, xs_ref, as_ref, ss_ref,
                    w_in_hbm, wao_hbm, wso_hbm, wout_hbm, wup_hbm, wdn_hbm, g1_ref, bg_ref, g2_ref,
                    op_ref, os_ref, wg_ref, wao_ref, wso_ref, wout_ref, wup_ref, wdn_ref, stage, sem):
    i = pl.program_id(0)
    cr, cc = stage.shape[1:]

    @pl.when(i == 0)
    def _():
        chunks = []
        chunks += _weight_chunks(w_in_hbm.at[layer], wg_ref, wg_ref.shape[0], wg_ref.shape[1], cr, cc,
                                 src_col0=w_in_hbm.shape[-1] - wg_ref.shape[1])
        for hbm, ref in ((wao_hbm, wao_ref), (wso_hbm, wso_ref), (wout_hbm, wout_ref), (wup_hbm, wup_ref),
                         (wdn_hbm, wdn_ref)):
            chunks += _weight_chunks(hbm.at[layer], ref, ref.shape[0], ref.shape[1], cr, cc)
        _stage_weights(chunks, stage, sem)

    w_refs = (g1_ref, wg_ref, bg_ref, wao_ref, wso_ref, wout_ref, g2_ref, wup_ref, wdn_ref)

    @pl.when(i < n_prompt)
    def _():
        _mix_ffn_rows(xp_ref, ap_ref, sp_ref, op_ref, *w_refs)

    @pl.when(i >= n_prompt)
    def _():
        _mix_ffn_rows(xs_ref, as_ref, ss_ref, os_ref, *w_refs)


def _mix_ffn(layer, xp, attn_p, ssm_p, xs, attn_s, ssm_s, w_in, wao, wso, wout, wup, wdn, g1, bg, g2):
    rp, D = xp.shape
    rs = xs.shape[0]
    tp = min(512, rp)
    ts = min(512, rs)
    assert rp % tp == 0 and rs % ts == 0
    n_p, n_s = rp // tp, rs // ts
    prow = lambda last: pl.BlockSpec((tp, last), lambda i: (jnp.minimum(i, n_p - 1), 0))
    srow = lambda last: pl.BlockSpec((ts, last), lambda i: (jnp.maximum(i - n_p, 0), 0))
    hbm = pl.BlockSpec(memory_space=pl.ANY)
    resident = lambda a: pltpu.VMEM(a.shape[1:], BF16)
    return pl.pallas_call(
        functools.partial(_mix_ffn_kernel, layer, n_p),
        grid=(n_p + n_s,),
        in_specs=[prow(D), prow(attn_p.shape[-1]), prow(ssm_p.shape[-1]),
                  srow(D), srow(attn_s.shape[-1]), srow(ssm_s.shape[-1])]
                 + [hbm] * 6 + [_layer_spec(a, layer) for a in (g1, bg, g2)],
        out_specs=[prow(D), srow(D)],
        out_shape=[jax.ShapeDtypeStruct((rp, D), F32), jax.ShapeDtypeStruct((rs, D), F32)],
        scratch_shapes=[pltpu.VMEM((D, bg.shape[-1]), BF16), resident(wao), resident(wso), resident(wout),
                        resident(wup), resident(wdn),
                        pltpu.VMEM((WEIGHT_SLOTS, 2 * LANES, D), F32), pltpu.SemaphoreType.DMA((WEIGHT_SLOTS,))],
        compiler_params=pltpu.CompilerParams(dimension_semantics=("arbitrary",), vmem_limit_bytes=VMEM_LIMIT),
        name="mix_ffn",
    )(xp, attn_p, ssm_p, xs, attn_s, ssm_s, w_in, wao, wso, wout, wup, wdn, g1, bg, g2)


def kernel(x_prompt, x_sample, cache_k, cache_v, state_ssm_re, state_ssm_im, norm1_g, w_in, b_gate, q_norm_g,
           k_norm_g, attn_sinks, lam_re, lam_im, log_step, b_re, b_im, c_re, c_im, d_skip, w_glu, b_glu,
           w_attn_o, w_ssm_o, w_out, norm2_g, w_up, w_down):
    B, T, D = x_prompt.shape
    DB, DS, _ = x_sample.shape
    L = w_in.shape[0]
    aw = N_HEADS * HEAD_DIM
    kw = N_KV_HEADS * HEAD_DIM
    sw = d_skip.shape[-1]
    G, P = lam_re.shape[1:]
    past = cache_k.shape[2]
    assert past == WINDOW and T % WINDOW == 0 and (B * WINDOW) % ROW_TILE == 0 and (DB * DS) % ROW_TILE == 0

    ar, ai, bx, cm = _ssm_prep(lam_re, lam_im, log_step, b_re, b_im, c_re, c_im)
    rows = lambda a: a.reshape(L, 1, -1)
    g1 = rows(norm1_g)
    mixer_w = (w_in, w_glu, g1, rows(jnp.tile(q_norm_g, (1, N_HEADS))), rows(jnp.tile(k_norm_g, (1, N_KV_HEADS))),
               ar, ai, bx, cm, rows(d_skip), rows(b_glu))
    ffn_w = (w_in, w_attn_o, w_ssm_o, w_out, w_up, w_down, g1, rows(b_gate), rows(norm2_g))
    kv_t = lambda a: a.transpose(0, 1, 3, 4, 2).reshape(L, DB, kw, past)
    st_t = lambda a: a.transpose(0, 2, 3, 1).reshape(L, G * P, DB)
    ckt, cvt, h0rt, h0it = kv_t(cache_k), kv_t(cache_v), st_t(state_ssm_re), st_t(state_ssm_im)

    yp = x_prompt.reshape(B * T, D)
    ys = x_sample.transpose(1, 0, 2).reshape(DS * DB, D)
    outs = [[] for _ in range(6)]
    kcat = vcat = None
    for l in range(L):
        attn_p, ssm_p, kp, vp, hrp, hip = _mixer_prompt(l, yp.reshape(B, T, D), attn_sinks, *mixer_w)
        attn_s, ssm_s, kcat, vcat, hrs, his = _mixer_sample(l, ys, ckt, cvt, h0rt, h0it, kcat, vcat, attn_sinks,
                                                            *mixer_w)
        yp, ys = _mix_ffn(l, yp, attn_p.reshape(B * T, aw), ssm_p.reshape(B * T, sw), ys, attn_s, ssm_s, *ffn_w)
        for o, v in zip(outs, (kp, vp, hrp, hip, hrs, his)):
            o.append(v)

    kp, vp, hrp, hip, hrs, his = (jnp.stack(o) for o in outs)
    kv_back = lambda a: a.reshape(a.shape[:2] + (N_KV_HEADS, HEAD_DIM, a.shape[-1])).transpose(0, 1, 4, 2, 3)
    st_back = lambda a: a.reshape(L, G, P, DB).transpose(0, 3, 1, 2)
    y_sample = ys.reshape(DS, DB, D).transpose(1, 0, 2)
    return (yp.reshape(B, T, D), y_sample, kv_back(kp), kv_back(vp), hrp.reshape(L, B, G, P), hip.reshape(L, B, G, P),
            kv_back(kcat), kv_back(vcat), st_back(hrs), st_back(his))
```
